```python
import jax, jax.numpy as jnp
from jax import lax
import numpy as np

D_MODEL = 1024
BATCH = 8
SEQ = 8192
DEPTH = 4

CHUNK = 64
N_MIXERS = 2
MEM_LEN = 256
HEAD_DIM = 64
D_TOK = D_MODEL // 2
D_MEMH = D_MODEL // 4
D_MIX = D_TOK + D_MEMH
N_SB_HEADS = D_TOK // HEAD_DIM
N_MEM_HEADS = D_MEMH // HEAD_DIM
POOL_WINDOWS = (2, 4, 8, 16)
N_POOL_GROUPS = len(POOL_WINDOWS)
POOL_GROUP = D_TOK // N_POOL_GROUPS
D_FF = 2 * D_MODEL
Q_BLOCK = 128
EPS = 1e-6
N_A = (DEPTH + N_MIXERS - 1) // N_MIXERS
N_B = DEPTH // N_MIXERS

kernel_name = "hybrid_pool_stickbreak_memory_trunk"


def rmsnorm(x, g):
    xf = x.astype(jnp.float32)
    y = xf * lax.rsqrt(jnp.mean(xf * xf, axis=-1, keepdims=True) + EPS) * g.astype(jnp.float32)
    return y.astype(x.dtype)


def swiglu(h, w_gate, w_up, w_down):
    return (jax.nn.silu(h @ w_gate) * (h @ w_up)) @ w_down


def pool_mixer(u, pool_w, pool_scale):
    B, S, _ = u.shape
    uf = u.astype(jnp.float32)
    cs = jnp.cumsum(uf, axis=1)
    pos = jnp.arange(S)
    outs = []
    for gi, w in enumerate(POOL_WINDOWS):
        sl = slice(gi * POOL_GROUP, (gi + 1) * POOL_GROUP)
        c = cs[..., sl]
        lagged = jnp.pad(c, ((0, 0), (w, 0), (0, 0)))[:, :S]
        cnt = jnp.minimum(pos + 1, w).astype(jnp.float32)[None, :, None]
        outs.append((c - lagged) / cnt - uf[..., sl])
    d = jnp.stack(outs, axis=2)
    y = jnp.einsum('bsgc,gcd->bsgd', d, pool_w.astype(jnp.float32)).reshape(B, S, D_TOK)
    return (y * pool_scale.astype(jnp.float32)).astype(u.dtype)


def stick_breaking(q, k, v):
    B, S, H, dh = q.shape
    nb = S // Q_BLOCK
    scale = dh ** -0.5
    qt = q.transpose(0, 2, 1, 3)
    kt = k.transpose(0, 2, 1, 3)
    vt = v.transpose(0, 2, 1, 3)
    idx = jnp.arange(Q_BLOCK)
    later_mat = (idx[:, None] > idx[None, :]).astype(jnp.float32)
    outs = []
    for bi in range(nb):
        nk = bi + 1
        K = nk * Q_BLOCK
        qb = qt[:, :, bi * Q_BLOCK:(bi + 1) * Q_BLOCK]
        z = jnp.einsum('bhqd,bhkd->bhqk', qb, kt[:, :, :K]).astype(jnp.float32) * scale
        qpos = bi * Q_BLOCK + idx
        mask = jnp.arange(K)[None, :] < qpos[:, None]
        ls = jax.nn.log_sigmoid(z)
        lf = jnp.where(mask, ls - z, 0.0)
        lfb = lf.reshape(B, H, Q_BLOCK, nk, Q_BLOCK)
        within = jnp.einsum('bhqnj,js->bhqns', lfb, later_mat,
                            precision=lax.Precision.HIGHEST)
        bsum = jnp.sum(lfb, axis=-1)
        later = lax.cumsum(bsum, axis=3, reverse=True) - bsum
        logw = ls.reshape(B, H, Q_BLOCK, nk, Q_BLOCK) + within + later[..., None]
        a = jnp.where(mask, jnp.exp(logw).reshape(B, H, Q_BLOCK, K), 0.0)
        outs.append(jnp.einsum('bhqk,bhkd->bhqd', a.astype(vt.dtype), vt[:, :, :K]))
    o = jnp.concatenate(outs, axis=2)
    return o.transpose(0, 2, 1, 3).reshape(B, S, H * dh)


def mem_attention(qm, mem_n, w_kv):
    B, S, _ = qm.shape
    L = mem_n.shape[1]
    kv = mem_n @ w_kv
    km = kv[..., :D_MEMH].reshape(B, L, N_MEM_HEADS, HEAD_DIM)
    vm = kv[..., D_MEMH:].reshape(B, L, N_MEM_HEADS, HEAD_DIM)
    qh = qm.reshape(B, S, N_MEM_HEADS, HEAD_DIM)
    s = jnp.einsum('bqhd,bkhd->bhqk', qh, km).astype(jnp.float32) * (HEAD_DIM ** -0.5)
    p = jax.nn.softmax(s, axis=-1)
    o = jnp.einsum('bhqk,bkhd->bqhd', p.astype(vm.dtype), vm)
    return o.reshape(B, S, D_MEMH)


def _fwd_setup_inputs(seed: int = 0) -> dict:
    key = jax.random.key(seed)
    ks = jax.random.split(key, 17)
    f32 = jnp.float32

    def w(k, shape, fan_in):
        return jax.random.normal(k, shape, f32) * (fan_in ** -0.5)

    return {
        "x": jax.random.normal(ks[0], (BATCH, SEQ, D_MODEL), f32),
        "mem": jax.random.normal(ks[1], (BATCH, MEM_LEN, D_MODEL), f32),
        "g_pre": 1.0 + 0.05 * jax.random.normal(ks[2], (DEPTH, 3, D_MODEL), f32),
        "g_post": 1.0 + 0.05 * jax.random.normal(ks[3], (DEPTH, 3, D_MODEL), f32),
        "g_mem": 1.0 + 0.05 * jax.random.normal(ks[4], (DEPTH, D_MODEL), f32),
        "ffn1_gate": w(ks[5], (DEPTH, D_MODEL, D_FF), D_MODEL),
        "ffn1_up": w(ks[6], (DEPTH, D_MODEL, D_FF), D_MODEL),
        "ffn1_down": w(ks[7], (DEPTH, D_FF, D_MODEL), D_FF),
        "ffn2_gate": w(ks[8], (DEPTH, D_MODEL, D_FF), D_MODEL),
        "ffn2_up": w(ks[9], (DEPTH, D_MODEL, D_FF), D_MODEL),
        "ffn2_down": w(ks[10], (DEPTH, D_FF, D_MODEL), D_FF),
        "w_in_pool": w(ks[11], (N_A, D_MODEL, D_TOK + D_MEMH), D_MODEL),
        "pool_w": w(ks[12], (N_A, N_POOL_GROUPS, POOL_GROUP, POOL_GROUP), POOL_GROUP),
        "pool_scale": 1.0 + 0.1 * jax.random.normal(ks[13], (N_A, D_TOK), f32),
        "w_in_sb": w(ks[14], (N_B, D_MODEL, 3 * D_TOK + D_MEMH), D_MODEL),
        "w_mem_kv": w(ks[15], (DEPTH, D_MODEL, 2 * D_MEMH), D_MODEL),
        "w_out": w(ks[16], (DEPTH, D_MIX, D_MODEL), D_MIX),
    }


def _fwd_reference(x, mem, g_pre, g_post, g_mem, ffn1_gate, ffn1_up, ffn1_down,
              ffn2_gate, ffn2_up, ffn2_down, w_in_pool, pool_w, pool_scale,
              w_in_sb, w_mem_kv, w_out):
    B, S, _ = x.shape
    h = x
    for i in range(DEPTH):
        f = swiglu(rmsnorm(h, g_pre[i, 0]), ffn1_gate[i], ffn1_up[i], ffn1_down[i])
        h = h + 0.5 * rmsnorm(f, g_post[i, 0])

        u = rmsnorm(h, g_pre[i, 1])
        mem_n = rmsnorm(mem, g_mem[i])
        j = i // N_MIXERS
        if i % N_MIXERS == 0:
            proj = u @ w_in_pool[j]
            tok = pool_mixer(proj[..., :D_TOK], pool_w[j], pool_scale[j])
            qm = proj[..., D_TOK:]
        else:
            proj = u @ w_in_sb[j]
            q = proj[..., :D_TOK].reshape(B, S, N_SB_HEADS, HEAD_DIM)
            k = proj[..., D_TOK:2 * D_TOK].reshape(B, S, N_SB_HEADS, HEAD_DIM)
            v = proj[..., 2 * D_TOK:3 * D_TOK].reshape(B, S, N_SB_HEADS, HEAD_DIM)
            tok = stick_breaking(q, k, v)
            qm = proj[..., 3 * D_TOK:]
        mo = mem_attention(qm, mem_n, w_mem_kv[i])
        mix = jnp.concatenate([tok, mo], axis=-1) @ w_out[i]
        h = h + rmsnorm(mix, g_post[i, 1])

        f = swiglu(rmsnorm(h, g_pre[i, 2]), ffn2_gate[i], ffn2_up[i], ffn2_down[i])
        h = h + 0.5 * rmsnorm(f, g_post[i, 2])
    return h


import jax as _jax
import jax.numpy as _jnp

TWIN_FORMAT = 'train_step'
FWD_PARAMS = ['x', 'mem', 'g_pre', 'g_post', 'g_mem', 'ffn1_gate', 'ffn1_up', 'ffn1_down', 'ffn2_gate', 'ffn2_up', 'ffn2_down', 'w_in_pool', 'pool_w', 'pool_scale', 'w_in_sb', 'w_mem_kv', 'w_out']
TWIN_WEIGHTS = ['g_pre', 'g_post', 'g_mem', 'ffn1_gate', 'ffn1_up', 'ffn1_down', 'ffn2_gate', 'ffn2_up', 'ffn2_down', 'w_in_pool', 'pool_w', 'pool_scale', 'w_in_sb', 'w_mem_kv', 'w_out']
TWIN_DIFF_INPUT = 'x'
TWIN_INPUTS = ['x', 'mem', 'g_pre', 'g_post', 'g_mem', 'ffn1_gate', 'ffn1_up', 'ffn1_down', 'ffn2_gate', 'ffn2_up', 'ffn2_down', 'w_in_pool', 'pool_w', 'pool_scale', 'w_in_sb', 'w_mem_kv', 'w_out', 'loss_target', 'm_g_pre', 'm_g_post', 'm_g_mem', 'm_ffn1_gate', 'm_ffn1_up', 'm_ffn1_down', 'm_ffn2_gate', 'm_ffn2_up', 'm_ffn2_down', 'm_w_in_pool', 'm_pool_w', 'm_pool_scale', 'm_w_in_sb', 'm_w_mem_kv', 'm_w_out', 'v_g_pre', 'v_g_post', 'v_g_mem', 'v_ffn1_gate', 'v_ffn1_up', 'v_ffn1_down', 'v_ffn2_gate', 'v_ffn2_up', 'v_ffn2_down', 'v_w_in_pool', 'v_pool_w', 'v_pool_scale', 'v_w_in_sb', 'v_w_mem_kv', 'v_w_out']
TWIN_OUTPUTS = ['loss', 'grad_x', 'grad_g_pre', 'grad_g_post', 'grad_g_mem', 'grad_ffn1_gate', 'grad_ffn1_up', 'grad_ffn1_down', 'grad_ffn2_gate', 'grad_ffn2_up', 'grad_ffn2_down', 'grad_w_in_pool', 'grad_pool_w', 'grad_pool_scale', 'grad_w_in_sb', 'grad_w_mem_kv', 'grad_w_out', 'delta_g_pre', 'delta_g_post', 'delta_g_mem', 'delta_ffn1_gate', 'delta_ffn1_up', 'delta_ffn1_down', 'delta_ffn2_gate', 'delta_ffn2_up', 'delta_ffn2_down', 'delta_w_in_pool', 'delta_pool_w', 'delta_pool_scale', 'delta_w_in_sb', 'delta_w_mem_kv', 'delta_w_out', 'new_m_g_pre', 'new_m_g_post', 'new_m_g_mem', 'new_m_ffn1_gate', 'new_m_ffn1_up', 'new_m_ffn1_down', 'new_m_ffn2_gate', 'new_m_ffn2_up', 'new_m_ffn2_down', 'new_m_w_in_pool', 'new_m_pool_w', 'new_m_pool_scale', 'new_m_w_in_sb', 'new_m_w_mem_kv', 'new_m_w_out', 'new_v_g_pre', 'new_v_g_post', 'new_v_g_mem', 'new_v_ffn1_gate', 'new_v_ffn1_up', 'new_v_ffn1_down', 'new_v_ffn2_gate', 'new_v_ffn2_up', 'new_v_ffn2_down', 'new_v_w_in_pool', 'new_v_pool_w', 'new_v_pool_scale', 'new_v_w_in_sb', 'new_v_w_mem_kv', 'new_v_w_out']
TWIN_LEAF_KINDS = {'loss': 'loss', 'grad_x': 'grad_x', 'grad_g_pre': 'grad_w', 'grad_g_post': 'grad_w', 'grad_g_mem': 'grad_w', 'grad_ffn1_gate': 'grad_w', 'grad_ffn1_up': 'grad_w', 'grad_ffn1_down': 'grad_w', 'grad_ffn2_gate': 'grad_w', 'grad_ffn2_up': 'grad_w', 'grad_ffn2_down': 'grad_w', 'grad_w_in_pool': 'grad_w', 'grad_pool_w': 'grad_w', 'grad_pool_scale': 'grad_w', 'grad_w_in_sb': 'grad_w', 'grad_w_mem_kv': 'grad_w', 'grad_w_out': 'grad_w', 'delta_g_pre': 'delta_w', 'delta_g_post': 'delta_w', 'delta_g_mem': 'delta_w', 'delta_ffn1_gate': 'delta_w', 'delta_ffn1_up': 'delta_w', 'delta_ffn1_down': 'delta_w', 'delta_ffn2_gate': 'delta_w', 'delta_ffn2_up': 'delta_w', 'delta_ffn2_down': 'delta_w', 'delta_w_in_pool': 'delta_w', 'delta_pool_w': 'delta_w', 'delta_pool_scale': 'delta_w', 'delta_w_in_sb': 'delta_w', 'delta_w_mem_kv': 'delta_w', 'delta_w_out': 'delta_w', 'new_m_g_pre': 'new_m', 'new_m_g_post': 'new_m', 'new_m_g_mem': 'new_m', 'new_m_ffn1_gate': 'new_m', 'new_m_ffn1_up': 'new_m', 'new_m_ffn1_down': 'new_m', 'new_m_ffn2_gate': 'new_m', 'new_m_ffn2_up': 'new_m', 'new_m_ffn2_down': 'new_m', 'new_m_w_in_pool': 'new_m', 'new_m_pool_w': 'new_m', 'new_m_pool_scale': 'new_m', 'new_m_w_in_sb': 'new_m', 'new_m_w_mem_kv': 'new_m', 'new_m_w_out': 'new_m', 'new_v_g_pre': 'new_v', 'new_v_g_post': 'new_v', 'new_v_g_mem': 'new_v', 'new_v_ffn1_gate': 'new_v', 'new_v_ffn1_up': 'new_v', 'new_v_ffn1_down': 'new_v', 'new_v_ffn2_gate': 'new_v', 'new_v_ffn2_up': 'new_v', 'new_v_ffn2_down': 'new_v', 'new_v_w_in_pool': 'new_v', 'new_v_pool_w': 'new_v', 'new_v_pool_scale': 'new_v', 'new_v_w_in_sb': 'new_v', 'new_v_w_mem_kv': 'new_v', 'new_v_w_out': 'new_v'}


def _forward(args):
    return _fwd_reference(*[args[k] for k in FWD_PARAMS])


def _output_shape():
    out = _jax.eval_shape(lambda: _forward(_fwd_setup_inputs(0)))
    return out.shape, out.dtype

N_MICROBATCH = 1
ADAM_LR = 0.001
ADAM_B1 = 0.9
ADAM_B2 = 0.999
ADAM_EPS = 1e-08
ADAM_WD = 0.01
ADAM_STEP = 10
PER_EXAMPLE_BATCH_AXIS = {'x': 0, 'mem': 0, 'loss_target': 0}
SHARED_INPUTS = []
_WEIGHT_DTYPES = {'g_pre': _jnp.float32, 'g_post': _jnp.float32, 'g_mem': _jnp.float32, 'ffn1_gate': _jnp.float32, 'ffn1_up': _jnp.float32, 'ffn1_down': _jnp.float32, 'ffn2_gate': _jnp.float32, 'ffn2_up': _jnp.float32, 'ffn2_down': _jnp.float32, 'w_in_pool': _jnp.float32, 'pool_w': _jnp.float32, 'pool_scale': _jnp.float32, 'w_in_sb': _jnp.float32, 'w_mem_kv': _jnp.float32, 'w_out': _jnp.float32}
MOMENT_SCALE = {'g_pre': 1.457206e+00, 'g_post': 3.880857e+01, 'g_mem': 2.339870e-01, 'ffn1_gate': 5.332818e-01, 'ffn1_up': 5.999425e-01, 'ffn1_down': 8.638460e-01, 'ffn2_gate': 3.964236e-01, 'ffn2_up': 5.199572e-01, 'ffn2_down': 7.435881e-01, 'w_in_pool': 2.809686e+00, 'pool_w': 3.749439e+00, 'pool_scale': 4.300468e+00, 'w_in_sb': 1.160600e+00, 'w_mem_kv': 3.117768e-01, 'w_out': 2.324270e+00}


def _to_microbatches(a, axis):
    t = _jnp.moveaxis(a, axis, 0)
    t = t.reshape((N_MICROBATCH, t.shape[0] // N_MICROBATCH) + t.shape[1:])
    return _jnp.moveaxis(t, 1, axis + 1)


def setup_inputs(seed: int = 0) -> dict:
    inp = _fwd_setup_inputs(seed)
    key = _jax.random.fold_in(_jax.random.key(seed), 7919)
    shape, _ = _output_shape()
    out = dict(inp)
    out["loss_target"] = _jax.random.normal(_jax.random.fold_in(key, 0), shape, _jnp.float32)
    for i, name in enumerate(TWIN_WEIGHTS):
        w = inp[name].astype(_jnp.float32)
        if MOMENT_SCALE is None:
            s = _jnp.sqrt(_jnp.mean(_jnp.square(w)) + 1e-30)
        else:
            s = MOMENT_SCALE[name]
        km, kv = _jax.random.split(_jax.random.fold_in(key, i + 1))
        out[name] = w
        out["m_" + name] = s * _jax.random.normal(km, w.shape, _jnp.float32)
        out["v_" + name] = (s * s) * _jax.random.uniform(kv, w.shape, _jnp.float32, 0.5, 1.5)
    if N_MICROBATCH > 1:
        for name, axis in PER_EXAMPLE_BATCH_AXIS.items():
            out[name] = _to_microbatches(out[name], axis)
    return {'x': out['x'], 'mem': out['mem'], 'g_pre': out['g_pre'], 'g_post': out['g_post'], 'g_mem': out['g_mem'], 'ffn1_gate': out['ffn1_gate'], 'ffn1_up': out['ffn1_up'], 'ffn1_down': out['ffn1_down'], 'ffn2_gate': out['ffn2_gate'], 'ffn2_up': out['ffn2_up'], 'ffn2_down': out['ffn2_down'], 'w_in_pool': out['w_in_pool'], 'pool_w': out['pool_w'], 'pool_scale': out['pool_scale'], 'w_in_sb': out['w_in_sb'], 'w_mem_kv': out['w_mem_kv'], 'w_out': out['w_out'], 'loss_target': out['loss_target'], 'm_g_pre': out['m_g_pre'], 'm_g_post': out['m_g_post'], 'm_g_mem': out['m_g_mem'], 'm_ffn1_gate': out['m_ffn1_gate'], 'm_ffn1_up': out['m_ffn1_up'], 'm_ffn1_down': out['m_ffn1_down'], 'm_ffn2_gate': out['m_ffn2_gate'], 'm_ffn2_up': out['m_ffn2_up'], 'm_ffn2_down': out['m_ffn2_down'], 'm_w_in_pool': out['m_w_in_pool'], 'm_pool_w': out['m_pool_w'], 'm_pool_scale': out['m_pool_scale'], 'm_w_in_sb': out['m_w_in_sb'], 'm_w_mem_kv': out['m_w_mem_kv'], 'm_w_out': out['m_w_out'], 'v_g_pre': out['v_g_pre'], 'v_g_post': out['v_g_post'], 'v_g_mem': out['v_g_mem'], 'v_ffn1_gate': out['v_ffn1_gate'], 'v_ffn1_up': out['v_ffn1_up'], 'v_ffn1_down': out['v_ffn1_down'], 'v_ffn2_gate': out['v_ffn2_gate'], 'v_ffn2_up': out['v_ffn2_up'], 'v_ffn2_down': out['v_ffn2_down'], 'v_w_in_pool': out['v_w_in_pool'], 'v_pool_w': out['v_pool_w'], 'v_pool_scale': out['v_pool_scale'], 'v_w_in_sb': out['v_w_in_sb'], 'v_w_mem_kv': out['v_w_mem_kv'], 'v_w_out': out['v_w_out']}


def _loss(weights, diff, rest, loss_target):
    with _jax.named_scope("forward"):
        args = {**rest, TWIN_DIFF_INPUT: diff, **{k: w.astype(_WEIGHT_DTYPES[k]) for k, w in weights.items()}}
        y = _forward(args)
    with _jax.named_scope("loss_head"):
        err = _jnp.square(y.astype(_jnp.float32) - loss_target)
        return 0.5 * _jnp.sum(_jnp.mean(err, axis=-1)) if err.ndim else 0.5 * err


def _adamw(w, g, m, v):
    m = ADAM_B1 * m + (1.0 - ADAM_B1) * g
    v = ADAM_B2 * v + (1.0 - ADAM_B2) * _jnp.square(g)
    m_hat = m / (1.0 - ADAM_B1 ** ADAM_STEP)
    v_hat = v / (1.0 - ADAM_B2 ** ADAM_STEP)
    delta = -ADAM_LR * (m_hat / (_jnp.sqrt(v_hat) + ADAM_EPS) + ADAM_WD * w)
    return delta, m, v


def reference(x, mem, g_pre, g_post, g_mem, ffn1_gate, ffn1_up, ffn1_down, ffn2_gate, ffn2_up, ffn2_down, w_in_pool, pool_w, pool_scale, w_in_sb, w_mem_kv, w_out, loss_target, m_g_pre, m_g_post, m_g_mem, m_ffn1_gate, m_ffn1_up, m_ffn1_down, m_ffn2_gate, m_ffn2_up, m_ffn2_down, m_w_in_pool, m_pool_w, m_pool_scale, m_w_in_sb, m_w_mem_kv, m_w_out, v_g_pre, v_g_post, v_g_mem, v_ffn1_gate, v_ffn1_up, v_ffn1_down, v_ffn2_gate, v_ffn2_up, v_ffn2_down, v_w_in_pool, v_pool_w, v_pool_scale, v_w_in_sb, v_w_mem_kv, v_w_out):
    given = dict(x=x, mem=mem, g_pre=g_pre, g_post=g_post, g_mem=g_mem, ffn1_gate=ffn1_gate, ffn1_up=ffn1_up, ffn1_down=ffn1_down, ffn2_gate=ffn2_gate, ffn2_up=ffn2_up, ffn2_down=ffn2_down, w_in_pool=w_in_pool, pool_w=pool_w, pool_scale=pool_scale, w_in_sb=w_in_sb, w_mem_kv=w_mem_kv, w_out=w_out, loss_target=loss_target, m_g_pre=m_g_pre, m_g_post=m_g_post, m_g_mem=m_g_mem, m_ffn1_gate=m_ffn1_gate, m_ffn1_up=m_ffn1_up, m_ffn1_down=m_ffn1_down, m_ffn2_gate=m_ffn2_gate, m_ffn2_up=m_ffn2_up, m_ffn2_down=m_ffn2_down, m_w_in_pool=m_w_in_pool, m_pool_w=m_pool_w, m_pool_scale=m_pool_scale, m_w_in_sb=m_w_in_sb, m_w_mem_kv=m_w_mem_kv, m_w_out=m_w_out, v_g_pre=v_g_pre, v_g_post=v_g_post, v_g_mem=v_g_mem, v_ffn1_gate=v_ffn1_gate, v_ffn1_up=v_ffn1_up, v_ffn1_down=v_ffn1_down, v_ffn2_gate=v_ffn2_gate, v_ffn2_up=v_ffn2_up, v_ffn2_down=v_ffn2_down, v_w_in_pool=v_w_in_pool, v_pool_w=v_pool_w, v_pool_scale=v_pool_scale, v_w_in_sb=v_w_in_sb, v_w_mem_kv=v_w_mem_kv, v_w_out=v_w_out)
    weights = {n: given[n] for n in TWIN_WEIGHTS}
    shared = {n: given[n] for n in SHARED_INPUTS}
    per_example = {n: given[n] for n in ['x', 'mem']}
    grad_fn = _jax.value_and_grad(_loss, argnums=(0, 1))

    def one_microbatch(ex, loss_target):
        ex = dict(ex)
        diff = ex.pop(TWIN_DIFF_INPUT)
        return grad_fn(weights, diff, {**shared, **ex}, loss_target)

    if N_MICROBATCH == 1:
        loss, (grad_w, grad_x) = one_microbatch(per_example, given["loss_target"])
    else:
        def body(carry, xs):
            loss_sum, grad_sum = carry
            l_k, (gw_k, gx_k) = one_microbatch(xs[0], xs[1])
            with _jax.named_scope("update"):
                return (loss_sum + l_k, _jax.tree.map(_jnp.add, grad_sum, gw_k)), gx_k

        init = (_jnp.zeros((), _jnp.float32), _jax.tree.map(_jnp.zeros_like, weights))
        (loss, grad_w), grad_x = _jax.lax.scan(body, init, (per_example, given["loss_target"]))
    with _jax.named_scope("update"):
        delta_w, new_m, new_v = {}, {}, {}
        for n in TWIN_WEIGHTS:
            delta_w[n], new_m[n], new_v[n] = _adamw(weights[n], grad_w[n], given["m_" + n], given["v_" + n])
    return (loss, grad_x, *[grad_w[n] for n in TWIN_WEIGHTS], *[delta_w[n] for n in TWIN_WEIGHTS],
            *[new_m[n] for n in TWIN_WEIGHTS], *[new_v[n] for n in TWIN_WEIGHTS])
```

```python
import functools

import jax
import jax.numpy as jnp
from jax import lax
from jax.experimental import pallas as pl
from jax.experimental.pallas import tpu as pltpu

F32 = jnp.float32
BF16 = jnp.bfloat16

N_DEV = 8
DEPTH = 4
D_MODEL = 1024
D_FF = 2048
D_TOK = 512
D_MEMH = 256
D_MIX = D_TOK + D_MEMH
D_SB = 3 * D_TOK + D_MEMH
HEAD_DIM = 64
Q_BLOCK = 128
POOL_WINDOWS = (2, 4, 8, 16)
POOL_GROUP = 128
POOL_HALO = 16
EPS = 1e-6
ATT_SCALE = HEAD_DIM ** -0.5

ADAM_LR = 0.001
ADAM_B1 = 0.9
ADAM_B2 = 0.999
ADAM_EPS = 1e-08
ADAM_WD = 0.01
ADAM_STEP = 10

VMEM_LIMIT_BYTES = 56 * 1024 * 1024
ROW_TILE = 256
WGRAD_TILE = 512
ADAM_TILE_ELEMS = 128 * 1024

MESH = pl.DeviceIdType.MESH
ANY = pl.BlockSpec(memory_space=pl.ANY)


def _tile(n, pref):
    t = 1 << (pref.bit_length() - 1)
    while n % t:
        t //= 2
    return t


def _dot(a, b):
    return jnp.dot(a, b, preferred_element_type=F32)


def _dot_nt(a, b):
    return lax.dot_general(a, b, (((1,), (1,)), ((), ())), preferred_element_type=F32)


def _dot_tn(a, b):
    return lax.dot_general(a, b, (((0,), (0,)), ((), ())), preferred_element_type=F32)


def _split_dot(x, m, terms):
    out = None
    rest = x
    for _ in range(terms):
        part = rest.astype(BF16)
        rest = rest - part.astype(F32)
        d = _dot(part, m)
        out = d if out is None else out + d
    return out


def _rms(x, g):
    r = lax.rsqrt(jnp.mean(x * x, axis=-1, keepdims=True) + EPS)
    return x * r * g


def _rms_bwd(x, g, dy):
    r = lax.rsqrt(jnp.mean(x * x, axis=-1, keepdims=True) + EPS)
    xh = x * r
    gdy = g * dy
    dx = r * (gdy - xh * jnp.mean(gdy * xh, axis=-1, keepdims=True))
    return dx, jnp.sum(dy * xh, axis=0, keepdims=True)


def _acc_rows(ref, val, first):
    @pl.when(first)
    def _():
        ref[...] = val

    @pl.when(jnp.logical_not(first))
    def _():
        ref[...] += val


def _params(sem=None):
    return pltpu.CompilerParams(dimension_semantics=sem, vmem_limit_bytes=VMEM_LIMIT_BYTES)


def _sds(shape, dtype):
    return jax.ShapeDtypeStruct(shape, dtype)


def _rows(tm, width, col=0):
    return pl.BlockSpec((tm, width), lambda i: (i, col))


def _full(shape):
    nd = len(shape)
    return pl.BlockSpec(shape, lambda *_: (0,) * nd)


def _layer(shape, l):
    return pl.BlockSpec((N_DEV, None) + tuple(shape), lambda *_: (0, l, 0, 0))


def _peers():
    x, y, c = lax.axis_index("x"), lax.axis_index("y"), lax.axis_index("c")
    peers = []
    for k in range(1, N_DEV):
        px = 1 - x if k & 4 else x
        py = 1 - y if k & 2 else y
        pc = 1 - c if k & 1 else c
        peers.append(((px, py, pc), 4 * px + 2 * py + pc))
    return 4 * x + 2 * y + c, peers


def _all_gather(shards):
    n = len(shards)

    def body(*refs):
        ins, outs = refs[:n], refs[n:2 * n]
        send_sems, recv_sems, local_sems = refs[2 * n:]
        me, peers = _peers()
        local = [pltpu.make_async_copy(ins[a], outs[a].at[me], local_sems.at[a]) for a in range(n)]
        for cp in local:
            cp.start()
        remote = []
        for k, (dev, _) in enumerate(peers):
            for a in range(n):
                cp = pltpu.make_async_remote_copy(
                    src_ref=ins[a], dst_ref=outs[a].at[me], send_sem=send_sems.at[a, k], recv_sem=recv_sems.at[a, k],
                    device_id=dev, device_id_type=MESH)
                cp.start()
                remote.append(cp)
        for cp in remote:
            cp.wait()
        for cp in local:
            cp.wait()

    return pl.pallas_call(
        body, name="all_gather_weights",
        in_specs=[ANY] * n, out_specs=[ANY] * n,
        out_shape=[_sds((N_DEV,) + s.shape, s.dtype) for s in shards],
        scratch_shapes=[pltpu.SemaphoreType.DMA((n, N_DEV - 1)), pltpu.SemaphoreType.DMA((n, N_DEV - 1)),
                        pltpu.SemaphoreType.DMA((n,))],
        compiler_params=pltpu.CompilerParams(has_side_effects=True),
        interpret=False,
    )(*shards)


def _exchange(blocked, whole):
    nb, n = len(blocked), len(blocked) + len(whole)
    arrays = list(blocked) + list(whole)

    def body(*refs):
        ins, outs = refs[:n], refs[n:2 * n]
        send_sems, recv_sems, local_sems = refs[2 * n:]
        me, peers = _peers()

        def src(a, to):
            return ins[a].at[to] if a < nb else ins[a]

        local = [pltpu.make_async_copy(src(a, me), outs[a].at[me], local_sems.at[a]) for a in range(n)]
        for cp in local:
            cp.start()
        remote = []
        for k, (dev, idx) in enumerate(peers):
            for a in range(n):
                cp = pltpu.make_async_remote_copy(
                    src_ref=src(a, idx), dst_ref=outs[a].at[me], send_sem=send_sems.at[a, k], recv_sem=recv_sems.at[a, k],
                    device_id=dev, device_id_type=MESH)
                cp.start()
                remote.append(cp)
        for cp in remote:
            cp.wait()
        for cp in local:
            cp.wait()

    return pl.pallas_call(
        body, name="exchange_gradients",
        in_specs=[ANY] * n, out_specs=[ANY] * n,
        out_shape=[_sds(a.shape, a.dtype) for a in blocked] + [_sds((N_DEV,) + a.shape, a.dtype) for a in whole],
        scratch_shapes=[pltpu.SemaphoreType.DMA((n, N_DEV - 1)), pltpu.SemaphoreType.DMA((n, N_DEV - 1)),
                        pltpu.SemaphoreType.DMA((n,))],
        compiler_params=pltpu.CompilerParams(has_side_effects=True),
        interpret=False,
    )(*arrays)


def _ffn_fwd(h, gpre, gpost, wg, wu, wd, l, name):
    S = h.shape[0]
    tm = _tile(S, ROW_TILE)
    nb = D_FF // N_DEV

    def body(h_ref, gpre_ref, gpost_ref, wg_ref, wu_ref, wd_ref, hn_ref, n_ref, gate_ref, up_ref, act_ref, f_ref):
        hv = h_ref[...]
        n = _rms(hv, gpre_ref[...]).astype(BF16)
        n_ref[...] = n
        f = jnp.zeros((tm, D_MODEL), F32)
        for d in range(N_DEV):
            cols = slice(d * nb, (d + 1) * nb)
            g = _dot(n, wg_ref[d])
            u = _dot(n, wu_ref[d])
            a = (g * jax.nn.sigmoid(g) * u).astype(BF16)
            gate_ref[:, cols] = g.astype(BF16)
            up_ref[:, cols] = u.astype(BF16)
            act_ref[:, cols] = a
            f = f + _dot(a, wd_ref[d])
        f_ref[...] = f
        hn_ref[...] = hv + 0.5 * _rms(f, gpost_ref[...])

    return pl.pallas_call(
        body, name=name, grid=(S // tm,),
        in_specs=[_rows(tm, D_MODEL), _full((1, D_MODEL)), _full((1, D_MODEL)),
                  _layer((D_MODEL, nb), l), _layer((D_MODEL, nb), l), _layer((nb, D_MODEL), l)],
        out_specs=[_rows(tm, D_MODEL), _rows(tm, D_MODEL), _rows(tm, D_FF), _rows(tm, D_FF), _rows(tm, D_FF),
                   _rows(tm, D_MODEL)],
        out_shape=[_sds((S, D_MODEL), F32), _sds((S, D_MODEL), BF16), _sds((S, D_FF), BF16), _sds((S, D_FF), BF16),
                   _sds((S, D_FF), BF16), _sds((S, D_MODEL), F32)],
        compiler_params=_params(("arbitrary",)),
        interpret=False,
    )(h, gpre, gpost, wg, wu, wd)


def _ffn_bwd(dho, h, f, gate, up, gpre, gpost, wg, wu, wd, l, name):
    S = h.shape[0]
    tm = _tile(S, ROW_TILE)
    nb = D_FF // N_DEV

    def body(dho_ref, h_ref, f_ref, gate_ref, up_ref, gpre_ref, gpost_ref, wg_ref, wu_ref, wd_ref,
             dh_ref, df_ref, dgate_ref, dup_ref, dgpre_ref, dgpost_ref):
        first = pl.program_id(0) == 0
        dho_v = dho_ref[...]
        dfx, dgpost = _rms_bwd(f_ref[...], gpost_ref[...], 0.5 * dho_v)
        dfb = dfx.astype(BF16)
        df_ref[...] = dfb
        dn = jnp.zeros((tm, D_MODEL), F32)
        for d in range(N_DEV):
            cols = slice(d * nb, (d + 1) * nb)
            dact = _dot_nt(dfb, wd_ref[d])
            g = gate_ref[:, cols].astype(F32)
            u = up_ref[:, cols].astype(F32)
            s = jax.nn.sigmoid(g)
            dg = (dact * u * (s * (1.0 + g * (1.0 - s)))).astype(BF16)
            du = (dact * (g * s)).astype(BF16)
            dgate_ref[:, cols] = dg
            dup_ref[:, cols] = du
            dn = dn + _dot_nt(dg, wg_ref[d]) + _dot_nt(du, wu_ref[d])
        dhx, dgpre = _rms_bwd(h_ref[...], gpre_ref[...], dn)
        dh_ref[...] = dho_v + dhx
        _acc_rows(dgpre_ref, dgpre, first)
        _acc_rows(dgpost_ref, dgpost, first)

    return pl.pallas_call(
        body, name=name, grid=(S // tm,),
        in_specs=[_rows(tm, D_MODEL), _rows(tm, D_MODEL), _rows(tm, D_MODEL), _rows(tm, D_FF), _rows(tm, D_FF),
                  _full((1, D_MODEL)), _full((1, D_MODEL)),
                  _layer((D_MODEL, nb), l), _layer((D_MODEL, nb), l), _layer((nb, D_MODEL), l)],
        out_specs=[_rows(tm, D_MODEL), _rows(tm, D_MODEL), _rows(tm, D_FF), _rows(tm, D_FF),
                   _full((1, D_MODEL)), _full((1, D_MODEL))],
        out_shape=[_sds((S, D_MODEL), F32), _sds((S, D_MODEL), BF16), _sds((S, D_FF), BF16), _sds((S, D_FF), BF16),
                   _sds((1, D_MODEL), F32), _sds((1, D_MODEL), F32)],
        compiler_params=_params(("arbitrary",)),
        interpret=False,
    )(dho, h, f, gate, up, gpre, gpost, wg, wu, wd)


def _wgrad(a_parts, b_parts, split, name, into=None, l=0, n_layers=1):
    S = a_parts[0].shape[0]
    bk = _tile(S, WGRAD_TILE)
    ms = [a.shape[1] for a in a_parts]
    ns = [b.shape[1] for b in b_parts]
    M, N = sum(ms), sum(ns)
    na, nbp = len(a_parts), len(b_parts)
    blk = (M // N_DEV, N) if split == "rows" else (M, N // N_DEV)
    steps = S // bk

    def body(*refs):
        a_refs, b_refs = refs[:na], refs[na:na + nbp]
        out_ref, acc_ref = refs[-2], refs[-1]
        k = pl.program_id(0)

        @pl.when(k == 0)
        def _():
            acc_ref[...] = jnp.zeros_like(acc_ref)

        r0 = 0
        for ai in range(na):
            av = a_refs[ai][...]
            c0 = 0
            for bi in range(nbp):
                acc_ref[r0:r0 + ms[ai], c0:c0 + ns[bi]] += _dot_tn(av, b_refs[bi][...])
                c0 += ns[bi]
            r0 += ms[ai]

        @pl.when(k == steps - 1)
        def _():
            for d in range(N_DEV):
                if split == "rows":
                    out_ref[d] = acc_ref[d * blk[0]:(d + 1) * blk[0], :]
                else:
                    out_ref[d] = acc_ref[:, d * blk[1]:(d + 1) * blk[1]]

    in_specs = [pl.BlockSpec((bk, m), lambda k: (k, 0)) for m in ms] + [pl.BlockSpec((bk, n), lambda k: (k, 0)) for n in ns]
    args = list(a_parts) + list(b_parts)
    aliases = {}
    if into is not None:
        in_specs.append(ANY)
        args.append(into)
        aliases = {len(args) - 1: 0}

    def body_with_alias(*refs):
        body(*refs[:na + nbp], *refs[-2:])

    return pl.pallas_call(
        body_with_alias if into is not None else body, name=name, grid=(steps,),
        in_specs=in_specs,
        out_specs=pl.BlockSpec((N_DEV, None) + blk, lambda k: (0, l, 0, 0)),
        out_shape=_sds((N_DEV, n_layers) + blk, F32),
        scratch_shapes=[pltpu.VMEM((M, N), F32)],
        input_output_aliases=aliases,
        compiler_params=_params(("arbitrary",)),
        interpret=False,
    )(*args)


def _mix_in_pool(h, g1, w_in, pool_w, pool_scale, l, name):
    S = h.shape[0]
    tm = _tile(S, ROW_TILE)
    kb = D_MODEL // N_DEV

    def body(h_ref, g_ref, w_ref, pw_ref, ps_ref, u_ref, dpre_ref, tok_ref, qm_ref, ext_ref):
        i = pl.program_id(0)
        u = _rms(h_ref[...], g_ref[...]).astype(BF16)
        u_ref[...] = u
        proj = jnp.zeros((tm, D_MIX), F32)
        for d in range(N_DEV):
            proj = proj + _dot(u[:, d * kb:(d + 1) * kb], w_ref[d])
        qm_ref[...] = proj[:, D_TOK:].astype(BF16)
        x = proj[:, :D_TOK]

        @pl.when(i == 0)
        def _():
            ext_ref[0:POOL_HALO, :] = jnp.zeros((POOL_HALO, D_TOK), F32)

        ext_ref[POOL_HALO:, :] = x
        pos = i * tm + lax.broadcasted_iota(jnp.int32, (tm, 1), 0)
        for gi, w in enumerate(POOL_WINDOWS):
            cols = slice(gi * POOL_GROUP, (gi + 1) * POOL_GROUP)
            xs = x[:, cols]
            wsum = xs
            for k in range(1, w):
                wsum = wsum + ext_ref[POOL_HALO - k:POOL_HALO - k + tm, cols]
            cnt = jnp.minimum(pos + 1, w).astype(F32)
            dg = (wsum / cnt - xs).astype(BF16)
            dpre_ref[:, cols] = dg
            yv = _dot(dg, pw_ref[gi].astype(BF16))
            tok_ref[:, cols] = (yv * ps_ref[:, cols]).astype(BF16)
        ext_ref[0:POOL_HALO, :] = x[tm - POOL_HALO:, :]

    return pl.pallas_call(
        body, name=name, grid=(S // tm,),
        in_specs=[_rows(tm, D_MODEL), _full((1, D_MODEL)), _layer((kb, D_MIX), l),
                  _full((len(POOL_WINDOWS), POOL_GROUP, POOL_GROUP)), _full((1, D_TOK))],
        out_specs=[_rows(tm, D_MODEL), _rows(tm, D_TOK), _rows(tm, D_TOK), _rows(tm, D_MEMH)],
        out_shape=[_sds((S, D_MODEL), BF16), _sds((S, D_TOK), BF16), _sds((S, D_TOK), BF16), _sds((S, D_MEMH), BF16)],
        scratch_shapes=[pltpu.VMEM((POOL_HALO + tm, D_TOK), F32)],
        compiler_params=_params(("arbitrary",)),
        interpret=False,
    )(h, g1, w_in, pool_w, pool_scale)


def _pool_bwd(dtok, dpre, pool_w, pool_scale, name):
    S = dtok.shape[0]
    tm = _tile(S, ROW_TILE)
    nt = S // tm
    ng = len(POOL_WINDOWS)

    def body(dtok_ref, dpre_ref, pw_ref, ps_ref, dx_ref, dpw_ref, dps_ref, ext_ref):
        i = pl.program_id(0)
        first = i == 0
        t0 = (nt - 1 - i) * tm
        pos = t0 + lax.broadcasted_iota(jnp.int32, (tm, 1), 0)

        @pl.when(first)
        def _():
            ext_ref[tm:, :] = jnp.zeros((POOL_HALO, D_TOK), F32)

        dps = []
        for gi, w in enumerate(POOL_WINDOWS):
            cols = slice(gi * POOL_GROUP, (gi + 1) * POOL_GROUP)
            dg = dpre_ref[:, cols]
            pw = pw_ref[gi].astype(BF16)
            dt = dtok_ref[:, cols].astype(F32)
            yv = _dot(dg, pw)
            dps.append(jnp.sum(dt * yv, axis=0, keepdims=True))
            dy = (dt * ps_ref[:, cols]).astype(BF16)
            _acc_rows(dpw_ref.at[gi], _dot_tn(dg, dy), first)
            dd = _dot_nt(dy, pw)
            cnt = jnp.minimum(pos + 1, w).astype(F32)
            ext_ref[0:tm, cols] = dd / cnt
            wsum = ext_ref[0:tm, cols]
            for k in range(1, w):
                wsum = wsum + ext_ref[k:k + tm, cols]
            dx_ref[:, cols] = (wsum - dd).astype(BF16)
        _acc_rows(dps_ref, jnp.concatenate(dps, axis=1), first)
        ext_ref[tm:, :] = ext_ref[0:POOL_HALO, :]

    rev = lambda i: (nt - 1 - i, 0)
    return pl.pallas_call(
        body, name=name, grid=(nt,),
        in_specs=[pl.BlockSpec((tm, D_TOK), rev), pl.BlockSpec((tm, D_TOK), rev),
                  _full((ng, POOL_GROUP, POOL_GROUP)), _full((1, D_TOK))],
        out_specs=[pl.BlockSpec((tm, D_TOK), rev), _full((ng, POOL_GROUP, POOL_GROUP)), _full((1, D_TOK))],
        out_shape=[_sds((S, D_TOK), BF16), _sds((ng, POOL_GROUP, POOL_GROUP), F32), _sds((1, D_TOK), F32)],
        scratch_shapes=[pltpu.VMEM((tm + POOL_HALO, D_TOK), F32)],
        compiler_params=_params(("arbitrary",)),
        interpret=False,
    )(dtok, dpre, pool_w, pool_scale)


def _mix_in_sb(h, g1, wt, name):
    S = h.shape[0]
    tm = _tile(S, ROW_TILE)
    cb = 256

    def body(h_ref, g_ref, wt_ref, u_ref, proj_ref):
        u = _rms(h_ref[...], g_ref[...]).astype(BF16)
        u_ref[...] = u
        for c in range(D_SB // cb):
            proj_ref[:, c * cb:(c + 1) * cb] = _dot_nt(u, wt_ref[c * cb:(c + 1) * cb, :]).astype(BF16)

    return pl.pallas_call(
        body, name=name, grid=(S // tm,),
        in_specs=[_rows(tm, D_MODEL), _full((1, D_MODEL)), _full((D_SB, D_MODEL))],
        out_specs=[_rows(tm, D_MODEL), _rows(tm, D_SB)],
        out_shape=[_sds((S, D_MODEL), BF16), _sds((S, D_SB), BF16)],
        compiler_params=_params(("arbitrary",)),
        interpret=False,
    )(h, g1, wt)


def _mix_in_bwd(dho, h, g1, parts, w, mode, l, name):
    S = h.shape[0]
    tm = _tile(S, ROW_TILE)
    widths = [p.shape[1] for p in parts]
    npart = len(parts)
    kb = D_MODEL // N_DEV

    def body(*refs):
        dho_ref, h_ref, g_ref = refs[:3]
        p_refs = refs[3:3 + npart]
        w_ref, dh_ref, dg_ref = refs[3 + npart:]
        first = pl.program_id(0) == 0
        if mode == "pool":
            dproj = jnp.concatenate([p[...] for p in p_refs], axis=1)
            du = jnp.concatenate([_dot_nt(dproj, w_ref[d]) for d in range(N_DEV)], axis=1)
        else:
            du = jnp.zeros((tm, D_MODEL), F32)
            r0 = 0
            for p, wd_ in zip(p_refs, widths):
                du = du + _dot(p[...], w_ref[r0:r0 + wd_, :])
                r0 += wd_
        dhx, dg = _rms_bwd(h_ref[...], g_ref[...], du)
        dh_ref[...] = dho_ref[...] + dhx
        _acc_rows(dg_ref, dg, first)

    w_spec = _layer((kb, D_MIX), l) if mode == "pool" else _full((D_SB, D_MODEL))
    return pl.pallas_call(
        body, name=name, grid=(S // tm,),
        in_specs=[_rows(tm, D_MODEL), _rows(tm, D_MODEL), _full((1, D_MODEL))] + [_rows(tm, wd_) for wd_ in widths] + [w_spec],
        out_specs=[_rows(tm, D_MODEL), _full((1, D_MODEL))],
        out_shape=[_sds((S, D_MODEL), F32), _sds((1, D_MODEL), F32)],
        compiler_params=_params(("arbitrary",)),
        interpret=False,
    )(dho, h, g1, *parts, w)


def _sb_block(qe, kblk, diag, later, tri_later):
    row = lax.broadcasted_iota(jnp.int32, (Q_BLOCK, Q_BLOCK), 0)
    col = lax.broadcasted_iota(jnp.int32, (Q_BLOCK, Q_BLOCK), 1)
    z = _dot_nt(qe, kblk) * ATT_SCALE
    mask = jnp.logical_or(col < row, jnp.logical_not(diag))
    en = jnp.exp(-jnp.abs(z))
    ls = jnp.minimum(z, 0.0) - jnp.log(1.0 + en)
    lf = jnp.where(mask, ls - z, 0.0)
    within = _split_dot(lf, tri_later, 3)
    a = jnp.where(mask, jnp.exp(ls + within + later), 0.0)
    return z, mask, en, a, jnp.sum(lf, axis=1, keepdims=True)


def _tri(strict):
    row = lax.broadcasted_iota(jnp.int32, (Q_BLOCK, Q_BLOCK), 0)
    col = lax.broadcasted_iota(jnp.int32, (Q_BLOCK, Q_BLOCK), 1)
    return (row > col if strict else row >= col).astype(BF16)


def _sb_fwd(proj, name):
    S = proj.shape[0]
    nq = S // Q_BLOCK
    npair = D_TOK // 128

    def body(q_ref, k_ref, v_ref, o_ref, tok_ref):
        qi = pl.program_id(1)
        lane = lax.broadcasted_iota(jnp.int32, (Q_BLOCK, 128), 1)
        tri_later = _tri(True)
        q = q_ref[...]
        total = jnp.zeros((Q_BLOCK, 128), F32)
        for e in range(2):
            hm = (lane >= e * HEAD_DIM) & (lane < (e + 1) * HEAD_DIM)
            qe = jnp.where(hm, q, jnp.zeros_like(q))

            def step(j, carry, qe=qe, hm=hm):
                acc, later = carry
                off = pl.multiple_of((qi - j) * Q_BLOCK, Q_BLOCK)
                kblk = k_ref[pl.ds(off, Q_BLOCK), :]
                vblk = v_ref[pl.ds(off, Q_BLOCK), :]
                _, _, _, a, bsum = _sb_block(qe, kblk, j == 0, later, tri_later)
                ve = jnp.where(hm, vblk, jnp.zeros_like(vblk))
                return acc + _split_dot(a, ve, 2), later + bsum

            acc, _ = lax.fori_loop(0, qi + 1, step, (jnp.zeros((Q_BLOCK, 128), F32), jnp.zeros((Q_BLOCK, 1), F32)))
            total = total + acc
        o_ref[...] = total
        tok_ref[...] = total.astype(BF16)

    return pl.pallas_call(
        body, name=name, grid=(npair, nq),
        in_specs=[pl.BlockSpec((Q_BLOCK, 128), lambda p, i: (i, p)),
                  pl.BlockSpec((S, 128), lambda p, i: (0, npair + p)),
                  pl.BlockSpec((S, 128), lambda p, i: (0, 2 * npair + p))],
        out_specs=[pl.BlockSpec((Q_BLOCK, 128), lambda p, i: (i, p)), pl.BlockSpec((Q_BLOCK, 128), lambda p, i: (i, p))],
        out_shape=[_sds((S, D_TOK), F32), _sds((S, D_TOK), BF16)],
        compiler_params=_params(("arbitrary", "arbitrary")),
        interpret=False,
    )(proj, proj, proj)


def _sb_bwd(proj, dtok, o32, name):
    S = proj.shape[0]
    nq = S // Q_BLOCK
    npair = D_TOK // 128

    def body(q_ref, k_ref, v_ref, do_ref, o_ref, dq_ref, dk_ref, dv_ref, dk_acc, dv_acc):
        qi = pl.program_id(1)

        @pl.when(qi == 0)
        def _():
            dk_acc[...] = jnp.zeros_like(dk_acc)
            dv_acc[...] = jnp.zeros_like(dv_acc)

        lane = lax.broadcasted_iota(jnp.int32, (Q_BLOCK, 128), 1)
        tri_later = _tri(True)
        tri_from = _tri(False)
        q = q_ref[...]
        do = do_ref[...]
        dov = do.astype(F32) * o_ref[...]
        dq_total = jnp.zeros((Q_BLOCK, 128), F32)
        for e in range(2):
            hm = (lane >= e * HEAD_DIM) & (lane < (e + 1) * HEAD_DIM)
            qe = jnp.where(hm, q, jnp.zeros_like(q))
            doe = jnp.where(hm, do, jnp.zeros_like(do))
            rowtot = jnp.sum(jnp.where(hm, dov, 0.0), axis=1, keepdims=True)

            def step(j, carry, qe=qe, doe=doe, hm=hm, rowtot=rowtot):
                dq, later, seen = carry
                off = pl.multiple_of((qi - j) * Q_BLOCK, Q_BLOCK)
                kblk = k_ref[pl.ds(off, Q_BLOCK), :]
                vblk = v_ref[pl.ds(off, Q_BLOCK), :]
                z, mask, en, a, bsum = _sb_block(qe, kblk, j == 0, later, tri_later)
                inv = 1.0 / (1.0 + en)
                beta = jnp.where(z >= 0, 1.0, en) * inv
                omb = jnp.where(z >= 0, en, 1.0) * inv
                dlogw = a * _dot_nt(doe, vblk)
                prefix = rowtot - seen - _split_dot(dlogw, tri_from, 2)
                dz = jnp.where(mask, dlogw * omb - beta * prefix, 0.0).astype(BF16)
                ke = jnp.where(hm, kblk, jnp.zeros_like(kblk))
                dk_acc[pl.ds(off, Q_BLOCK), :] += _dot_tn(dz, qe) * ATT_SCALE
                dv_acc[pl.ds(off, Q_BLOCK), :] += _dot_tn(a.astype(BF16), doe)
                return dq + _dot(dz, ke), later + bsum, seen + jnp.sum(dlogw, axis=1, keepdims=True)

            zero = jnp.zeros((Q_BLOCK, 1), F32)
            dq, _, _ = lax.fori_loop(0, qi + 1, step, (jnp.zeros((Q_BLOCK, 128), F32), zero, zero))
            dq_total = dq_total + dq
        dq_ref[...] = (dq_total * ATT_SCALE).astype(BF16)

        @pl.when(qi == nq - 1)
        def _():
            dk_ref[...] = dk_acc[...].astype(BF16)
            dv_ref[...] = dv_acc[...].astype(BF16)

    blk = pl.BlockSpec((Q_BLOCK, 128), lambda p, i: (i, p))
    col = pl.BlockSpec((S, 128), lambda p, i: (0, p))
    return pl.pallas_call(
        body, name=name, grid=(npair, nq),
        in_specs=[blk, pl.BlockSpec((S, 128), lambda p, i: (0, npair + p)),
                  pl.BlockSpec((S, 128), lambda p, i: (0, 2 * npair + p)), blk, blk],
        out_specs=[blk, col, col],
        out_shape=[_sds((S, D_TOK), BF16), _sds((S, D_TOK), BF16), _sds((S, D_TOK), BF16)],
        scratch_shapes=[pltpu.VMEM((S, 128), F32), pltpu.VMEM((S, 128), F32)],
        compiler_params=_params(("arbitrary", "arbitrary")),
        interpret=False,
    )(proj, proj, proj, dtok, o32)


def _mem_kv_fwd(mem, g_mem, w_kv, l, name):
    lm = mem.shape[0]
    kb = D_MODEL // N_DEV

    def body(mem_ref, g_ref, w_ref, mn_ref, km_ref, vm_ref):
        mn = _rms(mem_ref[...], g_ref[...]).astype(BF16)
        mn_ref[...] = mn
        kv = jnp.zeros((lm, 2 * D_MEMH), F32)
        for d in range(N_DEV):
            kv = kv + _dot(mn[:, d * kb:(d + 1) * kb], w_ref[d])
        km_ref[...] = kv[:, :D_MEMH].astype(BF16)
        vm_ref[...] = kv[:, D_MEMH:].astype(BF16)

    return pl.pallas_call(
        body, name=name, grid=(1,),
        in_specs=[_full((lm, D_MODEL)), _full((1, D_MODEL)), _layer((kb, 2 * D_MEMH), l)],
        out_specs=[_full((lm, D_MODEL)), _full((lm, D_MEMH)), _full((lm, D_MEMH))],
        out_shape=[_sds((lm, D_MODEL), BF16), _sds((lm, D_MEMH), BF16), _sds((lm, D_MEMH), BF16)],
        compiler_params=_params(("arbitrary",)),
        interpret=False,
    )(mem, g_mem, w_kv)


def _mem_kv_bwd(dkm, dvm, mem, g_mem, mem_n, w_kv, into, l, name):
    lm = mem.shape[0]
    kb = D_MODEL // N_DEV

    def body(dkm_ref, dvm_ref, mem_ref, g_ref, mn_ref, w_ref, into_ref, dw_ref, dg_ref):
        dkv = jnp.concatenate([dkm_ref[...], dvm_ref[...]], axis=1).astype(BF16)
        dw = _dot_tn(mn_ref[...], dkv)
        for d in range(N_DEV):
            dw_ref[d] = dw[d * kb:(d + 1) * kb, :]
        dmn = jnp.concatenate([_dot_nt(dkv, w_ref[d]) for d in range(N_DEV)], axis=1)
        _, dg = _rms_bwd(mem_ref[...], g_ref[...], dmn)
        dg_ref[...] = dg

    n_layers = into.shape[1]
    return pl.pallas_call(
        body, name=name, grid=(1,),
        in_specs=[_full((lm, D_MEMH)), _full((lm, D_MEMH)), _full((lm, D_MODEL)), _full((1, D_MODEL)),
                  _full((lm, D_MODEL)), _layer((kb, 2 * D_MEMH), l), ANY],
        out_specs=[_layer((kb, 2 * D_MEMH), l), _full((1, D_MODEL))],
        out_shape=[_sds((N_DEV, n_layers, kb, 2 * D_MEMH), F32), _sds((1, D_MODEL), F32)],
        input_output_aliases={6: 0},
        compiler_params=_params(("arbitrary",)),
        interpret=False,
    )(dkm, dvm, mem, g_mem, mem_n, w_kv, into)


def _mem_heads(tm):
    lane = lax.broadcasted_iota(jnp.int32, (tm, D_MEMH), 1)
    return [(lane >= e * HEAD_DIM) & (lane < (e + 1) * HEAD_DIM) for e in range(D_MEMH // HEAD_DIM)]


def _softmax(s):
    m = jnp.max(s, axis=-1, keepdims=True)
    p = jnp.exp(s - m)
    return p / jnp.sum(p, axis=-1, keepdims=True)


def _mix_out_fwd(h, tok, qm, qm_col, km, vm, w_out, gpost, l, name):
    S = h.shape[0]
    tm = _tile(S, ROW_TILE)
    lm = km.shape[0]
    nb = D_MODEL // N_DEV

    def body(h_ref, tok_ref, qm_ref, km_ref, vm_ref, w_ref, g_ref, hn_ref, mo_ref, mix_ref):
        qv = qm_ref[...]
        kv, vv = km_ref[...], vm_ref[...]
        mo = jnp.zeros((tm, D_MEMH), F32)
        for hm, hk in zip(_mem_heads(tm), _mem_heads(lm)):
            qe = jnp.where(hm, qv, jnp.zeros_like(qv))
            p = _softmax(_dot_nt(qe, kv) * ATT_SCALE)
            mo = mo + _dot(p.astype(BF16), jnp.where(hk, vv, jnp.zeros_like(vv)))
        mob = mo.astype(BF16)
        mo_ref[...] = mob
        tv = tok_ref[...]
        mix = jnp.concatenate(
            [_dot(tv, w_ref[d, 0:D_TOK, :]) + _dot(mob, w_ref[d, D_TOK:D_MIX, :]) for d in range(N_DEV)], axis=1)
        mix_ref[...] = mix
        hn_ref[...] = h_ref[...] + _rms(mix, g_ref[...])

    return pl.pallas_call(
        body, name=name, grid=(S // tm,),
        in_specs=[_rows(tm, D_MODEL), _rows(tm, D_TOK), _rows(tm, D_MEMH, qm_col), _full((lm, D_MEMH)), _full((lm, D_MEMH)),
                  _layer((D_MIX, nb), l), _full((1, D_MODEL))],
        out_specs=[_rows(tm, D_MODEL), _rows(tm, D_MEMH), _rows(tm, D_MODEL)],
        out_shape=[_sds((S, D_MODEL), F32), _sds((S, D_MEMH), BF16), _sds((S, D_MODEL), F32)],
        compiler_params=_params(("arbitrary",)),
        interpret=False,
    )(h, tok, qm, km, vm, w_out, gpost)


def _mix_out_bwd(dho, mix, qm, qm_col, km, vm, w_out, gpost, l, name):
    S = dho.shape[0]
    tm = _tile(S, ROW_TILE)
    lm = km.shape[0]
    nb = D_MODEL // N_DEV

    def body(dho_ref, mix_ref, qm_ref, km_ref, vm_ref, w_ref, g_ref,
             dmix_ref, dtok_ref, dqm_ref, dkm_ref, dvm_ref, dg_ref):
        first = pl.program_id(0) == 0
        dmx, dg = _rms_bwd(mix_ref[...], g_ref[...], dho_ref[...])
        dmb = dmx.astype(BF16)
        dmix_ref[...] = dmb
        _acc_rows(dg_ref, dg, first)
        dcat = jnp.zeros((tm, D_MIX), F32)
        for d in range(N_DEV):
            dcat = dcat + _dot_nt(dmb[:, d * nb:(d + 1) * nb], w_ref[d])
        dtok_ref[...] = dcat[:, :D_TOK].astype(BF16)
        dmo = dcat[:, D_TOK:].astype(BF16)
        qv = qm_ref[...]
        kv, vv = km_ref[...], vm_ref[...]
        dq = jnp.zeros((tm, D_MEMH), F32)
        dk = jnp.zeros((lm, D_MEMH), F32)
        dv = jnp.zeros((lm, D_MEMH), F32)
        for hm, hk in zip(_mem_heads(tm), _mem_heads(lm)):
            qe = jnp.where(hm, qv, jnp.zeros_like(qv))
            dme = jnp.where(hm, dmo, jnp.zeros_like(dmo))
            p = _softmax(_dot_nt(qe, kv) * ATT_SCALE)
            dp = _dot_nt(dme, vv)
            ds = (p * (dp - jnp.sum(p * dp, axis=-1, keepdims=True))).astype(BF16)
            dq = dq + _dot(ds, jnp.where(hk, kv, jnp.zeros_like(kv)))
            dk = dk + _dot_tn(ds, qe)
            dv = dv + _dot_tn(p.astype(BF16), dme)
        dqm_ref[...] = (dq * ATT_SCALE).astype(BF16)
        _acc_rows(dkm_ref, dk * ATT_SCALE, first)
        _acc_rows(dvm_ref, dv, first)

    return pl.pallas_call(
        body, name=name, grid=(S // tm,),
        in_specs=[_rows(tm, D_MODEL), _rows(tm, D_MODEL), _rows(tm, D_MEMH, qm_col), _full((lm, D_MEMH)), _full((lm, D_MEMH)),
                  _layer((D_MIX, nb), l), _full((1, D_MODEL))],
        out_specs=[_rows(tm, D_MODEL), _rows(tm, D_TOK), _rows(tm, D_MEMH), _full((lm, D_MEMH)), _full((lm, D_MEMH)),
                   _full((1, D_MODEL))],
        out_shape=[_sds((S, D_MODEL), BF16), _sds((S, D_TOK), BF16), _sds((S, D_MEMH), BF16),
                   _sds((lm, D_MEMH), F32), _sds((lm, D_MEMH), F32), _sds((1, D_MODEL), F32)],
        compiler_params=_params(("arbitrary",)),
        interpret=False,
    )(dho, mix, qm, km, vm, w_out, gpost)


def _loss_head(y, target, name):
    S = y.shape[0]
    tm = _tile(S, ROW_TILE)
    nt = S // tm

    def body(y_ref, t_ref, dy_ref, loss_ref, acc_ref):
        i = pl.program_id(0)
        e = y_ref[...] - t_ref[...]
        dy_ref[...] = e * (1.0 / D_MODEL)
        _acc_rows(acc_ref, jnp.sum(e * e, axis=0, keepdims=True), i == 0)

        @pl.when(i == nt - 1)
        def _():
            tot = jnp.sum(acc_ref[...], axis=1, keepdims=True) * (0.5 / D_MODEL)
            loss_ref[...] = jnp.broadcast_to(tot, (1, 128))

    return pl.pallas_call(
        body, name=name, grid=(nt,),
        in_specs=[_rows(tm, D_MODEL), _rows(tm, D_MODEL)],
        out_specs=[_rows(tm, D_MODEL), _full((1, 128))],
        out_shape=[_sds((S, D_MODEL), F32), _sds((1, 128), F32)],
        scratch_shapes=[pltpu.VMEM((1, D_MODEL), F32)],
        compiler_params=_params(("arbitrary",)),
        interpret=False,
    )(y, target)


def _adamw(recv, w, m, v, name):
    R, C = w.shape
    tr = R if R * C <= ADAM_TILE_ELEMS else _tile(R, ADAM_TILE_ELEMS // C)
    c1 = 1.0 - ADAM_B1 ** ADAM_STEP
    c2 = 1.0 - ADAM_B2 ** ADAM_STEP

    def body(r_ref, w_ref, m_ref, v_ref, g_ref, d_ref, nm_ref, nv_ref):
        g = r_ref[0]
        for s in range(1, N_DEV):
            g = g + r_ref[s]
        g_ref[...] = g
        nm = ADAM_B1 * m_ref[...] + (1.0 - ADAM_B1) * g
        nv = ADAM_B2 * v_ref[...] + (1.0 - ADAM_B2) * (g * g)
        nm_ref[...] = nm
        nv_ref[...] = nv
        d_ref[...] = -ADAM_LR * ((nm / c1) / (jnp.sqrt(nv / c2) + ADAM_EPS) + ADAM_WD * w_ref[...])

    t = pl.BlockSpec((tr, C), lambda i: (i, 0))
    return pl.pallas_call(
        body, name=name, grid=(R // tr,),
        in_specs=[pl.BlockSpec((N_DEV, tr, C), lambda i: (0, i, 0)), t, t, t],
        out_specs=[t, t, t, t],
        out_shape=[_sds((R, C), F32)] * 4,
        compiler_params=_params(("arbitrary",)),
        interpret=False,
    )(recv, w, m, v)


def _step(p, opt_m, opt_v, x, mem, target):
    S = x.shape[0]
    bf = lambda a: a.astype(BF16)

    wsb_t = jnp.swapaxes(p["w_in_sb"], 1, 2)
    names = ["ffn1_gate", "ffn1_up", "ffn1_down", "ffn2_gate", "ffn2_up", "ffn2_down", "w_in_pool", "w_mem_kv", "w_out"]
    shards = [bf(p[n]) for n in names] + [bf(wsb_t[0]), bf(wsb_t[1]), p["g_pre"], p["g_post"]]
    gathered = _all_gather(shards)
    gw = dict(zip(names, gathered[:len(names)]))
    wsb = [g.reshape(D_SB, D_MODEL) for g in gathered[len(names):len(names) + 2]]
    unshard = lambda g: jnp.transpose(g, (1, 2, 0, 3)).reshape(DEPTH, 3, D_MODEL)
    g_pre, g_post = unshard(gathered[-2]), unshard(gathered[-1])
    g_mem = p["g_mem"]
    row = lambda a: a.reshape(1, -1)

    saved = []
    h = x
    for i in range(DEPTH):
        j = i // 2
        st = {"h0": h}
        h, st["n1"], st["gate1"], st["up1"], st["act1"], st["f1"] = _ffn_fwd(
            h, row(g_pre[i, 0]), row(g_post[i, 0]), gw["ffn1_gate"], gw["ffn1_up"], gw["ffn1_down"], i, f"ffn1_fwd_{i}")
        st["h1"] = h
        if i % 2 == 0:
            st["u"], st["dpre"], st["tok"], st["qm"] = _mix_in_pool(
                h, row(g_pre[i, 1]), gw["w_in_pool"], p["pool_w"][j], row(p["pool_scale"][j]), j, f"mix_in_pool_{i}")
            qm, qm_col = st["qm"], 0
        else:
            st["u"], st["proj"] = _mix_in_sb(h, row(g_pre[i, 1]), wsb[j], f"mix_in_sb_{i}")
            st["o32"], st["tok"] = _sb_fwd(st["proj"], f"sb_fwd_{i}")
            qm, qm_col = st["proj"], 3 * D_TOK // D_MEMH
        st["mem_n"], st["km"], st["vm"] = _mem_kv_fwd(mem, row(g_mem[i]), gw["w_mem_kv"], i, f"mem_kv_fwd_{i}")
        h, st["mo"], st["mix"] = _mix_out_fwd(h, st["tok"], qm, qm_col, st["km"], st["vm"], gw["w_out"], row(g_post[i, 1]), i,
                                              f"mix_out_fwd_{i}")
        st["h2"] = h
        h, st["n2"], st["gate2"], st["up2"], st["act2"], st["f2"] = _ffn_fwd(
            h, row(g_pre[i, 2]), row(g_post[i, 2]), gw["ffn2_gate"], gw["ffn2_up"], gw["ffn2_down"], i, f"ffn2_fwd_{i}")
        saved.append(st)

    dh, loss_part = _loss_head(h, target, "loss_head")

    grads = {n: None for n in names}
    dwsb = [None, None]
    dg_pre = [[None] * 3 for _ in range(DEPTH)]
    dg_post = [[None] * 3 for _ in range(DEPTH)]
    dg_mem = [None] * DEPTH
    dpool_w = [None, None]
    dpool_scale = [None, None]

    def ffn_backward(dh, st, i, which, hkey, slot):
        sfx = str(which)
        dh, df, dgate, dup, dgpre, dgpost = _ffn_bwd(
            dh, st[hkey], st["f" + sfx], st["gate" + sfx], st["up" + sfx], row(g_pre[i, slot]), row(g_post[i, slot]),
            gw[f"ffn{sfx}_gate"], gw[f"ffn{sfx}_up"], gw[f"ffn{sfx}_down"], i, f"ffn{sfx}_bwd_{i}")
        dg_pre[i][slot], dg_post[i][slot] = dgpre, dgpost
        for wn, a, b, split in ((f"ffn{sfx}_gate", st["n" + sfx], dgate, "cols"), (f"ffn{sfx}_up", st["n" + sfx], dup, "cols"),
                                (f"ffn{sfx}_down", st["act" + sfx], df, "rows")):
            grads[wn] = _wgrad([a], [b], split, f"wgrad_{wn}_{i}", into=grads[wn], l=i, n_layers=DEPTH)
        return dh

    for i in reversed(range(DEPTH)):
        j = i // 2
        st = saved[i]
        dh = ffn_backward(dh, st, i, 2, "h2", 2)
        if i % 2 == 0:
            qm, qm_col = st["qm"], 0
        else:
            qm, qm_col = st["proj"], 3 * D_TOK // D_MEMH
        dmix, dtok, dqm, dkm, dvm, dg_post[i][1] = _mix_out_bwd(
            dh, st["mix"], qm, qm_col, st["km"], st["vm"], gw["w_out"], row(g_post[i, 1]), i, f"mix_out_bwd_{i}")
        grads["w_out"] = _wgrad([st["tok"], st["mo"]], [dmix], "cols", f"wgrad_w_out_{i}", into=grads["w_out"], l=i, n_layers=DEPTH)
        if grads["w_mem_kv"] is None:
            grads["w_mem_kv"] = lax.empty((N_DEV, DEPTH, D_MODEL // N_DEV, 2 * D_MEMH), F32)
        grads["w_mem_kv"], dg_mem[i] = _mem_kv_bwd(dkm, dvm, mem, row(g_mem[i]), st["mem_n"], gw["w_mem_kv"], grads["w_mem_kv"], i,
                                                   f"mem_kv_bwd_{i}")
        if i % 2 == 0:
            dx, dpool_w[j], dpool_scale[j] = _pool_bwd(dtok, st["dpre"], p["pool_w"][j], row(p["pool_scale"][j]), f"pool_bwd_{i}")
            parts = [dx, dqm]
            dh, dg_pre[i][1] = _mix_in_bwd(dh, st["h1"], row(g_pre[i, 1]), parts, gw["w_in_pool"], "pool", j, f"mix_in_bwd_{i}")
            grads["w_in_pool"] = _wgrad([st["u"]], parts, "rows", f"wgrad_w_in_pool_{i}", into=grads["w_in_pool"], l=j, n_layers=2)
        else:
            dq, dk, dv = _sb_bwd(st["proj"], dtok, st["o32"], f"sb_bwd_{i}")
            parts = [dq, dk, dv, dqm]
            dh, dg_pre[i][1] = _mix_in_bwd(dh, st["h1"], row(g_pre[i, 1]), parts, wsb[j], "sb", j, f"mix_in_bwd_{i}")
            dwsb[j] = _wgrad(parts, [st["u"]], "rows", f"wgrad_w_in_sb_{i}")[:, 0]
        dh = ffn_backward(dh, st, i, 1, "h0", 0)
    grad_x = dh

    shard8 = lambda rows_: jnp.transpose(jnp.stack([jnp.concatenate(r, axis=0) for r in rows_]).reshape(DEPTH, 3, N_DEV, -1),
                                         (2, 0, 1, 3))
    blocked = [grads[n] for n in names] + dwsb + [shard8(dg_pre), shard8(dg_post)]
    whole = [jnp.concatenate(dg_mem, axis=0), jnp.stack(dpool_w), jnp.concatenate(dpool_scale, axis=0), loss_part]
    recv = _exchange(blocked, whole)
    rb, rw = recv[:len(blocked)], recv[len(blocked):]

    out = {}

    def update(name, r, w, m, v, shape):
        flat = lambda a: a.reshape(-1, a.shape[-1])
        res = _adamw(r.reshape((N_DEV,) + flat(w).shape), flat(w), flat(m), flat(v), f"adamw_{name}")
        return [a.reshape(shape) for a in res]

    for n, r in zip(names, rb):
        out[n] = update(n, r, p[n], opt_m[n], opt_v[n], p[n].shape)
    tsb = lambda a: jnp.swapaxes(a, 1, 2)
    per_layer = [update(f"w_in_sb_{j}", rb[len(names) + j], wsb_t[j], tsb(opt_m["w_in_sb"])[j], tsb(opt_v["w_in_sb"])[j],
                        wsb_t[j].shape) for j in range(2)]
    out["w_in_sb"] = [tsb(jnp.stack([per_layer[0][q], per_layer[1][q]])) for q in range(4)]
    out["g_pre"] = update("g_pre", rb[-2], p["g_pre"], opt_m["g_pre"], opt_v["g_pre"], p["g_pre"].shape)
    out["g_post"] = update("g_post", rb[-1], p["g_post"], opt_m["g_post"], opt_v["g_post"], p["g_post"].shape)
    for n, r in zip(["g_mem", "pool_w", "pool_scale"], rw[:3]):
        out[n] = update(n, r, p[n], opt_m[n], opt_v[n], p[n].shape)
    loss = jnp.sum(rw[3][:, 0, 0])
    return loss, grad_x, out


WEIGHTS = ["g_pre", "g_post", "g_mem", "ffn1_gate", "ffn1_up", "ffn1_down", "ffn2_gate", "ffn2_up", "ffn2_down",
           "w_in_pool", "pool_w", "pool_scale", "w_in_sb", "w_mem_kv", "w_out"]


def kernel(x, mem, g_pre, g_post, g_mem, ffn1_gate, ffn1_up, ffn1_down, ffn2_gate, ffn2_up, ffn2_down, w_in_pool, pool_w, pool_scale, w_in_sb, w_mem_kv, w_out, loss_target, m_g_pre, m_g_post, m_g_mem, m_ffn1_gate, m_ffn1_up, m_ffn1_down, m_ffn2_gate, m_ffn2_up, m_ffn2_down, m_w_in_pool, m_pool_w, m_pool_scale, m_w_in_sb, m_w_mem_kv, m_w_out, v_g_pre, v_g_post, v_g_mem, v_ffn1_gate, v_ffn1_up, v_ffn1_down, v_ffn2_gate, v_ffn2_up, v_ffn2_down, v_w_in_pool, v_pool_w, v_pool_scale, v_w_in_sb, v_w_mem_kv, v_w_out):
    given = dict(locals())
    p = {n: given[n] for n in WEIGHTS}
    opt_m = {n: given["m_" + n] for n in WEIGHTS}
    opt_v = {n: given["v_" + n] for n in WEIGHTS}
    loss, grad_x, out = _step(p, opt_m, opt_v, x[0], mem[0], loss_target[0])
    res = [loss, grad_x[None]]
    for q in range(4):
        res += [out[n][q] for n in WEIGHTS]
    return tuple(res)
```

```python
import functools

import jax
import jax.numpy as jnp
from jax import lax
from jax.experimental import pallas as pl
from jax.experimental.pallas import tpu as pltpu

F32 = jnp.float32
BF16 = jnp.bfloat16
GRAD_WIRE = jnp.bfloat16

N_DEV = 8
DEPTH = 4
D_MODEL = 1024
D_FF = 2048
D_TOK = 512
D_MEMH = 256
D_MIX = D_TOK + D_MEMH
D_SB = 3 * D_TOK + D_MEMH
HEAD_DIM = 64
Q_BLOCK = 128
POOL_WINDOWS = (2, 4, 8, 16)
POOL_GROUP = 128
POOL_HALO = 16
EPS = 1e-6
ATT_SCALE = HEAD_DIM ** -0.5
SB_DEAD_LOG_WEIGHT = -110.0

ADAM_LR = 0.001
ADAM_B1 = 0.9
ADAM_B2 = 0.999
ADAM_EPS = 1e-08
ADAM_WD = 0.01
ADAM_STEP = 10

VMEM_LIMIT_BYTES = 56 * 1024 * 1024
ROW_TILE = 256
WGRAD_TILE = 512
ADAM_TILE_ELEMS = 128 * 1024

MESH = pl.DeviceIdType.MESH
ANY = pl.BlockSpec(memory_space=pl.ANY)


def _tile(n, pref):
    t = 1 << (pref.bit_length() - 1)
    while n % t:
        t //= 2
    return t


def _dot(a, b):
    return jnp.dot(a, b, preferred_element_type=F32)


def _dot_nt(a, b):
    return lax.dot_general(a, b, (((1,), (1,)), ((), ())), preferred_element_type=F32)


def _dot_tn(a, b):
    return lax.dot_general(a, b, (((0,), (0,)), ((), ())), preferred_element_type=F32)


def _split_dot(x, m, terms):
    out = None
    rest = x
    for _ in range(terms):
        part = rest.astype(BF16)
        rest = rest - part.astype(F32)
        d = _dot(part, m)
        out = d if out is None else out + d
    return out


def _rms(x, g):
    r = lax.rsqrt(jnp.mean(x * x, axis=-1, keepdims=True) + EPS)
    return x * r * g


def _rms_bwd(x, g, dy):
    r = lax.rsqrt(jnp.mean(x * x, axis=-1, keepdims=True) + EPS)
    xh = x * r
    gdy = g * dy
    dx = r * (gdy - xh * jnp.mean(gdy * xh, axis=-1, keepdims=True))
    return dx, jnp.sum(dy * xh, axis=0, keepdims=True)


def _acc_rows(ref, val, first):
    @pl.when(first)
    def _():
        ref[...] = val

    @pl.when(jnp.logical_not(first))
    def _():
        ref[...] += val


def _params(sem=None):
    return pltpu.CompilerParams(dimension_semantics=sem, vmem_limit_bytes=VMEM_LIMIT_BYTES)


def _sds(shape, dtype):
    return jax.ShapeDtypeStruct(shape, dtype)


def _rows(tm, width, col=0):
    return pl.BlockSpec((tm, width), lambda i: (i, col))


def _full(shape):
    nd = len(shape)
    return pl.BlockSpec(shape, lambda *_: (0,) * nd)


def _layer(shape, l):
    return pl.BlockSpec((N_DEV, None) + tuple(shape), lambda *_: (0, l, 0, 0))


def _peers():
    x, y, c = lax.axis_index("x"), lax.axis_index("y"), lax.axis_index("c")
    peers = []
    for k in range(1, N_DEV):
        px = 1 - x if k & 4 else x
        py = 1 - y if k & 2 else y
        pc = 1 - c if k & 1 else c
        peers.append(((px, py, pc), 4 * px + 2 * py + pc))
    return 4 * x + 2 * y + c, peers


def _all_gather(shards):
    n = len(shards)

    def body(*refs):
        ins, outs = refs[:n], refs[n:2 * n]
        send_sems, recv_sems, local_sems = refs[2 * n:]
        me, peers = _peers()
        local = [pltpu.make_async_copy(ins[a], outs[a].at[me], local_sems.at[a]) for a in range(n)]
        for cp in local:
            cp.start()
        remote = []
        for k, (dev, _) in enumerate(peers):
            for a in range(n):
                cp = pltpu.make_async_remote_copy(
                    src_ref=ins[a], dst_ref=outs[a].at[me], send_sem=send_sems.at[a, k], recv_sem=recv_sems.at[a, k],
                    device_id=dev, device_id_type=MESH)
                cp.start()
                remote.append(cp)
        for cp in remote:
            cp.wait()
        for cp in local:
            cp.wait()

    return pl.pallas_call(
        body, name="all_gather_weights",
        in_specs=[ANY] * n, out_specs=[ANY] * n,
        out_shape=[_sds((N_DEV,) + s.shape, s.dtype) for s in shards],
        scratch_shapes=[pltpu.SemaphoreType.DMA((n, N_DEV - 1)), pltpu.SemaphoreType.DMA((n, N_DEV - 1)),
                        pltpu.SemaphoreType.DMA((n,))],
        compiler_params=pltpu.CompilerParams(has_side_effects=True),
        interpret=False,
    )(*shards)


def _exchange(blocked, whole):
    nb, n = len(blocked), len(blocked) + len(whole)
    arrays = list(blocked) + list(whole)

    def body(*refs):
        ins, outs = refs[:n], refs[n:2 * n]
        send_sems, recv_sems, local_sems = refs[2 * n:]
        me, peers = _peers()

        def src(a, to):
            return ins[a].at[to] if a < nb else ins[a]

        local = [pltpu.make_async_copy(src(a, me), outs[a].at[me], local_sems.at[a]) for a in range(n)]
        for cp in local:
            cp.start()
        remote = []
        for k, (dev, idx) in enumerate(peers):
            for a in range(n):
                cp = pltpu.make_async_remote_copy(
                    src_ref=src(a, idx), dst_ref=outs[a].at[me], send_sem=send_sems.at[a, k], recv_sem=recv_sems.at[a, k],
                    device_id=dev, device_id_type=MESH)
                cp.start()
                remote.append(cp)
        for cp in remote:
            cp.wait()
        for cp in local:
            cp.wait()

    return pl.pallas_call(
        body, name="exchange_gradients",
        in_specs=[ANY] * n, out_specs=[ANY] * n,
        out_shape=[_sds(a.shape, a.dtype) for a in blocked] + [_sds((N_DEV,) + a.shape, a.dtype) for a in whole],
        scratch_shapes=[pltpu.SemaphoreType.DMA((n, N_DEV - 1)), pltpu.SemaphoreType.DMA((n, N_DEV - 1)),
                        pltpu.SemaphoreType.DMA((n,))],
        compiler_params=pltpu.CompilerParams(has_side_effects=True),
        interpret=False,
    )(*arrays)


def _ffn_fwd(h, gpre, gpost, wg, wu, wd, l, name):
    S = h.shape[0]
    tm = _tile(S, ROW_TILE)
    nb = D_FF // N_DEV

    def body(h_ref, gpre_ref, gpost_ref, wg_ref, wu_ref, wd_ref, hn_ref, n_ref, gate_ref, up_ref, act_ref, f_ref):
        hv = h_ref[...]
        n = _rms(hv, gpre_ref[...]).astype(BF16)
        n_ref[...] = n
        f = jnp.zeros((tm, D_MODEL), F32)
        for d in range(N_DEV):
            cols = slice(d * nb, (d + 1) * nb)
            g = _dot(n, wg_ref[d])
            u = _dot(n, wu_ref[d])
            a = (g * jax.nn.sigmoid(g) * u).astype(BF16)
            gate_ref[:, cols] = g.astype(BF16)
            up_ref[:, cols] = u.astype(BF16)
            act_ref[:, cols] = a
            f = f + _dot(a, wd_ref[d])
        f_ref[...] = f
        hn_ref[...] = hv + 0.5 * _rms(f, gpost_ref[...])

    return pl.pallas_call(
        body, name=name, grid=(S // tm,),
        in_specs=[_rows(tm, D_MODEL), _full((1, D_MODEL)), _full((1, D_MODEL)),
                  _layer((D_MODEL, nb), l), _layer((D_MODEL, nb), l), _layer((nb, D_MODEL), l)],
        out_specs=[_rows(tm, D_MODEL), _rows(tm, D_MODEL), _rows(tm, D_FF), _rows(tm, D_FF), _rows(tm, D_FF),
                   _rows(tm, D_MODEL)],
        out_shape=[_sds((S, D_MODEL), F32), _sds((S, D_MODEL), BF16), _sds((S, D_FF), BF16), _sds((S, D_FF), BF16),
                   _sds((S, D_FF), BF16), _sds((S, D_MODEL), F32)],
        compiler_params=_params(("arbitrary",)),
        interpret=False,
    )(h, gpre, gpost, wg, wu, wd)


def _ffn_bwd(dho, h, f, gate, up, gpre, gpost, wg, wu, wd, l, name):
    S = h.shape[0]
    tm = _tile(S, ROW_TILE)
    nb = D_FF // N_DEV

    def body(dho_ref, h_ref, f_ref, gate_ref, up_ref, gpre_ref, gpost_ref, wg_ref, wu_ref, wd_ref,
             dh_ref, df_ref, dgate_ref, dup_ref, dgpre_ref, dgpost_ref):
        first = pl.program_id(0) == 0
        dho_v = dho_ref[...]
        dfx, dgpost = _rms_bwd(f_ref[...], gpost_ref[...], 0.5 * dho_v)
        dfb = dfx.astype(BF16)
        df_ref[...] = dfb
        dn = jnp.zeros((tm, D_MODEL), F32)
        for d in range(N_DEV):
            cols = slice(d * nb, (d + 1) * nb)
            dact = _dot_nt(dfb, wd_ref[d])
            g = gate_ref[:, cols].astype(F32)
            u = up_ref[:, cols].astype(F32)
            s = jax.nn.sigmoid(g)
            dg = (dact * u * (s * (1.0 + g * (1.0 - s)))).astype(BF16)
            du = (dact * (g * s)).astype(BF16)
            dgate_ref[:, cols] = dg
            dup_ref[:, cols] = du
            dn = dn + _dot_nt(dg, wg_ref[d]) + _dot_nt(du, wu_ref[d])
        dhx, dgpre = _rms_bwd(h_ref[...], gpre_ref[...], dn)
        dh_ref[...] = dho_v + dhx
        _acc_rows(dgpre_ref, dgpre, first)
        _acc_rows(dgpost_ref, dgpost, first)

    return pl.pallas_call(
        body, name=name, grid=(S // tm,),
        in_specs=[_rows(tm, D_MODEL), _rows(tm, D_MODEL), _rows(tm, D_MODEL), _rows(tm, D_FF), _rows(tm, D_FF),
                  _full((1, D_MODEL)), _full((1, D_MODEL)),
                  _layer((D_MODEL, nb), l), _layer((D_MODEL, nb), l), _layer((nb, D_MODEL), l)],
        out_specs=[_rows(tm, D_MODEL), _rows(tm, D_MODEL), _rows(tm, D_FF), _rows(tm, D_FF),
                   _full((1, D_MODEL)), _full((1, D_MODEL))],
        out_shape=[_sds((S, D_MODEL), F32), _sds((S, D_MODEL), BF16), _sds((S, D_FF), BF16), _sds((S, D_FF), BF16),
                   _sds((1, D_MODEL), F32), _sds((1, D_MODEL), F32)],
        compiler_params=_params(("arbitrary",)),
        interpret=False,
    )(dho, h, f, gate, up, gpre, gpost, wg, wu, wd)


def _wgrad(a_parts, b_parts, split, name, into=None, l=0, n_layers=1):
    S = a_parts[0].shape[0]
    bk = _tile(S, WGRAD_TILE)
    ms = [a.shape[1] for a in a_parts]
    ns = [b.shape[1] for b in b_parts]
    M, N = sum(ms), sum(ns)
    na, nbp = len(a_parts), len(b_parts)
    blk = (M // N_DEV, N) if split == "rows" else (M, N // N_DEV)
    steps = S // bk

    def body(*refs):
        a_refs, b_refs = refs[:na], refs[na:na + nbp]
        out_ref, acc_ref = refs[-2], refs[-1]
        k = pl.program_id(0)

        @pl.when(k == 0)
        def _():
            acc_ref[...] = jnp.zeros_like(acc_ref)

        r0 = 0
        for ai in range(na):
            av = a_refs[ai][...]
            c0 = 0
            for bi in range(nbp):
                acc_ref[r0:r0 + ms[ai], c0:c0 + ns[bi]] += _dot_tn(av, b_refs[bi][...])
                c0 += ns[bi]
            r0 += ms[ai]

        @pl.when(k == steps - 1)
        def _():
            for d in range(N_DEV):
                if split == "rows":
                    out_ref[d] = acc_ref[d * blk[0]:(d + 1) * blk[0], :].astype(GRAD_WIRE)
                else:
                    out_ref[d] = acc_ref[:, d * blk[1]:(d + 1) * blk[1]].astype(GRAD_WIRE)

    in_specs = [pl.BlockSpec((bk, m), lambda k: (k, 0)) for m in ms] + [pl.BlockSpec((bk, n), lambda k: (k, 0)) for n in ns]
    args = list(a_parts) + list(b_parts)
    aliases = {}
    if into is not None:
        in_specs.append(ANY)
        args.append(into)
        aliases = {len(args) - 1: 0}

    def body_with_alias(*refs):
        body(*refs[:na + nbp], *refs[-2:])

    return pl.pallas_call(
        body_with_alias if into is not None else body, name=name, grid=(steps,),
        in_specs=in_specs,
        out_specs=pl.BlockSpec((N_DEV, None) + blk, lambda k: (0, l, 0, 0)),
        out_shape=_sds((N_DEV, n_layers) + blk, GRAD_WIRE),
        scratch_shapes=[pltpu.VMEM((M, N), F32)],
        input_output_aliases=aliases,
        compiler_params=_params(("arbitrary",)),
        interpret=False,
    )(*args)


def _mix_in_pool(h, g1, w_in, pool_w, pool_scale, l, name):
    S = h.shape[0]
    tm = _tile(S, ROW_TILE)
    kb = D_MODEL // N_DEV

    def body(h_ref, g_ref, w_ref, pw_ref, ps_ref, u_ref, dpre_ref, tok_ref, qm_ref, ext_ref):
        i = pl.program_id(0)
        u = _rms(h_ref[...], g_ref[...]).astype(BF16)
        u_ref[...] = u
        proj = jnp.zeros((tm, D_MIX), F32)
        for d in range(N_DEV):
            proj = proj + _dot(u[:, d * kb:(d + 1) * kb], w_ref[d])
        qm_ref[...] = proj[:, D_TOK:].astype(BF16)
        x = proj[:, :D_TOK]

        @pl.when(i == 0)
        def _():
            ext_ref[0:POOL_HALO, :] = jnp.zeros((POOL_HALO, D_TOK), F32)

        ext_ref[POOL_HALO:, :] = x
        pos = i * tm + lax.broadcasted_iota(jnp.int32, (tm, 1), 0)
        for gi, w in enumerate(POOL_WINDOWS):
            cols = slice(gi * POOL_GROUP, (gi + 1) * POOL_GROUP)
            xs = x[:, cols]
            wsum = xs
            for k in range(1, w):
                wsum = wsum + ext_ref[POOL_HALO - k:POOL_HALO - k + tm, cols]
            cnt = jnp.minimum(pos + 1, w).astype(F32)
            dg = (wsum / cnt - xs).astype(BF16)
            dpre_ref[:, cols] = dg
            yv = _dot(dg, pw_ref[gi].astype(BF16))
            tok_ref[:, cols] = (yv * ps_ref[:, cols]).astype(BF16)
        ext_ref[0:POOL_HALO, :] = x[tm - POOL_HALO:, :]

    return pl.pallas_call(
        body, name=name, grid=(S // tm,),
        in_specs=[_rows(tm, D_MODEL), _full((1, D_MODEL)), _layer((kb, D_MIX), l),
                  _full((len(POOL_WINDOWS), POOL_GROUP, POOL_GROUP)), _full((1, D_TOK))],
        out_specs=[_rows(tm, D_MODEL), _rows(tm, D_TOK), _rows(tm, D_TOK), _rows(tm, D_MEMH)],
        out_shape=[_sds((S, D_MODEL), BF16), _sds((S, D_TOK), BF16), _sds((S, D_TOK), BF16), _sds((S, D_MEMH), BF16)],
        scratch_shapes=[pltpu.VMEM((POOL_HALO + tm, D_TOK), F32)],
        compiler_params=_params(("arbitrary",)),
        interpret=False,
    )(h, g1, w_in, pool_w, pool_scale)


def _pool_bwd(dtok, dpre, pool_w, pool_scale, name):
    S = dtok.shape[0]
    tm = _tile(S, ROW_TILE)
    nt = S // tm
    ng = len(POOL_WINDOWS)

    def body(dtok_ref, dpre_ref, pw_ref, ps_ref, dx_ref, dpw_ref, dps_ref, ext_ref):
        i = pl.program_id(0)
        first = i == 0
        t0 = (nt - 1 - i) * tm
        pos = t0 + lax.broadcasted_iota(jnp.int32, (tm, 1), 0)

        @pl.when(first)
        def _():
            ext_ref[tm:, :] = jnp.zeros((POOL_HALO, D_TOK), F32)

        dps = []
        for gi, w in enumerate(POOL_WINDOWS):
            cols = slice(gi * POOL_GROUP, (gi + 1) * POOL_GROUP)
            dg = dpre_ref[:, cols]
            pw = pw_ref[gi].astype(BF16)
            dt = dtok_ref[:, cols].astype(F32)
            yv = _dot(dg, pw)
            dps.append(jnp.sum(dt * yv, axis=0, keepdims=True))
            dy = (dt * ps_ref[:, cols]).astype(BF16)
            _acc_rows(dpw_ref.at[gi], _dot_tn(dg, dy), first)
            dd = _dot_nt(dy, pw)
            cnt = jnp.minimum(pos + 1, w).astype(F32)
            ext_ref[0:tm, cols] = dd / cnt
            wsum = ext_ref[0:tm, cols]
            for k in range(1, w):
                wsum = wsum + ext_ref[k:k + tm, cols]
            dx_ref[:, cols] = (wsum - dd).astype(BF16)
        _acc_rows(dps_ref, jnp.concatenate(dps, axis=1), first)
        ext_ref[tm:, :] = ext_ref[0:POOL_HALO, :]

    rev = lambda i: (nt - 1 - i, 0)
    return pl.pallas_call(
        body, name=name, grid=(nt,),
        in_specs=[pl.BlockSpec((tm, D_TOK), rev), pl.BlockSpec((tm, D_TOK), rev),
                  _full((ng, POOL_GROUP, POOL_GROUP)), _full((1, D_TOK))],
        out_specs=[pl.BlockSpec((tm, D_TOK), rev), _full((ng, POOL_GROUP, POOL_GROUP)), _full((1, D_TOK))],
        out_shape=[_sds((S, D_TOK), BF16), _sds((ng, POOL_GROUP, POOL_GROUP), F32), _sds((1, D_TOK), F32)],
        scratch_shapes=[pltpu.VMEM((tm + POOL_HALO, D_TOK), F32)],
        compiler_params=_params(("arbitrary",)),
        interpret=False,
    )(dtok, dpre, pool_w, pool_scale)


def _mix_in_sb(h, g1, wt, name):
    S = h.shape[0]
    tm = _tile(S, ROW_TILE)
    cb = 256

    def body(h_ref, g_ref, wt_ref, u_ref, proj_ref):
        u = _rms(h_ref[...], g_ref[...]).astype(BF16)
        u_ref[...] = u
        for c in range(D_SB // cb):
            proj_ref[:, c * cb:(c + 1) * cb] = _dot_nt(u, wt_ref[c * cb:(c + 1) * cb, :]).astype(BF16)

    return pl.pallas_call(
        body, name=name, grid=(S // tm,),
        in_specs=[_rows(tm, D_MODEL), _full((1, D_MODEL)), _full((D_SB, D_MODEL))],
        out_specs=[_rows(tm, D_MODEL), _rows(tm, D_SB)],
        out_shape=[_sds((S, D_MODEL), BF16), _sds((S, D_SB), BF16)],
        compiler_params=_params(("arbitrary",)),
        interpret=False,
    )(h, g1, wt)


def _mix_in_bwd(dho, h, g1, parts, w, mode, l, name):
    S = h.shape[0]
    tm = _tile(S, ROW_TILE)
    widths = [p.shape[1] for p in parts]
    npart = len(parts)
    kb = D_MODEL // N_DEV

    def body(*refs):
        dho_ref, h_ref, g_ref = refs[:3]
        p_refs = refs[3:3 + npart]
        w_ref, dh_ref, dg_ref = refs[3 + npart:]
        first = pl.program_id(0) == 0
        if mode == "pool":
            dproj = jnp.concatenate([p[...] for p in p_refs], axis=1)
            du = jnp.concatenate([_dot_nt(dproj, w_ref[d]) for d in range(N_DEV)], axis=1)
        else:
            du = jnp.zeros((tm, D_MODEL), F32)
            r0 = 0
            for p, wd_ in zip(p_refs, widths):
                du = du + _dot(p[...], w_ref[r0:r0 + wd_, :])
                r0 += wd_
        dhx, dg = _rms_bwd(h_ref[...], g_ref[...], du)
        dh_ref[...] = dho_ref[...] + dhx
        _acc_rows(dg_ref, dg, first)

    w_spec = _layer((kb, D_MIX), l) if mode == "pool" else _full((D_SB, D_MODEL))
    return pl.pallas_call(
        body, name=name, grid=(S // tm,),
        in_specs=[_rows(tm, D_MODEL), _rows(tm, D_MODEL), _full((1, D_MODEL))] + [_rows(tm, wd_) for wd_ in widths] + [w_spec],
        out_specs=[_rows(tm, D_MODEL), _full((1, D_MODEL))],
        out_shape=[_sds((S, D_MODEL), F32), _sds((1, D_MODEL), F32)],
        compiler_params=_params(("arbitrary",)),
        interpret=False,
    )(dho, h, g1, *parts, w)


def _sb_block(qe, kblk, diag, later, tri_later):
    row = lax.broadcasted_iota(jnp.int32, (Q_BLOCK, Q_BLOCK), 0)
    col = lax.broadcasted_iota(jnp.int32, (Q_BLOCK, Q_BLOCK), 1)
    z = _dot_nt(qe, kblk) * ATT_SCALE
    mask = jnp.logical_or(col < row, jnp.logical_not(diag))
    en = jnp.exp(-jnp.abs(z))
    ls = jnp.minimum(z, 0.0) - jnp.log(1.0 + en)
    lf = jnp.where(mask, ls - z, 0.0)
    within = _split_dot(lf, tri_later, 3)
    a = jnp.where(mask, jnp.exp(ls + within + later), 0.0)
    return z, mask, en, a, jnp.sum(lf, axis=1, keepdims=True)


def _sb_alive(later):
    return jnp.max(later) > SB_DEAD_LOG_WEIGHT


def _sb_more(qi, carry):
    return jnp.logical_and(carry[0] <= qi, carry[1])


def _tri(strict):
    row = lax.broadcasted_iota(jnp.int32, (Q_BLOCK, Q_BLOCK), 0)
    col = lax.broadcasted_iota(jnp.int32, (Q_BLOCK, Q_BLOCK), 1)
    return (row > col if strict else row >= col).astype(BF16)


def _sb_fwd(proj, name):
    S = proj.shape[0]
    nq = S // Q_BLOCK
    npair = D_TOK // 128

    def body(q_ref, k_ref, v_ref, o_ref, tok_ref):
        qi = pl.program_id(1)
        lane = lax.broadcasted_iota(jnp.int32, (Q_BLOCK, 128), 1)
        tri_later = _tri(True)
        q = q_ref[...]
        total = jnp.zeros((Q_BLOCK, 128), F32)
        for e in range(2):
            hm = (lane >= e * HEAD_DIM) & (lane < (e + 1) * HEAD_DIM)
            qe = jnp.where(hm, q, jnp.zeros_like(q))

            def step(carry, qe=qe, hm=hm):
                j, _, acc, later = carry
                off = pl.multiple_of((qi - j) * Q_BLOCK, Q_BLOCK)
                kblk = k_ref[pl.ds(off, Q_BLOCK), :]
                vblk = v_ref[pl.ds(off, Q_BLOCK), :]
                _, _, _, a, bsum = _sb_block(qe, kblk, j == 0, later, tri_later)
                ve = jnp.where(hm, vblk, jnp.zeros_like(vblk))
                later = later + bsum
                return j + 1, _sb_alive(later), acc + _split_dot(a, ve, 2), later

            init = (jnp.int32(0), jnp.bool_(True), jnp.zeros((Q_BLOCK, 128), F32), jnp.zeros((Q_BLOCK, 1), F32))
            total = total + lax.while_loop(functools.partial(_sb_more, qi), step, init)[2]
        o_ref[...] = total
        tok_ref[...] = total.astype(BF16)

    return pl.pallas_call(
        body, name=name, grid=(npair, nq),
        in_specs=[pl.BlockSpec((Q_BLOCK, 128), lambda p, i: (i, p)),
                  pl.BlockSpec((S, 128), lambda p, i: (0, npair + p)),
                  pl.BlockSpec((S, 128), lambda p, i: (0, 2 * npair + p))],
        out_specs=[pl.BlockSpec((Q_BLOCK, 128), lambda p, i: (i, p)), pl.BlockSpec((Q_BLOCK, 128), lambda p, i: (i, p))],
        out_shape=[_sds((S, D_TOK), F32), _sds((S, D_TOK), BF16)],
        compiler_params=_params(("arbitrary", "arbitrary")),
        interpret=False,
    )(proj, proj, proj)


def _sb_bwd(proj, dtok, o32, name):
    S = proj.shape[0]
    nq = S // Q_BLOCK
    npair = D_TOK // 128

    def body(q_ref, k_ref, v_ref, do_ref, o_ref, dq_ref, dk_ref, dv_ref, dk_acc, dv_acc):
        qi = pl.program_id(1)

        @pl.when(qi == 0)
        def _():
            dk_acc[...] = jnp.zeros_like(dk_acc)
            dv_acc[...] = jnp.zeros_like(dv_acc)

        lane = lax.broadcasted_iota(jnp.int32, (Q_BLOCK, 128), 1)
        tri_later = _tri(True)
        tri_from = _tri(False)
        q = q_ref[...]
        do = do_ref[...]
        dov = do.astype(F32) * o_ref[...]
        dq_total = jnp.zeros((Q_BLOCK, 128), F32)
        for e in range(2):
            hm = (lane >= e * HEAD_DIM) & (lane < (e + 1) * HEAD_DIM)
            qe = jnp.where(hm, q, jnp.zeros_like(q))
            doe = jnp.where(hm, do, jnp.zeros_like(do))
            rowtot = jnp.sum(jnp.where(hm, dov, 0.0), axis=1, keepdims=True)

            def step(carry, qe=qe, doe=doe, hm=hm, rowtot=rowtot):
                j, _, dq, later, seen = carry
                off = pl.multiple_of((qi - j) * Q_BLOCK, Q_BLOCK)
                kblk = k_ref[pl.ds(off, Q_BLOCK), :]
                vblk = v_ref[pl.ds(off, Q_BLOCK), :]
                z, mask, en, a, bsum = _sb_block(qe, kblk, j == 0, later, tri_later)
                inv = 1.0 / (1.0 + en)
                beta = jnp.where(z >= 0, 1.0, en) * inv
                omb = jnp.where(z >= 0, en, 1.0) * inv
                dlogw = a * _dot_nt(doe, vblk)
                prefix = rowtot - seen - _split_dot(dlogw, tri_from, 2)
                dz = jnp.where(mask, dlogw * omb - beta * prefix, 0.0).astype(BF16)
                ke = jnp.where(hm, kblk, jnp.zeros_like(kblk))
                dk_acc[pl.ds(off, Q_BLOCK), :] += _dot_tn(dz, qe) * ATT_SCALE
                dv_acc[pl.ds(off, Q_BLOCK), :] += _dot_tn(a.astype(BF16), doe)
                later = later + bsum
                return j + 1, _sb_alive(later), dq + _dot(dz, ke), later, seen + jnp.sum(dlogw, axis=1, keepdims=True)

            zero = jnp.zeros((Q_BLOCK, 1), F32)
            init = (jnp.int32(0), jnp.bool_(True), jnp.zeros((Q_BLOCK, 128), F32), zero, zero)
            dq_total = dq_total + lax.while_loop(functools.partial(_sb_more, qi), step, init)[2]
        dq_ref[...] = (dq_total * ATT_SCALE).astype(BF16)

        @pl.when(qi == nq - 1)
        def _():
            dk_ref[...] = dk_acc[...].astype(BF16)
            dv_ref[...] = dv_acc[...].astype(BF16)

    blk = pl.BlockSpec((Q_BLOCK, 128), lambda p, i: (i, p))
    col = pl.BlockSpec((S, 128), lambda p, i: (0, p))
    return pl.pallas_call(
        body, name=name, grid=(npair, nq),
        in_specs=[blk, pl.BlockSpec((S, 128), lambda p, i: (0, npair + p)),
                  pl.BlockSpec((S, 128), lambda p, i: (0, 2 * npair + p)), blk, blk],
        out_specs=[blk, col, col],
        out_shape=[_sds((S, D_TOK), BF16), _sds((S, D_TOK), BF16), _sds((S, D_TOK), BF16)],
        scratch_shapes=[pltpu.VMEM((S, 128), F32), pltpu.VMEM((S, 128), F32)],
        compiler_params=_params(("arbitrary", "arbitrary")),
        interpret=False,
    )(proj, proj, proj, dtok, o32)


def _mem_kv_fwd(mem, g_mem, w_kv, l, name):
    lm = mem.shape[0]
    kb = D_MODEL // N_DEV

    def body(mem_ref, g_ref, w_ref, mn_ref, km_ref, vm_ref):
        mn = _rms(mem_ref[...], g_ref[...]).astype(BF16)
        mn_ref[...] = mn
        kv = jnp.zeros((lm, 2 * D_MEMH), F32)
        for d in range(N_DEV):
            kv = kv + _dot(mn[:, d * kb:(d + 1) * kb], w_ref[d])
        km_ref[...] = kv[:, :D_MEMH].astype(BF16)
        vm_ref[...] = kv[:, D_MEMH:].astype(BF16)

    return pl.pallas_call(
        body, name=name, grid=(1,),
        in_specs=[_full((lm, D_MODEL)), _full((1, D_MODEL)), _layer((kb, 2 * D_MEMH), l)],
        out_specs=[_full((lm, D_MODEL)), _full((lm, D_MEMH)), _full((lm, D_MEMH))],
        out_shape=[_sds((lm, D_MODEL), BF16), _sds((lm, D_MEMH), BF16), _sds((lm, D_MEMH), BF16)],
        compiler_params=_params(("arbitrary",)),
        interpret=False,
    )(mem, g_mem, w_kv)


def _mem_kv_bwd(dkm, dvm, mem, g_mem, mem_n, w_kv, into, l, name):
    lm = mem.shape[0]
    kb = D_MODEL // N_DEV

    def body(dkm_ref, dvm_ref, mem_ref, g_ref, mn_ref, w_ref, into_ref, dw_ref, dg_ref):
        dkv = jnp.concatenate([dkm_ref[...], dvm_ref[...]], axis=1).astype(BF16)
        dw = _dot_tn(mn_ref[...], dkv)
        for d in range(N_DEV):
            dw_ref[d] = dw[d * kb:(d + 1) * kb, :].astype(GRAD_WIRE)
        dmn = jnp.concatenate([_dot_nt(dkv, w_ref[d]) for d in range(N_DEV)], axis=1)
        _, dg = _rms_bwd(mem_ref[...], g_ref[...], dmn)
        dg_ref[...] = dg

    n_layers = into.shape[1]
    return pl.pallas_call(
        body, name=name, grid=(1,),
        in_specs=[_full((lm, D_MEMH)), _full((lm, D_MEMH)), _full((lm, D_MODEL)), _full((1, D_MODEL)),
                  _full((lm, D_MODEL)), _layer((kb, 2 * D_MEMH), l), ANY],
        out_specs=[_layer((kb, 2 * D_MEMH), l), _full((1, D_MODEL))],
        out_shape=[_sds((N_DEV, n_layers, kb, 2 * D_MEMH), GRAD_WIRE), _sds((1, D_MODEL), F32)],
        input_output_aliases={6: 0},
        compiler_params=_params(("arbitrary",)),
        interpret=False,
    )(dkm, dvm, mem, g_mem, mem_n, w_kv, into)


def _mem_heads(tm):
    lane = lax.broadcasted_iota(jnp.int32, (tm, D_MEMH), 1)
    return [(lane >= e * HEAD_DIM) & (lane < (e + 1) * HEAD_DIM) for e in range(D_MEMH // HEAD_DIM)]


def _softmax(s):
    m = jnp.max(s, axis=-1, keepdims=True)
    p = jnp.exp(s - m)
    return p / jnp.sum(p, axis=-1, keepdims=True)


def _mix_out_fwd(h, tok, qm, qm_col, km, vm, w_out, gpost, l, name):
    S = h.shape[0]
    tm = _tile(S, ROW_TILE)
    lm = km.shape[0]
    nb = D_MODEL // N_DEV

    def body(h_ref, tok_ref, qm_ref, km_ref, vm_ref, w_ref, g_ref, hn_ref, mo_ref, mix_ref):
        qv = qm_ref[...]
        kv, vv = km_ref[...], vm_ref[...]
        mo = jnp.zeros((tm, D_MEMH), F32)
        for hm, hk in zip(_mem_heads(tm), _mem_heads(lm)):
            qe = jnp.where(hm, qv, jnp.zeros_like(qv))
            p = _softmax(_dot_nt(qe, kv) * ATT_SCALE)
            mo = mo + _dot(p.astype(BF16), jnp.where(hk, vv, jnp.zeros_like(vv)))
        mob = mo.astype(BF16)
        mo_ref[...] = mob
        tv = tok_ref[...]
        mix = jnp.concatenate(
            [_dot(tv, w_ref[d, 0:D_TOK, :]) + _dot(mob, w_ref[d, D_TOK:D_MIX, :]) for d in range(N_DEV)], axis=1)
        mix_ref[...] = mix
        hn_ref[...] = h_ref[...] + _rms(mix, g_ref[...])

    return pl.pallas_call(
        body, name=name, grid=(S // tm,),
        in_specs=[_rows(tm, D_MODEL), _rows(tm, D_TOK), _rows(tm, D_MEMH, qm_col), _full((lm, D_MEMH)), _full((lm, D_MEMH)),
                  _layer((D_MIX, nb), l), _full((1, D_MODEL))],
        out_specs=[_rows(tm, D_MODEL), _rows(tm, D_MEMH), _rows(tm, D_MODEL)],
        out_shape=[_sds((S, D_MODEL), F32), _sds((S, D_MEMH), BF16), _sds((S, D_MODEL), F32)],
        compiler_params=_params(("arbitrary",)),
        interpret=False,
    )(h, tok, qm, km, vm, w_out, gpost)


def _mix_out_bwd(dho, mix, qm, qm_col, km, vm, w_out, gpost, l, name):
    S = dho.shape[0]
    tm = _tile(S, ROW_TILE)
    lm = km.shape[0]
    nb = D_MODEL // N_DEV

    def body(dho_ref, mix_ref, qm_ref, km_ref, vm_ref, w_ref, g_ref,
             dmix_ref, dtok_ref, dqm_ref, dkm_ref, dvm_ref, dg_ref):
        first = pl.program_id(0) == 0
        dmx, dg = _rms_bwd(mix_ref[...], g_ref[...], dho_ref[...])
        dmb = dmx.astype(BF16)
        dmix_ref[...] = dmb
        _acc_rows(dg_ref, dg, first)
        dcat = jnp.zeros((tm, D_MIX), F32)
        for d in range(N_DEV):
            dcat = dcat + _dot_nt(dmb[:, d * nb:(d + 1) * nb], w_ref[d])
        dtok_ref[...] = dcat[:, :D_TOK].astype(BF16)
        dmo = dcat[:, D_TOK:].astype(BF16)
        qv = qm_ref[...]
        kv, vv = km_ref[...], vm_ref[...]
        dq = jnp.zeros((tm, D_MEMH), F32)
        dk = jnp.zeros((lm, D_MEMH), F32)
        dv = jnp.zeros((lm, D_MEMH), F32)
        for hm, hk in zip(_mem_heads(tm), _mem_heads(lm)):
            qe = jnp.where(hm, qv, jnp.zeros_like(qv))
            dme = jnp.where(hm, dmo, jnp.zeros_like(dmo))
            p = _softmax(_dot_nt(qe, kv) * ATT_SCALE)
            dp = _dot_nt(dme, vv)
            ds = (p * (dp - jnp.sum(p * dp, axis=-1, keepdims=True))).astype(BF16)
            dq = dq + _dot(ds, jnp.where(hk, kv, jnp.zeros_like(kv)))
            dk = dk + _dot_tn(ds, qe)
            dv = dv + _dot_tn(p.astype(BF16), dme)
        dqm_ref[...] = (dq * ATT_SCALE).astype(BF16)
        _acc_rows(dkm_ref, dk * ATT_SCALE, first)
        _acc_rows(dvm_ref, dv, first)

    return pl.pallas_call(
        body, name=name, grid=(S // tm,),
        in_specs=[_rows(tm, D_MODEL), _rows(tm, D_MODEL), _rows(tm, D_MEMH, qm_col), _full((lm, D_MEMH)), _full((lm, D_MEMH)),
                  _layer((D_MIX, nb), l), _full((1, D_MODEL))],
        out_specs=[_rows(tm, D_MODEL), _rows(tm, D_TOK), _rows(tm, D_MEMH), _full((lm, D_MEMH)), _full((lm, D_MEMH)),
                   _full((1, D_MODEL))],
        out_shape=[_sds((S, D_MODEL), BF16), _sds((S, D_TOK), BF16), _sds((S, D_MEMH), BF16),
                   _sds((lm, D_MEMH), F32), _sds((lm, D_MEMH), F32), _sds((1, D_MODEL), F32)],
        compiler_params=_params(("arbitrary",)),
        interpret=False,
    )(dho, mix, qm, km, vm, w_out, gpost)


def _loss_head(y, target, name):
    S = y.shape[0]
    tm = _tile(S, ROW_TILE)
    nt = S // tm

    def body(y_ref, t_ref, dy_ref, loss_ref, acc_ref):
        i = pl.program_id(0)
        e = y_ref[...] - t_ref[...]
        dy_ref[...] = e * (1.0 / D_MODEL)
        _acc_rows(acc_ref, jnp.sum(e * e, axis=0, keepdims=True), i == 0)

        @pl.when(i == nt - 1)
        def _():
            tot = jnp.sum(acc_ref[...], axis=1, keepdims=True) * (0.5 / D_MODEL)
            loss_ref[...] = jnp.broadcast_to(tot, (1, 128))

    return pl.pallas_call(
        body, name=name, grid=(nt,),
        in_specs=[_rows(tm, D_MODEL), _rows(tm, D_MODEL)],
        out_specs=[_rows(tm, D_MODEL), _full((1, 128))],
        out_shape=[_sds((S, D_MODEL), F32), _sds((1, 128), F32)],
        scratch_shapes=[pltpu.VMEM((1, D_MODEL), F32)],
        compiler_params=_params(("arbitrary",)),
        interpret=False,
    )(y, target)


def _adamw(recv, w, m, v, name):
    R, C = w.shape
    tr = R if R * C <= ADAM_TILE_ELEMS else _tile(R, ADAM_TILE_ELEMS // C)
    c1 = 1.0 - ADAM_B1 ** ADAM_STEP
    c2 = 1.0 - ADAM_B2 ** ADAM_STEP

    def body(r_ref, w_ref, m_ref, v_ref, g_ref, d_ref, nm_ref, nv_ref):
        g = r_ref[0].astype(F32)
        for s in range(1, N_DEV):
            g = g + r_ref[s].astype(F32)
        g_ref[...] = g
        nm = ADAM_B1 * m_ref[...] + (1.0 - ADAM_B1) * g
        nv = ADAM_B2 * v_ref[...] + (1.0 - ADAM_B2) * (g * g)
        nm_ref[...] = nm
        nv_ref[...] = nv
        d_ref[...] = -ADAM_LR * ((nm / c1) / (jnp.sqrt(nv / c2) + ADAM_EPS) + ADAM_WD * w_ref[...])

    t = pl.BlockSpec((tr, C), lambda i: (i, 0))
    return pl.pallas_call(
        body, name=name, grid=(R // tr,),
        in_specs=[pl.BlockSpec((N_DEV, tr, C), lambda i: (0, i, 0)), t, t, t],
        out_specs=[t, t, t, t],
        out_shape=[_sds((R, C), F32)] * 4,
        compiler_params=_params(("arbitrary",)),
        interpret=False,
    )(recv, w, m, v)


def _step(p, opt_m, opt_v, x, mem, target):
    S = x.shape[0]
    bf = lambda a: a.astype(BF16)

    wsb_t = jnp.swapaxes(p["w_in_sb"], 1, 2)
    names = ["ffn1_gate", "ffn1_up", "ffn1_down", "ffn2_gate", "ffn2_up", "ffn2_down", "w_in_pool", "w_mem_kv", "w_out"]
    shards = [bf(p[n]) for n in names] + [bf(wsb_t[0]), bf(wsb_t[1]), p["g_pre"], p["g_post"]]
    gathered = _all_gather(shards)
    gw = dict(zip(names, gathered[:len(names)]))
    wsb = [g.reshape(D_SB, D_MODEL) for g in gathered[len(names):len(names) + 2]]
    unshard = lambda g: jnp.transpose(g, (1, 2, 0, 3)).reshape(DEPTH, 3, D_MODEL)
    g_pre, g_post = unshard(gathered[-2]), unshard(gathered[-1])
    g_mem = p["g_mem"]
    row = lambda a: a.reshape(1, -1)

    saved = []
    h = x
    for i in range(DEPTH):
        j = i // 2
        st = {"h0": h}
        h, st["n1"], st["gate1"], st["up1"], st["act1"], st["f1"] = _ffn_fwd(
            h, row(g_pre[i, 0]), row(g_post[i, 0]), gw["ffn1_gate"], gw["ffn1_up"], gw["ffn1_down"], i, f"ffn1_fwd_{i}")
        st["h1"] = h
        if i % 2 == 0:
            st["u"], st["dpre"], st["tok"], st["qm"] = _mix_in_pool(
                h, row(g_pre[i, 1]), gw["w_in_pool"], p["pool_w"][j], row(p["pool_scale"][j]), j, f"mix_in_pool_{i}")
            qm, qm_col = st["qm"], 0
        else:
            st["u"], st["proj"] = _mix_in_sb(h, row(g_pre[i, 1]), wsb[j], f"mix_in_sb_{i}")
            st["o32"], st["tok"] = _sb_fwd(st["proj"], f"sb_fwd_{i}")
            qm, qm_col = st["proj"], 3 * D_TOK // D_MEMH
        st["mem_n"], st["km"], st["vm"] = _mem_kv_fwd(mem, row(g_mem[i]), gw["w_mem_kv"], i, f"mem_kv_fwd_{i}")
        h, st["mo"], st["mix"] = _mix_out_fwd(h, st["tok"], qm, qm_col, st["km"], st["vm"], gw["w_out"], row(g_post[i, 1]), i,
                                              f"mix_out_fwd_{i}")
        st["h2"] = h
        h, st["n2"], st["gate2"], st["up2"], st["act2"], st["f2"] = _ffn_fwd(
            h, row(g_pre[i, 2]), row(g_post[i, 2]), gw["ffn2_gate"], gw["ffn2_up"], gw["ffn2_down"], i, f"ffn2_fwd_{i}")
        saved.append(st)

    dh, loss_part = _loss_head(h, target, "loss_head")

    grads = {n: None for n in names}
    dwsb = [None, None]
    dg_pre = [[None] * 3 for _ in range(DEPTH)]
    dg_post = [[None] * 3 for _ in range(DEPTH)]
    dg_mem = [None] * DEPTH
    dpool_w = [None, None]
    dpool_scale = [None, None]

    def ffn_backward(dh, st, i, which, hkey, slot):
        sfx = str(which)
        dh, df, dgate, dup, dgpre, dgpost = _ffn_bwd(
            dh, st[hkey], st["f" + sfx], st["gate" + sfx], st["up" + sfx], row(g_pre[i, slot]), row(g_post[i, slot]),
            gw[f"ffn{sfx}_gate"], gw[f"ffn{sfx}_up"], gw[f"ffn{sfx}_down"], i, f"ffn{sfx}_bwd_{i}")
        dg_pre[i][slot], dg_post[i][slot] = dgpre, dgpost
        for wn, a, b, split in ((f"ffn{sfx}_gate", st["n" + sfx], dgate, "cols"), (f"ffn{sfx}_up", st["n" + sfx], dup, "cols"),
                                (f"ffn{sfx}_down", st["act" + sfx], df, "rows")):
            grads[wn] = _wgrad([a], [b], split, f"wgrad_{wn}_{i}", into=grads[wn], l=i, n_layers=DEPTH)
        return dh

    for i in reversed(range(DEPTH)):
        j = i // 2
        st = saved[i]
        dh = ffn_backward(dh, st, i, 2, "h2", 2)
        if i % 2 == 0:
            qm, qm_col = st["qm"], 0
        else:
            qm, qm_col = st["proj"], 3 * D_TOK // D_MEMH
        dmix, dtok, dqm, dkm, dvm, dg_post[i][1] = _mix_out_bwd(
            dh, st["mix"], qm, qm_col, st["km"], st["vm"], gw["w_out"], row(g_post[i, 1]), i, f"mix_out_bwd_{i}")
        grads["w_out"] = _wgrad([st["tok"], st["mo"]], [dmix], "cols", f"wgrad_w_out_{i}", into=grads["w_out"], l=i, n_layers=DEPTH)
        if grads["w_mem_kv"] is None:
            grads["w_mem_kv"] = lax.empty((N_DEV, DEPTH, D_MODEL // N_DEV, 2 * D_MEMH), GRAD_WIRE)
        grads["w_mem_kv"], dg_mem[i] = _mem_kv_bwd(dkm, dvm, mem, row(g_mem[i]), st["mem_n"], gw["w_mem_kv"], grads["w_mem_kv"], i,
                                                   f"mem_kv_bwd_{i}")
        if i % 2 == 0:
            dx, dpool_w[j], dpool_scale[j] = _pool_bwd(dtok, st["dpre"], p["pool_w"][j], row(p["pool_scale"][j]), f"pool_bwd_{i}")
            parts = [dx, dqm]
            dh, dg_pre[i][1] = _mix_in_bwd(dh, st["h1"], row(g_pre[i, 1]), parts, gw["w_in_pool"], "pool", j, f"mix_in_bwd_{i}")
            grads["w_in_pool"] = _wgrad([st["u"]], parts, "rows", f"wgrad_w_in_pool_{i}", into=grads["w_in_pool"], l=j, n_layers=2)
        else:
            dq, dk, dv = _sb_bwd(st["proj"], dtok, st["o32"], f"sb_bwd_{i}")
            parts = [dq, dk, dv, dqm]
            dh, dg_pre[i][1] = _mix_in_bwd(dh, st["h1"], row(g_pre[i, 1]), parts, wsb[j], "sb", j, f"mix_in_bwd_{i}")
            dwsb[j] = _wgrad(parts, [st["u"]], "rows", f"wgrad_w_in_sb_{i}")[:, 0]
        dh = ffn_backward(dh, st, i, 1, "h0", 0)
    grad_x = dh

    shard8 = lambda rows_: jnp.transpose(jnp.stack([jnp.concatenate(r, axis=0) for r in rows_]).reshape(DEPTH, 3, N_DEV, -1),
                                         (2, 0, 1, 3))
    blocked = [grads[n] for n in names] + dwsb + [shard8(dg_pre), shard8(dg_post)]
    whole = [jnp.concatenate(dg_mem, axis=0), jnp.stack(dpool_w), jnp.concatenate(dpool_scale, axis=0), loss_part]
    recv = _exchange(blocked, whole)
    rb, rw = recv[:len(blocked)], recv[len(blocked):]

    out = {}

    def update(name, r, w, m, v, shape):
        flat = lambda a: a.reshape(-1, a.shape[-1])
        res = _adamw(r.reshape((N_DEV,) + flat(w).shape), flat(w), flat(m), flat(v), f"adamw_{name}")
        return [a.reshape(shape) for a in res]

    for n, r in zip(names, rb):
        out[n] = update(n, r, p[n], opt_m[n], opt_v[n], p[n].shape)
    tsb = lambda a: jnp.swapaxes(a, 1, 2)
    per_layer = [update(f"w_in_sb_{j}", rb[len(names) + j], wsb_t[j], tsb(opt_m["w_in_sb"])[j], tsb(opt_v["w_in_sb"])[j],
                        wsb_t[j].shape) for j in range(2)]
    out["w_in_sb"] = [tsb(jnp.stack([per_layer[0][q], per_layer[1][q]])) for q in range(4)]
    out["g_pre"] = update("g_pre", rb[-2], p["g_pre"], opt_m["g_pre"], opt_v["g_pre"], p["g_pre"].shape)
    out["g_post"] = update("g_post", rb[-1], p["g_post"], opt_m["g_post"], opt_v["g_post"], p["g_post"].shape)
    for n, r in zip(["g_mem", "pool_w", "pool_scale"], rw[:3]):
        out[n] = update(n, r, p[n], opt_m[n], opt_v[n], p[n].shape)
    loss = jnp.sum(rw[3][:, 0, 0])
    return loss, grad_x, out


WEIGHTS = ["g_pre", "g_post", "g_mem", "ffn1_gate", "ffn1_up", "ffn1_down", "ffn2_gate", "ffn2_up", "ffn2_down",
           "w_in_pool", "pool_w", "pool_scale", "w_in_sb", "w_mem_kv", "w_out"]


def kernel(x, mem, g_pre, g_post, g_mem, ffn1_gate, ffn1_up, ffn1_down, ffn2_gate, ffn2_up, ffn2_down, w_in_pool, pool_w, pool_scale, w_in_sb, w_mem_kv, w_out, loss_target, m_g_pre, m_g_post, m_g_mem, m_ffn1_gate, m_ffn1_up, m_ffn1_down, m_ffn2_gate, m_ffn2_up, m_ffn2_down, m_w_in_pool, m_pool_w, m_pool_scale, m_w_in_sb, m_w_mem_kv, m_w_out, v_g_pre, v_g_post, v_g_mem, v_ffn1_gate, v_ffn1_up, v_ffn1_down, v_ffn2_gate, v_ffn2_up, v_ffn2_down, v_w_in_pool, v_pool_w, v_pool_scale, v_w_in_sb, v_w_mem_kv, v_w_out):
    given = dict(locals())
    p = {n: given[n] for n in WEIGHTS}
    opt_m = {n: given["m_" + n] for n in WEIGHTS}
    opt_v = {n: given["v_" + n] for n in WEIGHTS}
    loss, grad_x, out = _step(p, opt_m, opt_v, x[0], mem[0], loss_target[0])
    res = [loss, grad_x[None]]
    for q in range(4):
        res += [out[n][q] for n in WEIGHTS]
    return tuple(res)
```

```python
import functools

import jax
import jax.numpy as jnp
from jax import lax
from jax.experimental import pallas as pl
from jax.experimental.pallas import tpu as pltpu

F32 = jnp.float32
BF16 = jnp.bfloat16
GRAD_WIRE = jnp.bfloat16

N_DEV = 8
DEPTH = 4
D_MODEL = 1024
D_FF = 2048
D_TOK = 512
D_MEMH = 256
D_MIX = D_TOK + D_MEMH
D_SB = 3 * D_TOK + D_MEMH
HEAD_DIM = 64
Q_BLOCK = 128
POOL_WINDOWS = (2, 4, 8, 16)
POOL_GROUP = 128
POOL_HALO = 16
EPS = 1e-6
ATT_SCALE = HEAD_DIM ** -0.5
SB_DEAD_LOG_WEIGHT = -110.0
SB_FWD_LANES = 256
SB_BWD_LANES = 128

ADAM_LR = 0.001
ADAM_B1 = 0.9
ADAM_B2 = 0.999
ADAM_EPS = 1e-08
ADAM_WD = 0.01
ADAM_STEP = 10

VMEM_LIMIT_BYTES = 56 * 1024 * 1024
ROW_TILE = 256
WGRAD_TILE = 512
ADAM_TILE_ELEMS = 128 * 1024

MESH = pl.DeviceIdType.MESH
ANY = pl.BlockSpec(memory_space=pl.ANY)


def _tile(n, pref):
    t = 1 << (pref.bit_length() - 1)
    while n % t:
        t //= 2
    return t


def _dot(a, b):
    return jnp.dot(a, b, preferred_element_type=F32)


def _dot_nt(a, b):
    return lax.dot_general(a, b, (((1,), (1,)), ((), ())), preferred_element_type=F32)


def _dot_tn(a, b):
    return lax.dot_general(a, b, (((0,), (0,)), ((), ())), preferred_element_type=F32)


def _split_dot(x, m, terms):
    out = None
    rest = x
    for _ in range(terms):
        part = rest.astype(BF16)
        rest = rest - part.astype(F32)
        d = _dot(part, m)
        out = d if out is None else out + d
    return out


def _rms(x, g):
    r = lax.rsqrt(jnp.mean(x * x, axis=-1, keepdims=True) + EPS)
    return x * r * g


def _rms_bwd(x, g, dy):
    r = lax.rsqrt(jnp.mean(x * x, axis=-1, keepdims=True) + EPS)
    xh = x * r
    gdy = g * dy
    dx = r * (gdy - xh * jnp.mean(gdy * xh, axis=-1, keepdims=True))
    return dx, jnp.sum(dy * xh, axis=0, keepdims=True)


def _acc_rows(ref, val, first):
    @pl.when(first)
    def _():
        ref[...] = val

    @pl.when(jnp.logical_not(first))
    def _():
        ref[...] += val


def _params(sem=None):
    return pltpu.CompilerParams(dimension_semantics=sem, vmem_limit_bytes=VMEM_LIMIT_BYTES)


def _sds(shape, dtype):
    return jax.ShapeDtypeStruct(shape, dtype)


def _rows(tm, width, col=0):
    return pl.BlockSpec((tm, width), lambda i: (i, col))


def _full(shape):
    nd = len(shape)
    return pl.BlockSpec(shape, lambda *_: (0,) * nd)


def _layer(shape, l):
    return pl.BlockSpec((N_DEV, None) + tuple(shape), lambda *_: (0, l, 0, 0))


def _peers():
    x, y, c = lax.axis_index("x"), lax.axis_index("y"), lax.axis_index("c")
    peers = []
    for k in range(1, N_DEV):
        px = 1 - x if k & 4 else x
        py = 1 - y if k & 2 else y
        pc = 1 - c if k & 1 else c
        peers.append(((px, py, pc), 4 * px + 2 * py + pc))
    return 4 * x + 2 * y + c, peers


def _all_gather(shards):
    n = len(shards)

    def body(*refs):
        ins, outs = refs[:n], refs[n:2 * n]
        send_sems, recv_sems, local_sems = refs[2 * n:]
        me, peers = _peers()
        local = [pltpu.make_async_copy(ins[a], outs[a].at[me], local_sems.at[a]) for a in range(n)]
        for cp in local:
            cp.start()
        remote = []
        for k, (dev, _) in enumerate(peers):
            for a in range(n):
                cp = pltpu.make_async_remote_copy(
                    src_ref=ins[a], dst_ref=outs[a].at[me], send_sem=send_sems.at[a, k], recv_sem=recv_sems.at[a, k],
                    device_id=dev, device_id_type=MESH)
                cp.start()
                remote.append(cp)
        for cp in remote:
            cp.wait()
        for cp in local:
            cp.wait()

    return pl.pallas_call(
        body, name="all_gather_weights",
        in_specs=[ANY] * n, out_specs=[ANY] * n,
        out_shape=[_sds((N_DEV,) + s.shape, s.dtype) for s in shards],
        scratch_shapes=[pltpu.SemaphoreType.DMA((n, N_DEV - 1)), pltpu.SemaphoreType.DMA((n, N_DEV - 1)),
                        pltpu.SemaphoreType.DMA((n,))],
        compiler_params=pltpu.CompilerParams(has_side_effects=True),
        interpret=False,
    )(*shards)


def _exchange(blocked, whole):
    nb, n = len(blocked), len(blocked) + len(whole)
    arrays = list(blocked) + list(whole)

    def body(*refs):
        ins, outs = refs[:n], refs[n:2 * n]
        send_sems, recv_sems, local_sems = refs[2 * n:]
        me, peers = _peers()

        def src(a, to):
            return ins[a].at[to] if a < nb else ins[a]

        local = [pltpu.make_async_copy(src(a, me), outs[a].at[me], local_sems.at[a]) for a in range(n)]
        for cp in local:
            cp.start()
        remote = []
        for k, (dev, idx) in enumerate(peers):
            for a in range(n):
                cp = pltpu.make_async_remote_copy(
                    src_ref=src(a, idx), dst_ref=outs[a].at[me], send_sem=send_sems.at[a, k], recv_sem=recv_sems.at[a, k],
                    device_id=dev, device_id_type=MESH)
                cp.start()
                remote.append(cp)
        for cp in remote:
            cp.wait()
        for cp in local:
            cp.wait()

    return pl.pallas_call(
        body, name="exchange_gradients",
        in_specs=[ANY] * n, out_specs=[ANY] * n,
        out_shape=[_sds(a.shape, a.dtype) for a in blocked] + [_sds((N_DEV,) + a.shape, a.dtype) for a in whole],
        scratch_shapes=[pltpu.SemaphoreType.DMA((n, N_DEV - 1)), pltpu.SemaphoreType.DMA((n, N_DEV - 1)),
                        pltpu.SemaphoreType.DMA((n,))],
        compiler_params=pltpu.CompilerParams(has_side_effects=True),
        interpret=False,
    )(*arrays)


def _ffn_fwd(h, gpre, gpost, wg, wu, wd, l, name):
    S = h.shape[0]
    tm = _tile(S, ROW_TILE)
    nb = D_FF // N_DEV

    def body(h_ref, gpre_ref, gpost_ref, wg_ref, wu_ref, wd_ref, hn_ref, n_ref, gate_ref, up_ref, act_ref, f_ref):
        hv = h_ref[...]
        n = _rms(hv, gpre_ref[...]).astype(BF16)
        n_ref[...] = n
        f = jnp.zeros((tm, D_MODEL), F32)
        for d in range(N_DEV):
            cols = slice(d * nb, (d + 1) * nb)
            g = _dot(n, wg_ref[d])
            u = _dot(n, wu_ref[d])
            a = (g * jax.nn.sigmoid(g) * u).astype(BF16)
            gate_ref[:, cols] = g.astype(BF16)
            up_ref[:, cols] = u.astype(BF16)
            act_ref[:, cols] = a
            f = f + _dot(a, wd_ref[d])
        f_ref[...] = f
        hn_ref[...] = hv + 0.5 * _rms(f, gpost_ref[...])

    return pl.pallas_call(
        body, name=name, grid=(S // tm,),
        in_specs=[_rows(tm, D_MODEL), _full((1, D_MODEL)), _full((1, D_MODEL)),
                  _layer((D_MODEL, nb), l), _layer((D_MODEL, nb), l), _layer((nb, D_MODEL), l)],
        out_specs=[_rows(tm, D_MODEL), _rows(tm, D_MODEL), _rows(tm, D_FF), _rows(tm, D_FF), _rows(tm, D_FF),
                   _rows(tm, D_MODEL)],
        out_shape=[_sds((S, D_MODEL), F32), _sds((S, D_MODEL), BF16), _sds((S, D_FF), BF16), _sds((S, D_FF), BF16),
                   _sds((S, D_FF), BF16), _sds((S, D_MODEL), F32)],
        compiler_params=_params(("arbitrary",)),
        interpret=False,
    )(h, gpre, gpost, wg, wu, wd)


def _ffn_bwd(dho, h, f, gate, up, gpre, gpost, wg, wu, wd, l, name):
    S = h.shape[0]
    tm = _tile(S, ROW_TILE)
    nb = D_FF // N_DEV

    def body(dho_ref, h_ref, f_ref, gate_ref, up_ref, gpre_ref, gpost_ref, wg_ref, wu_ref, wd_ref,
             dh_ref, df_ref, dgate_ref, dup_ref, dgpre_ref, dgpost_ref):
        first = pl.program_id(0) == 0
        dho_v = dho_ref[...]
        dfx, dgpost = _rms_bwd(f_ref[...], gpost_ref[...], 0.5 * dho_v)
        dfb = dfx.astype(BF16)
        df_ref[...] = dfb
        dn = jnp.zeros((tm, D_MODEL), F32)
        for d in range(N_DEV):
            cols = slice(d * nb, (d + 1) * nb)
            dact = _dot_nt(dfb, wd_ref[d])
            g = gate_ref[:, cols].astype(F32)
            u = up_ref[:, cols].astype(F32)
            s = jax.nn.sigmoid(g)
            dg = (dact * u * (s * (1.0 + g * (1.0 - s)))).astype(BF16)
            du = (dact * (g * s)).astype(BF16)
            dgate_ref[:, cols] = dg
            dup_ref[:, cols] = du
            dn = dn + _dot_nt(dg, wg_ref[d]) + _dot_nt(du, wu_ref[d])
        dhx, dgpre = _rms_bwd(h_ref[...], gpre_ref[...], dn)
        dh_ref[...] = dho_v + dhx
        _acc_rows(dgpre_ref, dgpre, first)
        _acc_rows(dgpost_ref, dgpost, first)

    return pl.pallas_call(
        body, name=name, grid=(S // tm,),
        in_specs=[_rows(tm, D_MODEL), _rows(tm, D_MODEL), _rows(tm, D_MODEL), _rows(tm, D_FF), _rows(tm, D_FF),
                  _full((1, D_MODEL)), _full((1, D_MODEL)),
                  _layer((D_MODEL, nb), l), _layer((D_MODEL, nb), l), _layer((nb, D_MODEL), l)],
        out_specs=[_rows(tm, D_MODEL), _rows(tm, D_MODEL), _rows(tm, D_FF), _rows(tm, D_FF),
                   _full((1, D_MODEL)), _full((1, D_MODEL))],
        out_shape=[_sds((S, D_MODEL), F32), _sds((S, D_MODEL), BF16), _sds((S, D_FF), BF16), _sds((S, D_FF), BF16),
                   _sds((1, D_MODEL), F32), _sds((1, D_MODEL), F32)],
        compiler_params=_params(("arbitrary",)),
        interpret=False,
    )(dho, h, f, gate, up, gpre, gpost, wg, wu, wd)


def _wgrad(a_parts, b_parts, split, name, into=None, l=0, n_layers=1):
    S = a_parts[0].shape[0]
    bk = _tile(S, WGRAD_TILE)
    ms = [a.shape[1] for a in a_parts]
    ns = [b.shape[1] for b in b_parts]
    M, N = sum(ms), sum(ns)
    na, nbp = len(a_parts), len(b_parts)
    blk = (M // N_DEV, N) if split == "rows" else (M, N // N_DEV)
    steps = S // bk

    def body(*refs):
        a_refs, b_refs = refs[:na], refs[na:na + nbp]
        out_ref, acc_ref = refs[-2], refs[-1]
        k = pl.program_id(0)

        @pl.when(k == 0)
        def _():
            acc_ref[...] = jnp.zeros_like(acc_ref)

        r0 = 0
        for ai in range(na):
            av = a_refs[ai][...]
            c0 = 0
            for bi in range(nbp):
                acc_ref[r0:r0 + ms[ai], c0:c0 + ns[bi]] += _dot_tn(av, b_refs[bi][...])
                c0 += ns[bi]
            r0 += ms[ai]

        @pl.when(k == steps - 1)
        def _():
            for d in range(N_DEV):
                if split == "rows":
                    out_ref[d] = acc_ref[d * blk[0]:(d + 1) * blk[0], :].astype(GRAD_WIRE)
                else:
                    out_ref[d] = acc_ref[:, d * blk[1]:(d + 1) * blk[1]].astype(GRAD_WIRE)

    in_specs = [pl.BlockSpec((bk, m), lambda k: (k, 0)) for m in ms] + [pl.BlockSpec((bk, n), lambda k: (k, 0)) for n in ns]
    args = list(a_parts) + list(b_parts)
    aliases = {}
    if into is not None:
        in_specs.append(ANY)
        args.append(into)
        aliases = {len(args) - 1: 0}

    def body_with_alias(*refs):
        body(*refs[:na + nbp], *refs[-2:])

    return pl.pallas_call(
        body_with_alias if into is not None else body, name=name, grid=(steps,),
        in_specs=in_specs,
        out_specs=pl.BlockSpec((N_DEV, None) + blk, lambda k: (0, l, 0, 0)),
        out_shape=_sds((N_DEV, n_layers) + blk, GRAD_WIRE),
        scratch_shapes=[pltpu.VMEM((M, N), F32)],
        input_output_aliases=aliases,
        compiler_params=_params(("arbitrary",)),
        interpret=False,
    )(*args)


def _mix_in_pool(h, g1, w_in, pool_w, pool_scale, l, name):
    S = h.shape[0]
    tm = _tile(S, ROW_TILE)
    kb = D_MODEL // N_DEV

    def body(h_ref, g_ref, w_ref, pw_ref, ps_ref, u_ref, dpre_ref, tok_ref, qm_ref, ext_ref):
        i = pl.program_id(0)
        u = _rms(h_ref[...], g_ref[...]).astype(BF16)
        u_ref[...] = u
        proj = jnp.zeros((tm, D_MIX), F32)
        for d in range(N_DEV):
            proj = proj + _dot(u[:, d * kb:(d + 1) * kb], w_ref[d])
        qm_ref[...] = proj[:, D_TOK:].astype(BF16)
        x = proj[:, :D_TOK]

        @pl.when(i == 0)
        def _():
            ext_ref[0:POOL_HALO, :] = jnp.zeros((POOL_HALO, D_TOK), F32)

        ext_ref[POOL_HALO:, :] = x
        pos = i * tm + lax.broadcasted_iota(jnp.int32, (tm, 1), 0)
        for gi, w in enumerate(POOL_WINDOWS):
            cols = slice(gi * POOL_GROUP, (gi + 1) * POOL_GROUP)
            xs = x[:, cols]
            wsum = xs
            for k in range(1, w):
                wsum = wsum + ext_ref[POOL_HALO - k:POOL_HALO - k + tm, cols]
            cnt = jnp.minimum(pos + 1, w).astype(F32)
            dg = (wsum / cnt - xs).astype(BF16)
            dpre_ref[:, cols] = dg
            yv = _dot(dg, pw_ref[gi].astype(BF16))
            tok_ref[:, cols] = (yv * ps_ref[:, cols]).astype(BF16)
        ext_ref[0:POOL_HALO, :] = x[tm - POOL_HALO:, :]

    return pl.pallas_call(
        body, name=name, grid=(S // tm,),
        in_specs=[_rows(tm, D_MODEL), _full((1, D_MODEL)), _layer((kb, D_MIX), l),
                  _full((len(POOL_WINDOWS), POOL_GROUP, POOL_GROUP)), _full((1, D_TOK))],
        out_specs=[_rows(tm, D_MODEL), _rows(tm, D_TOK), _rows(tm, D_TOK), _rows(tm, D_MEMH)],
        out_shape=[_sds((S, D_MODEL), BF16), _sds((S, D_TOK), BF16), _sds((S, D_TOK), BF16), _sds((S, D_MEMH), BF16)],
        scratch_shapes=[pltpu.VMEM((POOL_HALO + tm, D_TOK), F32)],
        compiler_params=_params(("arbitrary",)),
        interpret=False,
    )(h, g1, w_in, pool_w, pool_scale)


def _pool_bwd(dtok, dpre, pool_w, pool_scale, name):
    S = dtok.shape[0]
    tm = _tile(S, ROW_TILE)
    nt = S // tm
    ng = len(POOL_WINDOWS)

    def body(dtok_ref, dpre_ref, pw_ref, ps_ref, dx_ref, dpw_ref, dps_ref, ext_ref):
        i = pl.program_id(0)
        first = i == 0
        t0 = (nt - 1 - i) * tm
        pos = t0 + lax.broadcasted_iota(jnp.int32, (tm, 1), 0)

        @pl.when(first)
        def _():
            ext_ref[tm:, :] = jnp.zeros((POOL_HALO, D_TOK), F32)

        dps = []
        for gi, w in enumerate(POOL_WINDOWS):
            cols = slice(gi * POOL_GROUP, (gi + 1) * POOL_GROUP)
            dg = dpre_ref[:, cols]
            pw = pw_ref[gi].astype(BF16)
            dt = dtok_ref[:, cols].astype(F32)
            yv = _dot(dg, pw)
            dps.append(jnp.sum(dt * yv, axis=0, keepdims=True))
            dy = (dt * ps_ref[:, cols]).astype(BF16)
            _acc_rows(dpw_ref.at[gi], _dot_tn(dg, dy), first)
            dd = _dot_nt(dy, pw)
            cnt = jnp.minimum(pos + 1, w).astype(F32)
            ext_ref[0:tm, cols] = dd / cnt
            wsum = ext_ref[0:tm, cols]
            for k in range(1, w):
                wsum = wsum + ext_ref[k:k + tm, cols]
            dx_ref[:, cols] = (wsum - dd).astype(BF16)
        _acc_rows(dps_ref, jnp.concatenate(dps, axis=1), first)
        ext_ref[tm:, :] = ext_ref[0:POOL_HALO, :]

    rev = lambda i: (nt - 1 - i, 0)
    return pl.pallas_call(
        body, name=name, grid=(nt,),
        in_specs=[pl.BlockSpec((tm, D_TOK), rev), pl.BlockSpec((tm, D_TOK), rev),
                  _full((ng, POOL_GROUP, POOL_GROUP)), _full((1, D_TOK))],
        out_specs=[pl.BlockSpec((tm, D_TOK), rev), _full((ng, POOL_GROUP, POOL_GROUP)), _full((1, D_TOK))],
        out_shape=[_sds((S, D_TOK), BF16), _sds((ng, POOL_GROUP, POOL_GROUP), F32), _sds((1, D_TOK), F32)],
        scratch_shapes=[pltpu.VMEM((tm + POOL_HALO, D_TOK), F32)],
        compiler_params=_params(("arbitrary",)),
        interpret=False,
    )(dtok, dpre, pool_w, pool_scale)


def _mix_in_sb(h, g1, wt, name):
    S = h.shape[0]
    tm = _tile(S, ROW_TILE)
    cb = 256

    def body(h_ref, g_ref, wt_ref, u_ref, proj_ref):
        u = _rms(h_ref[...], g_ref[...]).astype(BF16)
        u_ref[...] = u
        for c in range(D_SB // cb):
            proj_ref[:, c * cb:(c + 1) * cb] = _dot_nt(u, wt_ref[c * cb:(c + 1) * cb, :]).astype(BF16)

    return pl.pallas_call(
        body, name=name, grid=(S // tm,),
        in_specs=[_rows(tm, D_MODEL), _full((1, D_MODEL)), _full((D_SB, D_MODEL))],
        out_specs=[_rows(tm, D_MODEL), _rows(tm, D_SB)],
        out_shape=[_sds((S, D_MODEL), BF16), _sds((S, D_SB), BF16)],
        compiler_params=_params(("arbitrary",)),
        interpret=False,
    )(h, g1, wt)


def _mix_in_bwd(dho, h, g1, parts, w, mode, l, name):
    S = h.shape[0]
    tm = _tile(S, ROW_TILE)
    widths = [p.shape[1] for p in parts]
    npart = len(parts)
    kb = D_MODEL // N_DEV

    def body(*refs):
        dho_ref, h_ref, g_ref = refs[:3]
        p_refs = refs[3:3 + npart]
        w_ref, dh_ref, dg_ref = refs[3 + npart:]
        first = pl.program_id(0) == 0
        if mode == "pool":
            dproj = jnp.concatenate([p[...] for p in p_refs], axis=1)
            du = jnp.concatenate([_dot_nt(dproj, w_ref[d]) for d in range(N_DEV)], axis=1)
        else:
            du = jnp.zeros((tm, D_MODEL), F32)
            r0 = 0
            for p, wd_ in zip(p_refs, widths):
                du = du + _dot(p[...], w_ref[r0:r0 + wd_, :])
                r0 += wd_
        dhx, dg = _rms_bwd(h_ref[...], g_ref[...], du)
        dh_ref[...] = dho_ref[...] + dhx
        _acc_rows(dg_ref, dg, first)

    w_spec = _layer((kb, D_MIX), l) if mode == "pool" else _full((D_SB, D_MODEL))
    return pl.pallas_call(
        body, name=name, grid=(S // tm,),
        in_specs=[_rows(tm, D_MODEL), _rows(tm, D_MODEL), _full((1, D_MODEL))] + [_rows(tm, wd_) for wd_ in widths] + [w_spec],
        out_specs=[_rows(tm, D_MODEL), _full((1, D_MODEL))],
        out_shape=[_sds((S, D_MODEL), F32), _sds((1, D_MODEL), F32)],
        compiler_params=_params(("arbitrary",)),
        interpret=False,
    )(dho, h, g1, *parts, w)


def _sb_block(qe, kblk, diag, later, tri_later):
    row = lax.broadcasted_iota(jnp.int32, (Q_BLOCK, Q_BLOCK), 0)
    col = lax.broadcasted_iota(jnp.int32, (Q_BLOCK, Q_BLOCK), 1)
    z = _dot_nt(qe, kblk) * ATT_SCALE
    mask = jnp.logical_or(col < row, jnp.logical_not(diag))
    en = jnp.exp(-jnp.abs(z))
    ls = jnp.minimum(z, 0.0) - jnp.log(1.0 + en)
    lf = jnp.where(mask, ls - z, 0.0)
    within = _split_dot(lf, tri_later, 2)
    a = jnp.where(mask, jnp.exp(ls + within + later), 0.0)
    return z, mask, en, a, jnp.sum(lf, axis=1, keepdims=True)


def _sb_alive(later):
    return jnp.max(later) > SB_DEAD_LOG_WEIGHT


def _sb_more(qi, carry):
    return jnp.logical_and(carry[0] <= qi, carry[1])


def _tri(strict):
    row = lax.broadcasted_iota(jnp.int32, (Q_BLOCK, Q_BLOCK), 0)
    col = lax.broadcasted_iota(jnp.int32, (Q_BLOCK, Q_BLOCK), 1)
    return (row > col if strict else row >= col).astype(BF16)


def _sb_heads(lanes):
    lane = lax.broadcasted_iota(jnp.int32, (Q_BLOCK, 128), 1)
    return [(g, (lane >= e * HEAD_DIM) & (lane < (e + 1) * HEAD_DIM)) for g in range(lanes // 128) for e in range(128 // HEAD_DIM)]


def _group(x, g):
    return x[:, g * 128:(g + 1) * 128]


def _masked(hm, x):
    return jnp.where(hm, x, jnp.zeros_like(x))


def _sb_fwd(proj, name):
    S = proj.shape[0]
    nq = S // Q_BLOCK
    W = SB_FWD_LANES
    nrow = D_TOK // W

    def body(q_ref, k_ref, v_ref, o_ref, tok_ref):
        qi = pl.program_id(1)
        heads = _sb_heads(W)
        tri_later = _tri(True)
        q = q_ref[...]
        qes = [_masked(hm, _group(q, g)) for g, hm in heads]

        def step(carry):
            j, _, accs, laters = carry
            off = pl.multiple_of((qi - j) * Q_BLOCK, Q_BLOCK)
            kb = k_ref[pl.ds(off, Q_BLOCK), :]
            vb = v_ref[pl.ds(off, Q_BLOCK), :]
            accs, laters = list(accs), list(laters)
            for h, (g, hm) in enumerate(heads):
                _, _, _, a, bsum = _sb_block(qes[h], _group(kb, g), j == 0, laters[h], tri_later)
                accs[g] = accs[g] + _split_dot(a, _masked(hm, _group(vb, g)), 2)
                laters[h] = laters[h] + bsum
            return j + 1, _sb_alive(functools.reduce(jnp.maximum, laters)), tuple(accs), tuple(laters)

        init = (jnp.int32(0), jnp.bool_(True), (jnp.zeros((Q_BLOCK, 128), F32),) * (W // 128),
                (jnp.zeros((Q_BLOCK, 1), F32),) * len(heads))
        accs = lax.while_loop(functools.partial(_sb_more, qi), step, init)[2]
        for g, acc in enumerate(accs):
            o_ref[:, g * 128:(g + 1) * 128] = acc
            tok_ref[:, g * 128:(g + 1) * 128] = acc.astype(BF16)

    blk = pl.BlockSpec((Q_BLOCK, W), lambda p, i: (i, p))
    return pl.pallas_call(
        body, name=name, grid=(nrow, nq),
        in_specs=[blk, pl.BlockSpec((S, W), lambda p, i: (0, nrow + p)), pl.BlockSpec((S, W), lambda p, i: (0, 2 * nrow + p))],
        out_specs=[blk, blk],
        out_shape=[_sds((S, D_TOK), F32), _sds((S, D_TOK), BF16)],
        compiler_params=_params(("arbitrary", "arbitrary")),
        interpret=False,
    )(proj, proj, proj)


def _sb_bwd(proj, dtok, o32, name):
    S = proj.shape[0]
    nq = S // Q_BLOCK
    W = SB_BWD_LANES
    nrow = D_TOK // W

    def body(q_ref, k_ref, v_ref, do_ref, o_ref, dq_ref, dk_ref, dv_ref, dk_acc, dv_acc):
        qi = pl.program_id(1)

        @pl.when(qi == 0)
        def _():
            dk_acc[...] = jnp.zeros_like(dk_acc)
            dv_acc[...] = jnp.zeros_like(dv_acc)

        heads = _sb_heads(W)
        tri_later = _tri(True)
        tri_from = _tri(False)
        q = q_ref[...]
        do = do_ref[...]
        dov = do.astype(F32) * o_ref[...]
        qes = [_masked(hm, _group(q, g)) for g, hm in heads]
        does = [_masked(hm, _group(do, g)) for g, hm in heads]
        rowtots = [jnp.sum(jnp.where(hm, _group(dov, g), 0.0), axis=1, keepdims=True) for g, hm in heads]

        def step(carry):
            j, _, dqs, laters, seens = carry
            off = pl.multiple_of((qi - j) * Q_BLOCK, Q_BLOCK)
            kb = k_ref[pl.ds(off, Q_BLOCK), :]
            vb = v_ref[pl.ds(off, Q_BLOCK), :]
            dqs, laters, seens = list(dqs), list(laters), list(seens)
            dks = [jnp.zeros((Q_BLOCK, 128), F32)] * (W // 128)
            dvs = [jnp.zeros((Q_BLOCK, 128), F32)] * (W // 128)
            for h, (g, hm) in enumerate(heads):
                kblk = _group(kb, g)
                z, mask, en, a, bsum = _sb_block(qes[h], kblk, j == 0, laters[h], tri_later)
                inv = 1.0 / (1.0 + en)
                beta = jnp.where(z >= 0, 1.0, en) * inv
                omb = jnp.where(z >= 0, en, 1.0) * inv
                dlogw = a * _dot_nt(does[h], _group(vb, g))
                prefix = rowtots[h] - seens[h] - _split_dot(dlogw, tri_from, 2)
                dz = jnp.where(mask, dlogw * omb - beta * prefix, 0.0).astype(BF16)
                dks[g] = dks[g] + _dot_tn(dz, qes[h])
                dvs[g] = dvs[g] + _dot_tn(a.astype(BF16), does[h])
                dqs[g] = dqs[g] + _dot(dz, _masked(hm, kblk))
                laters[h] = laters[h] + bsum
                seens[h] = seens[h] + jnp.sum(dlogw, axis=1, keepdims=True)
            for g in range(W // 128):
                dk_acc[pl.ds(off, Q_BLOCK), g * 128:(g + 1) * 128] += dks[g] * ATT_SCALE
                dv_acc[pl.ds(off, Q_BLOCK), g * 128:(g + 1) * 128] += dvs[g]
            return j + 1, _sb_alive(functools.reduce(jnp.maximum, laters)), tuple(dqs), tuple(laters), tuple(seens)

        zero = (jnp.zeros((Q_BLOCK, 1), F32),) * len(heads)
        init = (jnp.int32(0), jnp.bool_(True), (jnp.zeros((Q_BLOCK, 128), F32),) * (W // 128), zero, zero)
        dqs = lax.while_loop(functools.partial(_sb_more, qi), step, init)[2]
        for g, dq in enumerate(dqs):
            dq_ref[:, g * 128:(g + 1) * 128] = (dq * ATT_SCALE).astype(BF16)

        @pl.when(qi == nq - 1)
        def _():
            dk_ref[...] = dk_acc[...].astype(BF16)
            dv_ref[...] = dv_acc[...].astype(BF16)

    blk = pl.BlockSpec((Q_BLOCK, W), lambda p, i: (i, p))
    col = pl.BlockSpec((S, W), lambda p, i: (0, p))
    return pl.pallas_call(
        body, name=name, grid=(nrow, nq),
        in_specs=[blk, pl.BlockSpec((S, W), lambda p, i: (0, nrow + p)), pl.BlockSpec((S, W), lambda p, i: (0, 2 * nrow + p)),
                  blk, blk],
        out_specs=[blk, col, col],
        out_shape=[_sds((S, D_TOK), BF16), _sds((S, D_TOK), BF16), _sds((S, D_TOK), BF16)],
        scratch_shapes=[pltpu.VMEM((S, W), F32), pltpu.VMEM((S, W), F32)],
        compiler_params=_params(("arbitrary", "arbitrary")),
        interpret=False,
    )(proj, proj, proj, dtok, o32)


def _mem_kv_fwd(mem, g_mem, w_kv, l, name):
    lm = mem.shape[0]
    kb = D_MODEL // N_DEV

    def body(mem_ref, g_ref, w_ref, mn_ref, km_ref, vm_ref):
        mn = _rms(mem_ref[...], g_ref[...]).astype(BF16)
        mn_ref[...] = mn
        kv = jnp.zeros((lm, 2 * D_MEMH), F32)
        for d in range(N_DEV):
            kv = kv + _dot(mn[:, d * kb:(d + 1) * kb], w_ref[d])
        km_ref[...] = kv[:, :D_MEMH].astype(BF16)
        vm_ref[...] = kv[:, D_MEMH:].astype(BF16)

    return pl.pallas_call(
        body, name=name, grid=(1,),
        in_specs=[_full((lm, D_MODEL)), _full((1, D_MODEL)), _layer((kb, 2 * D_MEMH), l)],
        out_specs=[_full((lm, D_MODEL)), _full((lm, D_MEMH)), _full((lm, D_MEMH))],
        out_shape=[_sds((lm, D_MODEL), BF16), _sds((lm, D_MEMH), BF16), _sds((lm, D_MEMH), BF16)],
        compiler_params=_params(("arbitrary",)),
        interpret=False,
    )(mem, g_mem, w_kv)


def _mem_kv_bwd(dkm, dvm, mem, g_mem, mem_n, w_kv, into, l, name):
    lm = mem.shape[0]
    kb = D_MODEL // N_DEV

    def body(dkm_ref, dvm_ref, mem_ref, g_ref, mn_ref, w_ref, into_ref, dw_ref, dg_ref):
        dkv = jnp.concatenate([dkm_ref[...], dvm_ref[...]], axis=1).astype(BF16)
        dw = _dot_tn(mn_ref[...], dkv)
        for d in range(N_DEV):
            dw_ref[d] = dw[d * kb:(d + 1) * kb, :].astype(GRAD_WIRE)
        dmn = jnp.concatenate([_dot_nt(dkv, w_ref[d]) for d in range(N_DEV)], axis=1)
        _, dg = _rms_bwd(mem_ref[...], g_ref[...], dmn)
        dg_ref[...] = dg

    n_layers = into.shape[1]
    return pl.pallas_call(
        body, name=name, grid=(1,),
        in_specs=[_full((lm, D_MEMH)), _full((lm, D_MEMH)), _full((lm, D_MODEL)), _full((1, D_MODEL)),
                  _full((lm, D_MODEL)), _layer((kb, 2 * D_MEMH), l), ANY],
        out_specs=[_layer((kb, 2 * D_MEMH), l), _full((1, D_MODEL))],
        out_shape=[_sds((N_DEV, n_layers, kb, 2 * D_MEMH), GRAD_WIRE), _sds((1, D_MODEL), F32)],
        input_output_aliases={6: 0},
        compiler_params=_params(("arbitrary",)),
        interpret=False,
    )(dkm, dvm, mem, g_mem, mem_n, w_kv, into)


def _mem_heads(tm):
    lane = lax.broadcasted_iota(jnp.int32, (tm, D_MEMH), 1)
    return [(lane >= e * HEAD_DIM) & (lane < (e + 1) * HEAD_DIM) for e in range(D_MEMH // HEAD_DIM)]


def _softmax(s):
    m = jnp.max(s, axis=-1, keepdims=True)
    p = jnp.exp(s - m)
    return p / jnp.sum(p, axis=-1, keepdims=True)


def _mix_out_fwd(h, tok, qm, qm_col, km, vm, w_out, gpost, l, name):
    S = h.shape[0]
    tm = _tile(S, ROW_TILE)
    lm = km.shape[0]
    nb = D_MODEL // N_DEV

    def body(h_ref, tok_ref, qm_ref, km_ref, vm_ref, w_ref, g_ref, hn_ref, mo_ref, mix_ref):
        qv = qm_ref[...]
        kv, vv = km_ref[...], vm_ref[...]
        mo = jnp.zeros((tm, D_MEMH), F32)
        for hm, hk in zip(_mem_heads(tm), _mem_heads(lm)):
            qe = jnp.where(hm, qv, jnp.zeros_like(qv))
            p = _softmax(_dot_nt(qe, kv) * ATT_SCALE)
            mo = mo + _dot(p.astype(BF16), jnp.where(hk, vv, jnp.zeros_like(vv)))
        mob = mo.astype(BF16)
        mo_ref[...] = mob
        tv = tok_ref[...]
        mix = jnp.concatenate(
            [_dot(tv, w_ref[d, 0:D_TOK, :]) + _dot(mob, w_ref[d, D_TOK:D_MIX, :]) for d in range(N_DEV)], axis=1)
        mix_ref[...] = mix
        hn_ref[...] = h_ref[...] + _rms(mix, g_ref[...])

    return pl.pallas_call(
        body, name=name, grid=(S // tm,),
        in_specs=[_rows(tm, D_MODEL), _rows(tm, D_TOK), _rows(tm, D_MEMH, qm_col), _full((lm, D_MEMH)), _full((lm, D_MEMH)),
                  _layer((D_MIX, nb), l), _full((1, D_MODEL))],
        out_specs=[_rows(tm, D_MODEL), _rows(tm, D_MEMH), _rows(tm, D_MODEL)],
        out_shape=[_sds((S, D_MODEL), F32), _sds((S, D_MEMH), BF16), _sds((S, D_MODEL), F32)],
        compiler_params=_params(("arbitrary",)),
        interpret=False,
    )(h, tok, qm, km, vm, w_out, gpost)


def _mix_out_bwd(dho, mix, qm, qm_col, km, vm, w_out, gpost, l, name):
    S = dho.shape[0]
    tm = _tile(S, ROW_TILE)
    lm = km.shape[0]
    nb = D_MODEL // N_DEV

    def body(dho_ref, mix_ref, qm_ref, km_ref, vm_ref, w_ref, g_ref,
             dmix_ref, dtok_ref, dqm_ref, dkm_ref, dvm_ref, dg_ref):
        first = pl.program_id(0) == 0
        dmx, dg = _rms_bwd(mix_ref[...], g_ref[...], dho_ref[...])
        dmb = dmx.astype(BF16)
        dmix_ref[...] = dmb
        _acc_rows(dg_ref, dg, first)
        dcat = jnp.zeros((tm, D_MIX), F32)
        for d in range(N_DEV):
            dcat = dcat + _dot_nt(dmb[:, d * nb:(d + 1) * nb], w_ref[d])
        dtok_ref[...] = dcat[:, :D_TOK].astype(BF16)
        dmo = dcat[:, D_TOK:].astype(BF16)
        qv = qm_ref[...]
        kv, vv = km_ref[...], vm_ref[...]
        dq = jnp.zeros((tm, D_MEMH), F32)
        dk = jnp.zeros((lm, D_MEMH), F32)
        dv = jnp.zeros((lm, D_MEMH), F32)
        for hm, hk in zip(_mem_heads(tm), _mem_heads(lm)):
            qe = jnp.where(hm, qv, jnp.zeros_like(qv))
            dme = jnp.where(hm, dmo, jnp.zeros_like(dmo))
            p = _softmax(_dot_nt(qe, kv) * ATT_SCALE)
            dp = _dot_nt(dme, vv)
            ds = (p * (dp - jnp.sum(p * dp, axis=-1, keepdims=True))).astype(BF16)
            dq = dq + _dot(ds, jnp.where(hk, kv, jnp.zeros_like(kv)))
            dk = dk + _dot_tn(ds, qe)
            dv = dv + _dot_tn(p.astype(BF16), dme)
        dqm_ref[...] = (dq * ATT_SCALE).astype(BF16)
        _acc_rows(dkm_ref, dk * ATT_SCALE, first)
        _acc_rows(dvm_ref, dv, first)

    return pl.pallas_call(
        body, name=name, grid=(S // tm,),
        in_specs=[_rows(tm, D_MODEL), _rows(tm, D_MODEL), _rows(tm, D_MEMH, qm_col), _full((lm, D_MEMH)), _full((lm, D_MEMH)),
                  _layer((D_MIX, nb), l), _full((1, D_MODEL))],
        out_specs=[_rows(tm, D_MODEL), _rows(tm, D_TOK), _rows(tm, D_MEMH), _full((lm, D_MEMH)), _full((lm, D_MEMH)),
                   _full((1, D_MODEL))],
        out_shape=[_sds((S, D_MODEL), BF16), _sds((S, D_TOK), BF16), _sds((S, D_MEMH), BF16),
                   _sds((lm, D_MEMH), F32), _sds((lm, D_MEMH), F32), _sds((1, D_MODEL), F32)],
        compiler_params=_params(("arbitrary",)),
        interpret=False,
    )(dho, mix, qm, km, vm, w_out, gpost)


def _loss_head(y, target, name):
    S = y.shape[0]
    tm = _tile(S, ROW_TILE)
    nt = S // tm

    def body(y_ref, t_ref, dy_ref, loss_ref, acc_ref):
        i = pl.program_id(0)
        e = y_ref[...] - t_ref[...]
        dy_ref[...] = e * (1.0 / D_MODEL)
        _acc_rows(acc_ref, jnp.sum(e * e, axis=0, keepdims=True), i == 0)

        @pl.when(i == nt - 1)
        def _():
            tot = jnp.sum(acc_ref[...], axis=1, keepdims=True) * (0.5 / D_MODEL)
            loss_ref[...] = jnp.broadcast_to(tot, (1, 128))

    return pl.pallas_call(
        body, name=name, grid=(nt,),
        in_specs=[_rows(tm, D_MODEL), _rows(tm, D_MODEL)],
        out_specs=[_rows(tm, D_MODEL), _full((1, 128))],
        out_shape=[_sds((S, D_MODEL), F32), _sds((1, 128), F32)],
        scratch_shapes=[pltpu.VMEM((1, D_MODEL), F32)],
        compiler_params=_params(("arbitrary",)),
        interpret=False,
    )(y, target)


def _adamw(recv, w, m, v, name):
    R, C = w.shape
    tr = R if R * C <= ADAM_TILE_ELEMS else _tile(R, ADAM_TILE_ELEMS // C)
    c1 = 1.0 - ADAM_B1 ** ADAM_STEP
    c2 = 1.0 - ADAM_B2 ** ADAM_STEP

    def body(r_ref, w_ref, m_ref, v_ref, g_ref, d_ref, nm_ref, nv_ref):
        g = r_ref[0].astype(F32)
        for s in range(1, N_DEV):
            g = g + r_ref[s].astype(F32)
        g_ref[...] = g
        nm = ADAM_B1 * m_ref[...] + (1.0 - ADAM_B1) * g
        nv = ADAM_B2 * v_ref[...] + (1.0 - ADAM_B2) * (g * g)
        nm_ref[...] = nm
        nv_ref[...] = nv
        d_ref[...] = -ADAM_LR * ((nm / c1) / (jnp.sqrt(nv / c2) + ADAM_EPS) + ADAM_WD * w_ref[...])

    t = pl.BlockSpec((tr, C), lambda i: (i, 0))
    return pl.pallas_call(
        body, name=name, grid=(R // tr,),
        in_specs=[pl.BlockSpec((N_DEV, tr, C), lambda i: (0, i, 0)), t, t, t],
        out_specs=[t, t, t, t],
        out_shape=[_sds((R, C), F32)] * 4,
        compiler_params=_params(("arbitrary",)),
        interpret=False,
    )(recv, w, m, v)


def _step(p, opt_m, opt_v, x, mem, target):
    S = x.shape[0]
    bf = lambda a: a.astype(BF16)

    wsb_t = jnp.swapaxes(p["w_in_sb"], 1, 2)
    names = ["ffn1_gate", "ffn1_up", "ffn1_down", "ffn2_gate", "ffn2_up", "ffn2_down", "w_in_pool", "w_mem_kv", "w_out"]
    shards = [bf(p[n]) for n in names] + [bf(wsb_t[0]), bf(wsb_t[1]), p["g_pre"], p["g_post"]]
    gathered = _all_gather(shards)
    gw = dict(zip(names, gathered[:len(names)]))
    wsb = [g.reshape(D_SB, D_MODEL) for g in gathered[len(names):len(names) + 2]]
    unshard = lambda g: jnp.transpose(g, (1, 2, 0, 3)).reshape(DEPTH, 3, D_MODEL)
    g_pre, g_post = unshard(gathered[-2]), unshard(gathered[-1])
    g_mem = p["g_mem"]
    row = lambda a: a.reshape(1, -1)

    saved = []
    h = x
    for i in range(DEPTH):
        j = i // 2
        st = {"h0": h}
        h, st["n1"], st["gate1"], st["up1"], st["act1"], st["f1"] = _ffn_fwd(
            h, row(g_pre[i, 0]), row(g_post[i, 0]), gw["ffn1_gate"], gw["ffn1_up"], gw["ffn1_down"], i, f"ffn1_fwd_{i}")
        st["h1"] = h
        if i % 2 == 0:
            st["u"], st["dpre"], st["tok"], st["qm"] = _mix_in_pool(
                h, row(g_pre[i, 1]), gw["w_in_pool"], p["pool_w"][j], row(p["pool_scale"][j]), j, f"mix_in_pool_{i}")
            qm, qm_col = st["qm"], 0
        else:
            st["u"], st["proj"] = _mix_in_sb(h, row(g_pre[i, 1]), wsb[j], f"mix_in_sb_{i}")
            st["o32"], st["tok"] = _sb_fwd(st["proj"], f"sb_fwd_{i}")
            qm, qm_col = st["proj"], 3 * D_TOK // D_MEMH
        st["mem_n"], st["km"], st["vm"] = _mem_kv_fwd(mem, row(g_mem[i]), gw["w_mem_kv"], i, f"mem_kv_fwd_{i}")
        h, st["mo"], st["mix"] = _mix_out_fwd(h, st["tok"], qm, qm_col, st["km"], st["vm"], gw["w_out"], row(g_post[i, 1]), i,
                                              f"mix_out_fwd_{i}")
        st["h2"] = h
        h, st["n2"], st["gate2"], st["up2"], st["act2"], st["f2"] = _ffn_fwd(
            h, row(g_pre[i, 2]), row(g_post[i, 2]), gw["ffn2_gate"], gw["ffn2_up"], gw["ffn2_down"], i, f"ffn2_fwd_{i}")
        saved.append(st)

    dh, loss_part = _loss_head(h, target, "loss_head")

    grads = {n: None for n in names}
    dwsb = [None, None]
    dg_pre = [[None] * 3 for _ in range(DEPTH)]
    dg_post = [[None] * 3 for _ in range(DEPTH)]
    dg_mem = [None] * DEPTH
    dpool_w = [None, None]
    dpool_scale = [None, None]

    def ffn_backward(dh, st, i, which, hkey, slot):
        sfx = str(which)
        dh, df, dgate, dup, dgpre, dgpost = _ffn_bwd(
            dh, st[hkey], st["f" + sfx], st["gate" + sfx], st["up" + sfx], row(g_pre[i, slot]), row(g_post[i, slot]),
            gw[f"ffn{sfx}_gate"], gw[f"ffn{sfx}_up"], gw[f"ffn{sfx}_down"], i, f"ffn{sfx}_bwd_{i}")
        dg_pre[i][slot], dg_post[i][slot] = dgpre, dgpost
        for wn, a, b, split in ((f"ffn{sfx}_gate", st["n" + sfx], dgate, "cols"), (f"ffn{sfx}_up", st["n" + sfx], dup, "cols"),
                                (f"ffn{sfx}_down", st["act" + sfx], df, "rows")):
            grads[wn] = _wgrad([a], [b], split, f"wgrad_{wn}_{i}", into=grads[wn], l=i, n_layers=DEPTH)
        return dh

    for i in reversed(range(DEPTH)):
        j = i // 2
        st = saved[i]
        dh = ffn_backward(dh, st, i, 2, "h2", 2)
        if i % 2 == 0:
            qm, qm_col = st["qm"], 0
        else:
            qm, qm_col = st["proj"], 3 * D_TOK // D_MEMH
        dmix, dtok, dqm, dkm, dvm, dg_post[i][1] = _mix_out_bwd(
            dh, st["mix"], qm, qm_col, st["km"], st["vm"], gw["w_out"], row(g_post[i, 1]), i, f"mix_out_bwd_{i}")
        grads["w_out"] = _wgrad([st["tok"], st["mo"]], [dmix], "cols", f"wgrad_w_out_{i}", into=grads["w_out"], l=i, n_layers=DEPTH)
        if grads["w_mem_kv"] is None:
            grads["w_mem_kv"] = lax.empty((N_DEV, DEPTH, D_MODEL // N_DEV, 2 * D_MEMH), GRAD_WIRE)
        grads["w_mem_kv"], dg_mem[i] = _mem_kv_bwd(dkm, dvm, mem, row(g_mem[i]), st["mem_n"], gw["w_mem_kv"], grads["w_mem_kv"], i,
                                                   f"mem_kv_bwd_{i}")
        if i % 2 == 0:
            dx, dpool_w[j], dpool_scale[j] = _pool_bwd(dtok, st["dpre"], p["pool_w"][j], row(p["pool_scale"][j]), f"pool_bwd_{i}")
            parts = [dx, dqm]
            dh, dg_pre[i][1] = _mix_in_bwd(dh, st["h1"], row(g_pre[i, 1]), parts, gw["w_in_pool"], "pool", j, f"mix_in_bwd_{i}")
            grads["w_in_pool"] = _wgrad([st["u"]], parts, "rows", f"wgrad_w_in_pool_{i}", into=grads["w_in_pool"], l=j, n_layers=2)
        else:
            dq, dk, dv = _sb_bwd(st["proj"], dtok, st["o32"], f"sb_bwd_{i}")
            parts = [dq, dk, dv, dqm]
            dh, dg_pre[i][1] = _mix_in_bwd(dh, st["h1"], row(g_pre[i, 1]), parts, wsb[j], "sb", j, f"mix_in_bwd_{i}")
            dwsb[j] = _wgrad(parts, [st["u"]], "rows", f"wgrad_w_in_sb_{i}")[:, 0]
        dh = ffn_backward(dh, st, i, 1, "h0", 0)
    grad_x = dh

    shard8 = lambda rows_: jnp.transpose(jnp.stack([jnp.concatenate(r, axis=0) for r in rows_]).reshape(DEPTH, 3, N_DEV, -1),
                                         (2, 0, 1, 3))
    blocked = [grads[n] for n in names] + dwsb + [shard8(dg_pre), shard8(dg_post)]
    whole = [jnp.concatenate(dg_mem, axis=0), jnp.stack(dpool_w), jnp.concatenate(dpool_scale, axis=0), loss_part]
    recv = _exchange(blocked, whole)
    rb, rw = recv[:len(blocked)], recv[len(blocked):]

    out = {}

    def update(name, r, w, m, v, shape):
        flat = lambda a: a.reshape(-1, a.shape[-1])
        res = _adamw(r.reshape((N_DEV,) + flat(w).shape), flat(w), flat(m), flat(v), f"adamw_{name}")
        return [a.reshape(shape) for a in res]

    for n, r in zip(names, rb):
        out[n] = update(n, r, p[n], opt_m[n], opt_v[n], p[n].shape)
    tsb = lambda a: jnp.swapaxes(a, 1, 2)
    per_layer = [update(f"w_in_sb_{j}", rb[len(names) + j], wsb_t[j], tsb(opt_m["w_in_sb"])[j], tsb(opt_v["w_in_sb"])[j],
                        wsb_t[j].shape) for j in range(2)]
    out["w_in_sb"] = [tsb(jnp.stack([per_layer[0][q], per_layer[1][q]])) for q in range(4)]
    out["g_pre"] = update("g_pre", rb[-2], p["g_pre"], opt_m["g_pre"], opt_v["g_pre"], p["g_pre"].shape)
    out["g_post"] = update("g_post", rb[-1], p["g_post"], opt_m["g_post"], opt_v["g_post"], p["g_post"].shape)
    for n, r in zip(["g_mem", "pool_w", "pool_scale"], rw[:3]):
        out[n] = update(n, r, p[n], opt_m[n], opt_v[n], p[n].shape)
    loss = jnp.sum(rw[3][:, 0, 0])
    return loss, grad_x, out


WEIGHTS = ["g_pre", "g_post", "g_mem", "ffn1_gate", "ffn1_up", "ffn1_down", "ffn2_gate", "ffn2_up", "ffn2_down",
           "w_in_pool", "pool_w", "pool_scale", "w_in_sb", "w_mem_kv", "w_out"]


def kernel(x, mem, g_pre, g_post, g_mem, ffn1_gate, ffn1_up, ffn1_down, ffn2_gate, ffn2_up, ffn2_down, w_in_pool, pool_w, pool_scale, w_in_sb, w_mem_kv, w_out, loss_target, m_g_pre, m_g_post, m_g_mem, m_ffn1_gate, m_ffn1_up, m_ffn1_down, m_ffn2_gate, m_ffn2_up, m_ffn2_down, m_w_in_pool, m_pool_w, m_pool_scale, m_w_in_sb, m_w_mem_kv, m_w_out, v_g_pre, v_g_post, v_g_mem, v_ffn1_gate, v_ffn1_up, v_ffn1_down, v_ffn2_gate, v_ffn2_up, v_ffn2_down, v_w_in_pool, v_pool_w, v_pool_scale, v_w_in_sb, v_w_mem_kv, v_w_out):
    given = dict(locals())
    p = {n: given[n] for n in WEIGHTS}
    opt_m = {n: given["m_" + n] for n in WEIGHTS}
    opt_v = {n: given["v_" + n] for n in WEIGHTS}
    loss, grad_x, out = _step(p, opt_m, opt_v, x[0], mem[0], loss_target[0])
    res = [loss, grad_x[None]]
    for q in range(4):
        res += [out[n][q] for n in WEIGHTS]
    return tuple(res)
```

```python
import functools
from typing import NamedTuple, Optional

import jax
import jax.numpy as jnp
from jax import lax
from jax.experimental import pallas as pl
from jax.experimental.pallas import tpu as pltpu

F32 = jnp.float32
BF16 = jnp.bfloat16
GRAD_WIRE = jnp.bfloat16

N_DEV = 8
DEPTH = 4
D_MODEL = 1024
D_FF = 2048
D_TOK = 512
D_MEMH = 256
D_MIX = D_TOK + D_MEMH
D_SB = 3 * D_TOK + D_MEMH
HEAD_DIM = 64
Q_BLOCK = 128
POOL_WINDOWS = (2, 4, 8, 16)
POOL_GROUP = 128
POOL_HALO = 16
EPS = 1e-6
ATT_SCALE = HEAD_DIM ** -0.5
SB_DEAD_LOG_WEIGHT = -110.0
SB_FWD_LANES = 256
SB_BWD_LANES = 256

ADAM_LR = 0.001
ADAM_B1 = 0.9
ADAM_B2 = 0.999
ADAM_EPS = 1e-08
ADAM_WD = 0.01
ADAM_STEP = 10

VMEM_LIMIT_BYTES = 56 * 1024 * 1024
ROW_TILE = 256
WGRAD_TILE = 512
ADAM_TILE_ELEMS = 128 * 1024

MESH = pl.DeviceIdType.MESH
ANY = pl.BlockSpec(memory_space=pl.ANY)


def _tile(n, pref):
    t = 1 << (pref.bit_length() - 1)
    while n % t:
        t //= 2
    return t


def _dot(a, b):
    return jnp.dot(a, b, preferred_element_type=F32)


def _dot_nt(a, b):
    return lax.dot_general(a, b, (((1,), (1,)), ((), ())), preferred_element_type=F32)


def _dot_tn(a, b):
    return lax.dot_general(a, b, (((0,), (0,)), ((), ())), preferred_element_type=F32)


def _split_dot(x, m, terms):
    out = None
    rest = x
    for _ in range(terms):
        part = rest.astype(BF16)
        rest = rest - part.astype(F32)
        d = _dot(part, m)
        out = d if out is None else out + d
    return out


def _rms(x, g):
    r = lax.rsqrt(jnp.mean(x * x, axis=-1, keepdims=True) + EPS)
    return x * r * g


def _rms_bwd(x, g, dy):
    r = lax.rsqrt(jnp.mean(x * x, axis=-1, keepdims=True) + EPS)
    xh = x * r
    gdy = g * dy
    dx = r * (gdy - xh * jnp.mean(gdy * xh, axis=-1, keepdims=True))
    return dx, jnp.sum(dy * xh, axis=0, keepdims=True)


def _acc_rows(ref, val, first):
    @pl.when(first)
    def _():
        ref[...] = val

    @pl.when(jnp.logical_not(first))
    def _():
        ref[...] += val


def _params(sem=None):
    return pltpu.CompilerParams(dimension_semantics=sem, vmem_limit_bytes=VMEM_LIMIT_BYTES)


def _sds(shape, dtype):
    return jax.ShapeDtypeStruct(shape, dtype)


def _rows(tm, width, col=0):
    return pl.BlockSpec((tm, width), lambda i: (i, col))


def _full(shape):
    nd = len(shape)
    return pl.BlockSpec(shape, lambda *_: (0,) * nd)


def _peers():
    x, y, c = lax.axis_index("x"), lax.axis_index("y"), lax.axis_index("c")
    peers = []
    for k in range(1, N_DEV):
        px = 1 - x if k & 4 else x
        py = 1 - y if k & 2 else y
        pc = 1 - c if k & 1 else c
        peers.append(((px, py, pc), 4 * px + 2 * py + pc))
    return 4 * x + 2 * y + c, peers


class _Xfer(NamedTuple):
    src: jax.Array
    layer: Optional[int] = None
    scatter: bool = False

    @property
    def landing(self):
        block = self.src.shape if self.layer is None and not self.scatter else self.src.shape[1:]
        return _sds((N_DEV,) + block, self.src.dtype)


def _comm_copies(xfers, src_refs, dst_refs, send_sems, recv_sems, local_sems):
    me, peers = _peers()

    def src(t, to):
        ref = src_refs[t] if xfers[t].layer is None else src_refs[t].at[xfers[t].layer]
        return ref.at[to] if xfers[t].scatter else ref

    copies = [pltpu.make_async_copy(src(t, me), dst_refs[t].at[me], local_sems.at[t]) for t in range(len(xfers))]
    for k, (dev, idx) in enumerate(peers):
        for t in range(len(xfers)):
            copies.append(pltpu.make_async_remote_copy(
                src_ref=src(t, idx), dst_ref=dst_refs[t].at[me], send_sem=send_sems.at[t, k], recv_sem=recv_sems.at[t, k],
                device_id=dev, device_id_type=MESH))
    return copies


def _pcall(body, *, name, grid, in_specs, out_specs, out_shape, args, scratch=(), ride=()):
    n_in, n_out, n_scr, nx = len(in_specs), len(out_specs), len(scratch), len(ride)
    params = _params(("arbitrary",) * len(grid))
    if not ride:
        res = pl.pallas_call(body, name=name, grid=grid, in_specs=list(in_specs), out_specs=list(out_specs),
                             out_shape=list(out_shape), scratch_shapes=list(scratch), compiler_params=params,
                             interpret=False)(*args)
        return list(res), []
    (steps,) = grid

    def riding(*refs):
        o0 = n_in + nx
        s0 = o0 + n_out + nx
        comm = (ride, refs[n_in:o0], refs[o0 + n_out:s0], *refs[s0 + n_scr:])
        i = pl.program_id(0)

        @pl.when(i == 0)
        def _():
            for cp in _comm_copies(*comm):
                cp.start()

        body(*refs[:n_in], *refs[o0:o0 + n_out], *refs[s0:s0 + n_scr])

        @pl.when(i == steps - 1)
        def _():
            for cp in _comm_copies(*comm):
                cp.wait()

    sems = [pltpu.SemaphoreType.DMA((nx, N_DEV - 1)), pltpu.SemaphoreType.DMA((nx, N_DEV - 1)), pltpu.SemaphoreType.DMA((nx,))]
    res = pl.pallas_call(riding, name=name, grid=grid, in_specs=list(in_specs) + [ANY] * nx,
                         out_specs=list(out_specs) + [ANY] * nx, out_shape=list(out_shape) + [t.landing for t in ride],
                         scratch_shapes=list(scratch) + sems, compiler_params=params,
                         interpret=False)(*args, *[t.src for t in ride])
    return list(res[:n_out]), list(res[n_out:])


def _comm_call(xfers, name):
    return _pcall(lambda: None, name=name, grid=(1,), in_specs=[], out_specs=[], out_shape=[], args=[], ride=xfers)[1]


def _ffn_fwd(h, gpre, gpost, wg, wu, wd, name, ride=()):
    S = h.shape[0]
    tm = _tile(S, ROW_TILE)
    nb = D_FF // N_DEV

    def body(h_ref, gpre_ref, gpost_ref, wg_ref, wu_ref, wd_ref, hn_ref, n_ref, gate_ref, up_ref, act_ref, f_ref):
        hv = h_ref[...]
        n = _rms(hv, gpre_ref[...]).astype(BF16)
        n_ref[...] = n
        f = jnp.zeros((tm, D_MODEL), F32)
        for d in range(N_DEV):
            cols = slice(d * nb, (d + 1) * nb)
            g = _dot(n, wg_ref[d])
            u = _dot(n, wu_ref[d])
            a = (g * jax.nn.sigmoid(g) * u).astype(BF16)
            gate_ref[:, cols] = g.astype(BF16)
            up_ref[:, cols] = u.astype(BF16)
            act_ref[:, cols] = a
            f = f + _dot(a, wd_ref[d])
        f_ref[...] = f
        hn_ref[...] = hv + 0.5 * _rms(f, gpost_ref[...])

    return _pcall(
        body, name=name, grid=(S // tm,),
        in_specs=[_rows(tm, D_MODEL), _full((1, D_MODEL)), _full((1, D_MODEL)),
                  _full((N_DEV, D_MODEL, nb)), _full((N_DEV, D_MODEL, nb)), _full((N_DEV, nb, D_MODEL))],
        out_specs=[_rows(tm, D_MODEL), _rows(tm, D_MODEL), _rows(tm, D_FF), _rows(tm, D_FF), _rows(tm, D_FF),
                   _rows(tm, D_MODEL)],
        out_shape=[_sds((S, D_MODEL), F32), _sds((S, D_MODEL), BF16), _sds((S, D_FF), BF16), _sds((S, D_FF), BF16),
                   _sds((S, D_FF), BF16), _sds((S, D_MODEL), F32)],
        args=(h, gpre, gpost, wg, wu, wd), ride=ride)


def _ffn_bwd(dho, h, f, gate, up, gpre, gpost, wg, wu, wd, name, ride=()):
    S = h.shape[0]
    tm = _tile(S, ROW_TILE)
    nb = D_FF // N_DEV

    def body(dho_ref, h_ref, f_ref, gate_ref, up_ref, gpre_ref, gpost_ref, wg_ref, wu_ref, wd_ref,
             dh_ref, df_ref, dgate_ref, dup_ref, dgpre_ref, dgpost_ref):
        first = pl.program_id(0) == 0
        dho_v = dho_ref[...]
        dfx, dgpost = _rms_bwd(f_ref[...], gpost_ref[...], 0.5 * dho_v)
        dfb = dfx.astype(BF16)
        df_ref[...] = dfb
        dn = jnp.zeros((tm, D_MODEL), F32)
        for d in range(N_DEV):
            cols = slice(d * nb, (d + 1) * nb)
            dact = _dot_nt(dfb, wd_ref[d])
            g = gate_ref[:, cols].astype(F32)
            u = up_ref[:, cols].astype(F32)
            s = jax.nn.sigmoid(g)
            dg = (dact * u * (s * (1.0 + g * (1.0 - s)))).astype(BF16)
            du = (dact * (g * s)).astype(BF16)
            dgate_ref[:, cols] = dg
            dup_ref[:, cols] = du
            dn = dn + _dot_nt(dg, wg_ref[d]) + _dot_nt(du, wu_ref[d])
        dhx, dgpre = _rms_bwd(h_ref[...], gpre_ref[...], dn)
        dh_ref[...] = dho_v + dhx
        _acc_rows(dgpre_ref, dgpre, first)
        _acc_rows(dgpost_ref, dgpost, first)

    return _pcall(
        body, name=name, grid=(S // tm,),
        in_specs=[_rows(tm, D_MODEL), _rows(tm, D_MODEL), _rows(tm, D_MODEL), _rows(tm, D_FF), _rows(tm, D_FF),
                  _full((1, D_MODEL)), _full((1, D_MODEL)),
                  _full((N_DEV, D_MODEL, nb)), _full((N_DEV, D_MODEL, nb)), _full((N_DEV, nb, D_MODEL))],
        out_specs=[_rows(tm, D_MODEL), _rows(tm, D_MODEL), _rows(tm, D_FF), _rows(tm, D_FF),
                   _full((1, D_MODEL)), _full((1, D_MODEL))],
        out_shape=[_sds((S, D_MODEL), F32), _sds((S, D_MODEL), BF16), _sds((S, D_FF), BF16), _sds((S, D_FF), BF16),
                   _sds((1, D_MODEL), F32), _sds((1, D_MODEL), F32)],
        args=(dho, h, f, gate, up, gpre, gpost, wg, wu, wd), ride=ride)


def _wgrad(a_parts, b_parts, split, name, ride=()):
    S = a_parts[0].shape[0]
    bk = _tile(S, WGRAD_TILE)
    ms = [a.shape[1] for a in a_parts]
    ns = [b.shape[1] for b in b_parts]
    M, N = sum(ms), sum(ns)
    na, nbp = len(a_parts), len(b_parts)
    blk = (M // N_DEV, N) if split == "rows" else (M, N // N_DEV)
    steps = S // bk

    def body(*refs):
        a_refs, b_refs = refs[:na], refs[na:na + nbp]
        out_ref, acc_ref = refs[-2], refs[-1]
        k = pl.program_id(0)

        @pl.when(k == 0)
        def _():
            acc_ref[...] = jnp.zeros_like(acc_ref)

        r0 = 0
        for ai in range(na):
            av = a_refs[ai][...]
            c0 = 0
            for bi in range(nbp):
                acc_ref[r0:r0 + ms[ai], c0:c0 + ns[bi]] += _dot_tn(av, b_refs[bi][...])
                c0 += ns[bi]
            r0 += ms[ai]

        @pl.when(k == steps - 1)
        def _():
            for d in range(N_DEV):
                if split == "rows":
                    out_ref[d] = acc_ref[d * blk[0]:(d + 1) * blk[0], :].astype(GRAD_WIRE)
                else:
                    out_ref[d] = acc_ref[:, d * blk[1]:(d + 1) * blk[1]].astype(GRAD_WIRE)

    in_specs = [pl.BlockSpec((bk, m), lambda k: (k, 0)) for m in ms] + [pl.BlockSpec((bk, n), lambda k: (k, 0)) for n in ns]
    (out,), landed = _pcall(
        body, name=name, grid=(steps,), in_specs=in_specs, out_specs=[_full((N_DEV,) + blk)],
        out_shape=[_sds((N_DEV,) + blk, GRAD_WIRE)], scratch=[pltpu.VMEM((M, N), F32)],
        args=list(a_parts) + list(b_parts), ride=ride)
    return out, landed


def _mix_in_pool(h, g1, w_in, pool_w, pool_scale, name):
    S = h.shape[0]
    tm = _tile(S, ROW_TILE)
    kb = D_MODEL // N_DEV

    def body(h_ref, g_ref, w_ref, pw_ref, ps_ref, u_ref, dpre_ref, tok_ref, qm_ref, ext_ref):
        i = pl.program_id(0)
        u = _rms(h_ref[...], g_ref[...]).astype(BF16)
        u_ref[...] = u
        proj = jnp.zeros((tm, D_MIX), F32)
        for d in range(N_DEV):
            proj = proj + _dot(u[:, d * kb:(d + 1) * kb], w_ref[d])
        qm_ref[...] = proj[:, D_TOK:].astype(BF16)
        x = proj[:, :D_TOK]

        @pl.when(i == 0)
        def _():
            ext_ref[0:POOL_HALO, :] = jnp.zeros((POOL_HALO, D_TOK), F32)

        ext_ref[POOL_HALO:, :] = x
        pos = i * tm + lax.broadcasted_iota(jnp.int32, (tm, 1), 0)
        for gi, w in enumerate(POOL_WINDOWS):
            cols = slice(gi * POOL_GROUP, (gi + 1) * POOL_GROUP)
            xs = x[:, cols]
            wsum = xs
            for k in range(1, w):
                wsum = wsum + ext_ref[POOL_HALO - k:POOL_HALO - k + tm, cols]
            cnt = jnp.minimum(pos + 1, w).astype(F32)
            dg = (wsum / cnt - xs).astype(BF16)
            dpre_ref[:, cols] = dg
            yv = _dot(dg, pw_ref[gi].astype(BF16))
            tok_ref[:, cols] = (yv * ps_ref[:, cols]).astype(BF16)
        ext_ref[0:POOL_HALO, :] = x[tm - POOL_HALO:, :]

    return pl.pallas_call(
        body, name=name, grid=(S // tm,),
        in_specs=[_rows(tm, D_MODEL), _full((1, D_MODEL)), _full((N_DEV, kb, D_MIX)),
                  _full((len(POOL_WINDOWS), POOL_GROUP, POOL_GROUP)), _full((1, D_TOK))],
        out_specs=[_rows(tm, D_MODEL), _rows(tm, D_TOK), _rows(tm, D_TOK), _rows(tm, D_MEMH)],
        out_shape=[_sds((S, D_MODEL), BF16), _sds((S, D_TOK), BF16), _sds((S, D_TOK), BF16), _sds((S, D_MEMH), BF16)],
        scratch_shapes=[pltpu.VMEM((POOL_HALO + tm, D_TOK), F32)],
        compiler_params=_params(("arbitrary",)),
        interpret=False,
    )(h, g1, w_in, pool_w, pool_scale)


def _pool_bwd(dtok, dpre, pool_w, pool_scale, name):
    S = dtok.shape[0]
    tm = _tile(S, ROW_TILE)
    nt = S // tm
    ng = len(POOL_WINDOWS)

    def body(dtok_ref, dpre_ref, pw_ref, ps_ref, dx_ref, dpw_ref, dps_ref, ext_ref):
        i = pl.program_id(0)
        first = i == 0
        t0 = (nt - 1 - i) * tm
        pos = t0 + lax.broadcasted_iota(jnp.int32, (tm, 1), 0)

        @pl.when(first)
        def _():
            ext_ref[tm:, :] = jnp.zeros((POOL_HALO, D_TOK), F32)

        dps = []
        for gi, w in enumerate(POOL_WINDOWS):
            cols = slice(gi * POOL_GROUP, (gi + 1) * POOL_GROUP)
            dg = dpre_ref[:, cols]
            pw = pw_ref[gi].astype(BF16)
            dt = dtok_ref[:, cols].astype(F32)
            yv = _dot(dg, pw)
            dps.append(jnp.sum(dt * yv, axis=0, keepdims=True))
            dy = (dt * ps_ref[:, cols]).astype(BF16)
            _acc_rows(dpw_ref.at[gi], _dot_tn(dg, dy), first)
            dd = _dot_nt(dy, pw)
            cnt = jnp.minimum(pos + 1, w).astype(F32)
            ext_ref[0:tm, cols] = dd / cnt
            wsum = ext_ref[0:tm, cols]
            for k in range(1, w):
                wsum = wsum + ext_ref[k:k + tm, cols]
            dx_ref[:, cols] = (wsum - dd).astype(BF16)
        _acc_rows(dps_ref, jnp.concatenate(dps, axis=1), first)
        ext_ref[tm:, :] = ext_ref[0:POOL_HALO, :]

    rev = lambda i: (nt - 1 - i, 0)
    return pl.pallas_call(
        body, name=name, grid=(nt,),
        in_specs=[pl.BlockSpec((tm, D_TOK), rev), pl.BlockSpec((tm, D_TOK), rev),
                  _full((ng, POOL_GROUP, POOL_GROUP)), _full((1, D_TOK))],
        out_specs=[pl.BlockSpec((tm, D_TOK), rev), _full((ng, POOL_GROUP, POOL_GROUP)), _full((1, D_TOK))],
        out_shape=[_sds((S, D_TOK), BF16), _sds((ng, POOL_GROUP, POOL_GROUP), F32), _sds((1, D_TOK), F32)],
        scratch_shapes=[pltpu.VMEM((tm + POOL_HALO, D_TOK), F32)],
        compiler_params=_params(("arbitrary",)),
        interpret=False,
    )(dtok, dpre, pool_w, pool_scale)


def _mix_in_sb(h, g1, wt, name):
    S = h.shape[0]
    tm = _tile(S, ROW_TILE)
    cb = 256

    def body(h_ref, g_ref, wt_ref, u_ref, proj_ref):
        u = _rms(h_ref[...], g_ref[...]).astype(BF16)
        u_ref[...] = u
        for c in range(D_SB // cb):
            proj_ref[:, c * cb:(c + 1) * cb] = _dot_nt(u, wt_ref[c * cb:(c + 1) * cb, :]).astype(BF16)

    return pl.pallas_call(
        body, name=name, grid=(S // tm,),
        in_specs=[_rows(tm, D_MODEL), _full((1, D_MODEL)), _full((D_SB, D_MODEL))],
        out_specs=[_rows(tm, D_MODEL), _rows(tm, D_SB)],
        out_shape=[_sds((S, D_MODEL), BF16), _sds((S, D_SB), BF16)],
        compiler_params=_params(("arbitrary",)),
        interpret=False,
    )(h, g1, wt)


def _mix_in_bwd(dho, h, g1, parts, w, mode, name):
    S = h.shape[0]
    tm = _tile(S, ROW_TILE)
    widths = [p.shape[1] for p in parts]
    npart = len(parts)
    kb = D_MODEL // N_DEV

    def body(*refs):
        dho_ref, h_ref, g_ref = refs[:3]
        p_refs = refs[3:3 + npart]
        w_ref, dh_ref, dg_ref = refs[3 + npart:]
        first = pl.program_id(0) == 0
        if mode == "pool":
            dproj = jnp.concatenate([p[...] for p in p_refs], axis=1)
            du = jnp.concatenate([_dot_nt(dproj, w_ref[d]) for d in range(N_DEV)], axis=1)
        else:
            du = jnp.zeros((tm, D_MODEL), F32)
            r0 = 0
            for p, wd_ in zip(p_refs, widths):
                du = du + _dot(p[...], w_ref[r0:r0 + wd_, :])
                r0 += wd_
        dhx, dg = _rms_bwd(h_ref[...], g_ref[...], du)
        dh_ref[...] = dho_ref[...] + dhx
        _acc_rows(dg_ref, dg, first)

    w_spec = _full((N_DEV, kb, D_MIX)) if mode == "pool" else _full((D_SB, D_MODEL))
    return pl.pallas_call(
        body, name=name, grid=(S // tm,),
        in_specs=[_rows(tm, D_MODEL), _rows(tm, D_MODEL), _full((1, D_MODEL))] + [_rows(tm, wd_) for wd_ in widths] + [w_spec],
        out_specs=[_rows(tm, D_MODEL), _full((1, D_MODEL))],
        out_shape=[_sds((S, D_MODEL), F32), _sds((1, D_MODEL), F32)],
        compiler_params=_params(("arbitrary",)),
        interpret=False,
    )(dho, h, g1, *parts, w)


def _sb_block(qe, kblk, diag, later, tri_later):
    row = lax.broadcasted_iota(jnp.int32, (Q_BLOCK, Q_BLOCK), 0)
    col = lax.broadcasted_iota(jnp.int32, (Q_BLOCK, Q_BLOCK), 1)
    z = _dot_nt(qe, kblk) * ATT_SCALE
    mask = jnp.logical_or(col < row, jnp.logical_not(diag))
    en = jnp.exp(-jnp.abs(z))
    ls = jnp.minimum(z, 0.0) - jnp.log(1.0 + en)
    lf = jnp.where(mask, ls - z, 0.0)
    within = _split_dot(lf, tri_later, 2)
    a = jnp.where(mask, jnp.exp(ls + within + later), 0.0)
    return z, mask, en, a, jnp.sum(lf, axis=1, keepdims=True)


def _sb_alive(later):
    return jnp.max(later) > SB_DEAD_LOG_WEIGHT


def _sb_more(qi, carry):
    return jnp.logical_and(carry[0] <= qi, carry[1])


def _tri(strict):
    row = lax.broadcasted_iota(jnp.int32, (Q_BLOCK, Q_BLOCK), 0)
    col = lax.broadcasted_iota(jnp.int32, (Q_BLOCK, Q_BLOCK), 1)
    return (row > col if strict else row >= col).astype(BF16)


def _sb_heads(lanes):
    lane = lax.broadcasted_iota(jnp.int32, (Q_BLOCK, 128), 1)
    return [(g, (lane >= e * HEAD_DIM) & (lane < (e + 1) * HEAD_DIM)) for g in range(lanes // 128) for e in range(128 // HEAD_DIM)]


def _group(x, g):
    return x[:, g * 128:(g + 1) * 128]


def _masked(hm, x):
    return jnp.where(hm, x, jnp.zeros_like(x))


def _sb_fwd(proj, name):
    S = proj.shape[0]
    nq = S // Q_BLOCK
    W = SB_FWD_LANES
    nrow = D_TOK // W

    def body(q_ref, k_ref, v_ref, o_ref, tok_ref):
        qi = pl.program_id(1)
        heads = _sb_heads(W)
        tri_later = _tri(True)
        q = q_ref[...]
        qes = [_masked(hm, _group(q, g)) for g, hm in heads]

        def step(carry):
            j, _, accs, laters = carry
            off = pl.multiple_of((qi - j) * Q_BLOCK, Q_BLOCK)
            kb = k_ref[pl.ds(off, Q_BLOCK), :]
            vb = v_ref[pl.ds(off, Q_BLOCK), :]
            accs, laters = list(accs), list(laters)
            for h, (g, hm) in enumerate(heads):
                _, _, _, a, bsum = _sb_block(qes[h], _group(kb, g), j == 0, laters[h], tri_later)
                accs[g] = accs[g] + _split_dot(a, _masked(hm, _group(vb, g)), 2)
                laters[h] = laters[h] + bsum
            return j + 1, _sb_alive(functools.reduce(jnp.maximum, laters)), tuple(accs), tuple(laters)

        init = (jnp.int32(0), jnp.bool_(True), (jnp.zeros((Q_BLOCK, 128), F32),) * (W // 128),
                (jnp.zeros((Q_BLOCK, 1), F32),) * len(heads))
        accs = lax.while_loop(functools.partial(_sb_more, qi), step, init)[2]
        for g, acc in enumerate(accs):
            o_ref[:, g * 128:(g + 1) * 128] = acc
            tok_ref[:, g * 128:(g + 1) * 128] = acc.astype(BF16)

    blk = pl.BlockSpec((Q_BLOCK, W), lambda p, i: (i, p))
    return pl.pallas_call(
        body, name=name, grid=(nrow, nq),
        in_specs=[blk, pl.BlockSpec((S, W), lambda p, i: (0, nrow + p)), pl.BlockSpec((S, W), lambda p, i: (0, 2 * nrow + p))],
        out_specs=[blk, blk],
        out_shape=[_sds((S, D_TOK), F32), _sds((S, D_TOK), BF16)],
        compiler_params=_params(("arbitrary", "arbitrary")),
        interpret=False,
    )(proj, proj, proj)


def _sb_bwd(proj, dtok, o32, name):
    S = proj.shape[0]
    nq = S // Q_BLOCK
    W = SB_BWD_LANES
    nrow = D_TOK // W

    def body(q_ref, k_ref, v_ref, do_ref, o_ref, dq_ref, dk_ref, dv_ref, dk_acc, dv_acc):
        qi = pl.program_id(1)

        @pl.when(qi == 0)
        def _():
            dk_acc[...] = jnp.zeros_like(dk_acc)
            dv_acc[...] = jnp.zeros_like(dv_acc)

        heads = _sb_heads(W)
        tri_later = _tri(True)
        tri_from = _tri(False)
        q = q_ref[...]
        do = do_ref[...]
        dov = do.astype(F32) * o_ref[...]
        qes = [_masked(hm, _group(q, g)) for g, hm in heads]
        does = [_masked(hm, _group(do, g)) for g, hm in heads]
        rowtots = [jnp.sum(jnp.where(hm, _group(dov, g), 0.0), axis=1, keepdims=True) for g, hm in heads]

        def step(carry):
            j, _, dqs, laters, seens = carry
            off = pl.multiple_of((qi - j) * Q_BLOCK, Q_BLOCK)
            kb = k_ref[pl.ds(off, Q_BLOCK), :]
            vb = v_ref[pl.ds(off, Q_BLOCK), :]
            dqs, laters, seens = list(dqs), list(laters), list(seens)
            dks = [jnp.zeros((Q_BLOCK, 128), F32)] * (W // 128)
            dvs = [jnp.zeros((Q_BLOCK, 128), F32)] * (W // 128)
            for h, (g, hm) in enumerate(heads):
                kblk = _group(kb, g)
                z, mask, en, a, bsum = _sb_block(qes[h], kblk, j == 0, laters[h], tri_later)
                inv = 1.0 / (1.0 + en)
                beta = jnp.where(z >= 0, 1.0, en) * inv
                omb = jnp.where(z >= 0, en, 1.0) * inv
                dlogw = a * _dot_nt(does[h], _group(vb, g))
                prefix = rowtots[h] - seens[h] - _split_dot(dlogw, tri_from, 2)
                dz = jnp.where(mask, dlogw * omb - beta * prefix, 0.0).astype(BF16)
                dks[g] = dks[g] + _dot_tn(dz, qes[h])
                dvs[g] = dvs[g] + _dot_tn(a.astype(BF16), does[h])
                dqs[g] = dqs[g] + _dot(dz, _masked(hm, kblk))
                laters[h] = laters[h] + bsum
                seens[h] = seens[h] + jnp.sum(dlogw, axis=1, keepdims=True)
            for g in range(W // 128):
                dk_acc[pl.ds(off, Q_BLOCK), g * 128:(g + 1) * 128] += dks[g] * ATT_SCALE
                dv_acc[pl.ds(off, Q_BLOCK), g * 128:(g + 1) * 128] += dvs[g]
            return j + 1, _sb_alive(functools.reduce(jnp.maximum, laters)), tuple(dqs), tuple(laters), tuple(seens)

        zero = (jnp.zeros((Q_BLOCK, 1), F32),) * len(heads)
        init = (jnp.int32(0), jnp.bool_(True), (jnp.zeros((Q_BLOCK, 128), F32),) * (W // 128), zero, zero)
        dqs = lax.while_loop(functools.partial(_sb_more, qi), step, init)[2]
        for g, dq in enumerate(dqs):
            dq_ref[:, g * 128:(g + 1) * 128] = (dq * ATT_SCALE).astype(BF16)

        @pl.when(qi == nq - 1)
        def _():
            dk_ref[...] = dk_acc[...].astype(BF16)
            dv_ref[...] = dv_acc[...].astype(BF16)

    blk = pl.BlockSpec((Q_BLOCK, W), lambda p, i: (i, p))
    col = pl.BlockSpec((S, W), lambda p, i: (0, p))
    return pl.pallas_call(
        body, name=name, grid=(nrow, nq),
        in_specs=[blk, pl.BlockSpec((S, W), lambda p, i: (0, nrow + p)), pl.BlockSpec((S, W), lambda p, i: (0, 2 * nrow + p)),
                  blk, blk],
        out_specs=[blk, col, col],
        out_shape=[_sds((S, D_TOK), BF16), _sds((S, D_TOK), BF16), _sds((S, D_TOK), BF16)],
        scratch_shapes=[pltpu.VMEM((S, W), F32), pltpu.VMEM((S, W), F32)],
        compiler_params=_params(("arbitrary", "arbitrary")),
        interpret=False,
    )(proj, proj, proj, dtok, o32)


def _mem_kv_fwd(mem, g_mem, w_kv, name):
    lm = mem.shape[0]
    kb = D_MODEL // N_DEV

    def body(mem_ref, g_ref, w_ref, mn_ref, km_ref, vm_ref):
        mn = _rms(mem_ref[...], g_ref[...]).astype(BF16)
        mn_ref[...] = mn
        kv = jnp.zeros((lm, 2 * D_MEMH), F32)
        for d in range(N_DEV):
            kv = kv + _dot(mn[:, d * kb:(d + 1) * kb], w_ref[d])
        km_ref[...] = kv[:, :D_MEMH].astype(BF16)
        vm_ref[...] = kv[:, D_MEMH:].astype(BF16)

    return pl.pallas_call(
        body, name=name, grid=(1,),
        in_specs=[_full((lm, D_MODEL)), _full((1, D_MODEL)), _full((N_DEV, kb, 2 * D_MEMH))],
        out_specs=[_full((lm, D_MODEL)), _full((lm, D_MEMH)), _full((lm, D_MEMH))],
        out_shape=[_sds((lm, D_MODEL), BF16), _sds((lm, D_MEMH), BF16), _sds((lm, D_MEMH), BF16)],
        compiler_params=_params(("arbitrary",)),
        interpret=False,
    )(mem, g_mem, w_kv)


def _mem_kv_bwd(dkm, dvm, mem, g_mem, mem_n, w_kv, name):
    lm = mem.shape[0]
    kb = D_MODEL // N_DEV

    def body(dkm_ref, dvm_ref, mem_ref, g_ref, mn_ref, w_ref, dw_ref, dg_ref):
        dkv = jnp.concatenate([dkm_ref[...], dvm_ref[...]], axis=1).astype(BF16)
        dw = _dot_tn(mn_ref[...], dkv)
        for d in range(N_DEV):
            dw_ref[d] = dw[d * kb:(d + 1) * kb, :].astype(GRAD_WIRE)
        dmn = jnp.concatenate([_dot_nt(dkv, w_ref[d]) for d in range(N_DEV)], axis=1)
        _, dg = _rms_bwd(mem_ref[...], g_ref[...], dmn)
        dg_ref[...] = dg

    return pl.pallas_call(
        body, name=name, grid=(1,),
        in_specs=[_full((lm, D_MEMH)), _full((lm, D_MEMH)), _full((lm, D_MODEL)), _full((1, D_MODEL)),
                  _full((lm, D_MODEL)), _full((N_DEV, kb, 2 * D_MEMH))],
        out_specs=[_full((N_DEV, kb, 2 * D_MEMH)), _full((1, D_MODEL))],
        out_shape=[_sds((N_DEV, kb, 2 * D_MEMH), GRAD_WIRE), _sds((1, D_MODEL), F32)],
        compiler_params=_params(("arbitrary",)),
        interpret=False,
    )(dkm, dvm, mem, g_mem, mem_n, w_kv)


def _mem_heads(tm):
    lane = lax.broadcasted_iota(jnp.int32, (tm, D_MEMH), 1)
    return [(lane >= e * HEAD_DIM) & (lane < (e + 1) * HEAD_DIM) for e in range(D_MEMH // HEAD_DIM)]


def _softmax(s):
    m = jnp.max(s, axis=-1, keepdims=True)
    p = jnp.exp(s - m)
    return p / jnp.sum(p, axis=-1, keepdims=True)


def _mix_out_fwd(h, tok, qm, qm_col, km, vm, w_out, gpost, name, ride=()):
    S = h.shape[0]
    tm = _tile(S, ROW_TILE)
    lm = km.shape[0]
    nb = D_MODEL // N_DEV

    def body(h_ref, tok_ref, qm_ref, km_ref, vm_ref, w_ref, g_ref, hn_ref, mo_ref, mix_ref):
        qv = qm_ref[...]
        kv, vv = km_ref[...], vm_ref[...]
        mo = jnp.zeros((tm, D_MEMH), F32)
        for hm, hk in zip(_mem_heads(tm), _mem_heads(lm)):
            qe = jnp.where(hm, qv, jnp.zeros_like(qv))
            p = _softmax(_dot_nt(qe, kv) * ATT_SCALE)
            mo = mo + _dot(p.astype(BF16), jnp.where(hk, vv, jnp.zeros_like(vv)))
        mob = mo.astype(BF16)
        mo_ref[...] = mob
        tv = tok_ref[...]
        mix = jnp.concatenate(
            [_dot(tv, w_ref[d, 0:D_TOK, :]) + _dot(mob, w_ref[d, D_TOK:D_MIX, :]) for d in range(N_DEV)], axis=1)
        mix_ref[...] = mix
        hn_ref[...] = h_ref[...] + _rms(mix, g_ref[...])

    return _pcall(
        body, name=name, grid=(S // tm,),
        in_specs=[_rows(tm, D_MODEL), _rows(tm, D_TOK), _rows(tm, D_MEMH, qm_col), _full((lm, D_MEMH)), _full((lm, D_MEMH)),
                  _full((N_DEV, D_MIX, nb)), _full((1, D_MODEL))],
        out_specs=[_rows(tm, D_MODEL), _rows(tm, D_MEMH), _rows(tm, D_MODEL)],
        out_shape=[_sds((S, D_MODEL), F32), _sds((S, D_MEMH), BF16), _sds((S, D_MODEL), F32)],
        args=(h, tok, qm, km, vm, w_out, gpost), ride=ride)


def _mix_out_bwd(dho, mix, qm, qm_col, km, vm, w_out, gpost, name, ride=()):
    S = dho.shape[0]
    tm = _tile(S, ROW_TILE)
    lm = km.shape[0]
    nb = D_MODEL // N_DEV

    def body(dho_ref, mix_ref, qm_ref, km_ref, vm_ref, w_ref, g_ref,
             dmix_ref, dtok_ref, dqm_ref, dkm_ref, dvm_ref, dg_ref):
        first = pl.program_id(0) == 0
        dmx, dg = _rms_bwd(mix_ref[...], g_ref[...], dho_ref[...])
        dmb = dmx.astype(BF16)
        dmix_ref[...] = dmb
        _acc_rows(dg_ref, dg, first)
        dcat = jnp.zeros((tm, D_MIX), F32)
        for d in range(N_DEV):
            dcat = dcat + _dot_nt(dmb[:, d * nb:(d + 1) * nb], w_ref[d])
        dtok_ref[...] = dcat[:, :D_TOK].astype(BF16)
        dmo = dcat[:, D_TOK:].astype(BF16)
        qv = qm_ref[...]
        kv, vv = km_ref[...], vm_ref[...]
        dq = jnp.zeros((tm, D_MEMH), F32)
        dk = jnp.zeros((lm, D_MEMH), F32)
        dv = jnp.zeros((lm, D_MEMH), F32)
        for hm, hk in zip(_mem_heads(tm), _mem_heads(lm)):
            qe = jnp.where(hm, qv, jnp.zeros_like(qv))
            dme = jnp.where(hm, dmo, jnp.zeros_like(dmo))
            p = _softmax(_dot_nt(qe, kv) * ATT_SCALE)
            dp = _dot_nt(dme, vv)
            ds = (p * (dp - jnp.sum(p * dp, axis=-1, keepdims=True))).astype(BF16)
            dq = dq + _dot(ds, jnp.where(hk, kv, jnp.zeros_like(kv)))
            dk = dk + _dot_tn(ds, qe)
            dv = dv + _dot_tn(p.astype(BF16), dme)
        dqm_ref[...] = (dq * ATT_SCALE).astype(BF16)
        _acc_rows(dkm_ref, dk * ATT_SCALE, first)
        _acc_rows(dvm_ref, dv, first)

    return _pcall(
        body, name=name, grid=(S // tm,),
        in_specs=[_rows(tm, D_MODEL), _rows(tm, D_MODEL), _rows(tm, D_MEMH, qm_col), _full((lm, D_MEMH)), _full((lm, D_MEMH)),
                  _full((N_DEV, D_MIX, nb)), _full((1, D_MODEL))],
        out_specs=[_rows(tm, D_MODEL), _rows(tm, D_TOK), _rows(tm, D_MEMH), _full((lm, D_MEMH)), _full((lm, D_MEMH)),
                   _full((1, D_MODEL))],
        out_shape=[_sds((S, D_MODEL), BF16), _sds((S, D_TOK), BF16), _sds((S, D_MEMH), BF16),
                   _sds((lm, D_MEMH), F32), _sds((lm, D_MEMH), F32), _sds((1, D_MODEL), F32)],
        args=(dho, mix, qm, km, vm, w_out, gpost), ride=ride)


def _loss_head(y, target, name):
    S = y.shape[0]
    tm = _tile(S, ROW_TILE)
    nt = S // tm

    def body(y_ref, t_ref, dy_ref, loss_ref, acc_ref):
        i = pl.program_id(0)
        e = y_ref[...] - t_ref[...]
        dy_ref[...] = e * (1.0 / D_MODEL)
        _acc_rows(acc_ref, jnp.sum(e * e, axis=0, keepdims=True), i == 0)

        @pl.when(i == nt - 1)
        def _():
            tot = jnp.sum(acc_ref[...], axis=1, keepdims=True) * (0.5 / D_MODEL)
            loss_ref[...] = jnp.broadcast_to(tot, (1, 128))

    return pl.pallas_call(
        body, name=name, grid=(nt,),
        in_specs=[_rows(tm, D_MODEL), _rows(tm, D_MODEL)],
        out_specs=[_rows(tm, D_MODEL), _full((1, 128))],
        out_shape=[_sds((S, D_MODEL), F32), _sds((1, 128), F32)],
        scratch_shapes=[pltpu.VMEM((1, D_MODEL), F32)],
        compiler_params=_params(("arbitrary",)),
        interpret=False,
    )(y, target)


def _adamw(recv, w, m, v, l, into, name):
    L, R, C = w.shape
    tr = R if R * C <= ADAM_TILE_ELEMS else _tile(R, ADAM_TILE_ELEMS // C)
    c1 = 1.0 - ADAM_B1 ** ADAM_STEP
    c2 = 1.0 - ADAM_B2 ** ADAM_STEP

    def body(r_ref, w_ref, m_ref, v_ref, *rest):
        g_ref, d_ref, nm_ref, nv_ref = rest[-4:]
        g = r_ref[0].astype(F32)
        for s in range(1, N_DEV):
            g = g + r_ref[s].astype(F32)
        g_ref[...] = g
        nm = ADAM_B1 * m_ref[...] + (1.0 - ADAM_B1) * g
        nv = ADAM_B2 * v_ref[...] + (1.0 - ADAM_B2) * (g * g)
        nm_ref[...] = nm
        nv_ref[...] = nv
        d_ref[...] = -ADAM_LR * ((nm / c1) / (jnp.sqrt(nv / c2) + ADAM_EPS) + ADAM_WD * w_ref[...])

    t = pl.BlockSpec((None, tr, C), lambda i: (l, i, 0))
    kept = [] if into is None else list(into)
    return pl.pallas_call(
        body, name=name, grid=(R // tr,),
        in_specs=[pl.BlockSpec((N_DEV, tr, C), lambda i: (0, i, 0)), t, t, t] + [ANY] * len(kept),
        out_specs=[t, t, t, t],
        out_shape=[_sds((L, R, C), F32)] * 4,
        input_output_aliases={4 + q: q for q in range(len(kept))},
        compiler_params=_params(("arbitrary",)),
        interpret=False,
    )(recv, w, m, v, *kept)


def _step(p, opt_m, opt_v, x, mem, target):
    bf = lambda a: a.astype(BF16)
    row = lambda a: a.reshape(1, -1)
    tsb = lambda a: jnp.swapaxes(a, 1, 2)
    g_mem = p["g_mem"]

    wsb_t = tsb(p["w_in_sb"])
    shard = {n: bf(p[n]) for n in STACKED if n != "w_in_sb"}
    shard["w_in_sb"] = bf(wsb_t)
    w_in = lambda i: "w_in_pool" if i % 2 == 0 else "w_in_sb"
    ffn = lambda which, i: [(f"ffn{which}_{s}", i) for s in ("gate", "up", "down")]
    mixing = lambda i: [(w_in(i), i // 2), ("w_mem_kv", i), ("w_out", i)]

    gw = {}

    def gather(keys):
        return [_Xfer(shard[n], l) for n, l in keys]

    first = ffn(1, 0) + mixing(0)
    landed = _comm_call(gather(first) + [_Xfer(p["g_pre"]), _Xfer(p["g_post"])], "gather_first")
    gw.update(zip(first, landed))
    unshard = lambda g: jnp.transpose(g, (1, 2, 0, 3)).reshape(DEPTH, 3, D_MODEL)
    g_pre, g_post = unshard(landed[-2]), unshard(landed[-1])

    saved = []
    h = x
    for i in range(DEPTH):
        j = i // 2
        more = i + 1 < DEPTH
        st = {"h0": h}
        keys = ffn(2, i)
        (h, st["n1"], st["gate1"], st["up1"], st["act1"], st["f1"]), landed = _ffn_fwd(
            h, row(g_pre[i, 0]), row(g_post[i, 0]), *[gw[k] for k in ffn(1, i)], f"ffn1_fwd_{i}", ride=gather(keys))
        gw.update(zip(keys, landed))
        st["h1"] = h
        if i % 2 == 0:
            st["u"], st["dpre"], st["tok"], st["qm"] = _mix_in_pool(
                h, row(g_pre[i, 1]), gw[("w_in_pool", j)], p["pool_w"][j], row(p["pool_scale"][j]), f"mix_in_pool_{i}")
            qm, qm_col = st["qm"], 0
        else:
            st["u"], st["proj"] = _mix_in_sb(h, row(g_pre[i, 1]), gw[("w_in_sb", j)].reshape(D_SB, D_MODEL), f"mix_in_sb_{i}")
            st["o32"], st["tok"] = _sb_fwd(st["proj"], f"sb_fwd_{i}")
            qm, qm_col = st["proj"], 3 * D_TOK // D_MEMH
        st["mem_n"], st["km"], st["vm"] = _mem_kv_fwd(mem, row(g_mem[i]), gw[("w_mem_kv", i)], f"mem_kv_fwd_{i}")
        keys = ffn(1, i + 1)[:2] if more else []
        (h, st["mo"], st["mix"]), landed = _mix_out_fwd(
            h, st["tok"], qm, qm_col, st["km"], st["vm"], gw[("w_out", i)], row(g_post[i, 1]), f"mix_out_fwd_{i}", ride=gather(keys))
        gw.update(zip(keys, landed))
        st["h2"] = h
        keys = ffn(1, i + 1)[2:] + mixing(i + 1) if more else []
        (h, st["n2"], st["gate2"], st["up2"], st["act2"], st["f2"]), landed = _ffn_fwd(
            h, row(g_pre[i, 2]), row(g_post[i, 2]), *[gw[k] for k in ffn(2, i)], f"ffn2_fwd_{i}", ride=gather(keys))
        gw.update(zip(keys, landed))
        saved.append(st)

    dh, loss_part = _loss_head(h, target, "loss_head")

    grads = {}
    recv = {}
    dg_pre = [[None] * 3 for _ in range(DEPTH)]
    dg_post = [[None] * 3 for _ in range(DEPTH)]
    dg_mem = [None] * DEPTH
    dpool_w = [None, None]
    dpool_scale = [None, None]

    def scatter(keys):
        return [_Xfer(grads[k], scatter=True) for k in keys]

    def ffn_backward(dh, st, i, which, hkey, slot, riding, last):
        sfx = str(which)
        keys = ffn(which, i)
        (dh, df, dgate, dup, dg_pre[i][slot], dg_post[i][slot]), landed = _ffn_bwd(
            dh, st[hkey], st["f" + sfx], st["gate" + sfx], st["up" + sfx], row(g_pre[i, slot]), row(g_post[i, slot]),
            *[gw[k] for k in keys], f"ffn{sfx}_bwd_{i}", ride=scatter(riding))
        recv.update(zip(riding, landed))
        riders = [mixing(i), keys[:1], keys[1:2]] if last else [[], [], []]
        operands = ((st["n" + sfx], dgate, "cols"), (st["n" + sfx], dup, "cols"), (st["act" + sfx], df, "rows"))
        for key, (a, b, split), riding in zip(keys, operands, riders):
            grads[key], landed = _wgrad([a], [b], split, f"wgrad_{key[0]}_{i}", ride=scatter(riding))
            recv.update(zip(riding, landed))
        return dh

    for i in reversed(range(DEPTH)):
        j = i // 2
        st = saved[i]
        dh = ffn_backward(dh, st, i, 2, "h2", 2, ffn(1, i + 1) if i + 1 < DEPTH else [], False)
        if i % 2 == 0:
            qm, qm_col = st["qm"], 0
        else:
            qm, qm_col = st["proj"], 3 * D_TOK // D_MEMH
        keys = ffn(2, i)
        (dmix, dtok, dqm, dkm, dvm, dg_post[i][1]), landed = _mix_out_bwd(
            dh, st["mix"], qm, qm_col, st["km"], st["vm"], gw[("w_out", i)], row(g_post[i, 1]), f"mix_out_bwd_{i}", ride=scatter(keys))
        recv.update(zip(keys, landed))
        grads[("w_out", i)], _ = _wgrad([st["tok"], st["mo"]], [dmix], "cols", f"wgrad_w_out_{i}")
        grads[("w_mem_kv", i)], dg_mem[i] = _mem_kv_bwd(dkm, dvm, mem, row(g_mem[i]), st["mem_n"], gw[("w_mem_kv", i)],
                                                        f"mem_kv_bwd_{i}")
        if i % 2 == 0:
            dx, dpool_w[j], dpool_scale[j] = _pool_bwd(dtok, st["dpre"], p["pool_w"][j], row(p["pool_scale"][j]), f"pool_bwd_{i}")
            parts = [dx, dqm]
            dh, dg_pre[i][1] = _mix_in_bwd(dh, st["h1"], row(g_pre[i, 1]), parts, gw[("w_in_pool", j)], "pool", f"mix_in_bwd_{i}")
            grads[("w_in_pool", j)], _ = _wgrad([st["u"]], parts, "rows", f"wgrad_w_in_pool_{i}")
        else:
            dq, dk, dv = _sb_bwd(st["proj"], dtok, st["o32"], f"sb_bwd_{i}")
            parts = [dq, dk, dv, dqm]
            dh, dg_pre[i][1] = _mix_in_bwd(dh, st["h1"], row(g_pre[i, 1]), parts, gw[("w_in_sb", j)].reshape(D_SB, D_MODEL), "sb",
                                           f"mix_in_bwd_{i}")
            grads[("w_in_sb", j)], _ = _wgrad(parts, [st["u"]], "rows", f"wgrad_w_in_sb_{i}")
        dh = ffn_backward(dh, st, i, 1, "h0", 0, mixing(i) if i > 0 else [], i == 0)
    grad_x = dh

    shard8 = lambda rows_: jnp.transpose(jnp.stack([jnp.concatenate(r, axis=0) for r in rows_]).reshape(DEPTH, 3, N_DEV, -1),
                                         (2, 0, 1, 3))
    tail = ffn(1, 0)[2:]
    landed = _comm_call(
        scatter(tail) + [_Xfer(shard8(dg_pre), scatter=True), _Xfer(shard8(dg_post), scatter=True),
                         _Xfer(jnp.concatenate(dg_mem, axis=0)), _Xfer(jnp.stack(dpool_w)),
                         _Xfer(jnp.concatenate(dpool_scale, axis=0)), _Xfer(loss_part)], "exchange_last")
    recv.update(zip(tail, landed))
    small = dict(zip(["g_pre", "g_post", "g_mem", "pool_w", "pool_scale"], landed[len(tail):]))

    def update(name, slots, w, m, v):
        into = None
        for l, r in enumerate(slots):
            into = _adamw(r, w, m, v, l, into, f"adamw_{name}_{l}")
        return into

    out = {}
    for n in STACKED:
        w, m, v = (wsb_t, tsb(opt_m[n]), tsb(opt_v[n])) if n == "w_in_sb" else (p[n], opt_m[n], opt_v[n])
        res = update(n, [recv[(n, l)] for l in range(w.shape[0])], w, m, v)
        out[n] = [tsb(a) for a in res] if n == "w_in_sb" else res
    one = lambda a: a.reshape(1, -1, a.shape[-1])
    for n, r in small.items():
        res = update(n, [r.reshape((N_DEV,) + one(p[n]).shape[1:])], one(p[n]), one(opt_m[n]), one(opt_v[n]))
        out[n] = [a.reshape(p[n].shape) for a in res]
    loss = jnp.sum(landed[-1][:, 0, 0])
    return loss, grad_x, out


STACKED = ["ffn1_gate", "ffn1_up", "ffn1_down", "ffn2_gate", "ffn2_up", "ffn2_down", "w_in_pool", "w_in_sb", "w_mem_kv", "w_out"]
WEIGHTS = ["g_pre", "g_post", "g_mem", "ffn1_gate", "ffn1_up", "ffn1_down", "ffn2_gate", "ffn2_up", "ffn2_down",
           "w_in_pool", "pool_w", "pool_scale", "w_in_sb", "w_mem_kv", "w_out"]


def kernel(x, mem, g_pre, g_post, g_mem, ffn1_gate, ffn1_up, ffn1_down, ffn2_gate, ffn2_up, ffn2_down, w_in_pool, pool_w, pool_scale, w_in_sb, w_mem_kv, w_out, loss_target, m_g_pre, m_g_post, m_g_mem, m_ffn1_gate, m_ffn1_up, m_ffn1_down, m_ffn2_gate, m_ffn2_up, m_ffn2_down, m_w_in_pool, m_pool_w, m_pool_scale, m_w_in_sb, m_w_mem_kv, m_w_out, v_g_pre, v_g_post, v_g_mem, v_ffn1_gate, v_ffn1_up, v_ffn1_down, v_ffn2_gate, v_ffn2_up, v_ffn2_down, v_w_in_pool, v_pool_w, v_pool_scale, v_w_in_sb, v_w_mem_kv, v_w_out):
    given = dict(locals())
    p = {n: given[n] for n in WEIGHTS}
    opt_m = {n: given["m_" + n] for n in WEIGHTS}
    opt_v = {n: given["v_" + n] for n in WEIGHTS}
    loss, grad_x, out = _step(p, opt_m, opt_v, x[0], mem[0], loss_target[0])
    res = [loss, grad_x[None]]
    for q in range(4):
        res += [out[n][q] for n in WEIGHTS]
    return tuple(res)
```

```python
import functools
from typing import NamedTuple, Optional

import jax
import jax.numpy as jnp
from jax import lax
from jax.experimental import pallas as pl
from jax.experimental.pallas import tpu as pltpu

F32 = jnp.float32
BF16 = jnp.bfloat16
GRAD_WIRE = jnp.bfloat16

N_DEV = 8
DEPTH = 4
D_MODEL = 1024
D_FF = 2048
D_TOK = 512
D_MEMH = 256
D_MIX = D_TOK + D_MEMH
D_SB = 3 * D_TOK + D_MEMH
HEAD_DIM = 64
Q_BLOCK = 128
POOL_WINDOWS = (2, 4, 8, 16)
POOL_GROUP = 128
POOL_HALO = 16
EPS = 1e-6
ATT_SCALE = HEAD_DIM ** -0.5
SB_DEAD_LOG_WEIGHT = -110.0
SB_FWD_LANES = 256
SB_BWD_LANES = 256

ADAM_LR = 0.001
ADAM_B1 = 0.9
ADAM_B2 = 0.999
ADAM_EPS = 1e-08
ADAM_WD = 0.01
ADAM_STEP = 10

VMEM_LIMIT_BYTES = 56 * 1024 * 1024
ROW_TILE = 256
FFN_FWD_TILE = 512
FFN_BWD_TILE = 512
FFN_CHUNK = 256
WGRAD_TILE = 512
ADAM_TILE_ELEMS = 128 * 1024

MESH = pl.DeviceIdType.MESH
ANY = pl.BlockSpec(memory_space=pl.ANY)


def _tile(n, pref):
    t = 1 << (pref.bit_length() - 1)
    while n % t:
        t //= 2
    return t


def _dot(a, b):
    return jnp.dot(a, b, preferred_element_type=F32)


def _dot_nt(a, b):
    return lax.dot_general(a, b, (((1,), (1,)), ((), ())), preferred_element_type=F32)


def _dot_tn(a, b):
    return lax.dot_general(a, b, (((0,), (0,)), ((), ())), preferred_element_type=F32)


def _split_dot(x, m, terms):
    out = None
    rest = x
    for _ in range(terms):
        part = rest.astype(BF16)
        rest = rest - part.astype(F32)
        d = _dot(part, m)
        out = d if out is None else out + d
    return out


def _rms(x, g):
    r = lax.rsqrt(jnp.mean(x * x, axis=-1, keepdims=True) + EPS)
    return x * r * g


def _rms_bwd(x, g, dy):
    r = lax.rsqrt(jnp.mean(x * x, axis=-1, keepdims=True) + EPS)
    xh = x * r
    gdy = g * dy
    dx = r * (gdy - xh * jnp.mean(gdy * xh, axis=-1, keepdims=True))
    return dx, jnp.sum(dy * xh, axis=0, keepdims=True)


def _acc_rows(ref, val, first):
    @pl.when(first)
    def _():
        ref[...] = val

    @pl.when(jnp.logical_not(first))
    def _():
        ref[...] += val


def _params(sem=None):
    return pltpu.CompilerParams(dimension_semantics=sem, vmem_limit_bytes=VMEM_LIMIT_BYTES)


def _sds(shape, dtype):
    return jax.ShapeDtypeStruct(shape, dtype)


def _rows(tm, width, col=0):
    return pl.BlockSpec((tm, width), lambda i: (i, col))


def _full(shape):
    nd = len(shape)
    return pl.BlockSpec(shape, lambda *_: (0,) * nd)


def _resident(shape):
    nd = len(shape)
    return pl.BlockSpec(shape, lambda *_: (0,) * nd, pipeline_mode=pl.Buffered(1))


def _peers():
    x, y, c = lax.axis_index("x"), lax.axis_index("y"), lax.axis_index("c")
    peers = []
    for k in range(1, N_DEV):
        px = 1 - x if k & 4 else x
        py = 1 - y if k & 2 else y
        pc = 1 - c if k & 1 else c
        peers.append(((px, py, pc), 4 * px + 2 * py + pc))
    return 4 * x + 2 * y + c, peers


class _Xfer(NamedTuple):
    src: jax.Array
    layer: Optional[int] = None
    scatter: bool = False

    @property
    def landing(self):
        block = self.src.shape if self.layer is None and not self.scatter else self.src.shape[1:]
        return _sds((N_DEV,) + block, self.src.dtype)


def _comm_copies(xfers, src_refs, dst_refs, send_sems, recv_sems, local_sems):
    me, peers = _peers()

    def src(t, to):
        ref = src_refs[t] if xfers[t].layer is None else src_refs[t].at[xfers[t].layer]
        return ref.at[to] if xfers[t].scatter else ref

    copies = [pltpu.make_async_copy(src(t, me), dst_refs[t].at[me], local_sems.at[t]) for t in range(len(xfers))]
    for k, (dev, idx) in enumerate(peers):
        for t in range(len(xfers)):
            copies.append(pltpu.make_async_remote_copy(
                src_ref=src(t, idx), dst_ref=dst_refs[t].at[me], send_sem=send_sems.at[t, k], recv_sem=recv_sems.at[t, k],
                device_id=dev, device_id_type=MESH))
    return copies


def _pcall(body, *, name, grid, in_specs, out_specs, out_shape, args, scratch=(), ride=()):
    n_in, n_out, n_scr, nx = len(in_specs), len(out_specs), len(scratch), len(ride)
    params = _params(("arbitrary",) * len(grid))
    if not ride:
        res = pl.pallas_call(body, name=name, grid=grid, in_specs=list(in_specs), out_specs=list(out_specs),
                             out_shape=list(out_shape), scratch_shapes=list(scratch), compiler_params=params,
                             interpret=False)(*args)
        return list(res), []
    (steps,) = grid

    def riding(*refs):
        o0 = n_in + nx
        s0 = o0 + n_out + nx
        comm = (ride, refs[n_in:o0], refs[o0 + n_out:s0], *refs[s0 + n_scr:])
        i = pl.program_id(0)

        @pl.when(i == 0)
        def _():
            for cp in _comm_copies(*comm):
                cp.start()

        body(*refs[:n_in], *refs[o0:o0 + n_out], *refs[s0:s0 + n_scr])

        @pl.when(i == steps - 1)
        def _():
            for cp in _comm_copies(*comm):
                cp.wait()

    sems = [pltpu.SemaphoreType.DMA((nx, N_DEV - 1)), pltpu.SemaphoreType.DMA((nx, N_DEV - 1)), pltpu.SemaphoreType.DMA((nx,))]
    res = pl.pallas_call(riding, name=name, grid=grid, in_specs=list(in_specs) + [ANY] * nx,
                         out_specs=list(out_specs) + [ANY] * nx, out_shape=list(out_shape) + [t.landing for t in ride],
                         scratch_shapes=list(scratch) + sems, compiler_params=params,
                         interpret=False)(*args, *[t.src for t in ride])
    return list(res[:n_out]), list(res[n_out:])


def _comm_call(xfers, name):
    return _pcall(lambda: None, name=name, grid=(1,), in_specs=[], out_specs=[], out_shape=[], args=[], ride=xfers)[1]


def _ffn_fwd(h, gpre, gpost, wg, wu, wd, name, ride=()):
    S = h.shape[0]
    tm = _tile(S, FFN_FWD_TILE)
    nb = FFN_CHUNK

    def body(h_ref, gpre_ref, gpost_ref, wgt_ref, wut_ref, wd_ref, hn_ref, n_ref, gate_ref, up_ref, act_ref, f_ref):
        hv = h_ref[...]
        n = _rms(hv, gpre_ref[...]).astype(BF16)
        n_ref[...] = n
        for c in range(D_FF // nb):
            cols = slice(c * nb, (c + 1) * nb)
            g = _dot_nt(n, wgt_ref[cols, :])
            u = _dot_nt(n, wut_ref[cols, :])
            gate_ref[:, cols] = g.astype(BF16)
            up_ref[:, cols] = u.astype(BF16)
            act_ref[:, cols] = (g * jax.nn.sigmoid(g) * u).astype(BF16)
        f = _dot(act_ref[...], wd_ref[...])
        f_ref[...] = f
        hn_ref[...] = hv + 0.5 * _rms(f, gpost_ref[...])

    return _pcall(
        body, name=name, grid=(S // tm,),
        in_specs=[_rows(tm, D_MODEL), _full((1, D_MODEL)), _full((1, D_MODEL)),
                  _resident((D_FF, D_MODEL)), _resident((D_FF, D_MODEL)), _resident((D_FF, D_MODEL))],
        out_specs=[_rows(tm, D_MODEL), _rows(tm, D_MODEL), _rows(tm, D_FF), _rows(tm, D_FF), _rows(tm, D_FF),
                   _rows(tm, D_MODEL)],
        out_shape=[_sds((S, D_MODEL), F32), _sds((S, D_MODEL), BF16), _sds((S, D_FF), BF16), _sds((S, D_FF), BF16),
                   _sds((S, D_FF), BF16), _sds((S, D_MODEL), F32)],
        args=(h, gpre, gpost, wg, wu, wd), ride=ride)


def _ffn_bwd(dho, h, f, gate, up, gpre, gpost, wg, wu, wd, name, ride=()):
    S = h.shape[0]
    tm = _tile(S, FFN_BWD_TILE)
    nb = FFN_CHUNK

    def body(dho_ref, h_ref, f_ref, gate_ref, up_ref, gpre_ref, gpost_ref, wgt_ref, wut_ref, wd_ref,
             dh_ref, df_ref, dgate_ref, dup_ref, dgpre_ref, dgpost_ref):
        first = pl.program_id(0) == 0
        dho_v = dho_ref[...]
        dfx, dgpost = _rms_bwd(f_ref[...], gpost_ref[...], 0.5 * dho_v)
        dfb = dfx.astype(BF16)
        df_ref[...] = dfb
        for c in range(D_FF // nb):
            cols = slice(c * nb, (c + 1) * nb)
            dact = _dot_nt(dfb, wd_ref[cols, :])
            g = gate_ref[:, cols].astype(F32)
            u = up_ref[:, cols].astype(F32)
            s = jax.nn.sigmoid(g)
            dgate_ref[:, cols] = (dact * u * (s * (1.0 + g * (1.0 - s)))).astype(BF16)
            dup_ref[:, cols] = (dact * (g * s)).astype(BF16)
        dn = _dot(dgate_ref[...], wgt_ref[...]) + _dot(dup_ref[...], wut_ref[...])
        dhx, dgpre = _rms_bwd(h_ref[...], gpre_ref[...], dn)
        dh_ref[...] = dho_v + dhx
        _acc_rows(dgpre_ref, dgpre, first)
        _acc_rows(dgpost_ref, dgpost, first)

    return _pcall(
        body, name=name, grid=(S // tm,),
        in_specs=[_rows(tm, D_MODEL), _rows(tm, D_MODEL), _rows(tm, D_MODEL), _rows(tm, D_FF), _rows(tm, D_FF),
                  _full((1, D_MODEL)), _full((1, D_MODEL)),
                  _resident((D_FF, D_MODEL)), _resident((D_FF, D_MODEL)), _resident((D_FF, D_MODEL))],
        out_specs=[_rows(tm, D_MODEL), _rows(tm, D_MODEL), _rows(tm, D_FF), _rows(tm, D_FF),
                   _full((1, D_MODEL)), _full((1, D_MODEL))],
        out_shape=[_sds((S, D_MODEL), F32), _sds((S, D_MODEL), BF16), _sds((S, D_FF), BF16), _sds((S, D_FF), BF16),
                   _sds((1, D_MODEL), F32), _sds((1, D_MODEL), F32)],
        args=(dho, h, f, gate, up, gpre, gpost, wg, wu, wd), ride=ride)


def _wgrad(a_parts, b_parts, split, name, ride=()):
    S = a_parts[0].shape[0]
    bk = _tile(S, WGRAD_TILE)
    ms = [a.shape[1] for a in a_parts]
    ns = [b.shape[1] for b in b_parts]
    M, N = sum(ms), sum(ns)
    na, nbp = len(a_parts), len(b_parts)
    blk = (M // N_DEV, N) if split == "rows" else (M, N // N_DEV)
    steps = S // bk

    def body(*refs):
        a_refs, b_refs = refs[:na], refs[na:na + nbp]
        out_ref, acc_ref = refs[-2], refs[-1]
        k = pl.program_id(0)

        @pl.when(k == 0)
        def _():
            acc_ref[...] = jnp.zeros_like(acc_ref)

        r0 = 0
        for ai in range(na):
            av = a_refs[ai][...]
            c0 = 0
            for bi in range(nbp):
                acc_ref[r0:r0 + ms[ai], c0:c0 + ns[bi]] += _dot_tn(av, b_refs[bi][...])
                c0 += ns[bi]
            r0 += ms[ai]

        @pl.when(k == steps - 1)
        def _():
            for d in range(N_DEV):
                if split == "rows":
                    out_ref[d] = acc_ref[d * blk[0]:(d + 1) * blk[0], :].astype(GRAD_WIRE)
                else:
                    out_ref[d] = acc_ref[:, d * blk[1]:(d + 1) * blk[1]].astype(GRAD_WIRE)

    in_specs = [pl.BlockSpec((bk, m), lambda k: (k, 0)) for m in ms] + [pl.BlockSpec((bk, n), lambda k: (k, 0)) for n in ns]
    (out,), landed = _pcall(
        body, name=name, grid=(steps,), in_specs=in_specs, out_specs=[_full((N_DEV,) + blk)],
        out_shape=[_sds((N_DEV,) + blk, GRAD_WIRE)], scratch=[pltpu.VMEM((M, N), F32)],
        args=list(a_parts) + list(b_parts), ride=ride)
    return out, landed


def _mix_in_pool(h, g1, w_in, pool_w, pool_scale, name):
    S = h.shape[0]
    tm = _tile(S, ROW_TILE)
    kb = D_MODEL // N_DEV

    def body(h_ref, g_ref, w_ref, pw_ref, ps_ref, u_ref, dpre_ref, tok_ref, qm_ref, ext_ref):
        i = pl.program_id(0)
        u = _rms(h_ref[...], g_ref[...]).astype(BF16)
        u_ref[...] = u
        proj = jnp.zeros((tm, D_MIX), F32)
        for d in range(N_DEV):
            proj = proj + _dot(u[:, d * kb:(d + 1) * kb], w_ref[d])
        qm_ref[...] = proj[:, D_TOK:].astype(BF16)
        x = proj[:, :D_TOK]

        @pl.when(i == 0)
        def _():
            ext_ref[0:POOL_HALO, :] = jnp.zeros((POOL_HALO, D_TOK), F32)

        ext_ref[POOL_HALO:, :] = x
        pos = i * tm + lax.broadcasted_iota(jnp.int32, (tm, 1), 0)
        for gi, w in enumerate(POOL_WINDOWS):
            cols = slice(gi * POOL_GROUP, (gi + 1) * POOL_GROUP)
            xs = x[:, cols]
            wsum = xs
            for k in range(1, w):
                wsum = wsum + ext_ref[POOL_HALO - k:POOL_HALO - k + tm, cols]
            cnt = jnp.minimum(pos + 1, w).astype(F32)
            dg = (wsum / cnt - xs).astype(BF16)
            dpre_ref[:, cols] = dg
            yv = _dot(dg, pw_ref[gi].astype(BF16))
            tok_ref[:, cols] = (yv * ps_ref[:, cols]).astype(BF16)
        ext_ref[0:POOL_HALO, :] = x[tm - POOL_HALO:, :]

    return pl.pallas_call(
        body, name=name, grid=(S // tm,),
        in_specs=[_rows(tm, D_MODEL), _full((1, D_MODEL)), _full((N_DEV, kb, D_MIX)),
                  _full((len(POOL_WINDOWS), POOL_GROUP, POOL_GROUP)), _full((1, D_TOK))],
        out_specs=[_rows(tm, D_MODEL), _rows(tm, D_TOK), _rows(tm, D_TOK), _rows(tm, D_MEMH)],
        out_shape=[_sds((S, D_MODEL), BF16), _sds((S, D_TOK), BF16), _sds((S, D_TOK), BF16), _sds((S, D_MEMH), BF16)],
        scratch_shapes=[pltpu.VMEM((POOL_HALO + tm, D_TOK), F32)],
        compiler_params=_params(("arbitrary",)),
        interpret=False,
    )(h, g1, w_in, pool_w, pool_scale)


def _pool_bwd(dtok, dpre, pool_w, pool_scale, name):
    S = dtok.shape[0]
    tm = _tile(S, ROW_TILE)
    nt = S // tm
    ng = len(POOL_WINDOWS)

    def body(dtok_ref, dpre_ref, pw_ref, ps_ref, dx_ref, dpw_ref, dps_ref, ext_ref):
        i = pl.program_id(0)
        first = i == 0
        t0 = (nt - 1 - i) * tm
        pos = t0 + lax.broadcasted_iota(jnp.int32, (tm, 1), 0)

        @pl.when(first)
        def _():
            ext_ref[tm:, :] = jnp.zeros((POOL_HALO, D_TOK), F32)

        dps = []
        for gi, w in enumerate(POOL_WINDOWS):
            cols = slice(gi * POOL_GROUP, (gi + 1) * POOL_GROUP)
            dg = dpre_ref[:, cols]
            pw = pw_ref[gi].astype(BF16)
            dt = dtok_ref[:, cols].astype(F32)
            yv = _dot(dg, pw)
            dps.append(jnp.sum(dt * yv, axis=0, keepdims=True))
            dy = (dt * ps_ref[:, cols]).astype(BF16)
            _acc_rows(dpw_ref.at[gi], _dot_tn(dg, dy), first)
            dd = _dot_nt(dy, pw)
            cnt = jnp.minimum(pos + 1, w).astype(F32)
            ext_ref[0:tm, cols] = dd / cnt
            wsum = ext_ref[0:tm, cols]
            for k in range(1, w):
                wsum = wsum + ext_ref[k:k + tm, cols]
            dx_ref[:, cols] = (wsum - dd).astype(BF16)
        _acc_rows(dps_ref, jnp.concatenate(dps, axis=1), first)
        ext_ref[tm:, :] = ext_ref[0:POOL_HALO, :]

    rev = lambda i: (nt - 1 - i, 0)
    return pl.pallas_call(
        body, name=name, grid=(nt,),
        in_specs=[pl.BlockSpec((tm, D_TOK), rev), pl.BlockSpec((tm, D_TOK), rev),
                  _full((ng, POOL_GROUP, POOL_GROUP)), _full((1, D_TOK))],
        out_specs=[pl.BlockSpec((tm, D_TOK), rev), _full((ng, POOL_GROUP, POOL_GROUP)), _full((1, D_TOK))],
        out_shape=[_sds((S, D_TOK), BF16), _sds((ng, POOL_GROUP, POOL_GROUP), F32), _sds((1, D_TOK), F32)],
        scratch_shapes=[pltpu.VMEM((tm + POOL_HALO, D_TOK), F32)],
        compiler_params=_params(("arbitrary",)),
        interpret=False,
    )(dtok, dpre, pool_w, pool_scale)


def _mix_in_sb(h, g1, wt, name):
    S = h.shape[0]
    tm = _tile(S, ROW_TILE)
    cb = 256

    def body(h_ref, g_ref, wt_ref, u_ref, proj_ref):
        u = _rms(h_ref[...], g_ref[...]).astype(BF16)
        u_ref[...] = u
        for c in range(D_SB // cb):
            proj_ref[:, c * cb:(c + 1) * cb] = _dot_nt(u, wt_ref[c * cb:(c + 1) * cb, :]).astype(BF16)

    return pl.pallas_call(
        body, name=name, grid=(S // tm,),
        in_specs=[_rows(tm, D_MODEL), _full((1, D_MODEL)), _full((D_SB, D_MODEL))],
        out_specs=[_rows(tm, D_MODEL), _rows(tm, D_SB)],
        out_shape=[_sds((S, D_MODEL), BF16), _sds((S, D_SB), BF16)],
        compiler_params=_params(("arbitrary",)),
        interpret=False,
    )(h, g1, wt)


def _mix_in_bwd(dho, h, g1, parts, w, mode, name):
    S = h.shape[0]
    tm = _tile(S, ROW_TILE)
    widths = [p.shape[1] for p in parts]
    npart = len(parts)
    kb = D_MODEL // N_DEV

    def body(*refs):
        dho_ref, h_ref, g_ref = refs[:3]
        p_refs = refs[3:3 + npart]
        w_ref, dh_ref, dg_ref = refs[3 + npart:]
        first = pl.program_id(0) == 0
        if mode == "pool":
            dproj = jnp.concatenate([p[...] for p in p_refs], axis=1)
            du = jnp.concatenate([_dot_nt(dproj, w_ref[d]) for d in range(N_DEV)], axis=1)
        else:
            du = jnp.zeros((tm, D_MODEL), F32)
            r0 = 0
            for p, wd_ in zip(p_refs, widths):
                du = du + _dot(p[...], w_ref[r0:r0 + wd_, :])
                r0 += wd_
        dhx, dg = _rms_bwd(h_ref[...], g_ref[...], du)
        dh_ref[...] = dho_ref[...] + dhx
        _acc_rows(dg_ref, dg, first)

    w_spec = _full((N_DEV, kb, D_MIX)) if mode == "pool" else _full((D_SB, D_MODEL))
    return pl.pallas_call(
        body, name=name, grid=(S // tm,),
        in_specs=[_rows(tm, D_MODEL), _rows(tm, D_MODEL), _full((1, D_MODEL))] + [_rows(tm, wd_) for wd_ in widths] + [w_spec],
        out_specs=[_rows(tm, D_MODEL), _full((1, D_MODEL))],
        out_shape=[_sds((S, D_MODEL), F32), _sds((1, D_MODEL), F32)],
        compiler_params=_params(("arbitrary",)),
        interpret=False,
    )(dho, h, g1, *parts, w)


def _sb_block(qcats, kb, diag, later, tri_later):
    z = jnp.concatenate([_dot_nt(qc, _group(kb, g)) for g, qc in enumerate(qcats)], axis=0) * ATT_SCALE
    row = jnp.bitwise_and(lax.broadcasted_iota(jnp.int32, z.shape, 0), Q_BLOCK - 1)
    col = lax.broadcasted_iota(jnp.int32, z.shape, 1)
    mask = jnp.logical_or(col < row, jnp.logical_not(diag))
    en = jnp.exp(-jnp.abs(z))
    ls = jnp.minimum(z, 0.0) - jnp.log(1.0 + en)
    lf = jnp.where(mask, ls - z, 0.0)
    within = _split_dot(lf, tri_later, 2)
    a = jnp.where(mask, jnp.exp(ls + within + later), 0.0)
    return z, mask, en, a, jnp.sum(lf, axis=1, keepdims=True)


def _sb_alive(later):
    return jnp.max(later) > SB_DEAD_LOG_WEIGHT


def _sb_more(qi, carry):
    return jnp.logical_and(carry[0] <= qi, carry[1])


def _tri(strict):
    row = lax.broadcasted_iota(jnp.int32, (Q_BLOCK, Q_BLOCK), 0)
    col = lax.broadcasted_iota(jnp.int32, (Q_BLOCK, Q_BLOCK), 1)
    return (row > col if strict else row >= col).astype(BF16)


HEADS_PER_GROUP = 128 // HEAD_DIM
GROUP_ROWS = HEADS_PER_GROUP * Q_BLOCK


def _sb_head_masks():
    lane = lax.broadcasted_iota(jnp.int32, (Q_BLOCK, 128), 1)
    return [(lane >= e * HEAD_DIM) & (lane < (e + 1) * HEAD_DIM) for e in range(HEADS_PER_GROUP)]


def _group(x, g):
    return x[:, g * 128:(g + 1) * 128]


def _masked(hm, x):
    return jnp.where(hm, x, jnp.zeros_like(x))


def _stack_heads(hms, x):
    return jnp.concatenate([_masked(hm, x) for hm in hms], axis=0)


def _own_lanes(hms, r):
    return sum(_masked(hm, r[e * Q_BLOCK:(e + 1) * Q_BLOCK]) for e, hm in enumerate(hms))


def _sb_fwd(proj, name):
    S = proj.shape[0]
    nq = S // Q_BLOCK
    W = SB_FWD_LANES
    nrow = D_TOK // W
    groups = W // 128

    def body(q_ref, k_ref, v_ref, o_ref, tok_ref):
        qi = pl.program_id(1)
        hms = _sb_head_masks()
        tri_later = _tri(True)
        q = q_ref[...]
        qcats = [_stack_heads(hms, _group(q, g)) for g in range(groups)]

        def step(carry):
            j, _, accs, later = carry
            off = pl.multiple_of((qi - j) * Q_BLOCK, Q_BLOCK)
            kb = k_ref[pl.ds(off, Q_BLOCK), :]
            vb = v_ref[pl.ds(off, Q_BLOCK), :]
            _, _, _, a, bsum = _sb_block(qcats, kb, j == 0, later, tri_later)
            hi = a.astype(BF16)
            lo = (a - hi.astype(F32)).astype(BF16)
            accs = list(accs)
            for g in range(groups):
                rows = slice(g * GROUP_ROWS, (g + 1) * GROUP_ROWS)
                r = _dot(jnp.concatenate([hi[rows], lo[rows]], axis=0), _group(vb, g))
                accs[g] = accs[g] + _own_lanes(hms, r[:GROUP_ROWS] + r[GROUP_ROWS:])
            later = later + bsum
            return j + 1, _sb_alive(later), tuple(accs), later

        init = (jnp.int32(0), jnp.bool_(True), (jnp.zeros((Q_BLOCK, 128), F32),) * groups,
                jnp.zeros((groups * GROUP_ROWS, 1), F32))
        accs = lax.while_loop(functools.partial(_sb_more, qi), step, init)[2]
        for g, acc in enumerate(accs):
            o_ref[:, g * 128:(g + 1) * 128] = acc
            tok_ref[:, g * 128:(g + 1) * 128] = acc.astype(BF16)

    blk = pl.BlockSpec((Q_BLOCK, W), lambda p, i: (i, p))
    return pl.pallas_call(
        body, name=name, grid=(nrow, nq),
        in_specs=[blk, pl.BlockSpec((S, W), lambda p, i: (0, nrow + p)), pl.BlockSpec((S, W), lambda p, i: (0, 2 * nrow + p))],
        out_specs=[blk, blk],
        out_shape=[_sds((S, D_TOK), F32), _sds((S, D_TOK), BF16)],
        compiler_params=_params(("arbitrary", "arbitrary")),
        interpret=False,
    )(proj, proj, proj)


def _sb_bwd(proj, dtok, o32, name):
    S = proj.shape[0]
    nq = S // Q_BLOCK
    W = SB_BWD_LANES
    nrow = D_TOK // W
    groups = W // 128

    def body(q_ref, k_ref, v_ref, do_ref, o_ref, dq_ref, dk_ref, dv_ref, dk_acc, dv_acc):
        qi = pl.program_id(1)

        @pl.when(qi == 0)
        def _():
            dk_acc[...] = jnp.zeros_like(dk_acc)
            dv_acc[...] = jnp.zeros_like(dv_acc)

        hms = _sb_head_masks()
        tri_later = _tri(True)
        tri_from = _tri(False)
        q = q_ref[...]
        do = do_ref[...]
        dov = do.astype(F32) * o_ref[...]
        qcats = [_stack_heads(hms, _group(q, g)) for g in range(groups)]
        docats = [_stack_heads(hms, _group(do, g)) for g in range(groups)]
        rowtot = jnp.concatenate([jnp.sum(jnp.where(hm, _group(dov, g), 0.0), axis=1, keepdims=True)
                                  for g in range(groups) for hm in hms], axis=0)

        def step(carry):
            j, _, dqs, later, seen = carry
            off = pl.multiple_of((qi - j) * Q_BLOCK, Q_BLOCK)
            kb = k_ref[pl.ds(off, Q_BLOCK), :]
            vb = v_ref[pl.ds(off, Q_BLOCK), :]
            z, mask, en, a, bsum = _sb_block(qcats, kb, j == 0, later, tri_later)
            inv = 1.0 / (1.0 + en)
            beta = jnp.where(z >= 0, 1.0, en) * inv
            omb = jnp.where(z >= 0, en, 1.0) * inv
            dlogw = a * jnp.concatenate([_dot_nt(docats[g], _group(vb, g)) for g in range(groups)], axis=0)
            prefix = rowtot - seen - _split_dot(dlogw, tri_from, 2)
            dz = jnp.where(mask, dlogw * omb - beta * prefix, 0.0).astype(BF16)
            ab = a.astype(BF16)
            dqs = list(dqs)
            for g in range(groups):
                rows = slice(g * GROUP_ROWS, (g + 1) * GROUP_ROWS)
                lanes = slice(g * 128, (g + 1) * 128)
                dqs[g] = dqs[g] + _own_lanes(hms, _dot(dz[rows], _group(kb, g)))
                dk_acc[pl.ds(off, Q_BLOCK), lanes] += _dot_tn(dz[rows], qcats[g]) * ATT_SCALE
                dv_acc[pl.ds(off, Q_BLOCK), lanes] += _dot_tn(ab[rows], docats[g])
            later = later + bsum
            return j + 1, _sb_alive(later), tuple(dqs), later, seen + jnp.sum(dlogw, axis=1, keepdims=True)

        zero = jnp.zeros((groups * GROUP_ROWS, 1), F32)
        init = (jnp.int32(0), jnp.bool_(True), (jnp.zeros((Q_BLOCK, 128), F32),) * groups, zero, zero)
        dqs = lax.while_loop(functools.partial(_sb_more, qi), step, init)[2]
        for g, dq in enumerate(dqs):
            dq_ref[:, g * 128:(g + 1) * 128] = (dq * ATT_SCALE).astype(BF16)

        @pl.when(qi == nq - 1)
        def _():
            dk_ref[...] = dk_acc[...].astype(BF16)
            dv_ref[...] = dv_acc[...].astype(BF16)

    blk = pl.BlockSpec((Q_BLOCK, W), lambda p, i: (i, p))
    col = pl.BlockSpec((S, W), lambda p, i: (0, p))
    return pl.pallas_call(
        body, name=name, grid=(nrow, nq),
        in_specs=[blk, pl.BlockSpec((S, W), lambda p, i: (0, nrow + p)), pl.BlockSpec((S, W), lambda p, i: (0, 2 * nrow + p)),
                  blk, blk],
        out_specs=[blk, col, col],
        out_shape=[_sds((S, D_TOK), BF16), _sds((S, D_TOK), BF16), _sds((S, D_TOK), BF16)],
        scratch_shapes=[pltpu.VMEM((S, W), F32), pltpu.VMEM((S, W), F32)],
        compiler_params=_params(("arbitrary", "arbitrary")),
        interpret=False,
    )(proj, proj, proj, dtok, o32)


def _mem_kv_fwd(mem, g_mem, w_kv, name):
    lm = mem.shape[0]
    kb = D_MODEL // N_DEV

    def body(mem_ref, g_ref, w_ref, mn_ref, km_ref, vm_ref):
        mn = _rms(mem_ref[...], g_ref[...]).astype(BF16)
        mn_ref[...] = mn
        kv = jnp.zeros((lm, 2 * D_MEMH), F32)
        for d in range(N_DEV):
            kv = kv + _dot(mn[:, d * kb:(d + 1) * kb], w_ref[d])
        km_ref[...] = kv[:, :D_MEMH].astype(BF16)
        vm_ref[...] = kv[:, D_MEMH:].astype(BF16)

    return pl.pallas_call(
        body, name=name, grid=(1,),
        in_specs=[_full((lm, D_MODEL)), _full((1, D_MODEL)), _full((N_DEV, kb, 2 * D_MEMH))],
        out_specs=[_full((lm, D_MODEL)), _full((lm, D_MEMH)), _full((lm, D_MEMH))],
        out_shape=[_sds((lm, D_MODEL), BF16), _sds((lm, D_MEMH), BF16), _sds((lm, D_MEMH), BF16)],
        compiler_params=_params(("arbitrary",)),
        interpret=False,
    )(mem, g_mem, w_kv)


def _mem_kv_bwd(dkm, dvm, mem, g_mem, mem_n, w_kv, name):
    lm = mem.shape[0]
    kb = D_MODEL // N_DEV

    def body(dkm_ref, dvm_ref, mem_ref, g_ref, mn_ref, w_ref, dw_ref, dg_ref):
        dkv = jnp.concatenate([dkm_ref[...], dvm_ref[...]], axis=1).astype(BF16)
        dw = _dot_tn(mn_ref[...], dkv)
        for d in range(N_DEV):
            dw_ref[d] = dw[d * kb:(d + 1) * kb, :].astype(GRAD_WIRE)
        dmn = jnp.concatenate([_dot_nt(dkv, w_ref[d]) for d in range(N_DEV)], axis=1)
        _, dg = _rms_bwd(mem_ref[...], g_ref[...], dmn)
        dg_ref[...] = dg

    return pl.pallas_call(
        body, name=name, grid=(1,),
        in_specs=[_full((lm, D_MEMH)), _full((lm, D_MEMH)), _full((lm, D_MODEL)), _full((1, D_MODEL)),
                  _full((lm, D_MODEL)), _full((N_DEV, kb, 2 * D_MEMH))],
        out_specs=[_full((N_DEV, kb, 2 * D_MEMH)), _full((1, D_MODEL))],
        out_shape=[_sds((N_DEV, kb, 2 * D_MEMH), GRAD_WIRE), _sds((1, D_MODEL), F32)],
        compiler_params=_params(("arbitrary",)),
        interpret=False,
    )(dkm, dvm, mem, g_mem, mem_n, w_kv)


def _mem_heads(tm):
    lane = lax.broadcasted_iota(jnp.int32, (tm, D_MEMH), 1)
    return [(lane >= e * HEAD_DIM) & (lane < (e + 1) * HEAD_DIM) for e in range(D_MEMH // HEAD_DIM)]


def _softmax(s):
    m = jnp.max(s, axis=-1, keepdims=True)
    p = jnp.exp(s - m)
    return p / jnp.sum(p, axis=-1, keepdims=True)


def _mix_out_fwd(h, tok, qm, qm_col, km, vm, w_out, gpost, name, ride=()):
    S = h.shape[0]
    tm = _tile(S, ROW_TILE)
    lm = km.shape[0]
    nb = D_MODEL // N_DEV

    def body(h_ref, tok_ref, qm_ref, km_ref, vm_ref, w_ref, g_ref, hn_ref, mo_ref, mix_ref):
        qv = qm_ref[...]
        kv, vv = km_ref[...], vm_ref[...]
        mo = jnp.zeros((tm, D_MEMH), F32)
        for hm, hk in zip(_mem_heads(tm), _mem_heads(lm)):
            qe = jnp.where(hm, qv, jnp.zeros_like(qv))
            p = _softmax(_dot_nt(qe, kv) * ATT_SCALE)
            mo = mo + _dot(p.astype(BF16), jnp.where(hk, vv, jnp.zeros_like(vv)))
        mob = mo.astype(BF16)
        mo_ref[...] = mob
        tv = tok_ref[...]
        mix = jnp.concatenate(
            [_dot(tv, w_ref[d, 0:D_TOK, :]) + _dot(mob, w_ref[d, D_TOK:D_MIX, :]) for d in range(N_DEV)], axis=1)
        mix_ref[...] = mix
        hn_ref[...] = h_ref[...] + _rms(mix, g_ref[...])

    return _pcall(
        body, name=name, grid=(S // tm,),
        in_specs=[_rows(tm, D_MODEL), _rows(tm, D_TOK), _rows(tm, D_MEMH, qm_col), _full((lm, D_MEMH)), _full((lm, D_MEMH)),
                  _full((N_DEV, D_MIX, nb)), _full((1, D_MODEL))],
        out_specs=[_rows(tm, D_MODEL), _rows(tm, D_MEMH), _rows(tm, D_MODEL)],
        out_shape=[_sds((S, D_MODEL), F32), _sds((S, D_MEMH), BF16), _sds((S, D_MODEL), F32)],
        args=(h, tok, qm, km, vm, w_out, gpost), ride=ride)


def _mix_out_bwd(dho, mix, qm, qm_col, km, vm, w_out, gpost, name, ride=()):
    S = dho.shape[0]
    tm = _tile(S, ROW_TILE)
    lm = km.shape[0]
    nb = D_MODEL // N_DEV

    def body(dho_ref, mix_ref, qm_ref, km_ref, vm_ref, w_ref, g_ref,
             dmix_ref, dtok_ref, dqm_ref, dkm_ref, dvm_ref, dg_ref):
        first = pl.program_id(0) == 0
        dmx, dg = _rms_bwd(mix_ref[...], g_ref[...], dho_ref[...])
        dmb = dmx.astype(BF16)
        dmix_ref[...] = dmb
        _acc_rows(dg_ref, dg, first)
        dcat = jnp.zeros((tm, D_MIX), F32)
        for d in range(N_DEV):
            dcat = dcat + _dot_nt(dmb[:, d * nb:(d + 1) * nb], w_ref[d])
        dtok_ref[...] = dcat[:, :D_TOK].astype(BF16)
        dmo = dcat[:, D_TOK:].astype(BF16)
        qv = qm_ref[...]
        kv, vv = km_ref[...], vm_ref[...]
        dq = jnp.zeros((tm, D_MEMH), F32)
        dk = jnp.zeros((lm, D_MEMH), F32)
        dv = jnp.zeros((lm, D_MEMH), F32)
        for hm, hk in zip(_mem_heads(tm), _mem_heads(lm)):
            qe = jnp.where(hm, qv, jnp.zeros_like(qv))
            dme = jnp.where(hm, dmo, jnp.zeros_like(dmo))
            p = _softmax(_dot_nt(qe, kv) * ATT_SCALE)
            dp = _dot_nt(dme, vv)
            ds = (p * (dp - jnp.sum(p * dp, axis=-1, keepdims=True))).astype(BF16)
            dq = dq + _dot(ds, jnp.where(hk, kv, jnp.zeros_like(kv)))
            dk = dk + _dot_tn(ds, qe)
            dv = dv + _dot_tn(p.astype(BF16), dme)
        dqm_ref[...] = (dq * ATT_SCALE).astype(BF16)
        _acc_rows(dkm_ref, dk * ATT_SCALE, first)
        _acc_rows(dvm_ref, dv, first)

    return _pcall(
        body, name=name, grid=(S // tm,),
        in_specs=[_rows(tm, D_MODEL), _rows(tm, D_MODEL), _rows(tm, D_MEMH, qm_col), _full((lm, D_MEMH)), _full((lm, D_MEMH)),
                  _full((N_DEV, D_MIX, nb)), _full((1, D_MODEL))],
        out_specs=[_rows(tm, D_MODEL), _rows(tm, D_TOK), _rows(tm, D_MEMH), _full((lm, D_MEMH)), _full((lm, D_MEMH)),
                   _full((1, D_MODEL))],
        out_shape=[_sds((S, D_MODEL), BF16), _sds((S, D_TOK), BF16), _sds((S, D_MEMH), BF16),
                   _sds((lm, D_MEMH), F32), _sds((lm, D_MEMH), F32), _sds((1, D_MODEL), F32)],
        args=(dho, mix, qm, km, vm, w_out, gpost), ride=ride)


def _loss_head(y, target, name):
    S = y.shape[0]
    tm = _tile(S, ROW_TILE)
    nt = S // tm

    def body(y_ref, t_ref, dy_ref, loss_ref, acc_ref):
        i = pl.program_id(0)
        e = y_ref[...] - t_ref[...]
        dy_ref[...] = e * (1.0 / D_MODEL)
        _acc_rows(acc_ref, jnp.sum(e * e, axis=0, keepdims=True), i == 0)

        @pl.when(i == nt - 1)
        def _():
            tot = jnp.sum(acc_ref[...], axis=1, keepdims=True) * (0.5 / D_MODEL)
            loss_ref[...] = jnp.broadcast_to(tot, (1, 128))

    return pl.pallas_call(
        body, name=name, grid=(nt,),
        in_specs=[_rows(tm, D_MODEL), _rows(tm, D_MODEL)],
        out_specs=[_rows(tm, D_MODEL), _full((1, 128))],
        out_shape=[_sds((S, D_MODEL), F32), _sds((1, 128), F32)],
        scratch_shapes=[pltpu.VMEM((1, D_MODEL), F32)],
        compiler_params=_params(("arbitrary",)),
        interpret=False,
    )(y, target)


def _adamw(recv, w, m, v, l, into, name):
    L, R, C = w.shape
    tr = R if R * C <= ADAM_TILE_ELEMS else _tile(R, ADAM_TILE_ELEMS // C)
    c1 = 1.0 - ADAM_B1 ** ADAM_STEP
    c2 = 1.0 - ADAM_B2 ** ADAM_STEP

    def body(r_ref, w_ref, m_ref, v_ref, *rest):
        g_ref, d_ref, nm_ref, nv_ref = rest[-4:]
        g = r_ref[0].astype(F32)
        for s in range(1, N_DEV):
            g = g + r_ref[s].astype(F32)
        g_ref[...] = g
        nm = ADAM_B1 * m_ref[...] + (1.0 - ADAM_B1) * g
        nv = ADAM_B2 * v_ref[...] + (1.0 - ADAM_B2) * (g * g)
        nm_ref[...] = nm
        nv_ref[...] = nv
        d_ref[...] = -ADAM_LR * ((nm / c1) / (jnp.sqrt(nv / c2) + ADAM_EPS) + ADAM_WD * w_ref[...])

    t = pl.BlockSpec((None, tr, C), lambda i: (l, i, 0))
    kept = [] if into is None else list(into)
    return pl.pallas_call(
        body, name=name, grid=(R // tr,),
        in_specs=[pl.BlockSpec((N_DEV, tr, C), lambda i: (0, i, 0)), t, t, t] + [ANY] * len(kept),
        out_specs=[t, t, t, t],
        out_shape=[_sds((L, R, C), F32)] * 4,
        input_output_aliases={4 + q: q for q in range(len(kept))},
        compiler_params=_params(("arbitrary",)),
        interpret=False,
    )(recv, w, m, v, *kept)


def _step(p, opt_m, opt_v, x, mem, target):
    bf = lambda a: a.astype(BF16)
    row = lambda a: a.reshape(1, -1)
    tsb = lambda a: jnp.swapaxes(a, 1, 2)
    g_mem = p["g_mem"]

    wsb_t = tsb(p["w_in_sb"])
    travels_transposed = ("w_in_sb", "ffn1_gate", "ffn1_up", "ffn2_gate", "ffn2_up")
    shard = {n: bf(tsb(p[n]) if n in travels_transposed else p[n]) for n in STACKED}
    ffn_weights = lambda which, i: [gw[k].reshape(D_FF, D_MODEL) for k in ffn(which, i)]
    w_in = lambda i: "w_in_pool" if i % 2 == 0 else "w_in_sb"
    ffn = lambda which, i: [(f"ffn{which}_{s}", i) for s in ("gate", "up", "down")]
    mixing = lambda i: [(w_in(i), i // 2), ("w_mem_kv", i), ("w_out", i)]

    gw = {}

    def gather(keys):
        return [_Xfer(shard[n], l) for n, l in keys]

    first = ffn(1, 0) + mixing(0)
    landed = _comm_call(gather(first) + [_Xfer(p["g_pre"]), _Xfer(p["g_post"])], "gather_first")
    gw.update(zip(first, landed))
    unshard = lambda g: jnp.transpose(g, (1, 2, 0, 3)).reshape(DEPTH, 3, D_MODEL)
    g_pre, g_post = unshard(landed[-2]), unshard(landed[-1])

    saved = []
    h = x
    for i in range(DEPTH):
        j = i // 2
        more = i + 1 < DEPTH
        st = {"h0": h}
        keys = ffn(2, i)
        (h, st["n1"], st["gate1"], st["up1"], st["act1"], st["f1"]), landed = _ffn_fwd(
            h, row(g_pre[i, 0]), row(g_post[i, 0]), *ffn_weights(1, i), f"ffn1_fwd_{i}", ride=gather(keys))
        gw.update(zip(keys, landed))
        st["h1"] = h
        if i % 2 == 0:
            st["u"], st["dpre"], st["tok"], st["qm"] = _mix_in_pool(
                h, row(g_pre[i, 1]), gw[("w_in_pool", j)], p["pool_w"][j], row(p["pool_scale"][j]), f"mix_in_pool_{i}")
            qm, qm_col = st["qm"], 0
        else:
            st["u"], st["proj"] = _mix_in_sb(h, row(g_pre[i, 1]), gw[("w_in_sb", j)].reshape(D_SB, D_MODEL), f"mix_in_sb_{i}")
            st["o32"], st["tok"] = _sb_fwd(st["proj"], f"sb_fwd_{i}")
            qm, qm_col = st["proj"], 3 * D_TOK // D_MEMH
        st["mem_n"], st["km"], st["vm"] = _mem_kv_fwd(mem, row(g_mem[i]), gw[("w_mem_kv", i)], f"mem_kv_fwd_{i}")
        keys = ffn(1, i + 1)[:2] if more else []
        (h, st["mo"], st["mix"]), landed = _mix_out_fwd(
            h, st["tok"], qm, qm_col, st["km"], st["vm"], gw[("w_out", i)], row(g_post[i, 1]), f"mix_out_fwd_{i}", ride=gather(keys))
        gw.update(zip(keys, landed))
        st["h2"] = h
        keys = ffn(1, i + 1)[2:] + mixing(i + 1) if more else []
        (h, st["n2"], st["gate2"], st["up2"], st["act2"], st["f2"]), landed = _ffn_fwd(
            h, row(g_pre[i, 2]), row(g_post[i, 2]), *ffn_weights(2, i), f"ffn2_fwd_{i}", ride=gather(keys))
        gw.update(zip(keys, landed))
        saved.append(st)

    dh, loss_part = _loss_head(h, target, "loss_head")

    grads = {}
    recv = {}
    dg_pre = [[None] * 3 for _ in range(DEPTH)]
    dg_post = [[None] * 3 for _ in range(DEPTH)]
    dg_mem = [None] * DEPTH
    dpool_w = [None, None]
    dpool_scale = [None, None]

    def scatter(keys):
        return [_Xfer(grads[k], scatter=True) for k in keys]

    def ffn_backward(dh, st, i, which, hkey, slot, riding, last):
        sfx = str(which)
        keys = ffn(which, i)
        (dh, df, dgate, dup, dg_pre[i][slot], dg_post[i][slot]), landed = _ffn_bwd(
            dh, st[hkey], st["f" + sfx], st["gate" + sfx], st["up" + sfx], row(g_pre[i, slot]), row(g_post[i, slot]),
            *ffn_weights(which, i), f"ffn{sfx}_bwd_{i}", ride=scatter(riding))
        recv.update(zip(riding, landed))
        riders = [mixing(i), keys[:1], keys[1:2]] if last else [[], [], []]
        operands = ((st["n" + sfx], dgate, "cols"), (st["n" + sfx], dup, "cols"), (st["act" + sfx], df, "rows"))
        for key, (a, b, split), riding in zip(keys, operands, riders):
            grads[key], landed = _wgrad([a], [b], split, f"wgrad_{key[0]}_{i}", ride=scatter(riding))
            recv.update(zip(riding, landed))
        return dh

    for i in reversed(range(DEPTH)):
        j = i // 2
        st = saved[i]
        dh = ffn_backward(dh, st, i, 2, "h2", 2, ffn(1, i + 1) if i + 1 < DEPTH else [], False)
        if i % 2 == 0:
            qm, qm_col = st["qm"], 0
        else:
            qm, qm_col = st["proj"], 3 * D_TOK // D_MEMH
        keys = ffn(2, i)
        (dmix, dtok, dqm, dkm, dvm, dg_post[i][1]), landed = _mix_out_bwd(
            dh, st["mix"], qm, qm_col, st["km"], st["vm"], gw[("w_out", i)], row(g_post[i, 1]), f"mix_out_bwd_{i}", ride=scatter(keys))
        recv.update(zip(keys, landed))
        grads[("w_out", i)], _ = _wgrad([st["tok"], st["mo"]], [dmix], "cols", f"wgrad_w_out_{i}")
        grads[("w_mem_kv", i)], dg_mem[i] = _mem_kv_bwd(dkm, dvm, mem, row(g_mem[i]), st["mem_n"], gw[("w_mem_kv", i)],
                                                        f"mem_kv_bwd_{i}")
        if i % 2 == 0:
            dx, dpool_w[j], dpool_scale[j] = _pool_bwd(dtok, st["dpre"], p["pool_w"][j], row(p["pool_scale"][j]), f"pool_bwd_{i}")
            parts = [dx, dqm]
            dh, dg_pre[i][1] = _mix_in_bwd(dh, st["h1"], row(g_pre[i, 1]), parts, gw[("w_in_pool", j)], "pool", f"mix_in_bwd_{i}")
            grads[("w_in_pool", j)], _ = _wgrad([st["u"]], parts, "rows", f"wgrad_w_in_pool_{i}")
        else:
            dq, dk, dv = _sb_bwd(st["proj"], dtok, st["o32"], f"sb_bwd_{i}")
            parts = [dq, dk, dv, dqm]
            dh, dg_pre[i][1] = _mix_in_bwd(dh, st["h1"], row(g_pre[i, 1]), parts, gw[("w_in_sb", j)].reshape(D_SB, D_MODEL), "sb",
                                           f"mix_in_bwd_{i}")
            grads[("w_in_sb", j)], _ = _wgrad(parts, [st["u"]], "rows", f"wgrad_w_in_sb_{i}")
        dh = ffn_backward(dh, st, i, 1, "h0", 0, mixing(i) if i > 0 else [], i == 0)
    grad_x = dh

    shard8 = lambda rows_: jnp.transpose(jnp.stack([jnp.concatenate(r, axis=0) for r in rows_]).reshape(DEPTH, 3, N_DEV, -1),
                                         (2, 0, 1, 3))
    tail = ffn(1, 0)[2:]
    landed = _comm_call(
        scatter(tail) + [_Xfer(shard8(dg_pre), scatter=True), _Xfer(shard8(dg_post), scatter=True),
                         _Xfer(jnp.concatenate(dg_mem, axis=0)), _Xfer(jnp.stack(dpool_w)),
                         _Xfer(jnp.concatenate(dpool_scale, axis=0)), _Xfer(loss_part)], "exchange_last")
    recv.update(zip(tail, landed))
    small = dict(zip(["g_pre", "g_post", "g_mem", "pool_w", "pool_scale"], landed[len(tail):]))

    def update(name, slots, w, m, v):
        into = None
        for l, r in enumerate(slots):
            into = _adamw(r, w, m, v, l, into, f"adamw_{name}_{l}")
        return into

    out = {}
    for n in STACKED:
        w, m, v = (wsb_t, tsb(opt_m[n]), tsb(opt_v[n])) if n == "w_in_sb" else (p[n], opt_m[n], opt_v[n])
        res = update(n, [recv[(n, l)] for l in range(w.shape[0])], w, m, v)
        out[n] = [tsb(a) for a in res] if n == "w_in_sb" else res
    one = lambda a: a.reshape(1, -1, a.shape[-1])
    for n, r in small.items():
        res = update(n, [r.reshape((N_DEV,) + one(p[n]).shape[1:])], one(p[n]), one(opt_m[n]), one(opt_v[n]))
        out[n] = [a.reshape(p[n].shape) for a in res]
    loss = jnp.sum(landed[-1][:, 0, 0])
    return loss, grad_x, out


STACKED = ["ffn1_gate", "ffn1_up", "ffn1_down", "ffn2_gate", "ffn2_up", "ffn2_down", "w_in_pool", "w_in_sb", "w_mem_kv", "w_out"]
WEIGHTS = ["g_pre", "g_post", "g_mem", "ffn1_gate", "ffn1_up", "ffn1_down", "ffn2_gate", "ffn2_up", "ffn2_down",
           "w_in_pool", "pool_w", "pool_scale", "w_in_sb", "w_mem_kv", "w_out"]


def kernel(x, mem, g_pre, g_post, g_mem, ffn1_gate, ffn1_up, ffn1_down, ffn2_gate, ffn2_up, ffn2_down, w_in_pool, pool_w, pool_scale, w_in_sb, w_mem_kv, w_out, loss_target, m_g_pre, m_g_post, m_g_mem, m_ffn1_gate, m_ffn1_up, m_ffn1_down, m_ffn2_gate, m_ffn2_up, m_ffn2_down, m_w_in_pool, m_pool_w, m_pool_scale, m_w_in_sb, m_w_mem_kv, m_w_out, v_g_pre, v_g_post, v_g_mem, v_ffn1_gate, v_ffn1_up, v_ffn1_down, v_ffn2_gate, v_ffn2_up, v_ffn2_down, v_w_in_pool, v_pool_w, v_pool_scale, v_w_in_sb, v_w_mem_kv, v_w_out):
    given = dict(locals())
    p = {n: given[n] for n in WEIGHTS}
    opt_m = {n: given["m_" + n] for n in WEIGHTS}
    opt_v = {n: given["v_" + n] for n in WEIGHTS}
    loss, grad_x, out = _step(p, opt_m, opt_v, x[0], mem[0], loss_target[0])
    res = [loss, grad_x[None]]
    for q in range(4):
        res += [out[n][q] for n in WEIGHTS]
    return tuple(res)
```

```python
import functools
from typing import NamedTuple, Optional

import jax
import jax.numpy as jnp
from jax import lax
from jax.experimental import pallas as pl
from jax.experimental.pallas import tpu as pltpu

F32 = jnp.float32
BF16 = jnp.bfloat16
GRAD_WIRE = jnp.bfloat16

N_DEV = 8
DEPTH = 4
D_MODEL = 1024
D_FF = 2048
D_TOK = 512
D_MEMH = 256
D_MIX = D_TOK + D_MEMH
D_SB = 3 * D_TOK + D_MEMH
HEAD_DIM = 64
Q_BLOCK = 128
POOL_WINDOWS = (2, 4, 8, 16)
POOL_GROUP = 128
POOL_HALO = 16
EPS = 1e-6
ATT_SCALE = HEAD_DIM ** -0.5
SB_DEAD_LOG_WEIGHT = -110.0
SB_FWD_LANES = 256
SB_BWD_LANES = 256

ADAM_LR = 0.001
ADAM_B1 = 0.9
ADAM_B2 = 0.999
ADAM_EPS = 1e-08
ADAM_WD = 0.01
ADAM_STEP = 10

VMEM_LIMIT_BYTES = 56 * 1024 * 1024
ROW_TILE = 256
FFN_FWD_TILE = 512
FFN_BWD_TILE = 512
FFN_CHUNK = 256
WGRAD_TILE = 512
ADAM_TILE_ELEMS = 128 * 1024

MESH = pl.DeviceIdType.MESH
ANY = pl.BlockSpec(memory_space=pl.ANY)


def _tile(n, pref):
    t = 1 << (pref.bit_length() - 1)
    while n % t:
        t //= 2
    return t


def _dot(a, b):
    return jnp.dot(a, b, preferred_element_type=F32)


def _dot_nt(a, b):
    return lax.dot_general(a, b, (((1,), (1,)), ((), ())), preferred_element_type=F32)


def _dot_tn(a, b):
    return lax.dot_general(a, b, (((0,), (0,)), ((), ())), preferred_element_type=F32)


def _split_dot(x, m, terms):
    out = None
    rest = x
    for _ in range(terms):
        part = rest.astype(BF16)
        rest = rest - part.astype(F32)
        d = _dot(part, m)
        out = d if out is None else out + d
    return out


def _rms(x, g):
    r = lax.rsqrt(jnp.mean(x * x, axis=-1, keepdims=True) + EPS)
    return x * r * g


def _rms_bwd(x, g, dy):
    r = lax.rsqrt(jnp.mean(x * x, axis=-1, keepdims=True) + EPS)
    xh = x * r
    gdy = g * dy
    dx = r * (gdy - xh * jnp.mean(gdy * xh, axis=-1, keepdims=True))
    return dx, jnp.sum(dy * xh, axis=0, keepdims=True)


def _acc_rows(ref, val, first):
    @pl.when(first)
    def _():
        ref[...] = val

    @pl.when(jnp.logical_not(first))
    def _():
        ref[...] += val


def _params(sem=None):
    return pltpu.CompilerParams(dimension_semantics=sem, vmem_limit_bytes=VMEM_LIMIT_BYTES)


def _sds(shape, dtype):
    return jax.ShapeDtypeStruct(shape, dtype)


def _rows(tm, width, col=0):
    return pl.BlockSpec((tm, width), lambda i: (i, col))


def _full(shape):
    nd = len(shape)
    return pl.BlockSpec(shape, lambda *_: (0,) * nd)


def _resident(shape):
    nd = len(shape)
    return pl.BlockSpec(shape, lambda *_: (0,) * nd, pipeline_mode=pl.Buffered(1))


def _peers():
    x, y, c = lax.axis_index("x"), lax.axis_index("y"), lax.axis_index("c")
    peers = []
    for k in range(1, N_DEV):
        px = 1 - x if k & 4 else x
        py = 1 - y if k & 2 else y
        pc = 1 - c if k & 1 else c
        peers.append(((px, py, pc), 4 * px + 2 * py + pc))
    return 4 * x + 2 * y + c, peers


class _Xfer(NamedTuple):
    src: jax.Array
    layer: Optional[int] = None
    scatter: bool = False

    @property
    def landing(self):
        block = self.src.shape if self.layer is None and not self.scatter else self.src.shape[1:]
        return _sds((N_DEV,) + block, self.src.dtype)


def _comm_copies(xfers, src_refs, dst_refs, send_sems, recv_sems, local_sems):
    me, peers = _peers()

    def src(t, to):
        ref = src_refs[t] if xfers[t].layer is None else src_refs[t].at[xfers[t].layer]
        return ref.at[to] if xfers[t].scatter else ref

    copies = [pltpu.make_async_copy(src(t, me), dst_refs[t].at[me], local_sems.at[t]) for t in range(len(xfers))]
    for k, (dev, idx) in enumerate(peers):
        for t in range(len(xfers)):
            copies.append(pltpu.make_async_remote_copy(
                src_ref=src(t, idx), dst_ref=dst_refs[t].at[me], send_sem=send_sems.at[t, k], recv_sem=recv_sems.at[t, k],
                device_id=dev, device_id_type=MESH))
    return copies


def _pcall(body, *, name, grid, in_specs, out_specs, out_shape, args, scratch=(), ride=()):
    n_in, n_out, n_scr, nx = len(in_specs), len(out_specs), len(scratch), len(ride)
    params = _params(("arbitrary",) * len(grid))
    if not ride:
        res = pl.pallas_call(body, name=name, grid=grid, in_specs=list(in_specs), out_specs=list(out_specs),
                             out_shape=list(out_shape), scratch_shapes=list(scratch), compiler_params=params,
                             interpret=False)(*args)
        return list(res), []
    def riding(*refs):
        o0 = n_in + nx
        s0 = o0 + n_out + nx
        comm = (ride, refs[n_in:o0], refs[o0 + n_out:s0], *refs[s0 + n_scr:])
        ids = [pl.program_id(a) for a in range(len(grid))]
        first = functools.reduce(jnp.logical_and, [i == 0 for i in ids])
        last = functools.reduce(jnp.logical_and, [i == n - 1 for i, n in zip(ids, grid)])

        @pl.when(first)
        def _():
            for cp in _comm_copies(*comm):
                cp.start()

        body(*refs[:n_in], *refs[o0:o0 + n_out], *refs[s0:s0 + n_scr])

        @pl.when(last)
        def _():
            for cp in _comm_copies(*comm):
                cp.wait()

    sems = [pltpu.SemaphoreType.DMA((nx, N_DEV - 1)), pltpu.SemaphoreType.DMA((nx, N_DEV - 1)), pltpu.SemaphoreType.DMA((nx,))]
    res = pl.pallas_call(riding, name=name, grid=grid, in_specs=list(in_specs) + [ANY] * nx,
                         out_specs=list(out_specs) + [ANY] * nx, out_shape=list(out_shape) + [t.landing for t in ride],
                         scratch_shapes=list(scratch) + sems, compiler_params=params,
                         interpret=False)(*args, *[t.src for t in ride])
    return list(res[:n_out]), list(res[n_out:])


def _comm_call(xfers, name):
    return _pcall(lambda: None, name=name, grid=(1,), in_specs=[], out_specs=[], out_shape=[], args=[], ride=xfers)[1]


def _ffn_fwd(h, gpre, gpost, wg, wu, wd, name, ride=()):
    S = h.shape[0]
    tm = _tile(S, FFN_FWD_TILE)
    nb = FFN_CHUNK

    def body(h_ref, gpre_ref, gpost_ref, wgt_ref, wut_ref, wd_ref, hn_ref, n_ref, gate_ref, up_ref, act_ref, f_ref):
        hv = h_ref[...]
        n = _rms(hv, gpre_ref[...]).astype(BF16)
        n_ref[...] = n
        for c in range(D_FF // nb):
            cols = slice(c * nb, (c + 1) * nb)
            g = _dot_nt(n, wgt_ref[cols, :])
            u = _dot_nt(n, wut_ref[cols, :])
            gate_ref[:, cols] = g.astype(BF16)
            up_ref[:, cols] = u.astype(BF16)
            act_ref[:, cols] = (g * jax.nn.sigmoid(g) * u).astype(BF16)
        f = _dot(act_ref[...], wd_ref[...])
        f_ref[...] = f
        hn_ref[...] = hv + 0.5 * _rms(f, gpost_ref[...])

    return _pcall(
        body, name=name, grid=(S // tm,),
        in_specs=[_rows(tm, D_MODEL), _full((1, D_MODEL)), _full((1, D_MODEL)),
                  _resident((D_FF, D_MODEL)), _resident((D_FF, D_MODEL)), _resident((D_FF, D_MODEL))],
        out_specs=[_rows(tm, D_MODEL), _rows(tm, D_MODEL), _rows(tm, D_FF), _rows(tm, D_FF), _rows(tm, D_FF),
                   _rows(tm, D_MODEL)],
        out_shape=[_sds((S, D_MODEL), F32), _sds((S, D_MODEL), BF16), _sds((S, D_FF), BF16), _sds((S, D_FF), BF16),
                   _sds((S, D_FF), BF16), _sds((S, D_MODEL), F32)],
        args=(h, gpre, gpost, wg, wu, wd), ride=ride)


def _ffn_bwd(dho, h, f, gate, up, gpre, gpost, wg, wu, wd, name, ride=()):
    S = h.shape[0]
    tm = _tile(S, FFN_BWD_TILE)
    nb = FFN_CHUNK

    def body(dho_ref, h_ref, f_ref, gate_ref, up_ref, gpre_ref, gpost_ref, wgt_ref, wut_ref, wd_ref,
             dh_ref, df_ref, dgate_ref, dup_ref, dgpre_ref, dgpost_ref):
        first = pl.program_id(0) == 0
        dho_v = dho_ref[...]
        dfx, dgpost = _rms_bwd(f_ref[...], gpost_ref[...], 0.5 * dho_v)
        dfb = dfx.astype(BF16)
        df_ref[...] = dfb
        for c in range(D_FF // nb):
            cols = slice(c * nb, (c + 1) * nb)
            dact = _dot_nt(dfb, wd_ref[cols, :])
            g = gate_ref[:, cols].astype(F32)
            u = up_ref[:, cols].astype(F32)
            s = jax.nn.sigmoid(g)
            dgate_ref[:, cols] = (dact * u * (s * (1.0 + g * (1.0 - s)))).astype(BF16)
            dup_ref[:, cols] = (dact * (g * s)).astype(BF16)
        dn = _dot(dgate_ref[...], wgt_ref[...]) + _dot(dup_ref[...], wut_ref[...])
        dhx, dgpre = _rms_bwd(h_ref[...], gpre_ref[...], dn)
        dh_ref[...] = dho_v + dhx
        _acc_rows(dgpre_ref, dgpre, first)
        _acc_rows(dgpost_ref, dgpost, first)

    return _pcall(
        body, name=name, grid=(S // tm,),
        in_specs=[_rows(tm, D_MODEL), _rows(tm, D_MODEL), _rows(tm, D_MODEL), _rows(tm, D_FF), _rows(tm, D_FF),
                  _full((1, D_MODEL)), _full((1, D_MODEL)),
                  _resident((D_FF, D_MODEL)), _resident((D_FF, D_MODEL)), _resident((D_FF, D_MODEL))],
        out_specs=[_rows(tm, D_MODEL), _rows(tm, D_MODEL), _rows(tm, D_FF), _rows(tm, D_FF),
                   _full((1, D_MODEL)), _full((1, D_MODEL))],
        out_shape=[_sds((S, D_MODEL), F32), _sds((S, D_MODEL), BF16), _sds((S, D_FF), BF16), _sds((S, D_FF), BF16),
                   _sds((1, D_MODEL), F32), _sds((1, D_MODEL), F32)],
        args=(dho, h, f, gate, up, gpre, gpost, wg, wu, wd), ride=ride)


def _wgrad(a_parts, b_parts, split, name, ride=()):
    S = a_parts[0].shape[0]
    bk = _tile(S, WGRAD_TILE)
    ms = [a.shape[1] for a in a_parts]
    ns = [b.shape[1] for b in b_parts]
    M, N = sum(ms), sum(ns)
    na, nbp = len(a_parts), len(b_parts)
    blk = (M // N_DEV, N) if split == "rows" else (M, N // N_DEV)
    steps = S // bk

    def body(*refs):
        a_refs, b_refs = refs[:na], refs[na:na + nbp]
        out_ref, acc_ref = refs[-2], refs[-1]
        k = pl.program_id(0)

        @pl.when(k == 0)
        def _():
            acc_ref[...] = jnp.zeros_like(acc_ref)

        r0 = 0
        for ai in range(na):
            av = a_refs[ai][...]
            c0 = 0
            for bi in range(nbp):
                acc_ref[r0:r0 + ms[ai], c0:c0 + ns[bi]] += _dot_tn(av, b_refs[bi][...])
                c0 += ns[bi]
            r0 += ms[ai]

        @pl.when(k == steps - 1)
        def _():
            for d in range(N_DEV):
                if split == "rows":
                    out_ref[d] = acc_ref[d * blk[0]:(d + 1) * blk[0], :].astype(GRAD_WIRE)
                else:
                    out_ref[d] = acc_ref[:, d * blk[1]:(d + 1) * blk[1]].astype(GRAD_WIRE)

    in_specs = [pl.BlockSpec((bk, m), lambda k: (k, 0)) for m in ms] + [pl.BlockSpec((bk, n), lambda k: (k, 0)) for n in ns]
    (out,), landed = _pcall(
        body, name=name, grid=(steps,), in_specs=in_specs, out_specs=[_full((N_DEV,) + blk)],
        out_shape=[_sds((N_DEV,) + blk, GRAD_WIRE)], scratch=[pltpu.VMEM((M, N), F32)],
        args=list(a_parts) + list(b_parts), ride=ride)
    return out, landed


def _mix_in_pool(h, g1, w_in, pool_w, pool_scale, name, ride=()):
    S = h.shape[0]
    tm = _tile(S, ROW_TILE)
    kb = D_MODEL // N_DEV

    def body(h_ref, g_ref, w_ref, pw_ref, ps_ref, u_ref, dpre_ref, tok_ref, qm_ref, ext_ref):
        i = pl.program_id(0)
        u = _rms(h_ref[...], g_ref[...]).astype(BF16)
        u_ref[...] = u
        proj = jnp.zeros((tm, D_MIX), F32)
        for d in range(N_DEV):
            proj = proj + _dot(u[:, d * kb:(d + 1) * kb], w_ref[d])
        qm_ref[...] = proj[:, D_TOK:].astype(BF16)
        x = proj[:, :D_TOK]

        @pl.when(i == 0)
        def _():
            ext_ref[0:POOL_HALO, :] = jnp.zeros((POOL_HALO, D_TOK), F32)

        ext_ref[POOL_HALO:, :] = x
        pos = i * tm + lax.broadcasted_iota(jnp.int32, (tm, 1), 0)
        for gi, w in enumerate(POOL_WINDOWS):
            cols = slice(gi * POOL_GROUP, (gi + 1) * POOL_GROUP)
            xs = x[:, cols]
            wsum = xs
            for k in range(1, w):
                wsum = wsum + ext_ref[POOL_HALO - k:POOL_HALO - k + tm, cols]
            cnt = jnp.minimum(pos + 1, w).astype(F32)
            dg = (wsum / cnt - xs).astype(BF16)
            dpre_ref[:, cols] = dg
            yv = _dot(dg, pw_ref[gi].astype(BF16))
            tok_ref[:, cols] = (yv * ps_ref[:, cols]).astype(BF16)
        ext_ref[0:POOL_HALO, :] = x[tm - POOL_HALO:, :]

    return _pcall(
        body, name=name, grid=(S // tm,),
        in_specs=[_rows(tm, D_MODEL), _full((1, D_MODEL)), _full((N_DEV, kb, D_MIX)),
                  _full((len(POOL_WINDOWS), POOL_GROUP, POOL_GROUP)), _full((1, D_TOK))],
        out_specs=[_rows(tm, D_MODEL), _rows(tm, D_TOK), _rows(tm, D_TOK), _rows(tm, D_MEMH)],
        out_shape=[_sds((S, D_MODEL), BF16), _sds((S, D_TOK), BF16), _sds((S, D_TOK), BF16), _sds((S, D_MEMH), BF16)],
        scratch=[pltpu.VMEM((POOL_HALO + tm, D_TOK), F32)],
        args=(h, g1, w_in, pool_w, pool_scale), ride=ride)


def _pool_bwd(dtok, dpre, pool_w, pool_scale, name):
    S = dtok.shape[0]
    tm = _tile(S, ROW_TILE)
    nt = S // tm
    ng = len(POOL_WINDOWS)

    def body(dtok_ref, dpre_ref, pw_ref, ps_ref, dx_ref, dpw_ref, dps_ref, ext_ref):
        i = pl.program_id(0)
        first = i == 0
        t0 = (nt - 1 - i) * tm
        pos = t0 + lax.broadcasted_iota(jnp.int32, (tm, 1), 0)

        @pl.when(first)
        def _():
            ext_ref[tm:, :] = jnp.zeros((POOL_HALO, D_TOK), F32)

        dps = []
        for gi, w in enumerate(POOL_WINDOWS):
            cols = slice(gi * POOL_GROUP, (gi + 1) * POOL_GROUP)
            dg = dpre_ref[:, cols]
            pw = pw_ref[gi].astype(BF16)
            dt = dtok_ref[:, cols].astype(F32)
            yv = _dot(dg, pw)
            dps.append(jnp.sum(dt * yv, axis=0, keepdims=True))
            dy = (dt * ps_ref[:, cols]).astype(BF16)
            _acc_rows(dpw_ref.at[gi], _dot_tn(dg, dy), first)
            dd = _dot_nt(dy, pw)
            cnt = jnp.minimum(pos + 1, w).astype(F32)
            ext_ref[0:tm, cols] = dd / cnt
            wsum = ext_ref[0:tm, cols]
            for k in range(1, w):
                wsum = wsum + ext_ref[k:k + tm, cols]
            dx_ref[:, cols] = (wsum - dd).astype(BF16)
        _acc_rows(dps_ref, jnp.concatenate(dps, axis=1), first)
        ext_ref[tm:, :] = ext_ref[0:POOL_HALO, :]

    rev = lambda i: (nt - 1 - i, 0)
    return pl.pallas_call(
        body, name=name, grid=(nt,),
        in_specs=[pl.BlockSpec((tm, D_TOK), rev), pl.BlockSpec((tm, D_TOK), rev),
                  _full((ng, POOL_GROUP, POOL_GROUP)), _full((1, D_TOK))],
        out_specs=[pl.BlockSpec((tm, D_TOK), rev), _full((ng, POOL_GROUP, POOL_GROUP)), _full((1, D_TOK))],
        out_shape=[_sds((S, D_TOK), BF16), _sds((ng, POOL_GROUP, POOL_GROUP), F32), _sds((1, D_TOK), F32)],
        scratch_shapes=[pltpu.VMEM((tm + POOL_HALO, D_TOK), F32)],
        compiler_params=_params(("arbitrary",)),
        interpret=False,
    )(dtok, dpre, pool_w, pool_scale)


def _mix_in_sb(h, g1, wt, name):
    S = h.shape[0]
    tm = _tile(S, ROW_TILE)
    cb = 256

    def body(h_ref, g_ref, wt_ref, u_ref, proj_ref):
        u = _rms(h_ref[...], g_ref[...]).astype(BF16)
        u_ref[...] = u
        for c in range(D_SB // cb):
            proj_ref[:, c * cb:(c + 1) * cb] = _dot_nt(u, wt_ref[c * cb:(c + 1) * cb, :]).astype(BF16)

    return pl.pallas_call(
        body, name=name, grid=(S // tm,),
        in_specs=[_rows(tm, D_MODEL), _full((1, D_MODEL)), _resident((D_SB, D_MODEL))],
        out_specs=[_rows(tm, D_MODEL), _rows(tm, D_SB)],
        out_shape=[_sds((S, D_MODEL), BF16), _sds((S, D_SB), BF16)],
        compiler_params=_params(("arbitrary",)),
        interpret=False,
    )(h, g1, wt)


def _mix_in_bwd(dho, h, g1, parts, w, mode, name):
    S = h.shape[0]
    tm = _tile(S, ROW_TILE)
    widths = [p.shape[1] for p in parts]
    npart = len(parts)
    kb = D_MODEL // N_DEV

    def body(*refs):
        dho_ref, h_ref, g_ref = refs[:3]
        p_refs = refs[3:3 + npart]
        w_ref, dh_ref, dg_ref = refs[3 + npart:]
        first = pl.program_id(0) == 0
        if mode == "pool":
            dproj = jnp.concatenate([p[...] for p in p_refs], axis=1)
            du = jnp.concatenate([_dot_nt(dproj, w_ref[d]) for d in range(N_DEV)], axis=1)
        else:
            du = jnp.zeros((tm, D_MODEL), F32)
            r0 = 0
            for p, wd_ in zip(p_refs, widths):
                du = du + _dot(p[...], w_ref[r0:r0 + wd_, :])
                r0 += wd_
        dhx, dg = _rms_bwd(h_ref[...], g_ref[...], du)
        dh_ref[...] = dho_ref[...] + dhx
        _acc_rows(dg_ref, dg, first)

    w_spec = _full((N_DEV, kb, D_MIX)) if mode == "pool" else _full((D_SB, D_MODEL))
    return pl.pallas_call(
        body, name=name, grid=(S // tm,),
        in_specs=[_rows(tm, D_MODEL), _rows(tm, D_MODEL), _full((1, D_MODEL))] + [_rows(tm, wd_) for wd_ in widths] + [w_spec],
        out_specs=[_rows(tm, D_MODEL), _full((1, D_MODEL))],
        out_shape=[_sds((S, D_MODEL), F32), _sds((1, D_MODEL), F32)],
        compiler_params=_params(("arbitrary",)),
        interpret=False,
    )(dho, h, g1, *parts, w)


def _sb_block(qcats, kb, mask, later, tri_later):
    z = jnp.concatenate([_dot_nt(qc, _group(kb, g)) for g, qc in enumerate(qcats)], axis=0) * ATT_SCALE
    en = jnp.exp(-jnp.abs(z))
    ls = jnp.minimum(z, 0.0) - jnp.log(1.0 + en)
    lf = ls - z if mask is None else jnp.where(mask, ls - z, 0.0)
    within = _split_dot(lf, tri_later, 2)
    a = jnp.exp(ls + within + later)
    if mask is not None:
        a = jnp.where(mask, a, 0.0)
    return z, en, a, jnp.sum(lf, axis=1, keepdims=True)


def _sb_causal(rows):
    row = jnp.bitwise_and(lax.broadcasted_iota(jnp.int32, (rows, Q_BLOCK), 0), Q_BLOCK - 1)
    return lax.broadcasted_iota(jnp.int32, (rows, Q_BLOCK), 1) < row


def _sb_walk(qi, block, state, later_of, unrolled):
    if unrolled:
        state = block(qi, _sb_causal, state)
        state = block(jnp.maximum(qi - 1, 0), qi >= 1, state)

    def step(carry):
        j, _, state = carry
        mask = None if unrolled else lambda rows: jnp.logical_or(_sb_causal(rows), j > 0)
        state = block(qi - j, mask, state)
        return j + 1, _sb_alive(later_of(state)), state

    first = jnp.int32(2 if unrolled else 0)
    return lax.while_loop(functools.partial(_sb_more, qi), step, (first, _sb_alive(later_of(state)), state))[2]


def _sb_alive(later):
    return jnp.max(later) > SB_DEAD_LOG_WEIGHT


def _sb_more(qi, carry):
    return jnp.logical_and(carry[0] <= qi, carry[1])


def _tri(strict):
    row = lax.broadcasted_iota(jnp.int32, (Q_BLOCK, Q_BLOCK), 0)
    col = lax.broadcasted_iota(jnp.int32, (Q_BLOCK, Q_BLOCK), 1)
    return (row > col if strict else row >= col).astype(BF16)


HEADS_PER_GROUP = 128 // HEAD_DIM
GROUP_ROWS = HEADS_PER_GROUP * Q_BLOCK


def _sb_head_masks():
    lane = lax.broadcasted_iota(jnp.int32, (Q_BLOCK, 128), 1)
    return [(lane >= e * HEAD_DIM) & (lane < (e + 1) * HEAD_DIM) for e in range(HEADS_PER_GROUP)]


def _group(x, g):
    return x[:, g * 128:(g + 1) * 128]


def _masked(hm, x):
    return jnp.where(hm, x, jnp.zeros_like(x))


def _stack_heads(hms, x):
    return jnp.concatenate([_masked(hm, x) for hm in hms], axis=0)


def _own_lanes(hms, r):
    return sum(_masked(hm, r[e * Q_BLOCK:(e + 1) * Q_BLOCK]) for e, hm in enumerate(hms))


def _sb_fwd(proj, name, ride=()):
    S = proj.shape[0]
    nq = S // Q_BLOCK
    W = SB_FWD_LANES
    nrow = D_TOK // W
    groups = W // 128

    def body(q_ref, k_ref, v_ref, o_ref, tok_ref):
        qi = pl.program_id(1)
        hms = _sb_head_masks()
        tri_later = _tri(True)
        q = q_ref[...]
        qcats = [_stack_heads(hms, _group(q, g)) for g in range(groups)]

        def block(kblock, mask, state):
            accs, later = state
            if callable(mask):
                mask = mask(later.shape[0])
            off = pl.multiple_of(kblock * Q_BLOCK, Q_BLOCK)
            kb = k_ref[pl.ds(off, Q_BLOCK), :]
            vb = v_ref[pl.ds(off, Q_BLOCK), :]
            _, _, a, bsum = _sb_block(qcats, kb, mask, later, tri_later)
            hi = a.astype(BF16)
            lo = (a - hi.astype(F32)).astype(BF16)
            accs = list(accs)
            for g in range(groups):
                rows = slice(g * GROUP_ROWS, (g + 1) * GROUP_ROWS)
                r = _dot(jnp.concatenate([hi[rows], lo[rows]], axis=0), _group(vb, g))
                accs[g] = accs[g] + _own_lanes(hms, r[:GROUP_ROWS] + r[GROUP_ROWS:])
            return tuple(accs), later + bsum

        init = ((jnp.zeros((Q_BLOCK, 128), F32),) * groups, jnp.zeros((groups * GROUP_ROWS, 1), F32))
        accs = _sb_walk(qi, block, init, lambda state: state[1], unrolled=True)[0]
        for g, acc in enumerate(accs):
            o_ref[:, g * 128:(g + 1) * 128] = acc
            tok_ref[:, g * 128:(g + 1) * 128] = acc.astype(BF16)

    blk = pl.BlockSpec((Q_BLOCK, W), lambda p, i: (i, p))
    return _pcall(
        body, name=name, grid=(nrow, nq),
        in_specs=[blk, pl.BlockSpec((S, W), lambda p, i: (0, nrow + p)), pl.BlockSpec((S, W), lambda p, i: (0, 2 * nrow + p))],
        out_specs=[blk, blk],
        out_shape=[_sds((S, D_TOK), F32), _sds((S, D_TOK), BF16)],
        args=(proj, proj, proj), ride=ride)


def _sb_bwd(proj, dtok, o32, name, ride=()):
    S = proj.shape[0]
    nq = S // Q_BLOCK
    W = SB_BWD_LANES
    nrow = D_TOK // W
    groups = W // 128

    def body(q_ref, k_ref, v_ref, do_ref, o_ref, dq_ref, dk_ref, dv_ref, dk_acc, dv_acc):
        qi = pl.program_id(1)

        @pl.when(qi == 0)
        def _():
            dk_acc[...] = jnp.zeros_like(dk_acc)
            dv_acc[...] = jnp.zeros_like(dv_acc)

        hms = _sb_head_masks()
        tri_later = _tri(True)
        tri_from = _tri(False)
        q = q_ref[...]
        do = do_ref[...]
        dov = do.astype(F32) * o_ref[...]
        qcats = [_stack_heads(hms, _group(q, g)) for g in range(groups)]
        docats = [_stack_heads(hms, _group(do, g)) for g in range(groups)]
        rowtot = jnp.concatenate([jnp.sum(jnp.where(hm, _group(dov, g), 0.0), axis=1, keepdims=True)
                                  for g in range(groups) for hm in hms], axis=0)

        def block(kblock, mask, state):
            dqs, later, seen = state
            if callable(mask):
                mask = mask(later.shape[0])
            off = pl.multiple_of(kblock * Q_BLOCK, Q_BLOCK)
            kb = k_ref[pl.ds(off, Q_BLOCK), :]
            vb = v_ref[pl.ds(off, Q_BLOCK), :]
            z, en, a, bsum = _sb_block(qcats, kb, mask, later, tri_later)
            inv = 1.0 / (1.0 + en)
            beta = jnp.where(z >= 0, 1.0, en) * inv
            omb = jnp.where(z >= 0, en, 1.0) * inv
            dlogw = a * jnp.concatenate([_dot_nt(docats[g], _group(vb, g)) for g in range(groups)], axis=0)
            prefix = rowtot - seen - _split_dot(dlogw, tri_from, 2)
            dz = dlogw * omb - beta * prefix
            if mask is not None:
                dz = jnp.where(mask, dz, 0.0)
            dz = dz.astype(BF16)
            ab = a.astype(BF16)
            dqs = list(dqs)
            for g in range(groups):
                rows = slice(g * GROUP_ROWS, (g + 1) * GROUP_ROWS)
                lanes = slice(g * 128, (g + 1) * 128)
                dqs[g] = dqs[g] + _own_lanes(hms, _dot(dz[rows], _group(kb, g)))
                dk_acc[pl.ds(off, Q_BLOCK), lanes] += _dot_tn(dz[rows], qcats[g]) * ATT_SCALE
                dv_acc[pl.ds(off, Q_BLOCK), lanes] += _dot_tn(ab[rows], docats[g])
            return tuple(dqs), later + bsum, seen + jnp.sum(dlogw, axis=1, keepdims=True)

        zero = jnp.zeros((groups * GROUP_ROWS, 1), F32)
        init = ((jnp.zeros((Q_BLOCK, 128), F32),) * groups, zero, zero)
        dqs = _sb_walk(qi, block, init, lambda state: state[1], unrolled=False)[0]
        for g, dq in enumerate(dqs):
            dq_ref[:, g * 128:(g + 1) * 128] = (dq * ATT_SCALE).astype(BF16)

        @pl.when(qi == nq - 1)
        def _():
            dk_ref[...] = dk_acc[...].astype(BF16)
            dv_ref[...] = dv_acc[...].astype(BF16)

    blk = pl.BlockSpec((Q_BLOCK, W), lambda p, i: (i, p))
    col = pl.BlockSpec((S, W), lambda p, i: (0, p))
    return _pcall(
        body, name=name, grid=(nrow, nq),
        in_specs=[blk, pl.BlockSpec((S, W), lambda p, i: (0, nrow + p)), pl.BlockSpec((S, W), lambda p, i: (0, 2 * nrow + p)),
                  blk, blk],
        out_specs=[blk, col, col],
        out_shape=[_sds((S, D_TOK), BF16), _sds((S, D_TOK), BF16), _sds((S, D_TOK), BF16)],
        scratch=[pltpu.VMEM((S, W), F32), pltpu.VMEM((S, W), F32)],
        args=(proj, proj, proj, dtok, o32), ride=ride)


def _mem_kv_fwd(mem, g_mem, w_kv, name):
    lm = mem.shape[0]
    kb = D_MODEL // N_DEV

    def body(mem_ref, g_ref, w_ref, mn_ref, km_ref, vm_ref):
        mn = _rms(mem_ref[...], g_ref[...]).astype(BF16)
        mn_ref[...] = mn
        kv = jnp.zeros((lm, 2 * D_MEMH), F32)
        for d in range(N_DEV):
            kv = kv + _dot(mn[:, d * kb:(d + 1) * kb], w_ref[d])
        km_ref[...] = kv[:, :D_MEMH].astype(BF16)
        vm_ref[...] = kv[:, D_MEMH:].astype(BF16)

    return pl.pallas_call(
        body, name=name, grid=(1,),
        in_specs=[_full((lm, D_MODEL)), _full((1, D_MODEL)), _full((N_DEV, kb, 2 * D_MEMH))],
        out_specs=[_full((lm, D_MODEL)), _full((lm, D_MEMH)), _full((lm, D_MEMH))],
        out_shape=[_sds((lm, D_MODEL), BF16), _sds((lm, D_MEMH), BF16), _sds((lm, D_MEMH), BF16)],
        compiler_params=_params(("arbitrary",)),
        interpret=False,
    )(mem, g_mem, w_kv)


def _mem_kv_bwd(dkm, dvm, mem, g_mem, mem_n, w_kv, name):
    lm = mem.shape[0]
    kb = D_MODEL // N_DEV

    def body(dkm_ref, dvm_ref, mem_ref, g_ref, mn_ref, w_ref, dw_ref, dg_ref):
        dkv = jnp.concatenate([dkm_ref[...], dvm_ref[...]], axis=1).astype(BF16)
        dw = _dot_tn(mn_ref[...], dkv)
        for d in range(N_DEV):
            dw_ref[d] = dw[d * kb:(d + 1) * kb, :].astype(GRAD_WIRE)
        dmn = jnp.concatenate([_dot_nt(dkv, w_ref[d]) for d in range(N_DEV)], axis=1)
        _, dg = _rms_bwd(mem_ref[...], g_ref[...], dmn)
        dg_ref[...] = dg

    return pl.pallas_call(
        body, name=name, grid=(1,),
        in_specs=[_full((lm, D_MEMH)), _full((lm, D_MEMH)), _full((lm, D_MODEL)), _full((1, D_MODEL)),
                  _full((lm, D_MODEL)), _full((N_DEV, kb, 2 * D_MEMH))],
        out_specs=[_full((N_DEV, kb, 2 * D_MEMH)), _full((1, D_MODEL))],
        out_shape=[_sds((N_DEV, kb, 2 * D_MEMH), GRAD_WIRE), _sds((1, D_MODEL), F32)],
        compiler_params=_params(("arbitrary",)),
        interpret=False,
    )(dkm, dvm, mem, g_mem, mem_n, w_kv)


def _mem_heads(tm):
    lane = lax.broadcasted_iota(jnp.int32, (tm, D_MEMH), 1)
    return [(lane >= e * HEAD_DIM) & (lane < (e + 1) * HEAD_DIM) for e in range(D_MEMH // HEAD_DIM)]


def _softmax(s):
    m = jnp.max(s, axis=-1, keepdims=True)
    p = jnp.exp(s - m)
    return p / jnp.sum(p, axis=-1, keepdims=True)


def _mix_out_fwd(h, tok, qm, qm_col, km, vm, w_out, gpost, name, ride=()):
    S = h.shape[0]
    tm = _tile(S, ROW_TILE)
    lm = km.shape[0]
    nb = D_MODEL // N_DEV

    def body(h_ref, tok_ref, qm_ref, km_ref, vm_ref, w_ref, g_ref, hn_ref, mo_ref, mix_ref):
        qv = qm_ref[...]
        kv, vv = km_ref[...], vm_ref[...]
        mo = jnp.zeros((tm, D_MEMH), F32)
        for hm, hk in zip(_mem_heads(tm), _mem_heads(lm)):
            qe = jnp.where(hm, qv, jnp.zeros_like(qv))
            p = _softmax(_dot_nt(qe, kv) * ATT_SCALE)
            mo = mo + _dot(p.astype(BF16), jnp.where(hk, vv, jnp.zeros_like(vv)))
        mob = mo.astype(BF16)
        mo_ref[...] = mob
        tv = tok_ref[...]
        mix = jnp.concatenate(
            [_dot(tv, w_ref[d, 0:D_TOK, :]) + _dot(mob, w_ref[d, D_TOK:D_MIX, :]) for d in range(N_DEV)], axis=1)
        mix_ref[...] = mix
        hn_ref[...] = h_ref[...] + _rms(mix, g_ref[...])

    return _pcall(
        body, name=name, grid=(S // tm,),
        in_specs=[_rows(tm, D_MODEL), _rows(tm, D_TOK), _rows(tm, D_MEMH, qm_col), _full((lm, D_MEMH)), _full((lm, D_MEMH)),
                  _full((N_DEV, D_MIX, nb)), _full((1, D_MODEL))],
        out_specs=[_rows(tm, D_MODEL), _rows(tm, D_MEMH), _rows(tm, D_MODEL)],
        out_shape=[_sds((S, D_MODEL), F32), _sds((S, D_MEMH), BF16), _sds((S, D_MODEL), F32)],
        args=(h, tok, qm, km, vm, w_out, gpost), ride=ride)


def _mix_out_bwd(dho, mix, qm, qm_col, km, vm, w_out, gpost, name, ride=()):
    S = dho.shape[0]
    tm = _tile(S, ROW_TILE)
    lm = km.shape[0]
    nb = D_MODEL // N_DEV

    def body(dho_ref, mix_ref, qm_ref, km_ref, vm_ref, w_ref, g_ref,
             dmix_ref, dtok_ref, dqm_ref, dkm_ref, dvm_ref, dg_ref):
        first = pl.program_id(0) == 0
        dmx, dg = _rms_bwd(mix_ref[...], g_ref[...], dho_ref[...])
        dmb = dmx.astype(BF16)
        dmix_ref[...] = dmb
        _acc_rows(dg_ref, dg, first)
        dcat = jnp.zeros((tm, D_MIX), F32)
        for d in range(N_DEV):
            dcat = dcat + _dot_nt(dmb[:, d * nb:(d + 1) * nb], w_ref[d])
        dtok_ref[...] = dcat[:, :D_TOK].astype(BF16)
        dmo = dcat[:, D_TOK:].astype(BF16)
        qv = qm_ref[...]
        kv, vv = km_ref[...], vm_ref[...]
        dq = jnp.zeros((tm, D_MEMH), F32)
        dk = jnp.zeros((lm, D_MEMH), F32)
        dv = jnp.zeros((lm, D_MEMH), F32)
        for hm, hk in zip(_mem_heads(tm), _mem_heads(lm)):
            qe = jnp.where(hm, qv, jnp.zeros_like(qv))
            dme = jnp.where(hm, dmo, jnp.zeros_like(dmo))
            p = _softmax(_dot_nt(qe, kv) * ATT_SCALE)
            dp = _dot_nt(dme, vv)
            ds = (p * (dp - jnp.sum(p * dp, axis=-1, keepdims=True))).astype(BF16)
            dq = dq + _dot(ds, jnp.where(hk, kv, jnp.zeros_like(kv)))
            dk = dk + _dot_tn(ds, qe)
            dv = dv + _dot_tn(p.astype(BF16), dme)
        dqm_ref[...] = (dq * ATT_SCALE).astype(BF16)
        _acc_rows(dkm_ref, dk * ATT_SCALE, first)
        _acc_rows(dvm_ref, dv, first)

    return _pcall(
        body, name=name, grid=(S // tm,),
        in_specs=[_rows(tm, D_MODEL), _rows(tm, D_MODEL), _rows(tm, D_MEMH, qm_col), _full((lm, D_MEMH)), _full((lm, D_MEMH)),
                  _full((N_DEV, D_MIX, nb)), _full((1, D_MODEL))],
        out_specs=[_rows(tm, D_MODEL), _rows(tm, D_TOK), _rows(tm, D_MEMH), _full((lm, D_MEMH)), _full((lm, D_MEMH)),
                   _full((1, D_MODEL))],
        out_shape=[_sds((S, D_MODEL), BF16), _sds((S, D_TOK), BF16), _sds((S, D_MEMH), BF16),
                   _sds((lm, D_MEMH), F32), _sds((lm, D_MEMH), F32), _sds((1, D_MODEL), F32)],
        args=(dho, mix, qm, km, vm, w_out, gpost), ride=ride)


def _loss_head(y, target, name):
    S = y.shape[0]
    tm = _tile(S, ROW_TILE)
    nt = S // tm

    def body(y_ref, t_ref, dy_ref, loss_ref, acc_ref):
        i = pl.program_id(0)
        e = y_ref[...] - t_ref[...]
        dy_ref[...] = e * (1.0 / D_MODEL)
        _acc_rows(acc_ref, jnp.sum(e * e, axis=0, keepdims=True), i == 0)

        @pl.when(i == nt - 1)
        def _():
            tot = jnp.sum(acc_ref[...], axis=1, keepdims=True) * (0.5 / D_MODEL)
            loss_ref[...] = jnp.broadcast_to(tot, (1, 128))

    return pl.pallas_call(
        body, name=name, grid=(nt,),
        in_specs=[_rows(tm, D_MODEL), _rows(tm, D_MODEL)],
        out_specs=[_rows(tm, D_MODEL), _full((1, 128))],
        out_shape=[_sds((S, D_MODEL), F32), _sds((1, 128), F32)],
        scratch_shapes=[pltpu.VMEM((1, D_MODEL), F32)],
        compiler_params=_params(("arbitrary",)),
        interpret=False,
    )(y, target)


def _adamw(recv, w, m, v, l, into, name):
    L, R, C = w.shape
    tr = R if R * C <= ADAM_TILE_ELEMS else _tile(R, ADAM_TILE_ELEMS // C)
    c1 = 1.0 - ADAM_B1 ** ADAM_STEP
    c2 = 1.0 - ADAM_B2 ** ADAM_STEP

    def body(r_ref, w_ref, m_ref, v_ref, *rest):
        g_ref, d_ref, nm_ref, nv_ref = rest[-4:]
        g = r_ref[0].astype(F32)
        for s in range(1, N_DEV):
            g = g + r_ref[s].astype(F32)
        g_ref[...] = g
        nm = ADAM_B1 * m_ref[...] + (1.0 - ADAM_B1) * g
        nv = ADAM_B2 * v_ref[...] + (1.0 - ADAM_B2) * (g * g)
        nm_ref[...] = nm
        nv_ref[...] = nv
        d_ref[...] = -ADAM_LR * ((nm / c1) / (jnp.sqrt(nv / c2) + ADAM_EPS) + ADAM_WD * w_ref[...])

    t = pl.BlockSpec((None, tr, C), lambda i: (l, i, 0))
    kept = [] if into is None else list(into)
    return pl.pallas_call(
        body, name=name, grid=(R // tr,),
        in_specs=[pl.BlockSpec((N_DEV, tr, C), lambda i: (0, i, 0)), t, t, t] + [ANY] * len(kept),
        out_specs=[t, t, t, t],
        out_shape=[_sds((L, R, C), F32)] * 4,
        input_output_aliases={4 + q: q for q in range(len(kept))},
        compiler_params=_params(("arbitrary",)),
        interpret=False,
    )(recv, w, m, v, *kept)


def _step(p, opt_m, opt_v, x, mem, target):
    bf = lambda a: a.astype(BF16)
    row = lambda a: a.reshape(1, -1)
    tsb = lambda a: jnp.swapaxes(a, 1, 2)
    g_mem = p["g_mem"]

    wsb_t = tsb(p["w_in_sb"])
    travels_transposed = ("w_in_sb", "ffn1_gate", "ffn1_up", "ffn2_gate", "ffn2_up")
    shard = {n: bf(tsb(p[n]) if n in travels_transposed else p[n]) for n in STACKED}
    ffn_weights = lambda which, i: [gw[k].reshape(D_FF, D_MODEL) for k in ffn(which, i)]
    w_in = lambda i: "w_in_pool" if i % 2 == 0 else "w_in_sb"
    ffn = lambda which, i: [(f"ffn{which}_{s}", i) for s in ("gate", "up", "down")]
    mixing = lambda i: [(w_in(i), i // 2), ("w_mem_kv", i), ("w_out", i)]

    gw = {}

    def gather(keys):
        return [_Xfer(shard[n], l) for n, l in keys]

    first = ffn(1, 0) + mixing(0)
    landed = _comm_call(gather(first) + [_Xfer(p["g_pre"]), _Xfer(p["g_post"])], "gather_first")
    gw.update(zip(first, landed))
    unshard = lambda g: jnp.transpose(g, (1, 2, 0, 3)).reshape(DEPTH, 3, D_MODEL)
    g_pre, g_post = unshard(landed[-2]), unshard(landed[-1])

    def ahead(i):
        nxt = i + 1 < DEPTH
        if i % 2 == 0:
            return {"ffn1": ffn(2, i)[:2], "mix_in": ffn(2, i)[2:], "sb": [], "mix_out": ffn(1, i + 1)[:1] if nxt else [],
                    "ffn2": ffn(1, i + 1)[1:] if nxt else []}
        return {"ffn1": mixing(i) if i > 0 else [], "mix_in": [], "sb": ffn(2, i) + (ffn(1, i + 1)[:2] if nxt else []),
                "mix_out": [], "ffn2": ffn(1, i + 1)[2:] + mixing(i + 1) if nxt else []}

    def gathering(keys, call):
        res, landed = call(ride=gather(keys))
        gw.update(zip(keys, landed))
        return res

    saved = []
    h = x
    for i in range(DEPTH):
        j = i // 2
        st = {"h0": h}
        carry = ahead(i)
        h, st["n1"], st["gate1"], st["up1"], st["act1"], st["f1"] = gathering(carry["ffn1"], functools.partial(
            _ffn_fwd, h, row(g_pre[i, 0]), row(g_post[i, 0]), *ffn_weights(1, i), f"ffn1_fwd_{i}"))
        st["h1"] = h
        if i % 2 == 0:
            st["u"], st["dpre"], st["tok"], st["qm"] = gathering(carry["mix_in"], functools.partial(
                _mix_in_pool, h, row(g_pre[i, 1]), gw[("w_in_pool", j)], p["pool_w"][j], row(p["pool_scale"][j]),
                f"mix_in_pool_{i}"))
            qm, qm_col = st["qm"], 0
        else:
            st["u"], st["proj"] = _mix_in_sb(h, row(g_pre[i, 1]), gw[("w_in_sb", j)].reshape(D_SB, D_MODEL), f"mix_in_sb_{i}")
            st["o32"], st["tok"] = gathering(carry["sb"], functools.partial(_sb_fwd, st["proj"], f"sb_fwd_{i}"))
            qm, qm_col = st["proj"], 3 * D_TOK // D_MEMH
        st["mem_n"], st["km"], st["vm"] = _mem_kv_fwd(mem, row(g_mem[i]), gw[("w_mem_kv", i)], f"mem_kv_fwd_{i}")
        h, st["mo"], st["mix"] = gathering(carry["mix_out"], functools.partial(
            _mix_out_fwd, h, st["tok"], qm, qm_col, st["km"], st["vm"], gw[("w_out", i)], row(g_post[i, 1]), f"mix_out_fwd_{i}"))
        st["h2"] = h
        h, st["n2"], st["gate2"], st["up2"], st["act2"], st["f2"] = gathering(carry["ffn2"], functools.partial(
            _ffn_fwd, h, row(g_pre[i, 2]), row(g_post[i, 2]), *ffn_weights(2, i), f"ffn2_fwd_{i}"))
        saved.append(st)

    dh, loss_part = _loss_head(h, target, "loss_head")

    grads = {}
    recv = {}
    dg_pre = [[None] * 3 for _ in range(DEPTH)]
    dg_post = [[None] * 3 for _ in range(DEPTH)]
    dg_mem = [None] * DEPTH
    dpool_w = [None, None]
    dpool_scale = [None, None]

    def scatter(keys):
        return [_Xfer(grads[k], scatter=True) for k in keys]

    def ffn_backward(dh, st, i, which, hkey, slot, riding, last):
        sfx = str(which)
        keys = ffn(which, i)
        (dh, df, dgate, dup, dg_pre[i][slot], dg_post[i][slot]), landed = _ffn_bwd(
            dh, st[hkey], st["f" + sfx], st["gate" + sfx], st["up" + sfx], row(g_pre[i, slot]), row(g_post[i, slot]),
            *ffn_weights(which, i), f"ffn{sfx}_bwd_{i}", ride=scatter(riding))
        recv.update(zip(riding, landed))
        riders = [mixing(i), keys[:1], keys[1:2]] if last else [[], [], []]
        operands = ((st["n" + sfx], dgate, "cols"), (st["n" + sfx], dup, "cols"), (st["act" + sfx], df, "rows"))
        for key, (a, b, split), riding in zip(keys, operands, riders):
            grads[key], landed = _wgrad([a], [b], split, f"wgrad_{key[0]}_{i}", ride=scatter(riding))
            recv.update(zip(riding, landed))
        return dh

    def behind(i):
        prev = ffn(1, i + 1) if i + 1 < DEPTH else []
        if i % 2 == 1:
            return {"ffn2": prev, "mix_out": [], "sb": ffn(2, i), "ffn1": mixing(i)}
        if i > 0:
            return {"ffn2": prev, "mix_out": ffn(2, i)[:2], "sb": [], "ffn1": ffn(2, i)[2:] + mixing(i)}
        return {"ffn2": prev, "mix_out": ffn(2, 0), "sb": [], "ffn1": []}

    for i in reversed(range(DEPTH)):
        j = i // 2
        st = saved[i]
        carry = behind(i)
        dh = ffn_backward(dh, st, i, 2, "h2", 2, carry["ffn2"], False)
        if i % 2 == 0:
            qm, qm_col = st["qm"], 0
        else:
            qm, qm_col = st["proj"], 3 * D_TOK // D_MEMH
        keys = carry["mix_out"]
        (dmix, dtok, dqm, dkm, dvm, dg_post[i][1]), landed = _mix_out_bwd(
            dh, st["mix"], qm, qm_col, st["km"], st["vm"], gw[("w_out", i)], row(g_post[i, 1]), f"mix_out_bwd_{i}", ride=scatter(keys))
        recv.update(zip(keys, landed))
        grads[("w_out", i)], _ = _wgrad([st["tok"], st["mo"]], [dmix], "cols", f"wgrad_w_out_{i}")
        grads[("w_mem_kv", i)], dg_mem[i] = _mem_kv_bwd(dkm, dvm, mem, row(g_mem[i]), st["mem_n"], gw[("w_mem_kv", i)],
                                                        f"mem_kv_bwd_{i}")
        if i % 2 == 0:
            dx, dpool_w[j], dpool_scale[j] = _pool_bwd(dtok, st["dpre"], p["pool_w"][j], row(p["pool_scale"][j]), f"pool_bwd_{i}")
            parts = [dx, dqm]
            dh, dg_pre[i][1] = _mix_in_bwd(dh, st["h1"], row(g_pre[i, 1]), parts, gw[("w_in_pool", j)], "pool", f"mix_in_bwd_{i}")
            grads[("w_in_pool", j)], _ = _wgrad([st["u"]], parts, "rows", f"wgrad_w_in_pool_{i}")
        else:
            (dq, dk, dv), landed = _sb_bwd(st["proj"], dtok, st["o32"], f"sb_bwd_{i}", ride=scatter(carry["sb"]))
            recv.update(zip(carry["sb"], landed))
            parts = [dq, dk, dv, dqm]
            dh, dg_pre[i][1] = _mix_in_bwd(dh, st["h1"], row(g_pre[i, 1]), parts, gw[("w_in_sb", j)].reshape(D_SB, D_MODEL), "sb",
                                           f"mix_in_bwd_{i}")
            grads[("w_in_sb", j)], _ = _wgrad(parts, [st["u"]], "rows", f"wgrad_w_in_sb_{i}")
        dh = ffn_backward(dh, st, i, 1, "h0", 0, carry["ffn1"], i == 0)
    grad_x = dh

    shard8 = lambda rows_: jnp.transpose(jnp.stack([jnp.concatenate(r, axis=0) for r in rows_]).reshape(DEPTH, 3, N_DEV, -1),
                                         (2, 0, 1, 3))
    tail = ffn(1, 0)[2:]
    landed = _comm_call(
        scatter(tail) + [_Xfer(shard8(dg_pre), scatter=True), _Xfer(shard8(dg_post), scatter=True),
                         _Xfer(jnp.concatenate(dg_mem, axis=0)), _Xfer(jnp.stack(dpool_w)),
                         _Xfer(jnp.concatenate(dpool_scale, axis=0)), _Xfer(loss_part)], "exchange_last")
    recv.update(zip(tail, landed))
    small = dict(zip(["g_pre", "g_post", "g_mem", "pool_w", "pool_scale"], landed[len(tail):]))

    def update(name, slots, w, m, v):
        into = None
        for l, r in enumerate(slots):
            into = _adamw(r, w, m, v, l, into, f"adamw_{name}_{l}")
        return into

    out = {}
    for n in STACKED:
        w, m, v = (wsb_t, tsb(opt_m[n]), tsb(opt_v[n])) if n == "w_in_sb" else (p[n], opt_m[n], opt_v[n])
        res = update(n, [recv[(n, l)] for l in range(w.shape[0])], w, m, v)
        out[n] = [tsb(a) for a in res] if n == "w_in_sb" else res
    one = lambda a: a.reshape(1, -1, a.shape[-1])
    for n, r in small.items():
        res = update(n, [r.reshape((N_DEV,) + one(p[n]).shape[1:])], one(p[n]), one(opt_m[n]), one(opt_v[n]))
        out[n] = [a.reshape(p[n].shape) for a in res]
    loss = jnp.sum(landed[-1][:, 0, 0])
    return loss, grad_x, out


STACKED = ["ffn1_gate", "ffn1_up", "ffn1_down", "ffn2_gate", "ffn2_up", "ffn2_down", "w_in_pool", "w_in_sb", "w_mem_kv", "w_out"]
WEIGHTS = ["g_pre", "g_post", "g_mem", "ffn1_gate", "ffn1_up", "ffn1_down", "ffn2_gate", "ffn2_up", "ffn2_down",
           "w_in_pool", "pool_w", "pool_scale", "w_in_sb", "w_mem_kv", "w_out"]


def kernel(x, mem, g_pre, g_post, g_mem, ffn1_gate, ffn1_up, ffn1_down, ffn2_gate, ffn2_up, ffn2_down, w_in_pool, pool_w, pool_scale, w_in_sb, w_mem_kv, w_out, loss_target, m_g_pre, m_g_post, m_g_mem, m_ffn1_gate, m_ffn1_up, m_ffn1_down, m_ffn2_gate, m_ffn2_up, m_ffn2_down, m_w_in_pool, m_pool_w, m_pool_scale, m_w_in_sb, m_w_mem_kv, m_w_out, v_g_pre, v_g_post, v_g_mem, v_ffn1_gate, v_ffn1_up, v_ffn1_down, v_ffn2_gate, v_ffn2_up, v_ffn2_down, v_w_in_pool, v_pool_w, v_pool_scale, v_w_in_sb, v_w_mem_kv, v_w_out):
    given = dict(locals())
    p = {n: given[n] for n in WEIGHTS}
    opt_m = {n: given["m_" + n] for n in WEIGHTS}
    opt_v = {n: given["v_" + n] for n in WEIGHTS}
    loss, grad_x, out = _step(p, opt_m, opt_v, x[0], mem[0], loss_target[0])
    res = [loss, grad_x[None]]
    for q in range(4):
        res += [out[n][q] for n in WEIGHTS]
    return tuple(res)
```

```python
import functools
from typing import NamedTuple, Optional

import jax
import jax.numpy as jnp
from jax import lax
from jax.experimental import pallas as pl
from jax.experimental.pallas import tpu as pltpu

F32 = jnp.float32
BF16 = jnp.bfloat16
GRAD_WIRE = jnp.bfloat16

N_DEV = 8
DEPTH = 4
D_MODEL = 1024
D_FF = 2048
D_TOK = 512
D_MEMH = 256
D_MIX = D_TOK + D_MEMH
D_SB = 3 * D_TOK + D_MEMH
HEAD_DIM = 64
Q_BLOCK = 128
POOL_WINDOWS = (2, 4, 8, 16)
POOL_GROUP = 128
POOL_HALO = 16
EPS = 1e-6
ATT_SCALE = HEAD_DIM ** -0.5
SB_DEAD_LOG_WEIGHT = -110.0
SB_FWD_LANES = 256
SB_BWD_LANES = 256

ADAM_LR = 0.001
ADAM_B1 = 0.9
ADAM_B2 = 0.999
ADAM_EPS = 1e-08
ADAM_WD = 0.01
ADAM_STEP = 10

VMEM_LIMIT_BYTES = 56 * 1024 * 1024
ROW_TILE = 256
FFN_FWD_TILE = 512
FFN_BWD_TILE = 512
FFN_CHUNK = 256
WGRAD_TILE = 512
ADAM_TILE_ELEMS = 128 * 1024

MESH = pl.DeviceIdType.MESH
ANY = pl.BlockSpec(memory_space=pl.ANY)


def _tile(n, pref):
    t = 1 << (pref.bit_length() - 1)
    while n % t:
        t //= 2
    return t


def _dot(a, b):
    return jnp.dot(a, b, preferred_element_type=F32)


def _dot_nt(a, b):
    return lax.dot_general(a, b, (((1,), (1,)), ((), ())), preferred_element_type=F32)


def _dot_tn(a, b):
    return lax.dot_general(a, b, (((0,), (0,)), ((), ())), preferred_element_type=F32)


def _split_dot(x, m, terms):
    out = None
    rest = x
    for _ in range(terms):
        part = rest.astype(BF16)
        rest = rest - part.astype(F32)
        d = _dot(part, m)
        out = d if out is None else out + d
    return out


def _rms(x, g):
    r = lax.rsqrt(jnp.mean(x * x, axis=-1, keepdims=True) + EPS)
    return x * r * g


def _rms_bwd(x, g, dy):
    r = lax.rsqrt(jnp.mean(x * x, axis=-1, keepdims=True) + EPS)
    xh = x * r
    gdy = g * dy
    dx = r * (gdy - xh * jnp.mean(gdy * xh, axis=-1, keepdims=True))
    return dx, jnp.sum(dy * xh, axis=0, keepdims=True)


def _acc_rows(ref, val, first):
    @pl.when(first)
    def _():
        ref[...] = val

    @pl.when(jnp.logical_not(first))
    def _():
        ref[...] += val


def _params(sem=None):
    return pltpu.CompilerParams(dimension_semantics=sem, vmem_limit_bytes=VMEM_LIMIT_BYTES)


def _sds(shape, dtype):
    return jax.ShapeDtypeStruct(shape, dtype)


def _rows(tm, width, col=0):
    return pl.BlockSpec((tm, width), lambda i: (i, col))


def _full(shape):
    nd = len(shape)
    return pl.BlockSpec(shape, lambda *_: (0,) * nd)


def _resident(shape):
    nd = len(shape)
    return pl.BlockSpec(shape, lambda *_: (0,) * nd, pipeline_mode=pl.Buffered(1))


def _peers():
    x, y, c = lax.axis_index("x"), lax.axis_index("y"), lax.axis_index("c")
    peers = []
    for k in range(1, N_DEV):
        px = 1 - x if k & 4 else x
        py = 1 - y if k & 2 else y
        pc = 1 - c if k & 1 else c
        peers.append(((px, py, pc), 4 * px + 2 * py + pc))
    return 4 * x + 2 * y + c, peers


class _Xfer(NamedTuple):
    src: jax.Array
    layer: Optional[int] = None
    scatter: bool = False

    @property
    def landing(self):
        block = self.src.shape if self.layer is None and not self.scatter else self.src.shape[1:]
        return _sds((N_DEV,) + block, self.src.dtype)


def _comm_copies(xfers, src_refs, dst_refs, send_sems, recv_sems, local_sems):
    me, peers = _peers()

    def src(t, to):
        ref = src_refs[t] if xfers[t].layer is None else src_refs[t].at[xfers[t].layer]
        return ref.at[to] if xfers[t].scatter else ref

    copies = [pltpu.make_async_copy(src(t, me), dst_refs[t].at[me], local_sems.at[t]) for t in range(len(xfers))]
    for k, (dev, idx) in enumerate(peers):
        for t in range(len(xfers)):
            copies.append(pltpu.make_async_remote_copy(
                src_ref=src(t, idx), dst_ref=dst_refs[t].at[me], send_sem=send_sems.at[t, k], recv_sem=recv_sems.at[t, k],
                device_id=dev, device_id_type=MESH))
    return copies


def _pcall(body, *, name, grid, in_specs, out_specs, out_shape, args, scratch=(), ride=()):
    n_in, n_out, n_scr, nx = len(in_specs), len(out_specs), len(scratch), len(ride)
    params = _params(("arbitrary",) * len(grid))
    if not ride:
        res = pl.pallas_call(body, name=name, grid=grid, in_specs=list(in_specs), out_specs=list(out_specs),
                             out_shape=list(out_shape), scratch_shapes=list(scratch), compiler_params=params,
                             interpret=False)(*args)
        return list(res), []
    def riding(*refs):
        o0 = n_in + nx
        s0 = o0 + n_out + nx
        comm = (ride, refs[n_in:o0], refs[o0 + n_out:s0], *refs[s0 + n_scr:])
        ids = [pl.program_id(a) for a in range(len(grid))]
        first = functools.reduce(jnp.logical_and, [i == 0 for i in ids])
        last = functools.reduce(jnp.logical_and, [i == n - 1 for i, n in zip(ids, grid)])

        @pl.when(first)
        def _():
            for cp in _comm_copies(*comm):
                cp.start()

        body(*refs[:n_in], *refs[o0:o0 + n_out], *refs[s0:s0 + n_scr])

        @pl.when(last)
        def _():
            for cp in _comm_copies(*comm):
                cp.wait()

    sems = [pltpu.SemaphoreType.DMA((nx, N_DEV - 1)), pltpu.SemaphoreType.DMA((nx, N_DEV - 1)), pltpu.SemaphoreType.DMA((nx,))]
    res = pl.pallas_call(riding, name=name, grid=grid, in_specs=list(in_specs) + [ANY] * nx,
                         out_specs=list(out_specs) + [ANY] * nx, out_shape=list(out_shape) + [t.landing for t in ride],
                         scratch_shapes=list(scratch) + sems, compiler_params=params,
                         interpret=False)(*args, *[t.src for t in ride])
    return list(res[:n_out]), list(res[n_out:])


def _comm_call(xfers, name):
    return _pcall(lambda: None, name=name, grid=(1,), in_specs=[], out_specs=[], out_shape=[], args=[], ride=xfers)[1]


def _gather_two_level(xfers, name):
    n = len(xfers)
    assert not any(t.scatter for t in xfers)

    def body(*refs):
        ins, outs = refs[:n], refs[n:2 * n]
        send_sems, recv_sems, local_sems = refs[2 * n:]
        x, y, c = lax.axis_index("x"), lax.axis_index("y"), lax.axis_index("c")
        sibling = (x, y, 1 - c)
        chips = [(1 - x, y), (x, 1 - y), (1 - x, 1 - y)]
        slot = lambda dev: 4 * dev[0] + 2 * dev[1] + dev[2]

        def copy(t, k, block, to, own=False):
            src = ins[t] if xfers[t].layer is None else ins[t].at[xfers[t].layer]
            return pltpu.make_async_remote_copy(
                src_ref=src if own else outs[t].at[slot(block)], dst_ref=outs[t].at[slot(block)],
                send_sem=send_sems.at[t, k], recv_sem=recv_sems.at[t, k], device_id=to, device_id_type=MESH)

        me = (x, y, c)
        local, sent = [], []
        for t in range(n):
            src = ins[t] if xfers[t].layer is None else ins[t].at[xfers[t].layer]
            mine = pltpu.make_async_copy(src, outs[t].at[slot(me)], local_sems.at[t])
            mine.start()
            local.append(mine)
            first = [copy(t, 0, me, sibling, own=True)] + [copy(t, 1 + j, me, (*chip, c), own=True) for j, chip in enumerate(chips)]
            for cp in first:
                cp.start()
            sent += first
        for t in range(n):
            for j, chip in enumerate(chips):
                copy(t, 1 + j, (*chip, c), me).wait_recv()
                passed = copy(t, 4 + j, (*chip, c), sibling)
                passed.start()
                sent.append(passed)
        for t in range(n):
            copy(t, 0, sibling, me).wait_recv()
            for j, chip in enumerate(chips):
                copy(t, 4 + j, (*chip, 1 - c), me).wait_recv()
        for cp in sent:
            cp.wait_send()
        for cp in local:
            cp.wait()

    return pl.pallas_call(
        body, name=name, in_specs=[ANY] * n, out_specs=[ANY] * n, out_shape=[t.landing for t in xfers],
        scratch_shapes=[pltpu.SemaphoreType.DMA((n, N_DEV - 1)), pltpu.SemaphoreType.DMA((n, N_DEV - 1)),
                        pltpu.SemaphoreType.DMA((n,))],
        interpret=False,
    )(*[t.src for t in xfers])


def _ffn_fwd(h, gpre, gpost, wg, wu, wd, name, ride=()):
    S = h.shape[0]
    tm = _tile(S, FFN_FWD_TILE)
    nb = FFN_CHUNK

    def body(h_ref, gpre_ref, gpost_ref, wgt_ref, wut_ref, wd_ref, hn_ref, n_ref, gate_ref, up_ref, act_ref, f_ref):
        hv = h_ref[...]
        n = _rms(hv, gpre_ref[...]).astype(BF16)
        n_ref[...] = n
        for c in range(D_FF // nb):
            cols = slice(c * nb, (c + 1) * nb)
            g = _dot_nt(n, wgt_ref[cols, :])
            u = _dot_nt(n, wut_ref[cols, :])
            gate_ref[:, cols] = g.astype(BF16)
            up_ref[:, cols] = u.astype(BF16)
            act_ref[:, cols] = (g * jax.nn.sigmoid(g) * u).astype(BF16)
        f = _dot(act_ref[...], wd_ref[...])
        f_ref[...] = f
        hn_ref[...] = hv + 0.5 * _rms(f, gpost_ref[...])

    return _pcall(
        body, name=name, grid=(S // tm,),
        in_specs=[_rows(tm, D_MODEL), _full((1, D_MODEL)), _full((1, D_MODEL)),
                  _resident((D_FF, D_MODEL)), _resident((D_FF, D_MODEL)), _resident((D_FF, D_MODEL))],
        out_specs=[_rows(tm, D_MODEL), _rows(tm, D_MODEL), _rows(tm, D_FF), _rows(tm, D_FF), _rows(tm, D_FF),
                   _rows(tm, D_MODEL)],
        out_shape=[_sds((S, D_MODEL), F32), _sds((S, D_MODEL), BF16), _sds((S, D_FF), BF16), _sds((S, D_FF), BF16),
                   _sds((S, D_FF), BF16), _sds((S, D_MODEL), F32)],
        args=(h, gpre, gpost, wg, wu, wd), ride=ride)


def _ffn_bwd(dho, h, f, gate, up, gpre, gpost, wg, wu, wd, name, ride=()):
    S = h.shape[0]
    tm = _tile(S, FFN_BWD_TILE)
    nb = FFN_CHUNK

    def body(dho_ref, h_ref, f_ref, gate_ref, up_ref, gpre_ref, gpost_ref, wgt_ref, wut_ref, wd_ref,
             dh_ref, df_ref, dgate_ref, dup_ref, dgpre_ref, dgpost_ref):
        first = pl.program_id(0) == 0
        dho_v = dho_ref[...]
        dfx, dgpost = _rms_bwd(f_ref[...], gpost_ref[...], 0.5 * dho_v)
        dfb = dfx.astype(BF16)
        df_ref[...] = dfb
        for c in range(D_FF // nb):
            cols = slice(c * nb, (c + 1) * nb)
            dact = _dot_nt(dfb, wd_ref[cols, :])
            g = gate_ref[:, cols].astype(F32)
            u = up_ref[:, cols].astype(F32)
            s = jax.nn.sigmoid(g)
            dgate_ref[:, cols] = (dact * u * (s * (1.0 + g * (1.0 - s)))).astype(BF16)
            dup_ref[:, cols] = (dact * (g * s)).astype(BF16)
        dn = _dot(dgate_ref[...], wgt_ref[...]) + _dot(dup_ref[...], wut_ref[...])
        dhx, dgpre = _rms_bwd(h_ref[...], gpre_ref[...], dn)
        dh_ref[...] = dho_v + dhx
        _acc_rows(dgpre_ref, dgpre, first)
        _acc_rows(dgpost_ref, dgpost, first)

    return _pcall(
        body, name=name, grid=(S // tm,),
        in_specs=[_rows(tm, D_MODEL), _rows(tm, D_MODEL), _rows(tm, D_MODEL), _rows(tm, D_FF), _rows(tm, D_FF),
                  _full((1, D_MODEL)), _full((1, D_MODEL)),
                  _resident((D_FF, D_MODEL)), _resident((D_FF, D_MODEL)), _resident((D_FF, D_MODEL))],
        out_specs=[_rows(tm, D_MODEL), _rows(tm, D_MODEL), _rows(tm, D_FF), _rows(tm, D_FF),
                   _full((1, D_MODEL)), _full((1, D_MODEL))],
        out_shape=[_sds((S, D_MODEL), F32), _sds((S, D_MODEL), BF16), _sds((S, D_FF), BF16), _sds((S, D_FF), BF16),
                   _sds((1, D_MODEL), F32), _sds((1, D_MODEL), F32)],
        args=(dho, h, f, gate, up, gpre, gpost, wg, wu, wd), ride=ride)


def _wgrad(a_parts, b_parts, split, name, ride=()):
    S = a_parts[0].shape[0]
    bk = _tile(S, WGRAD_TILE)
    ms = [a.shape[1] for a in a_parts]
    ns = [b.shape[1] for b in b_parts]
    M, N = sum(ms), sum(ns)
    na, nbp = len(a_parts), len(b_parts)
    blk = (M // N_DEV, N) if split == "rows" else (M, N // N_DEV)
    steps = S // bk

    def body(*refs):
        a_refs, b_refs = refs[:na], refs[na:na + nbp]
        out_ref, acc_ref = refs[-2], refs[-1]
        k = pl.program_id(0)

        @pl.when(k == 0)
        def _():
            acc_ref[...] = jnp.zeros_like(acc_ref)

        r0 = 0
        for ai in range(na):
            av = a_refs[ai][...]
            c0 = 0
            for bi in range(nbp):
                acc_ref[r0:r0 + ms[ai], c0:c0 + ns[bi]] += _dot_tn(av, b_refs[bi][...])
                c0 += ns[bi]
            r0 += ms[ai]

        @pl.when(k == steps - 1)
        def _():
            for d in range(N_DEV):
                if split == "rows":
                    out_ref[d] = acc_ref[d * blk[0]:(d + 1) * blk[0], :].astype(GRAD_WIRE)
                else:
                    out_ref[d] = acc_ref[:, d * blk[1]:(d + 1) * blk[1]].astype(GRAD_WIRE)

    in_specs = [pl.BlockSpec((bk, m), lambda k: (k, 0)) for m in ms] + [pl.BlockSpec((bk, n), lambda k: (k, 0)) for n in ns]
    (out,), landed = _pcall(
        body, name=name, grid=(steps,), in_specs=in_specs, out_specs=[_full((N_DEV,) + blk)],
        out_shape=[_sds((N_DEV,) + blk, GRAD_WIRE)], scratch=[pltpu.VMEM((M, N), F32)],
        args=list(a_parts) + list(b_parts), ride=ride)
    return out, landed


def _mix_in_pool(h, g1, w_in, pool_w, pool_scale, name, ride=()):
    S = h.shape[0]
    tm = _tile(S, ROW_TILE)
    kb = D_MODEL // N_DEV

    def body(h_ref, g_ref, w_ref, pw_ref, ps_ref, u_ref, dpre_ref, tok_ref, qm_ref, ext_ref):
        i = pl.program_id(0)
        u = _rms(h_ref[...], g_ref[...]).astype(BF16)
        u_ref[...] = u
        proj = jnp.zeros((tm, D_MIX), F32)
        for d in range(N_DEV):
            proj = proj + _dot(u[:, d * kb:(d + 1) * kb], w_ref[d])
        qm_ref[...] = proj[:, D_TOK:].astype(BF16)
        x = proj[:, :D_TOK]

        @pl.when(i == 0)
        def _():
            ext_ref[0:POOL_HALO, :] = jnp.zeros((POOL_HALO, D_TOK), F32)

        ext_ref[POOL_HALO:, :] = x
        pos = i * tm + lax.broadcasted_iota(jnp.int32, (tm, 1), 0)
        for gi, w in enumerate(POOL_WINDOWS):
            cols = slice(gi * POOL_GROUP, (gi + 1) * POOL_GROUP)
            xs = x[:, cols]
            wsum = xs
            for k in range(1, w):
                wsum = wsum + ext_ref[POOL_HALO - k:POOL_HALO - k + tm, cols]
            cnt = jnp.minimum(pos + 1, w).astype(F32)
            dg = (wsum / cnt - xs).astype(BF16)
            dpre_ref[:, cols] = dg
            yv = _dot(dg, pw_ref[gi].astype(BF16))
            tok_ref[:, cols] = (yv * ps_ref[:, cols]).astype(BF16)
        ext_ref[0:POOL_HALO, :] = x[tm - POOL_HALO:, :]

    return _pcall(
        body, name=name, grid=(S // tm,),
        in_specs=[_rows(tm, D_MODEL), _full((1, D_MODEL)), _full((N_DEV, kb, D_MIX)),
                  _full((len(POOL_WINDOWS), POOL_GROUP, POOL_GROUP)), _full((1, D_TOK))],
        out_specs=[_rows(tm, D_MODEL), _rows(tm, D_TOK), _rows(tm, D_TOK), _rows(tm, D_MEMH)],
        out_shape=[_sds((S, D_MODEL), BF16), _sds((S, D_TOK), BF16), _sds((S, D_TOK), BF16), _sds((S, D_MEMH), BF16)],
        scratch=[pltpu.VMEM((POOL_HALO + tm, D_TOK), F32)],
        args=(h, g1, w_in, pool_w, pool_scale), ride=ride)


def _pool_bwd(dtok, dpre, pool_w, pool_scale, name):
    S = dtok.shape[0]
    tm = _tile(S, ROW_TILE)
    nt = S // tm
    ng = len(POOL_WINDOWS)

    def body(dtok_ref, dpre_ref, pw_ref, ps_ref, dx_ref, dpw_ref, dps_ref, ext_ref):
        i = pl.program_id(0)
        first = i == 0
        t0 = (nt - 1 - i) * tm
        pos = t0 + lax.broadcasted_iota(jnp.int32, (tm, 1), 0)

        @pl.when(first)
        def _():
            ext_ref[tm:, :] = jnp.zeros((POOL_HALO, D_TOK), F32)

        dps = []
        for gi, w in enumerate(POOL_WINDOWS):
            cols = slice(gi * POOL_GROUP, (gi + 1) * POOL_GROUP)
            dg = dpre_ref[:, cols]
            pw = pw_ref[gi].astype(BF16)
            dt = dtok_ref[:, cols].astype(F32)
            yv = _dot(dg, pw)
            dps.append(jnp.sum(dt * yv, axis=0, keepdims=True))
            dy = (dt * ps_ref[:, cols]).astype(BF16)
            _acc_rows(dpw_ref.at[gi], _dot_tn(dg, dy), first)
            dd = _dot_nt(dy, pw)
            cnt = jnp.minimum(pos + 1, w).astype(F32)
            ext_ref[0:tm, cols] = dd / cnt
            wsum = ext_ref[0:tm, cols]
            for k in range(1, w):
                wsum = wsum + ext_ref[k:k + tm, cols]
            dx_ref[:, cols] = (wsum - dd).astype(BF16)
        _acc_rows(dps_ref, jnp.concatenate(dps, axis=1), first)
        ext_ref[tm:, :] = ext_ref[0:POOL_HALO, :]

    rev = lambda i: (nt - 1 - i, 0)
    return pl.pallas_call(
        body, name=name, grid=(nt,),
        in_specs=[pl.BlockSpec((tm, D_TOK), rev), pl.BlockSpec((tm, D_TOK), rev),
                  _full((ng, POOL_GROUP, POOL_GROUP)), _full((1, D_TOK))],
        out_specs=[pl.BlockSpec((tm, D_TOK), rev), _full((ng, POOL_GROUP, POOL_GROUP)), _full((1, D_TOK))],
        out_shape=[_sds((S, D_TOK), BF16), _sds((ng, POOL_GROUP, POOL_GROUP), F32), _sds((1, D_TOK), F32)],
        scratch_shapes=[pltpu.VMEM((tm + POOL_HALO, D_TOK), F32)],
        compiler_params=_params(("arbitrary",)),
        interpret=False,
    )(dtok, dpre, pool_w, pool_scale)


def _mix_in_sb(h, g1, wt, name):
    S = h.shape[0]
    tm = _tile(S, ROW_TILE)
    cb = 256

    def body(h_ref, g_ref, wt_ref, u_ref, proj_ref):
        u = _rms(h_ref[...], g_ref[...]).astype(BF16)
        u_ref[...] = u
        for c in range(D_SB // cb):
            proj_ref[:, c * cb:(c + 1) * cb] = _dot_nt(u, wt_ref[c * cb:(c + 1) * cb, :]).astype(BF16)

    return pl.pallas_call(
        body, name=name, grid=(S // tm,),
        in_specs=[_rows(tm, D_MODEL), _full((1, D_MODEL)), _resident((D_SB, D_MODEL))],
        out_specs=[_rows(tm, D_MODEL), _rows(tm, D_SB)],
        out_shape=[_sds((S, D_MODEL), BF16), _sds((S, D_SB), BF16)],
        compiler_params=_params(("arbitrary",)),
        interpret=False,
    )(h, g1, wt)


def _mix_in_bwd(dho, h, g1, parts, w, mode, name):
    S = h.shape[0]
    tm = _tile(S, ROW_TILE)
    widths = [p.shape[1] for p in parts]
    npart = len(parts)
    kb = D_MODEL // N_DEV

    def body(*refs):
        dho_ref, h_ref, g_ref = refs[:3]
        p_refs = refs[3:3 + npart]
        w_ref, dh_ref, dg_ref = refs[3 + npart:]
        first = pl.program_id(0) == 0
        if mode == "pool":
            dproj = jnp.concatenate([p[...] for p in p_refs], axis=1)
            du = jnp.concatenate([_dot_nt(dproj, w_ref[d]) for d in range(N_DEV)], axis=1)
        else:
            du = jnp.zeros((tm, D_MODEL), F32)
            r0 = 0
            for p, wd_ in zip(p_refs, widths):
                du = du + _dot(p[...], w_ref[r0:r0 + wd_, :])
                r0 += wd_
        dhx, dg = _rms_bwd(h_ref[...], g_ref[...], du)
        dh_ref[...] = dho_ref[...] + dhx
        _acc_rows(dg_ref, dg, first)

    w_spec = _full((N_DEV, kb, D_MIX)) if mode == "pool" else _full((D_SB, D_MODEL))
    return pl.pallas_call(
        body, name=name, grid=(S // tm,),
        in_specs=[_rows(tm, D_MODEL), _rows(tm, D_MODEL), _full((1, D_MODEL))] + [_rows(tm, wd_) for wd_ in widths] + [w_spec],
        out_specs=[_rows(tm, D_MODEL), _full((1, D_MODEL))],
        out_shape=[_sds((S, D_MODEL), F32), _sds((1, D_MODEL), F32)],
        compiler_params=_params(("arbitrary",)),
        interpret=False,
    )(dho, h, g1, *parts, w)


def _sb_block(qcats, kb, mask, later, tri_later):
    z = jnp.concatenate([_dot_nt(qc, _group(kb, g)) for g, qc in enumerate(qcats)], axis=0) * ATT_SCALE
    en = jnp.exp(-jnp.abs(z))
    ls = jnp.minimum(z, 0.0) - jnp.log(1.0 + en)
    lf = ls - z if mask is None else jnp.where(mask, ls - z, 0.0)
    within = _split_dot(lf, tri_later, 2)
    a = jnp.exp(ls + within + later)
    if mask is not None:
        a = jnp.where(mask, a, 0.0)
    return z, en, a, jnp.sum(lf, axis=1, keepdims=True)


def _sb_causal(rows):
    row = jnp.bitwise_and(lax.broadcasted_iota(jnp.int32, (rows, Q_BLOCK), 0), Q_BLOCK - 1)
    return lax.broadcasted_iota(jnp.int32, (rows, Q_BLOCK), 1) < row


def _sb_walk(qi, block, state, later_of, unrolled):
    if unrolled:
        state = block(qi, _sb_causal, state)
        state = block(jnp.maximum(qi - 1, 0), qi >= 1, state)

    def step(carry):
        j, _, state = carry
        mask = None if unrolled else lambda rows: jnp.logical_or(_sb_causal(rows), j > 0)
        state = block(qi - j, mask, state)
        return j + 1, _sb_alive(later_of(state)), state

    first = jnp.int32(2 if unrolled else 0)
    return lax.while_loop(functools.partial(_sb_more, qi), step, (first, _sb_alive(later_of(state)), state))[2]


def _sb_alive(later):
    return jnp.max(later) > SB_DEAD_LOG_WEIGHT


def _sb_more(qi, carry):
    return jnp.logical_and(carry[0] <= qi, carry[1])


def _tri(strict):
    row = lax.broadcasted_iota(jnp.int32, (Q_BLOCK, Q_BLOCK), 0)
    col = lax.broadcasted_iota(jnp.int32, (Q_BLOCK, Q_BLOCK), 1)
    return (row > col if strict else row >= col).astype(BF16)


HEADS_PER_GROUP = 128 // HEAD_DIM
GROUP_ROWS = HEADS_PER_GROUP * Q_BLOCK


def _sb_head_masks():
    lane = lax.broadcasted_iota(jnp.int32, (Q_BLOCK, 128), 1)
    return [(lane >= e * HEAD_DIM) & (lane < (e + 1) * HEAD_DIM) for e in range(HEADS_PER_GROUP)]


def _group(x, g):
    return x[:, g * 128:(g + 1) * 128]


def _masked(hm, x):
    return jnp.where(hm, x, jnp.zeros_like(x))


def _stack_heads(hms, x):
    return jnp.concatenate([_masked(hm, x) for hm in hms], axis=0)


def _own_lanes(hms, r, rows=Q_BLOCK):
    return sum(_masked(hm, r[e * rows:(e + 1) * rows]) for e, hm in enumerate(hms))


def _sb_fwd(proj, name, ride=()):
    S = proj.shape[0]
    nq = S // Q_BLOCK
    W = SB_FWD_LANES
    nrow = D_TOK // W
    groups = W // 128

    def body(q_ref, k_ref, v_ref, o_ref, tok_ref):
        qi = pl.program_id(1)
        hms = _sb_head_masks()
        tri_later = _tri(True)
        q = q_ref[...]
        qcats = [_stack_heads(hms, _group(q, g)) for g in range(groups)]

        def block(kblock, mask, state):
            accs, later = state
            if callable(mask):
                mask = mask(later.shape[0])
            off = pl.multiple_of(kblock * Q_BLOCK, Q_BLOCK)
            kb = k_ref[pl.ds(off, Q_BLOCK), :]
            vb = v_ref[pl.ds(off, Q_BLOCK), :]
            _, _, a, bsum = _sb_block(qcats, kb, mask, later, tri_later)
            hi = a.astype(BF16)
            lo = (a - hi.astype(F32)).astype(BF16)
            accs = list(accs)
            for g in range(groups):
                rows = slice(g * GROUP_ROWS, (g + 1) * GROUP_ROWS)
                r = _dot(jnp.concatenate([hi[rows], lo[rows]], axis=0), _group(vb, g))
                accs[g] = accs[g] + _own_lanes(hms, r[:GROUP_ROWS] + r[GROUP_ROWS:])
            return tuple(accs), later + bsum

        init = ((jnp.zeros((Q_BLOCK, 128), F32),) * groups, jnp.zeros((groups * GROUP_ROWS, 1), F32))
        accs = _sb_walk(qi, block, init, lambda state: state[1], unrolled=True)[0]
        for g, acc in enumerate(accs):
            o_ref[:, g * 128:(g + 1) * 128] = acc
            tok_ref[:, g * 128:(g + 1) * 128] = acc.astype(BF16)

    blk = pl.BlockSpec((Q_BLOCK, W), lambda p, i: (i, p))
    return _pcall(
        body, name=name, grid=(nrow, nq),
        in_specs=[blk, pl.BlockSpec((S, W), lambda p, i: (0, nrow + p)), pl.BlockSpec((S, W), lambda p, i: (0, 2 * nrow + p))],
        out_specs=[blk, blk],
        out_shape=[_sds((S, D_TOK), F32), _sds((S, D_TOK), BF16)],
        args=(proj, proj, proj), ride=ride)


def _sb_bwd(proj, dtok, o32, name, ride=()):
    S = proj.shape[0]
    nq = S // Q_BLOCK
    W = SB_BWD_LANES
    nrow = D_TOK // W
    groups = W // 128

    def body(q_ref, k_ref, v_ref, do_ref, o_ref, dq_ref, dk_ref, dv_ref, dk_acc, dv_acc):
        qi = pl.program_id(1)

        @pl.when(qi == 0)
        def _():
            dk_acc[...] = jnp.zeros_like(dk_acc)
            dv_acc[...] = jnp.zeros_like(dv_acc)

        hms = _sb_head_masks()
        tri_later = _tri(True)
        tri_from = _tri(False)
        q = q_ref[...]
        do = do_ref[...]
        dov = do.astype(F32) * o_ref[...]
        qcats = [_stack_heads(hms, _group(q, g)) for g in range(groups)]
        docats = [_stack_heads(hms, _group(do, g)) for g in range(groups)]
        rowtot = jnp.concatenate([jnp.sum(jnp.where(hm, _group(dov, g), 0.0), axis=1, keepdims=True)
                                  for g in range(groups) for hm in hms], axis=0)

        def block(kblock, mask, state):
            dqs, later, seen = state
            if callable(mask):
                mask = mask(later.shape[0])
            off = pl.multiple_of(kblock * Q_BLOCK, Q_BLOCK)
            kb = k_ref[pl.ds(off, Q_BLOCK), :]
            vb = v_ref[pl.ds(off, Q_BLOCK), :]
            z, en, a, bsum = _sb_block(qcats, kb, mask, later, tri_later)
            inv = 1.0 / (1.0 + en)
            beta = jnp.where(z >= 0, 1.0, en) * inv
            omb = jnp.where(z >= 0, en, 1.0) * inv
            dlogw = a * jnp.concatenate([_dot_nt(docats[g], _group(vb, g)) for g in range(groups)], axis=0)
            prefix = rowtot - seen - _split_dot(dlogw, tri_from, 2)
            dz = dlogw * omb - beta * prefix
            if mask is not None:
                dz = jnp.where(mask, dz, 0.0)
            dz = dz.astype(BF16)
            ab = a.astype(BF16)
            dqs = list(dqs)
            for g in range(groups):
                rows = slice(g * GROUP_ROWS, (g + 1) * GROUP_ROWS)
                lanes = slice(g * 128, (g + 1) * 128)
                dqs[g] = dqs[g] + _own_lanes(hms, _dot(dz[rows], _group(kb, g)))
                dk_acc[pl.ds(off, Q_BLOCK), lanes] += _dot_tn(dz[rows], qcats[g]) * ATT_SCALE
                dv_acc[pl.ds(off, Q_BLOCK), lanes] += _dot_tn(ab[rows], docats[g])
            return tuple(dqs), later + bsum, seen + jnp.sum(dlogw, axis=1, keepdims=True)

        zero = jnp.zeros((groups * GROUP_ROWS, 1), F32)
        init = ((jnp.zeros((Q_BLOCK, 128), F32),) * groups, zero, zero)
        dqs = _sb_walk(qi, block, init, lambda state: state[1], unrolled=False)[0]
        for g, dq in enumerate(dqs):
            dq_ref[:, g * 128:(g + 1) * 128] = (dq * ATT_SCALE).astype(BF16)

        @pl.when(qi == nq - 1)
        def _():
            dk_ref[...] = dk_acc[...].astype(BF16)
            dv_ref[...] = dv_acc[...].astype(BF16)

    blk = pl.BlockSpec((Q_BLOCK, W), lambda p, i: (i, p))
    col = pl.BlockSpec((S, W), lambda p, i: (0, p))
    return _pcall(
        body, name=name, grid=(nrow, nq),
        in_specs=[blk, pl.BlockSpec((S, W), lambda p, i: (0, nrow + p)), pl.BlockSpec((S, W), lambda p, i: (0, 2 * nrow + p)),
                  blk, blk],
        out_specs=[blk, col, col],
        out_shape=[_sds((S, D_TOK), BF16), _sds((S, D_TOK), BF16), _sds((S, D_TOK), BF16)],
        scratch=[pltpu.VMEM((S, W), F32), pltpu.VMEM((S, W), F32)],
        args=(proj, proj, proj, dtok, o32), ride=ride)


def _mem_kv_fwd(mem, g_mem, w_kv, name):
    lm = mem.shape[0]
    kb = D_MODEL // N_DEV

    def body(mem_ref, g_ref, w_ref, mn_ref, km_ref, vm_ref):
        mn = _rms(mem_ref[...], g_ref[...]).astype(BF16)
        mn_ref[...] = mn
        kv = jnp.zeros((lm, 2 * D_MEMH), F32)
        for d in range(N_DEV):
            kv = kv + _dot(mn[:, d * kb:(d + 1) * kb], w_ref[d])
        km_ref[...] = kv[:, :D_MEMH].astype(BF16)
        vm_ref[...] = kv[:, D_MEMH:].astype(BF16)

    return pl.pallas_call(
        body, name=name, grid=(1,),
        in_specs=[_full((lm, D_MODEL)), _full((1, D_MODEL)), _full((N_DEV, kb, 2 * D_MEMH))],
        out_specs=[_full((lm, D_MODEL)), _full((lm, D_MEMH)), _full((lm, D_MEMH))],
        out_shape=[_sds((lm, D_MODEL), BF16), _sds((lm, D_MEMH), BF16), _sds((lm, D_MEMH), BF16)],
        compiler_params=_params(("arbitrary",)),
        interpret=False,
    )(mem, g_mem, w_kv)


def _mem_kv_bwd(dkm, dvm, mem, g_mem, mem_n, w_kv, name):
    lm = mem.shape[0]
    kb = D_MODEL // N_DEV

    def body(dkm_ref, dvm_ref, mem_ref, g_ref, mn_ref, w_ref, dw_ref, dg_ref):
        dkv = jnp.concatenate([dkm_ref[...], dvm_ref[...]], axis=1).astype(BF16)
        dw = _dot_tn(mn_ref[...], dkv)
        for d in range(N_DEV):
            dw_ref[d] = dw[d * kb:(d + 1) * kb, :].astype(GRAD_WIRE)
        dmn = jnp.concatenate([_dot_nt(dkv, w_ref[d]) for d in range(N_DEV)], axis=1)
        _, dg = _rms_bwd(mem_ref[...], g_ref[...], dmn)
        dg_ref[...] = dg

    return pl.pallas_call(
        body, name=name, grid=(1,),
        in_specs=[_full((lm, D_MEMH)), _full((lm, D_MEMH)), _full((lm, D_MODEL)), _full((1, D_MODEL)),
                  _full((lm, D_MODEL)), _full((N_DEV, kb, 2 * D_MEMH))],
        out_specs=[_full((N_DEV, kb, 2 * D_MEMH)), _full((1, D_MODEL))],
        out_shape=[_sds((N_DEV, kb, 2 * D_MEMH), GRAD_WIRE), _sds((1, D_MODEL), F32)],
        compiler_params=_params(("arbitrary",)),
        interpret=False,
    )(dkm, dvm, mem, g_mem, mem_n, w_kv)


def _mem_heads(tm):
    lane = lax.broadcasted_iota(jnp.int32, (tm, D_MEMH), 1)
    return [(lane >= e * HEAD_DIM) & (lane < (e + 1) * HEAD_DIM) for e in range(D_MEMH // HEAD_DIM)]


def _softmax(s):
    m = jnp.max(s, axis=-1, keepdims=True)
    p = jnp.exp(s - m)
    return p / jnp.sum(p, axis=-1, keepdims=True)


def _mix_out_fwd(h, tok, qm, qm_col, km, vm, w_out, gpost, name, ride=()):
    S = h.shape[0]
    tm = _tile(S, ROW_TILE)
    lm = km.shape[0]
    nb = D_MODEL // N_DEV

    def body(h_ref, tok_ref, qm_ref, km_ref, vm_ref, w_ref, g_ref, hn_ref, mo_ref, mix_ref):
        qv = qm_ref[...]
        kv, vv = km_ref[...], vm_ref[...]
        hms = _mem_heads(tm)
        p = _softmax(_dot_nt(_stack_heads(hms, qv), kv) * ATT_SCALE)
        mob = _own_lanes(hms, _dot(p.astype(BF16), vv), tm).astype(BF16)
        mo_ref[...] = mob
        tv = tok_ref[...]
        mix = jnp.concatenate(
            [_dot(tv, w_ref[d, 0:D_TOK, :]) + _dot(mob, w_ref[d, D_TOK:D_MIX, :]) for d in range(N_DEV)], axis=1)
        mix_ref[...] = mix
        hn_ref[...] = h_ref[...] + _rms(mix, g_ref[...])

    return _pcall(
        body, name=name, grid=(S // tm,),
        in_specs=[_rows(tm, D_MODEL), _rows(tm, D_TOK), _rows(tm, D_MEMH, qm_col), _full((lm, D_MEMH)), _full((lm, D_MEMH)),
                  _full((N_DEV, D_MIX, nb)), _full((1, D_MODEL))],
        out_specs=[_rows(tm, D_MODEL), _rows(tm, D_MEMH), _rows(tm, D_MODEL)],
        out_shape=[_sds((S, D_MODEL), F32), _sds((S, D_MEMH), BF16), _sds((S, D_MODEL), F32)],
        args=(h, tok, qm, km, vm, w_out, gpost), ride=ride)


def _mix_out_bwd(dho, mix, qm, qm_col, km, vm, w_out, gpost, name, ride=()):
    S = dho.shape[0]
    tm = _tile(S, ROW_TILE)
    lm = km.shape[0]
    nb = D_MODEL // N_DEV

    def body(dho_ref, mix_ref, qm_ref, km_ref, vm_ref, w_ref, g_ref,
             dmix_ref, dtok_ref, dqm_ref, dkm_ref, dvm_ref, dg_ref):
        first = pl.program_id(0) == 0
        dmx, dg = _rms_bwd(mix_ref[...], g_ref[...], dho_ref[...])
        dmb = dmx.astype(BF16)
        dmix_ref[...] = dmb
        _acc_rows(dg_ref, dg, first)
        dcat = jnp.zeros((tm, D_MIX), F32)
        for d in range(N_DEV):
            dcat = dcat + _dot_nt(dmb[:, d * nb:(d + 1) * nb], w_ref[d])
        dtok_ref[...] = dcat[:, :D_TOK].astype(BF16)
        dmo = dcat[:, D_TOK:].astype(BF16)
        qv = qm_ref[...]
        kv, vv = km_ref[...], vm_ref[...]
        hms = _mem_heads(tm)
        qcat = _stack_heads(hms, qv)
        dmcat = _stack_heads(hms, dmo)
        p = _softmax(_dot_nt(qcat, kv) * ATT_SCALE)
        dp = _dot_nt(dmcat, vv)
        ds = (p * (dp - jnp.sum(p * dp, axis=-1, keepdims=True))).astype(BF16)
        dq = _own_lanes(hms, _dot(ds, kv), tm)
        dk = _dot_tn(ds, qcat)
        dv = _dot_tn(p.astype(BF16), dmcat)
        dqm_ref[...] = (dq * ATT_SCALE).astype(BF16)
        _acc_rows(dkm_ref, dk * ATT_SCALE, first)
        _acc_rows(dvm_ref, dv, first)

    return _pcall(
        body, name=name, grid=(S // tm,),
        in_specs=[_rows(tm, D_MODEL), _rows(tm, D_MODEL), _rows(tm, D_MEMH, qm_col), _full((lm, D_MEMH)), _full((lm, D_MEMH)),
                  _full((N_DEV, D_MIX, nb)), _full((1, D_MODEL))],
        out_specs=[_rows(tm, D_MODEL), _rows(tm, D_TOK), _rows(tm, D_MEMH), _full((lm, D_MEMH)), _full((lm, D_MEMH)),
                   _full((1, D_MODEL))],
        out_shape=[_sds((S, D_MODEL), BF16), _sds((S, D_TOK), BF16), _sds((S, D_MEMH), BF16),
                   _sds((lm, D_MEMH), F32), _sds((lm, D_MEMH), F32), _sds((1, D_MODEL), F32)],
        args=(dho, mix, qm, km, vm, w_out, gpost), ride=ride)


def _loss_head(y, target, name):
    S = y.shape[0]
    tm = _tile(S, ROW_TILE)
    nt = S // tm

    def body(y_ref, t_ref, dy_ref, loss_ref, acc_ref):
        i = pl.program_id(0)
        e = y_ref[...] - t_ref[...]
        dy_ref[...] = e * (1.0 / D_MODEL)
        _acc_rows(acc_ref, jnp.sum(e * e, axis=0, keepdims=True), i == 0)

        @pl.when(i == nt - 1)
        def _():
            tot = jnp.sum(acc_ref[...], axis=1, keepdims=True) * (0.5 / D_MODEL)
            loss_ref[...] = jnp.broadcast_to(tot, (1, 128))

    return pl.pallas_call(
        body, name=name, grid=(nt,),
        in_specs=[_rows(tm, D_MODEL), _rows(tm, D_MODEL)],
        out_specs=[_rows(tm, D_MODEL), _full((1, 128))],
        out_shape=[_sds((S, D_MODEL), F32), _sds((1, 128), F32)],
        scratch_shapes=[pltpu.VMEM((1, D_MODEL), F32)],
        compiler_params=_params(("arbitrary",)),
        interpret=False,
    )(y, target)


def _adamw(recv, w, m, v, l, into, name):
    L, R, C = w.shape
    tr = R if R * C <= ADAM_TILE_ELEMS else _tile(R, ADAM_TILE_ELEMS // C)
    c1 = 1.0 - ADAM_B1 ** ADAM_STEP
    c2 = 1.0 - ADAM_B2 ** ADAM_STEP

    def body(r_ref, w_ref, m_ref, v_ref, *rest):
        g_ref, d_ref, nm_ref, nv_ref = rest[-4:]
        g = r_ref[0].astype(F32)
        for s in range(1, N_DEV):
            g = g + r_ref[s].astype(F32)
        g_ref[...] = g
        nm = ADAM_B1 * m_ref[...] + (1.0 - ADAM_B1) * g
        nv = ADAM_B2 * v_ref[...] + (1.0 - ADAM_B2) * (g * g)
        nm_ref[...] = nm
        nv_ref[...] = nv
        d_ref[...] = -ADAM_LR * ((nm / c1) / (jnp.sqrt(nv / c2) + ADAM_EPS) + ADAM_WD * w_ref[...])

    t = pl.BlockSpec((None, tr, C), lambda i: (l, i, 0))
    kept = [] if into is None else list(into)
    return pl.pallas_call(
        body, name=name, grid=(R // tr,),
        in_specs=[pl.BlockSpec((N_DEV, tr, C), lambda i: (0, i, 0)), t, t, t] + [ANY] * len(kept),
        out_specs=[t, t, t, t],
        out_shape=[_sds((L, R, C), F32)] * 4,
        input_output_aliases={4 + q: q for q in range(len(kept))},
        compiler_params=_params(("arbitrary",)),
        interpret=False,
    )(recv, w, m, v, *kept)


def _step(p, opt_m, opt_v, x, mem, target):
    bf = lambda a: a.astype(BF16)
    row = lambda a: a.reshape(1, -1)
    tsb = lambda a: jnp.swapaxes(a, 1, 2)
    g_mem = p["g_mem"]

    wsb_t = tsb(p["w_in_sb"])
    travels_transposed = ("w_in_sb", "ffn1_gate", "ffn1_up", "ffn2_gate", "ffn2_up")
    shard = {n: bf(tsb(p[n]) if n in travels_transposed else p[n]) for n in STACKED}
    ffn_weights = lambda which, i: [gw[k].reshape(D_FF, D_MODEL) for k in ffn(which, i)]
    w_in = lambda i: "w_in_pool" if i % 2 == 0 else "w_in_sb"
    ffn = lambda which, i: [(f"ffn{which}_{s}", i) for s in ("gate", "up", "down")]
    mixing = lambda i: [(w_in(i), i // 2), ("w_mem_kv", i), ("w_out", i)]

    gw = {}

    def gather(keys):
        return [_Xfer(shard[n], l) for n, l in keys]

    first = ffn(1, 0)
    landed = _gather_two_level(gather(first) + [_Xfer(p["g_pre"]), _Xfer(p["g_post"])], "gather_first")
    gw.update(zip(first, landed))
    unshard = lambda g: jnp.transpose(g, (1, 2, 0, 3)).reshape(DEPTH, 3, D_MODEL)
    g_pre, g_post = unshard(landed[-2]), unshard(landed[-1])

    def ahead(i):
        nxt = i + 1 < DEPTH
        if i % 2 == 0:
            return {"ffn1": ffn(2, i)[:2] + (mixing(0) if i == 0 else []), "mix_in": ffn(2, i)[2:], "sb": [],
                    "mix_out": ffn(1, i + 1)[:1] if nxt else [], "ffn2": ffn(1, i + 1)[1:] if nxt else []}
        return {"ffn1": mixing(i), "mix_in": [], "sb": ffn(2, i) + (ffn(1, i + 1)[:2] if nxt else []),
                "mix_out": [], "ffn2": ffn(1, i + 1)[2:] + mixing(i + 1) if nxt else []}

    def gathering(keys, call):
        res, landed = call(ride=gather(keys))
        gw.update(zip(keys, landed))
        return res

    saved = []
    h = x
    for i in range(DEPTH):
        j = i // 2
        st = {"h0": h}
        carry = ahead(i)
        h, st["n1"], st["gate1"], st["up1"], st["act1"], st["f1"] = gathering(carry["ffn1"], functools.partial(
            _ffn_fwd, h, row(g_pre[i, 0]), row(g_post[i, 0]), *ffn_weights(1, i), f"ffn1_fwd_{i}"))
        st["h1"] = h
        if i % 2 == 0:
            st["u"], st["dpre"], st["tok"], st["qm"] = gathering(carry["mix_in"], functools.partial(
                _mix_in_pool, h, row(g_pre[i, 1]), gw[("w_in_pool", j)], p["pool_w"][j], row(p["pool_scale"][j]),
                f"mix_in_pool_{i}"))
            qm, qm_col = st["qm"], 0
        else:
            st["u"], st["proj"] = _mix_in_sb(h, row(g_pre[i, 1]), gw[("w_in_sb", j)].reshape(D_SB, D_MODEL), f"mix_in_sb_{i}")
            st["o32"], st["tok"] = gathering(carry["sb"], functools.partial(_sb_fwd, st["proj"], f"sb_fwd_{i}"))
            qm, qm_col = st["proj"], 3 * D_TOK // D_MEMH
        st["mem_n"], st["km"], st["vm"] = _mem_kv_fwd(mem, row(g_mem[i]), gw[("w_mem_kv", i)], f"mem_kv_fwd_{i}")
        h, st["mo"], st["mix"] = gathering(carry["mix_out"], functools.partial(
            _mix_out_fwd, h, st["tok"], qm, qm_col, st["km"], st["vm"], gw[("w_out", i)], row(g_post[i, 1]), f"mix_out_fwd_{i}"))
        st["h2"] = h
        h, st["n2"], st["gate2"], st["up2"], st["act2"], st["f2"] = gathering(carry["ffn2"], functools.partial(
            _ffn_fwd, h, row(g_pre[i, 2]), row(g_post[i, 2]), *ffn_weights(2, i), f"ffn2_fwd_{i}"))
        saved.append(st)

    dh, loss_part = _loss_head(h, target, "loss_head")

    grads = {}
    recv = {}
    dg_pre = [[None] * 3 for _ in range(DEPTH)]
    dg_post = [[None] * 3 for _ in range(DEPTH)]
    dg_mem = [None] * DEPTH
    dpool_w = [None, None]
    dpool_scale = [None, None]

    def scatter(keys):
        return [_Xfer(grads[k], scatter=True) for k in keys]

    def ffn_backward(dh, st, i, which, hkey, slot, riding, last):
        sfx = str(which)
        keys = ffn(which, i)
        (dh, df, dgate, dup, dg_pre[i][slot], dg_post[i][slot]), landed = _ffn_bwd(
            dh, st[hkey], st["f" + sfx], st["gate" + sfx], st["up" + sfx], row(g_pre[i, slot]), row(g_post[i, slot]),
            *ffn_weights(which, i), f"ffn{sfx}_bwd_{i}", ride=scatter(riding))
        recv.update(zip(riding, landed))
        riders = [mixing(i), keys[:1], keys[1:2]] if last else [[], [], []]
        operands = ((st["n" + sfx], dgate, "cols"), (st["n" + sfx], dup, "cols"), (st["act" + sfx], df, "rows"))
        for key, (a, b, split), riding in zip(keys, operands, riders):
            grads[key], landed = _wgrad([a], [b], split, f"wgrad_{key[0]}_{i}", ride=scatter(riding))
            recv.update(zip(riding, landed))
        return dh

    def behind(i):
        prev = ffn(1, i + 1) if i + 1 < DEPTH else []
        if i % 2 == 1:
            return {"ffn2": prev, "mix_out": [], "sb": ffn(2, i), "ffn1": mixing(i)}
        if i > 0:
            return {"ffn2": prev, "mix_out": ffn(2, i)[:2], "sb": [], "ffn1": ffn(2, i)[2:] + mixing(i)}
        return {"ffn2": prev, "mix_out": ffn(2, 0), "sb": [], "ffn1": []}

    for i in reversed(range(DEPTH)):
        j = i // 2
        st = saved[i]
        carry = behind(i)
        dh = ffn_backward(dh, st, i, 2, "h2", 2, carry["ffn2"], False)
        if i % 2 == 0:
            qm, qm_col = st["qm"], 0
        else:
            qm, qm_col = st["proj"], 3 * D_TOK // D_MEMH
        keys = carry["mix_out"]
        (dmix, dtok, dqm, dkm, dvm, dg_post[i][1]), landed = _mix_out_bwd(
            dh, st["mix"], qm, qm_col, st["km"], st["vm"], gw[("w_out", i)], row(g_post[i, 1]), f"mix_out_bwd_{i}", ride=scatter(keys))
        recv.update(zip(keys, landed))
        grads[("w_out", i)], _ = _wgrad([st["tok"], st["mo"]], [dmix], "cols", f"wgrad_w_out_{i}")
        grads[("w_mem_kv", i)], dg_mem[i] = _mem_kv_bwd(dkm, dvm, mem, row(g_mem[i]), st["mem_n"], gw[("w_mem_kv", i)],
                                                        f"mem_kv_bwd_{i}")
        if i % 2 == 0:
            dx, dpool_w[j], dpool_scale[j] = _pool_bwd(dtok, st["dpre"], p["pool_w"][j], row(p["pool_scale"][j]), f"pool_bwd_{i}")
            parts = [dx, dqm]
            dh, dg_pre[i][1] = _mix_in_bwd(dh, st["h1"], row(g_pre[i, 1]), parts, gw[("w_in_pool", j)], "pool", f"mix_in_bwd_{i}")
            grads[("w_in_pool", j)], _ = _wgrad([st["u"]], parts, "rows", f"wgrad_w_in_pool_{i}")
        else:
            (dq, dk, dv), landed = _sb_bwd(st["proj"], dtok, st["o32"], f"sb_bwd_{i}", ride=scatter(carry["sb"]))
            recv.update(zip(carry["sb"], landed))
            parts = [dq, dk, dv, dqm]
            dh, dg_pre[i][1] = _mix_in_bwd(dh, st["h1"], row(g_pre[i, 1]), parts, gw[("w_in_sb", j)].reshape(D_SB, D_MODEL), "sb",
                                           f"mix_in_bwd_{i}")
            grads[("w_in_sb", j)], _ = _wgrad(parts, [st["u"]], "rows", f"wgrad_w_in_sb_{i}")
        dh = ffn_backward(dh, st, i, 1, "h0", 0, carry["ffn1"], i == 0)
    grad_x = dh

    shard8 = lambda rows_: jnp.transpose(jnp.stack([jnp.concatenate(r, axis=0) for r in rows_]).reshape(DEPTH, 3, N_DEV, -1),
                                         (2, 0, 1, 3))
    tail = ffn(1, 0)[2:]
    landed = _comm_call(
        scatter(tail) + [_Xfer(shard8(dg_pre), scatter=True), _Xfer(shard8(dg_post), scatter=True),
                         _Xfer(jnp.concatenate(dg_mem, axis=0)), _Xfer(jnp.stack(dpool_w)),
                         _Xfer(jnp.concatenate(dpool_scale, axis=0)), _Xfer(loss_part)], "exchange_last")
    recv.update(zip(tail, landed))
    small = dict(zip(["g_pre", "g_post", "g_mem", "pool_w", "pool_scale"], landed[len(tail):]))

    def update(name, slots, w, m, v):
        into = None
        for l, r in enumerate(slots):
            into = _adamw(r, w, m, v, l, into, f"adamw_{name}_{l}")
        return into

    out = {}
    for n in STACKED:
        w, m, v = (wsb_t, tsb(opt_m[n]), tsb(opt_v[n])) if n == "w_in_sb" else (p[n], opt_m[n], opt_v[n])
        res = update(n, [recv[(n, l)] for l in range(w.shape[0])], w, m, v)
        out[n] = [tsb(a) for a in res] if n == "w_in_sb" else res
    one = lambda a: a.reshape(1, -1, a.shape[-1])
    for n, r in small.items():
        res = update(n, [r.reshape((N_DEV,) + one(p[n]).shape[1:])], one(p[n]), one(opt_m[n]), one(opt_v[n]))
        out[n] = [a.reshape(p[n].shape) for a in res]
    loss = jnp.sum(landed[-1][:, 0, 0])
    return loss, grad_x, out


STACKED = ["ffn1_gate", "ffn1_up", "ffn1_down", "ffn2_gate", "ffn2_up", "ffn2_down", "w_in_pool", "w_in_sb", "w_mem_kv", "w_out"]
WEIGHTS = ["g_pre", "g_post", "g_mem", "ffn1_gate", "ffn1_up", "ffn1_down", "ffn2_gate", "ffn2_up", "ffn2_down",
           "w_in_pool", "pool_w", "pool_scale", "w_in_sb", "w_mem_kv", "w_out"]


def kernel(x, mem, g_pre, g_post, g_mem, ffn1_gate, ffn1_up, ffn1_down, ffn2_gate, ffn2_up, ffn2_down, w_in_pool, pool_w, pool_scale, w_in_sb, w_mem_kv, w_out, loss_target, m_g_pre, m_g_post, m_g_mem, m_ffn1_gate, m_ffn1_up, m_ffn1_down, m_ffn2_gate, m_ffn2_up, m_ffn2_down, m_w_in_pool, m_pool_w, m_pool_scale, m_w_in_sb, m_w_mem_kv, m_w_out, v_g_pre, v_g_post, v_g_mem, v_ffn1_gate, v_ffn1_up, v_ffn1_down, v_ffn2_gate, v_ffn2_up, v_ffn2_down, v_w_in_pool, v_pool_w, v_pool_scale, v_w_in_sb, v_w_mem_kv, v_w_out):
    given = dict(locals())
    p = {n: given[n] for n in WEIGHTS}
    opt_m = {n: given["m_" + n] for n in WEIGHTS}
    opt_v = {n: given["v_" + n] for n in WEIGHTS}
    loss, grad_x, out = _step(p, opt_m, opt_v, x[0], mem[0], loss_target[0])
    res = [loss, grad_x[None]]
    for q in range(4):
        res += [out[n][q] for n in WEIGHTS]
    return tuple(res)
```

```python
import functools
from typing import NamedTuple, Optional

import jax
import jax.numpy as jnp
from jax import lax
from jax.experimental import pallas as pl
from jax.experimental.pallas import tpu as pltpu

F32 = jnp.float32
BF16 = jnp.bfloat16
GRAD_WIRE = jnp.bfloat16

N_DEV = 8
DEPTH = 4
D_MODEL = 1024
D_FF = 2048
D_TOK = 512
D_MEMH = 256
D_MIX = D_TOK + D_MEMH
D_SB = 3 * D_TOK + D_MEMH
HEAD_DIM = 64
Q_BLOCK = 128
POOL_WINDOWS = (2, 4, 8, 16)
POOL_GROUP = 128
POOL_HALO = 16
EPS = 1e-6
ATT_SCALE = HEAD_DIM ** -0.5
SB_DEAD_LOG_WEIGHT = -110.0
SB_FWD_LANES = 512
SB_BWD_LANES = 256

ADAM_LR = 0.001
ADAM_B1 = 0.9
ADAM_B2 = 0.999
ADAM_EPS = 1e-08
ADAM_WD = 0.01
ADAM_STEP = 10

VMEM_LIMIT_BYTES = 56 * 1024 * 1024
ROW_TILE = 256
FFN_FWD_TILE = 512
FFN_BWD_TILE = 512
FFN_CHUNK = 256
WGRAD_TILE = 512
ADAM_TILE_ELEMS = 128 * 1024

MESH = pl.DeviceIdType.MESH
ANY = pl.BlockSpec(memory_space=pl.ANY)


def _tile(n, pref):
    t = 1 << (pref.bit_length() - 1)
    while n % t:
        t //= 2
    return t


def _dot(a, b):
    return jnp.dot(a, b, preferred_element_type=F32)


def _dot_nt(a, b):
    return lax.dot_general(a, b, (((1,), (1,)), ((), ())), preferred_element_type=F32)


def _dot_tn(a, b):
    return lax.dot_general(a, b, (((0,), (0,)), ((), ())), preferred_element_type=F32)


def _split_dot(x, m, terms):
    out = None
    rest = x
    for _ in range(terms):
        part = rest.astype(BF16)
        rest = rest - part.astype(F32)
        d = _dot(part, m)
        out = d if out is None else out + d
    return out


def _rms(x, g):
    r = lax.rsqrt(jnp.mean(x * x, axis=-1, keepdims=True) + EPS)
    return x * r * g


def _rms_bwd(x, g, dy):
    r = lax.rsqrt(jnp.mean(x * x, axis=-1, keepdims=True) + EPS)
    xh = x * r
    gdy = g * dy
    dx = r * (gdy - xh * jnp.mean(gdy * xh, axis=-1, keepdims=True))
    return dx, jnp.sum(dy * xh, axis=0, keepdims=True)


def _acc_rows(ref, val, first):
    @pl.when(first)
    def _():
        ref[...] = val

    @pl.when(jnp.logical_not(first))
    def _():
        ref[...] += val


def _params(sem=None):
    return pltpu.CompilerParams(dimension_semantics=sem, vmem_limit_bytes=VMEM_LIMIT_BYTES)


def _sds(shape, dtype):
    return jax.ShapeDtypeStruct(shape, dtype)


def _rows(tm, width, col=0):
    return pl.BlockSpec((tm, width), lambda i: (i, col))


def _full(shape):
    nd = len(shape)
    return pl.BlockSpec(shape, lambda *_: (0,) * nd)


def _resident(shape):
    nd = len(shape)
    return pl.BlockSpec(shape, lambda *_: (0,) * nd, pipeline_mode=pl.Buffered(1))


def _peers():
    x, y, c = lax.axis_index("x"), lax.axis_index("y"), lax.axis_index("c")
    peers = []
    for k in range(1, N_DEV):
        px = 1 - x if k & 4 else x
        py = 1 - y if k & 2 else y
        pc = 1 - c if k & 1 else c
        peers.append(((px, py, pc), 4 * px + 2 * py + pc))
    return 4 * x + 2 * y + c, peers


class _Xfer(NamedTuple):
    src: jax.Array
    layer: Optional[int] = None
    scatter: bool = False

    @property
    def landing(self):
        block = self.src.shape if self.layer is None and not self.scatter else self.src.shape[1:]
        return _sds((N_DEV,) + block, self.src.dtype)


def _comm_copies(xfers, src_refs, dst_refs, send_sems, recv_sems, local_sems):
    me, peers = _peers()

    def src(t, to):
        ref = src_refs[t] if xfers[t].layer is None else src_refs[t].at[xfers[t].layer]
        return ref.at[to] if xfers[t].scatter else ref

    copies = [pltpu.make_async_copy(src(t, me), dst_refs[t].at[me], local_sems.at[t]) for t in range(len(xfers))]
    for k, (dev, idx) in enumerate(peers):
        for t in range(len(xfers)):
            copies.append(pltpu.make_async_remote_copy(
                src_ref=src(t, idx), dst_ref=dst_refs[t].at[me], send_sem=send_sems.at[t, k], recv_sem=recv_sems.at[t, k],
                device_id=dev, device_id_type=MESH))
    return copies


def _pcall(body, *, name, grid, in_specs, out_specs, out_shape, args, scratch=(), ride=()):
    n_in, n_out, n_scr, nx = len(in_specs), len(out_specs), len(scratch), len(ride)
    params = _params(("arbitrary",) * len(grid))
    if not ride:
        res = pl.pallas_call(body, name=name, grid=grid, in_specs=list(in_specs), out_specs=list(out_specs),
                             out_shape=list(out_shape), scratch_shapes=list(scratch), compiler_params=params,
                             interpret=False)(*args)
        return list(res), []
    def riding(*refs):
        o0 = n_in + nx
        s0 = o0 + n_out + nx
        comm = (ride, refs[n_in:o0], refs[o0 + n_out:s0], *refs[s0 + n_scr:])
        ids = [pl.program_id(a) for a in range(len(grid))]
        first = functools.reduce(jnp.logical_and, [i == 0 for i in ids])
        last = functools.reduce(jnp.logical_and, [i == n - 1 for i, n in zip(ids, grid)])

        @pl.when(first)
        def _():
            for cp in _comm_copies(*comm):
                cp.start()

        body(*refs[:n_in], *refs[o0:o0 + n_out], *refs[s0:s0 + n_scr])

        @pl.when(last)
        def _():
            for cp in _comm_copies(*comm):
                cp.wait()

    sems = [pltpu.SemaphoreType.DMA((nx, N_DEV - 1)), pltpu.SemaphoreType.DMA((nx, N_DEV - 1)), pltpu.SemaphoreType.DMA((nx,))]
    res = pl.pallas_call(riding, name=name, grid=grid, in_specs=list(in_specs) + [ANY] * nx,
                         out_specs=list(out_specs) + [ANY] * nx, out_shape=list(out_shape) + [t.landing for t in ride],
                         scratch_shapes=list(scratch) + sems, compiler_params=params,
                         interpret=False)(*args, *[t.src for t in ride])
    return list(res[:n_out]), list(res[n_out:])


def _comm_call(xfers, name):
    return _pcall(lambda: None, name=name, grid=(1,), in_specs=[], out_specs=[], out_shape=[], args=[], ride=xfers)[1]


def _gather_two_level(xfers, name):
    n = len(xfers)
    assert not any(t.scatter for t in xfers)

    def body(*refs):
        ins, outs = refs[:n], refs[n:2 * n]
        send_sems, recv_sems, local_sems = refs[2 * n:]
        x, y, c = lax.axis_index("x"), lax.axis_index("y"), lax.axis_index("c")
        sibling = (x, y, 1 - c)
        chips = [(1 - x, y), (x, 1 - y), (1 - x, 1 - y)]
        slot = lambda dev: 4 * dev[0] + 2 * dev[1] + dev[2]

        def copy(t, k, block, to, own=False):
            src = ins[t] if xfers[t].layer is None else ins[t].at[xfers[t].layer]
            return pltpu.make_async_remote_copy(
                src_ref=src if own else outs[t].at[slot(block)], dst_ref=outs[t].at[slot(block)],
                send_sem=send_sems.at[t, k], recv_sem=recv_sems.at[t, k], device_id=to, device_id_type=MESH)

        me = (x, y, c)
        local, sent = [], []
        for t in range(n):
            src = ins[t] if xfers[t].layer is None else ins[t].at[xfers[t].layer]
            mine = pltpu.make_async_copy(src, outs[t].at[slot(me)], local_sems.at[t])
            mine.start()
            local.append(mine)
            first = [copy(t, 0, me, sibling, own=True)] + [copy(t, 1 + j, me, (*chip, c), own=True) for j, chip in enumerate(chips)]
            for cp in first:
                cp.start()
            sent += first
        for t in range(n):
            for j, chip in enumerate(chips):
                copy(t, 1 + j, (*chip, c), me).wait_recv()
                passed = copy(t, 4 + j, (*chip, c), sibling)
                passed.start()
                sent.append(passed)
        for t in range(n):
            copy(t, 0, sibling, me).wait_recv()
            for j, chip in enumerate(chips):
                copy(t, 4 + j, (*chip, 1 - c), me).wait_recv()
        for cp in sent:
            cp.wait_send()
        for cp in local:
            cp.wait()

    return pl.pallas_call(
        body, name=name, in_specs=[ANY] * n, out_specs=[ANY] * n, out_shape=[t.landing for t in xfers],
        scratch_shapes=[pltpu.SemaphoreType.DMA((n, N_DEV - 1)), pltpu.SemaphoreType.DMA((n, N_DEV - 1)),
                        pltpu.SemaphoreType.DMA((n,))],
        interpret=False,
    )(*[t.src for t in xfers])


def _ffn_fwd(h, gpre, gpost, wg, wu, wd, name, ride=()):
    S = h.shape[0]
    tm = _tile(S, FFN_FWD_TILE)
    nb = FFN_CHUNK

    def body(h_ref, gpre_ref, gpost_ref, wgt_ref, wut_ref, wd_ref, hn_ref, n_ref, gate_ref, up_ref, act_ref, f_ref):
        hv = h_ref[...]
        n = _rms(hv, gpre_ref[...]).astype(BF16)
        n_ref[...] = n
        for c in range(D_FF // nb):
            cols = slice(c * nb, (c + 1) * nb)
            g = _dot_nt(n, wgt_ref[cols, :])
            u = _dot_nt(n, wut_ref[cols, :])
            gate_ref[:, cols] = g.astype(BF16)
            up_ref[:, cols] = u.astype(BF16)
            act_ref[:, cols] = (g * jax.nn.sigmoid(g) * u).astype(BF16)
        f = _dot(act_ref[...], wd_ref[...])
        f_ref[...] = f
        hn_ref[...] = hv + 0.5 * _rms(f, gpost_ref[...])

    return _pcall(
        body, name=name, grid=(S // tm,),
        in_specs=[_rows(tm, D_MODEL), _full((1, D_MODEL)), _full((1, D_MODEL)),
                  _resident((D_FF, D_MODEL)), _resident((D_FF, D_MODEL)), _resident((D_FF, D_MODEL))],
        out_specs=[_rows(tm, D_MODEL), _rows(tm, D_MODEL), _rows(tm, D_FF), _rows(tm, D_FF), _rows(tm, D_FF),
                   _rows(tm, D_MODEL)],
        out_shape=[_sds((S, D_MODEL), F32), _sds((S, D_MODEL), BF16), _sds((S, D_FF), BF16), _sds((S, D_FF), BF16),
                   _sds((S, D_FF), BF16), _sds((S, D_MODEL), F32)],
        args=(h, gpre, gpost, wg, wu, wd), ride=ride)


def _ffn_bwd(dho, h, f, gate, up, gpre, gpost, wg, wu, wd, name, ride=()):
    S = h.shape[0]
    tm = _tile(S, FFN_BWD_TILE)
    nb = FFN_CHUNK

    def body(dho_ref, h_ref, f_ref, gate_ref, up_ref, gpre_ref, gpost_ref, wgt_ref, wut_ref, wd_ref,
             dh_ref, df_ref, dgate_ref, dup_ref, dgpre_ref, dgpost_ref):
        first = pl.program_id(0) == 0
        dho_v = dho_ref[...]
        dfx, dgpost = _rms_bwd(f_ref[...], gpost_ref[...], 0.5 * dho_v)
        dfb = dfx.astype(BF16)
        df_ref[...] = dfb
        for c in range(D_FF // nb):
            cols = slice(c * nb, (c + 1) * nb)
            dact = _dot_nt(dfb, wd_ref[cols, :])
            g = gate_ref[:, cols].astype(F32)
            u = up_ref[:, cols].astype(F32)
            s = jax.nn.sigmoid(g)
            dgate_ref[:, cols] = (dact * u * (s * (1.0 + g * (1.0 - s)))).astype(BF16)
            dup_ref[:, cols] = (dact * (g * s)).astype(BF16)
        dn = _dot(dgate_ref[...], wgt_ref[...]) + _dot(dup_ref[...], wut_ref[...])
        dhx, dgpre = _rms_bwd(h_ref[...], gpre_ref[...], dn)
        dh_ref[...] = dho_v + dhx
        _acc_rows(dgpre_ref, dgpre, first)
        _acc_rows(dgpost_ref, dgpost, first)

    return _pcall(
        body, name=name, grid=(S // tm,),
        in_specs=[_rows(tm, D_MODEL), _rows(tm, D_MODEL), _rows(tm, D_MODEL), _rows(tm, D_FF), _rows(tm, D_FF),
                  _full((1, D_MODEL)), _full((1, D_MODEL)),
                  _resident((D_FF, D_MODEL)), _resident((D_FF, D_MODEL)), _resident((D_FF, D_MODEL))],
        out_specs=[_rows(tm, D_MODEL), _rows(tm, D_MODEL), _rows(tm, D_FF), _rows(tm, D_FF),
                   _full((1, D_MODEL)), _full((1, D_MODEL))],
        out_shape=[_sds((S, D_MODEL), F32), _sds((S, D_MODEL), BF16), _sds((S, D_FF), BF16), _sds((S, D_FF), BF16),
                   _sds((1, D_MODEL), F32), _sds((1, D_MODEL), F32)],
        args=(dho, h, f, gate, up, gpre, gpost, wg, wu, wd), ride=ride)


def _wgrad(a_parts, b_parts, split, name, ride=()):
    S = a_parts[0].shape[0]
    bk = _tile(S, WGRAD_TILE)
    ms = [a.shape[1] for a in a_parts]
    ns = [b.shape[1] for b in b_parts]
    M, N = sum(ms), sum(ns)
    na, nbp = len(a_parts), len(b_parts)
    blk = (M // N_DEV, N) if split == "rows" else (M, N // N_DEV)
    steps = S // bk

    def body(*refs):
        a_refs, b_refs = refs[:na], refs[na:na + nbp]
        out_ref, acc_ref = refs[-2], refs[-1]
        k = pl.program_id(0)

        @pl.when(k == 0)
        def _():
            acc_ref[...] = jnp.zeros_like(acc_ref)

        r0 = 0
        for ai in range(na):
            av = a_refs[ai][...]
            c0 = 0
            for bi in range(nbp):
                acc_ref[r0:r0 + ms[ai], c0:c0 + ns[bi]] += _dot_tn(av, b_refs[bi][...])
                c0 += ns[bi]
            r0 += ms[ai]

        @pl.when(k == steps - 1)
        def _():
            for d in range(N_DEV):
                if split == "rows":
                    out_ref[d] = acc_ref[d * blk[0]:(d + 1) * blk[0], :].astype(GRAD_WIRE)
                else:
                    out_ref[d] = acc_ref[:, d * blk[1]:(d + 1) * blk[1]].astype(GRAD_WIRE)

    in_specs = [pl.BlockSpec((bk, m), lambda k: (k, 0)) for m in ms] + [pl.BlockSpec((bk, n), lambda k: (k, 0)) for n in ns]
    (out,), landed = _pcall(
        body, name=name, grid=(steps,), in_specs=in_specs, out_specs=[_full((N_DEV,) + blk)],
        out_shape=[_sds((N_DEV,) + blk, GRAD_WIRE)], scratch=[pltpu.VMEM((M, N), F32)],
        args=list(a_parts) + list(b_parts), ride=ride)
    return out, landed


def _mix_in_pool(h, g1, w_in, pool_w, pool_scale, name, ride=()):
    S = h.shape[0]
    tm = _tile(S, ROW_TILE)
    kb = D_MODEL // N_DEV

    def body(h_ref, g_ref, w_ref, pw_ref, ps_ref, u_ref, dpre_ref, tok_ref, qm_ref, ext_ref):
        i = pl.program_id(0)
        u = _rms(h_ref[...], g_ref[...]).astype(BF16)
        u_ref[...] = u
        proj = jnp.zeros((tm, D_MIX), F32)
        for d in range(N_DEV):
            proj = proj + _dot(u[:, d * kb:(d + 1) * kb], w_ref[d])
        qm_ref[...] = proj[:, D_TOK:].astype(BF16)
        x = proj[:, :D_TOK]

        @pl.when(i == 0)
        def _():
            ext_ref[0:POOL_HALO, :] = jnp.zeros((POOL_HALO, D_TOK), F32)

        ext_ref[POOL_HALO:, :] = x
        pos = i * tm + lax.broadcasted_iota(jnp.int32, (tm, 1), 0)
        for gi, w in enumerate(POOL_WINDOWS):
            cols = slice(gi * POOL_GROUP, (gi + 1) * POOL_GROUP)
            xs = x[:, cols]
            wsum = xs
            for k in range(1, w):
                wsum = wsum + ext_ref[POOL_HALO - k:POOL_HALO - k + tm, cols]
            cnt = jnp.minimum(pos + 1, w).astype(F32)
            dg = (wsum / cnt - xs).astype(BF16)
            dpre_ref[:, cols] = dg
            yv = _dot(dg, pw_ref[gi].astype(BF16))
            tok_ref[:, cols] = (yv * ps_ref[:, cols]).astype(BF16)
        ext_ref[0:POOL_HALO, :] = x[tm - POOL_HALO:, :]

    return _pcall(
        body, name=name, grid=(S // tm,),
        in_specs=[_rows(tm, D_MODEL), _full((1, D_MODEL)), _full((N_DEV, kb, D_MIX)),
                  _full((len(POOL_WINDOWS), POOL_GROUP, POOL_GROUP)), _full((1, D_TOK))],
        out_specs=[_rows(tm, D_MODEL), _rows(tm, D_TOK), _rows(tm, D_TOK), _rows(tm, D_MEMH)],
        out_shape=[_sds((S, D_MODEL), BF16), _sds((S, D_TOK), BF16), _sds((S, D_TOK), BF16), _sds((S, D_MEMH), BF16)],
        scratch=[pltpu.VMEM((POOL_HALO + tm, D_TOK), F32)],
        args=(h, g1, w_in, pool_w, pool_scale), ride=ride)


def _pool_bwd(dtok, dpre, pool_w, pool_scale, name):
    S = dtok.shape[0]
    tm = _tile(S, ROW_TILE)
    nt = S // tm
    ng = len(POOL_WINDOWS)

    def body(dtok_ref, dpre_ref, pw_ref, ps_ref, dx_ref, dpw_ref, dps_ref, ext_ref):
        i = pl.program_id(0)
        first = i == 0
        t0 = (nt - 1 - i) * tm
        pos = t0 + lax.broadcasted_iota(jnp.int32, (tm, 1), 0)

        @pl.when(first)
        def _():
            ext_ref[tm:, :] = jnp.zeros((POOL_HALO, D_TOK), F32)

        dps = []
        for gi, w in enumerate(POOL_WINDOWS):
            cols = slice(gi * POOL_GROUP, (gi + 1) * POOL_GROUP)
            dg = dpre_ref[:, cols]
            pw = pw_ref[gi].astype(BF16)
            dt = dtok_ref[:, cols].astype(F32)
            yv = _dot(dg, pw)
            dps.append(jnp.sum(dt * yv, axis=0, keepdims=True))
            dy = (dt * ps_ref[:, cols]).astype(BF16)
            _acc_rows(dpw_ref.at[gi], _dot_tn(dg, dy), first)
            dd = _dot_nt(dy, pw)
            cnt = jnp.minimum(pos + 1, w).astype(F32)
            ext_ref[0:tm, cols] = dd / cnt
            wsum = ext_ref[0:tm, cols]
            for k in range(1, w):
                wsum = wsum + ext_ref[k:k + tm, cols]
            dx_ref[:, cols] = (wsum - dd).astype(BF16)
        _acc_rows(dps_ref, jnp.concatenate(dps, axis=1), first)
        ext_ref[tm:, :] = ext_ref[0:POOL_HALO, :]

    rev = lambda i: (nt - 1 - i, 0)
    return pl.pallas_call(
        body, name=name, grid=(nt,),
        in_specs=[pl.BlockSpec((tm, D_TOK), rev), pl.BlockSpec((tm, D_TOK), rev),
                  _full((ng, POOL_GROUP, POOL_GROUP)), _full((1, D_TOK))],
        out_specs=[pl.BlockSpec((tm, D_TOK), rev), _full((ng, POOL_GROUP, POOL_GROUP)), _full((1, D_TOK))],
        out_shape=[_sds((S, D_TOK), BF16), _sds((ng, POOL_GROUP, POOL_GROUP), F32), _sds((1, D_TOK), F32)],
        scratch_shapes=[pltpu.VMEM((tm + POOL_HALO, D_TOK), F32)],
        compiler_params=_params(("arbitrary",)),
        interpret=False,
    )(dtok, dpre, pool_w, pool_scale)


def _mix_in_sb(h, g1, wt, name):
    S = h.shape[0]
    tm = _tile(S, ROW_TILE)
    cb = 256

    def body(h_ref, g_ref, wt_ref, u_ref, proj_ref):
        u = _rms(h_ref[...], g_ref[...]).astype(BF16)
        u_ref[...] = u
        for c in range(D_SB // cb):
            proj_ref[:, c * cb:(c + 1) * cb] = _dot_nt(u, wt_ref[c * cb:(c + 1) * cb, :]).astype(BF16)

    return pl.pallas_call(
        body, name=name, grid=(S // tm,),
        in_specs=[_rows(tm, D_MODEL), _full((1, D_MODEL)), _resident((D_SB, D_MODEL))],
        out_specs=[_rows(tm, D_MODEL), _rows(tm, D_SB)],
        out_shape=[_sds((S, D_MODEL), BF16), _sds((S, D_SB), BF16)],
        compiler_params=_params(("arbitrary",)),
        interpret=False,
    )(h, g1, wt)


def _mix_in_bwd(dho, h, g1, parts, w, mode, name):
    S = h.shape[0]
    tm = _tile(S, ROW_TILE)
    widths = [p.shape[1] for p in parts]
    npart = len(parts)
    kb = D_MODEL // N_DEV

    def body(*refs):
        dho_ref, h_ref, g_ref = refs[:3]
        p_refs = refs[3:3 + npart]
        w_ref, dh_ref, dg_ref = refs[3 + npart:]
        first = pl.program_id(0) == 0
        if mode == "pool":
            dproj = jnp.concatenate([p[...] for p in p_refs], axis=1)
            du = jnp.concatenate([_dot_nt(dproj, w_ref[d]) for d in range(N_DEV)], axis=1)
        else:
            du = jnp.zeros((tm, D_MODEL), F32)
            r0 = 0
            for p, wd_ in zip(p_refs, widths):
                du = du + _dot(p[...], w_ref[r0:r0 + wd_, :])
                r0 += wd_
        dhx, dg = _rms_bwd(h_ref[...], g_ref[...], du)
        dh_ref[...] = dho_ref[...] + dhx
        _acc_rows(dg_ref, dg, first)

    w_spec = _full((N_DEV, kb, D_MIX)) if mode == "pool" else _full((D_SB, D_MODEL))
    return pl.pallas_call(
        body, name=name, grid=(S // tm,),
        in_specs=[_rows(tm, D_MODEL), _rows(tm, D_MODEL), _full((1, D_MODEL))] + [_rows(tm, wd_) for wd_ in widths] + [w_spec],
        out_specs=[_rows(tm, D_MODEL), _full((1, D_MODEL))],
        out_shape=[_sds((S, D_MODEL), F32), _sds((1, D_MODEL), F32)],
        compiler_params=_params(("arbitrary",)),
        interpret=False,
    )(dho, h, g1, *parts, w)


def _sb_block(qcats, kb, mask, later, tri_later):
    z = jnp.concatenate([_dot_nt(qc, _group(kb, g)) for g, qc in enumerate(qcats)], axis=0) * ATT_SCALE
    en = jnp.exp(-jnp.abs(z))
    ls = jnp.minimum(z, 0.0) - jnp.log(1.0 + en)
    lf = ls - z if mask is None else jnp.where(mask, ls - z, 0.0)
    within = _split_dot(lf, tri_later, 2)
    a = jnp.exp(ls + within + later)
    if mask is not None:
        a = jnp.where(mask, a, 0.0)
    return z, en, a, jnp.sum(lf, axis=1, keepdims=True)


def _sb_causal(rows):
    row = jnp.bitwise_and(lax.broadcasted_iota(jnp.int32, (rows, Q_BLOCK), 0), Q_BLOCK - 1)
    return lax.broadcasted_iota(jnp.int32, (rows, Q_BLOCK), 1) < row


def _sb_walk(qi, block, state, later_of, unrolled):
    if unrolled:
        state = block(qi, _sb_causal, state)
        state = block(jnp.maximum(qi - 1, 0), qi >= 1, state)

    def step(carry):
        j, _, state = carry
        mask = None if unrolled else lambda rows: jnp.logical_or(_sb_causal(rows), j > 0)
        state = block(qi - j, mask, state)
        return j + 1, _sb_alive(later_of(state)), state

    first = jnp.int32(2 if unrolled else 0)
    return lax.while_loop(functools.partial(_sb_more, qi), step, (first, _sb_alive(later_of(state)), state))[2]


def _sb_alive(later):
    return jnp.max(later) > SB_DEAD_LOG_WEIGHT


def _sb_more(qi, carry):
    return jnp.logical_and(carry[0] <= qi, carry[1])


def _tri(strict):
    row = lax.broadcasted_iota(jnp.int32, (Q_BLOCK, Q_BLOCK), 0)
    col = lax.broadcasted_iota(jnp.int32, (Q_BLOCK, Q_BLOCK), 1)
    return (row > col if strict else row >= col).astype(BF16)


HEADS_PER_GROUP = 128 // HEAD_DIM
GROUP_ROWS = HEADS_PER_GROUP * Q_BLOCK


def _sb_head_masks():
    lane = lax.broadcasted_iota(jnp.int32, (Q_BLOCK, 128), 1)
    return [(lane >= e * HEAD_DIM) & (lane < (e + 1) * HEAD_DIM) for e in range(HEADS_PER_GROUP)]


def _group(x, g):
    return x[:, g * 128:(g + 1) * 128]


def _masked(hm, x):
    return jnp.where(hm, x, jnp.zeros_like(x))


def _stack_heads(hms, x):
    return jnp.concatenate([_masked(hm, x) for hm in hms], axis=0)


def _own_lanes(hms, r, rows=Q_BLOCK):
    return sum(_masked(hm, r[e * rows:(e + 1) * rows]) for e, hm in enumerate(hms))


def _sb_fwd(proj, name, ride=()):
    S = proj.shape[0]
    nq = S // Q_BLOCK
    W = SB_FWD_LANES
    nrow = D_TOK // W
    groups = W // 128

    def body(q_ref, k_ref, v_ref, o_ref, tok_ref):
        qi = pl.program_id(1)
        hms = _sb_head_masks()
        tri_later = _tri(True)
        q = q_ref[...]
        qcats = [_stack_heads(hms, _group(q, g)) for g in range(groups)]

        def block(kblock, mask, state):
            accs, later = state
            if callable(mask):
                mask = mask(later.shape[0])
            off = pl.multiple_of(kblock * Q_BLOCK, Q_BLOCK)
            kb = k_ref[pl.ds(off, Q_BLOCK), :]
            vb = v_ref[pl.ds(off, Q_BLOCK), :]
            _, _, a, bsum = _sb_block(qcats, kb, mask, later, tri_later)
            hi = a.astype(BF16)
            lo = (a - hi.astype(F32)).astype(BF16)
            accs = list(accs)
            for g in range(groups):
                rows = slice(g * GROUP_ROWS, (g + 1) * GROUP_ROWS)
                r = _dot(jnp.concatenate([hi[rows], lo[rows]], axis=0), _group(vb, g))
                accs[g] = accs[g] + _own_lanes(hms, r[:GROUP_ROWS] + r[GROUP_ROWS:])
            return tuple(accs), later + bsum

        init = ((jnp.zeros((Q_BLOCK, 128), F32),) * groups, jnp.zeros((groups * GROUP_ROWS, 1), F32))
        accs = _sb_walk(qi, block, init, lambda state: state[1], unrolled=True)[0]
        for g, acc in enumerate(accs):
            o_ref[:, g * 128:(g + 1) * 128] = acc
            tok_ref[:, g * 128:(g + 1) * 128] = acc.astype(BF16)

    blk = pl.BlockSpec((Q_BLOCK, W), lambda p, i: (i, p))
    return _pcall(
        body, name=name, grid=(nrow, nq),
        in_specs=[blk, pl.BlockSpec((S, W), lambda p, i: (0, nrow + p), pipeline_mode=pl.Buffered(1)),
                  pl.BlockSpec((S, W), lambda p, i: (0, 2 * nrow + p), pipeline_mode=pl.Buffered(1))],
        out_specs=[blk, blk],
        out_shape=[_sds((S, D_TOK), F32), _sds((S, D_TOK), BF16)],
        args=(proj, proj, proj), ride=ride)


def _sb_bwd(proj, dtok, o32, name, ride=()):
    S = proj.shape[0]
    nq = S // Q_BLOCK
    W = SB_BWD_LANES
    nrow = D_TOK // W
    groups = W // 128

    def body(q_ref, k_ref, v_ref, do_ref, o_ref, dq_ref, dk_ref, dv_ref, dk_acc, dv_acc):
        qi = pl.program_id(1)

        @pl.when(qi == 0)
        def _():
            dk_acc[...] = jnp.zeros_like(dk_acc)
            dv_acc[...] = jnp.zeros_like(dv_acc)

        hms = _sb_head_masks()
        tri_later = _tri(True)
        tri_from = _tri(False)
        q = q_ref[...]
        do = do_ref[...]
        dov = do.astype(F32) * o_ref[...]
        qcats = [_stack_heads(hms, _group(q, g)) for g in range(groups)]
        docats = [_stack_heads(hms, _group(do, g)) for g in range(groups)]
        rowtot = jnp.concatenate([jnp.sum(jnp.where(hm, _group(dov, g), 0.0), axis=1, keepdims=True)
                                  for g in range(groups) for hm in hms], axis=0)

        def block(kblock, mask, state):
            dqs, later, seen = state
            if callable(mask):
                mask = mask(later.shape[0])
            off = pl.multiple_of(kblock * Q_BLOCK, Q_BLOCK)
            kb = k_ref[pl.ds(off, Q_BLOCK), :]
            vb = v_ref[pl.ds(off, Q_BLOCK), :]
            z, en, a, bsum = _sb_block(qcats, kb, mask, later, tri_later)
            inv = 1.0 / (1.0 + en)
            beta = jnp.where(z >= 0, 1.0, en) * inv
            omb = jnp.where(z >= 0, en, 1.0) * inv
            dlogw = a * jnp.concatenate([_dot_nt(docats[g], _group(vb, g)) for g in range(groups)], axis=0)
            prefix = rowtot - seen - _split_dot(dlogw, tri_from, 2)
            dz = dlogw * omb - beta * prefix
            if mask is not None:
                dz = jnp.where(mask, dz, 0.0)
            dz = dz.astype(BF16)
            ab = a.astype(BF16)
            dqs = list(dqs)
            for g in range(groups):
                rows = slice(g * GROUP_ROWS, (g + 1) * GROUP_ROWS)
                lanes = slice(g * 128, (g + 1) * 128)
                dqs[g] = dqs[g] + _own_lanes(hms, _dot(dz[rows], _group(kb, g)))
                dk_acc[pl.ds(off, Q_BLOCK), lanes] += _dot_tn(dz[rows], qcats[g]) * ATT_SCALE
                dv_acc[pl.ds(off, Q_BLOCK), lanes] += _dot_tn(ab[rows], docats[g])
            return tuple(dqs), later + bsum, seen + jnp.sum(dlogw, axis=1, keepdims=True)

        zero = jnp.zeros((groups * GROUP_ROWS, 1), F32)
        init = ((jnp.zeros((Q_BLOCK, 128), F32),) * groups, zero, zero)
        dqs = _sb_walk(qi, block, init, lambda state: state[1], unrolled=False)[0]
        for g, dq in enumerate(dqs):
            dq_ref[:, g * 128:(g + 1) * 128] = (dq * ATT_SCALE).astype(BF16)

        @pl.when(qi == nq - 1)
        def _():
            dk_ref[...] = dk_acc[...].astype(BF16)
            dv_ref[...] = dv_acc[...].astype(BF16)

    blk = pl.BlockSpec((Q_BLOCK, W), lambda p, i: (i, p))
    col = pl.BlockSpec((S, W), lambda p, i: (0, p))
    return _pcall(
        body, name=name, grid=(nrow, nq),
        in_specs=[blk, pl.BlockSpec((S, W), lambda p, i: (0, nrow + p)), pl.BlockSpec((S, W), lambda p, i: (0, 2 * nrow + p)),
                  blk, blk],
        out_specs=[blk, col, col],
        out_shape=[_sds((S, D_TOK), BF16), _sds((S, D_TOK), BF16), _sds((S, D_TOK), BF16)],
        scratch=[pltpu.VMEM((S, W), F32), pltpu.VMEM((S, W), F32)],
        args=(proj, proj, proj, dtok, o32), ride=ride)


def _mem_kv_fwd(mem, g_mem, w_kv, name):
    lm = mem.shape[0]
    kb = D_MODEL // N_DEV

    def body(mem_ref, g_ref, w_ref, mn_ref, km_ref, vm_ref):
        mn = _rms(mem_ref[...], g_ref[...]).astype(BF16)
        mn_ref[...] = mn
        kv = jnp.zeros((lm, 2 * D_MEMH), F32)
        for d in range(N_DEV):
            kv = kv + _dot(mn[:, d * kb:(d + 1) * kb], w_ref[d])
        km_ref[...] = kv[:, :D_MEMH].astype(BF16)
        vm_ref[...] = kv[:, D_MEMH:].astype(BF16)

    return pl.pallas_call(
        body, name=name, grid=(1,),
        in_specs=[_full((lm, D_MODEL)), _full((1, D_MODEL)), _full((N_DEV, kb, 2 * D_MEMH))],
        out_specs=[_full((lm, D_MODEL)), _full((lm, D_MEMH)), _full((lm, D_MEMH))],
        out_shape=[_sds((lm, D_MODEL), BF16), _sds((lm, D_MEMH), BF16), _sds((lm, D_MEMH), BF16)],
        compiler_params=_params(("arbitrary",)),
        interpret=False,
    )(mem, g_mem, w_kv)


def _mem_kv_bwd(dkm, dvm, mem, g_mem, mem_n, w_kv, name):
    lm = mem.shape[0]
    kb = D_MODEL // N_DEV

    def body(dkm_ref, dvm_ref, mem_ref, g_ref, mn_ref, w_ref, dw_ref, dg_ref):
        dkv = jnp.concatenate([dkm_ref[...], dvm_ref[...]], axis=1).astype(BF16)
        dw = _dot_tn(mn_ref[...], dkv)
        for d in range(N_DEV):
            dw_ref[d] = dw[d * kb:(d + 1) * kb, :].astype(GRAD_WIRE)
        dmn = jnp.concatenate([_dot_nt(dkv, w_ref[d]) for d in range(N_DEV)], axis=1)
        _, dg = _rms_bwd(mem_ref[...], g_ref[...], dmn)
        dg_ref[...] = dg

    return pl.pallas_call(
        body, name=name, grid=(1,),
        in_specs=[_full((lm, D_MEMH)), _full((lm, D_MEMH)), _full((lm, D_MODEL)), _full((1, D_MODEL)),
                  _full((lm, D_MODEL)), _full((N_DEV, kb, 2 * D_MEMH))],
        out_specs=[_full((N_DEV, kb, 2 * D_MEMH)), _full((1, D_MODEL))],
        out_shape=[_sds((N_DEV, kb, 2 * D_MEMH), GRAD_WIRE), _sds((1, D_MODEL), F32)],
        compiler_params=_params(("arbitrary",)),
        interpret=False,
    )(dkm, dvm, mem, g_mem, mem_n, w_kv)


def _mem_heads(tm):
    lane = lax.broadcasted_iota(jnp.int32, (tm, D_MEMH), 1)
    return [(lane >= e * HEAD_DIM) & (lane < (e + 1) * HEAD_DIM) for e in range(D_MEMH // HEAD_DIM)]


def _softmax(s):
    m = jnp.max(s, axis=-1, keepdims=True)
    p = jnp.exp(s - m)
    return p / jnp.sum(p, axis=-1, keepdims=True)


def _mix_out_fwd(h, tok, qm, qm_col, km, vm, w_out, gpost, name, ride=()):
    S = h.shape[0]
    tm = _tile(S, ROW_TILE)
    lm = km.shape[0]
    nb = D_MODEL // N_DEV

    def body(h_ref, tok_ref, qm_ref, km_ref, vm_ref, w_ref, g_ref, hn_ref, mo_ref, mix_ref):
        qv = qm_ref[...]
        kv, vv = km_ref[...], vm_ref[...]
        hms = _mem_heads(tm)
        p = _softmax(_dot_nt(_stack_heads(hms, qv), kv) * ATT_SCALE)
        mob = _own_lanes(hms, _dot(p.astype(BF16), vv), tm).astype(BF16)
        mo_ref[...] = mob
        tv = tok_ref[...]
        mix = jnp.concatenate(
            [_dot(tv, w_ref[d, 0:D_TOK, :]) + _dot(mob, w_ref[d, D_TOK:D_MIX, :]) for d in range(N_DEV)], axis=1)
        mix_ref[...] = mix
        hn_ref[...] = h_ref[...] + _rms(mix, g_ref[...])

    return _pcall(
        body, name=name, grid=(S // tm,),
        in_specs=[_rows(tm, D_MODEL), _rows(tm, D_TOK), _rows(tm, D_MEMH, qm_col), _full((lm, D_MEMH)), _full((lm, D_MEMH)),
                  _full((N_DEV, D_MIX, nb)), _full((1, D_MODEL))],
        out_specs=[_rows(tm, D_MODEL), _rows(tm, D_MEMH), _rows(tm, D_MODEL)],
        out_shape=[_sds((S, D_MODEL), F32), _sds((S, D_MEMH), BF16), _sds((S, D_MODEL), F32)],
        args=(h, tok, qm, km, vm, w_out, gpost), ride=ride)


def _mix_out_bwd(dho, mix, qm, qm_col, km, vm, w_out, gpost, name, ride=()):
    S = dho.shape[0]
    tm = _tile(S, ROW_TILE)
    lm = km.shape[0]
    nb = D_MODEL // N_DEV

    def body(dho_ref, mix_ref, qm_ref, km_ref, vm_ref, w_ref, g_ref,
             dmix_ref, dtok_ref, dqm_ref, dkm_ref, dvm_ref, dg_ref):
        first = pl.program_id(0) == 0
        dmx, dg = _rms_bwd(mix_ref[...], g_ref[...], dho_ref[...])
        dmb = dmx.astype(BF16)
        dmix_ref[...] = dmb
        _acc_rows(dg_ref, dg, first)
        dcat = jnp.zeros((tm, D_MIX), F32)
        for d in range(N_DEV):
            dcat = dcat + _dot_nt(dmb[:, d * nb:(d + 1) * nb], w_ref[d])
        dtok_ref[...] = dcat[:, :D_TOK].astype(BF16)
        dmo = dcat[:, D_TOK:].astype(BF16)
        qv = qm_ref[...]
        kv, vv = km_ref[...], vm_ref[...]
        hms = _mem_heads(tm)
        qcat = _stack_heads(hms, qv)
        dmcat = _stack_heads(hms, dmo)
        p = _softmax(_dot_nt(qcat, kv) * ATT_SCALE)
        dp = _dot_nt(dmcat, vv)
        ds = (p * (dp - jnp.sum(p * dp, axis=-1, keepdims=True))).astype(BF16)
        dq = _own_lanes(hms, _dot(ds, kv), tm)
        dk = _dot_tn(ds, qcat)
        dv = _dot_tn(p.astype(BF16), dmcat)
        dqm_ref[...] = (dq * ATT_SCALE).astype(BF16)
        _acc_rows(dkm_ref, dk * ATT_SCALE, first)
        _acc_rows(dvm_ref, dv, first)

    return _pcall(
        body, name=name, grid=(S // tm,),
        in_specs=[_rows(tm, D_MODEL), _rows(tm, D_MODEL), _rows(tm, D_MEMH, qm_col), _full((lm, D_MEMH)), _full((lm, D_MEMH)),
                  _full((N_DEV, D_MIX, nb)), _full((1, D_MODEL))],
        out_specs=[_rows(tm, D_MODEL), _rows(tm, D_TOK), _rows(tm, D_MEMH), _full((lm, D_MEMH)), _full((lm, D_MEMH)),
                   _full((1, D_MODEL))],
        out_shape=[_sds((S, D_MODEL), BF16), _sds((S, D_TOK), BF16), _sds((S, D_MEMH), BF16),
                   _sds((lm, D_MEMH), F32), _sds((lm, D_MEMH), F32), _sds((1, D_MODEL), F32)],
        args=(dho, mix, qm, km, vm, w_out, gpost), ride=ride)


def _loss_head(y, target, name):
    S = y.shape[0]
    tm = _tile(S, ROW_TILE)
    nt = S // tm

    def body(y_ref, t_ref, dy_ref, loss_ref, acc_ref):
        i = pl.program_id(0)
        e = y_ref[...] - t_ref[...]
        dy_ref[...] = e * (1.0 / D_MODEL)
        _acc_rows(acc_ref, jnp.sum(e * e, axis=0, keepdims=True), i == 0)

        @pl.when(i == nt - 1)
        def _():
            tot = jnp.sum(acc_ref[...], axis=1, keepdims=True) * (0.5 / D_MODEL)
            loss_ref[...] = jnp.broadcast_to(tot, (1, 128))

    return pl.pallas_call(
        body, name=name, grid=(nt,),
        in_specs=[_rows(tm, D_MODEL), _rows(tm, D_MODEL)],
        out_specs=[_rows(tm, D_MODEL), _full((1, 128))],
        out_shape=[_sds((S, D_MODEL), F32), _sds((1, 128), F32)],
        scratch_shapes=[pltpu.VMEM((1, D_MODEL), F32)],
        compiler_params=_params(("arbitrary",)),
        interpret=False,
    )(y, target)


def _adamw(recv, w, m, v, l, into, name):
    L, R, C = w.shape
    tr = R if R * C <= ADAM_TILE_ELEMS else _tile(R, ADAM_TILE_ELEMS // C)
    c1 = 1.0 - ADAM_B1 ** ADAM_STEP
    c2 = 1.0 - ADAM_B2 ** ADAM_STEP

    def body(r_ref, w_ref, m_ref, v_ref, *rest):
        g_ref, d_ref, nm_ref, nv_ref = rest[-4:]
        g = r_ref[0].astype(F32)
        for s in range(1, N_DEV):
            g = g + r_ref[s].astype(F32)
        g_ref[...] = g
        nm = ADAM_B1 * m_ref[...] + (1.0 - ADAM_B1) * g
        nv = ADAM_B2 * v_ref[...] + (1.0 - ADAM_B2) * (g * g)
        nm_ref[...] = nm
        nv_ref[...] = nv
        d_ref[...] = -ADAM_LR * ((nm / c1) / (jnp.sqrt(nv / c2) + ADAM_EPS) + ADAM_WD * w_ref[...])

    t = pl.BlockSpec((None, tr, C), lambda i: (l, i, 0))
    kept = [] if into is None else list(into)
    return pl.pallas_call(
        body, name=name, grid=(R // tr,),
        in_specs=[pl.BlockSpec((N_DEV, tr, C), lambda i: (0, i, 0)), t, t, t] + [ANY] * len(kept),
        out_specs=[t, t, t, t],
        out_shape=[_sds((L, R, C), F32)] * 4,
        input_output_aliases={4 + q: q for q in range(len(kept))},
        compiler_params=_params(("arbitrary",)),
        interpret=False,
    )(recv, w, m, v, *kept)


def _step(p, opt_m, opt_v, x, mem, target):
    bf = lambda a: a.astype(BF16)
    row = lambda a: a.reshape(1, -1)
    tsb = lambda a: jnp.swapaxes(a, 1, 2)
    g_mem = p["g_mem"]

    wsb_t = tsb(p["w_in_sb"])
    travels_transposed = ("w_in_sb", "ffn1_gate", "ffn1_up", "ffn2_gate", "ffn2_up")
    shard = {n: bf(tsb(p[n]) if n in travels_transposed else p[n]) for n in STACKED}
    ffn_weights = lambda which, i: [gw[k].reshape(D_FF, D_MODEL) for k in ffn(which, i)]
    w_in = lambda i: "w_in_pool" if i % 2 == 0 else "w_in_sb"
    ffn = lambda which, i: [(f"ffn{which}_{s}", i) for s in ("gate", "up", "down")]
    mixing = lambda i: [(w_in(i), i // 2), ("w_mem_kv", i), ("w_out", i)]

    gw = {}

    def gather(keys):
        return [_Xfer(shard[n], l) for n, l in keys]

    first = ffn(1, 0)
    landed = _gather_two_level(gather(first) + [_Xfer(p["g_pre"]), _Xfer(p["g_post"])], "gather_first")
    gw.update(zip(first, landed))
    unshard = lambda g: jnp.transpose(g, (1, 2, 0, 3)).reshape(DEPTH, 3, D_MODEL)
    g_pre, g_post = unshard(landed[-2]), unshard(landed[-1])

    def ahead(i):
        nxt = i + 1 < DEPTH
        if i % 2 == 0:
            start = i == 0
            return {"ffn1": ffn(2, i)[:2] + (mixing(0)[:1] if start else []), "mix_in": ffn(2, i)[2:] + (mixing(0)[1:] if start else []),
                    "sb": [], "mix_out": ffn(1, i + 1)[:1] if nxt else [], "ffn2": ffn(1, i + 1)[1:] if nxt else []}
        return {"ffn1": mixing(i), "mix_in": [], "sb": ffn(2, i) + (ffn(1, i + 1)[:2] if nxt else []),
                "mix_out": [], "ffn2": ffn(1, i + 1)[2:] + mixing(i + 1) if nxt else []}

    def gathering(keys, call):
        res, landed = call(ride=gather(keys))
        gw.update(zip(keys, landed))
        return res

    saved = []
    h = x
    for i in range(DEPTH):
        j = i // 2
        st = {"h0": h}
        carry = ahead(i)
        h, st["n1"], st["gate1"], st["up1"], st["act1"], st["f1"] = gathering(carry["ffn1"], functools.partial(
            _ffn_fwd, h, row(g_pre[i, 0]), row(g_post[i, 0]), *ffn_weights(1, i), f"ffn1_fwd_{i}"))
        st["h1"] = h
        if i % 2 == 0:
            st["u"], st["dpre"], st["tok"], st["qm"] = gathering(carry["mix_in"], functools.partial(
                _mix_in_pool, h, row(g_pre[i, 1]), gw[("w_in_pool", j)], p["pool_w"][j], row(p["pool_scale"][j]),
                f"mix_in_pool_{i}"))
            qm, qm_col = st["qm"], 0
        else:
            st["u"], st["proj"] = _mix_in_sb(h, row(g_pre[i, 1]), gw[("w_in_sb", j)].reshape(D_SB, D_MODEL), f"mix_in_sb_{i}")
            st["o32"], st["tok"] = gathering(carry["sb"], functools.partial(_sb_fwd, st["proj"], f"sb_fwd_{i}"))
            qm, qm_col = st["proj"], 3 * D_TOK // D_MEMH
        st["mem_n"], st["km"], st["vm"] = _mem_kv_fwd(mem, row(g_mem[i]), gw[("w_mem_kv", i)], f"mem_kv_fwd_{i}")
        h, st["mo"], st["mix"] = gathering(carry["mix_out"], functools.partial(
            _mix_out_fwd, h, st["tok"], qm, qm_col, st["km"], st["vm"], gw[("w_out", i)], row(g_post[i, 1]), f"mix_out_fwd_{i}"))
        st["h2"] = h
        h, st["n2"], st["gate2"], st["up2"], st["act2"], st["f2"] = gathering(carry["ffn2"], functools.partial(
            _ffn_fwd, h, row(g_pre[i, 2]), row(g_post[i, 2]), *ffn_weights(2, i), f"ffn2_fwd_{i}"))
        saved.append(st)

    dh, loss_part = _loss_head(h, target, "loss_head")

    grads = {}
    recv = {}
    dg_pre = [[None] * 3 for _ in range(DEPTH)]
    dg_post = [[None] * 3 for _ in range(DEPTH)]
    dg_mem = [None] * DEPTH
    dpool_w = [None, None]
    dpool_scale = [None, None]

    def scatter(keys):
        return [_Xfer(grads[k], scatter=True) for k in keys]

    def ffn_backward(dh, st, i, which, hkey, slot, riding, last):
        sfx = str(which)
        keys = ffn(which, i)
        (dh, df, dgate, dup, dg_pre[i][slot], dg_post[i][slot]), landed = _ffn_bwd(
            dh, st[hkey], st["f" + sfx], st["gate" + sfx], st["up" + sfx], row(g_pre[i, slot]), row(g_post[i, slot]),
            *ffn_weights(which, i), f"ffn{sfx}_bwd_{i}", ride=scatter(riding))
        recv.update(zip(riding, landed))
        riders = [mixing(i), keys[:1], keys[1:2]] if last else [[], [], []]
        operands = ((st["n" + sfx], dgate, "cols"), (st["n" + sfx], dup, "cols"), (st["act" + sfx], df, "rows"))
        for key, (a, b, split), riding in zip(keys, operands, riders):
            grads[key], landed = _wgrad([a], [b], split, f"wgrad_{key[0]}_{i}", ride=scatter(riding))
            recv.update(zip(riding, landed))
        return dh

    def behind(i):
        prev = ffn(1, i + 1) if i + 1 < DEPTH else []
        if i % 2 == 1:
            return {"ffn2": prev, "mix_out": [], "w_out": [], "sb": ffn(2, i), "ffn1": mixing(i)}
        if i > 0:
            return {"ffn2": prev, "mix_out": ffn(2, i)[:2], "w_out": [], "sb": [], "ffn1": ffn(2, i)[2:] + mixing(i)}
        return {"ffn2": prev, "mix_out": ffn(2, 0)[:2], "w_out": ffn(2, 0)[2:], "sb": [], "ffn1": []}

    for i in reversed(range(DEPTH)):
        j = i // 2
        st = saved[i]
        carry = behind(i)
        dh = ffn_backward(dh, st, i, 2, "h2", 2, carry["ffn2"], False)
        if i % 2 == 0:
            qm, qm_col = st["qm"], 0
        else:
            qm, qm_col = st["proj"], 3 * D_TOK // D_MEMH
        keys = carry["mix_out"]
        (dmix, dtok, dqm, dkm, dvm, dg_post[i][1]), landed = _mix_out_bwd(
            dh, st["mix"], qm, qm_col, st["km"], st["vm"], gw[("w_out", i)], row(g_post[i, 1]), f"mix_out_bwd_{i}", ride=scatter(keys))
        recv.update(zip(keys, landed))
        grads[("w_out", i)], landed = _wgrad([st["tok"], st["mo"]], [dmix], "cols", f"wgrad_w_out_{i}", ride=scatter(carry["w_out"]))
        recv.update(zip(carry["w_out"], landed))
        grads[("w_mem_kv", i)], dg_mem[i] = _mem_kv_bwd(dkm, dvm, mem, row(g_mem[i]), st["mem_n"], gw[("w_mem_kv", i)],
                                                        f"mem_kv_bwd_{i}")
        if i % 2 == 0:
            dx, dpool_w[j], dpool_scale[j] = _pool_bwd(dtok, st["dpre"], p["pool_w"][j], row(p["pool_scale"][j]), f"pool_bwd_{i}")
            parts = [dx, dqm]
            dh, dg_pre[i][1] = _mix_in_bwd(dh, st["h1"], row(g_pre[i, 1]), parts, gw[("w_in_pool", j)], "pool", f"mix_in_bwd_{i}")
            grads[("w_in_pool", j)], _ = _wgrad([st["u"]], parts, "rows", f"wgrad_w_in_pool_{i}")
        else:
            (dq, dk, dv), landed = _sb_bwd(st["proj"], dtok, st["o32"], f"sb_bwd_{i}", ride=scatter(carry["sb"]))
            recv.update(zip(carry["sb"], landed))
            parts = [dq, dk, dv, dqm]
            dh, dg_pre[i][1] = _mix_in_bwd(dh, st["h1"], row(g_pre[i, 1]), parts, gw[("w_in_sb", j)].reshape(D_SB, D_MODEL), "sb",
                                           f"mix_in_bwd_{i}")
            grads[("w_in_sb", j)], _ = _wgrad(parts, [st["u"]], "rows", f"wgrad_w_in_sb_{i}")
        dh = ffn_backward(dh, st, i, 1, "h0", 0, carry["ffn1"], i == 0)
    grad_x = dh

    shard8 = lambda rows_: jnp.transpose(jnp.stack([jnp.concatenate(r, axis=0) for r in rows_]).reshape(DEPTH, 3, N_DEV, -1),
                                         (2, 0, 1, 3))
    tail = ffn(1, 0)[2:]
    landed = _comm_call(
        scatter(tail) + [_Xfer(shard8(dg_pre), scatter=True), _Xfer(shard8(dg_post), scatter=True),
                         _Xfer(jnp.concatenate(dg_mem, axis=0)), _Xfer(jnp.stack(dpool_w)),
                         _Xfer(jnp.concatenate(dpool_scale, axis=0)), _Xfer(loss_part)], "exchange_last")
    recv.update(zip(tail, landed))
    small = dict(zip(["g_pre", "g_post", "g_mem", "pool_w", "pool_scale"], landed[len(tail):]))

    def update(name, slots, w, m, v):
        into = None
        for l, r in enumerate(slots):
            into = _adamw(r, w, m, v, l, into, f"adamw_{name}_{l}")
        return into

    out = {}
    for n in STACKED:
        w, m, v = (wsb_t, tsb(opt_m[n]), tsb(opt_v[n])) if n == "w_in_sb" else (p[n], opt_m[n], opt_v[n])
        res = update(n, [recv[(n, l)] for l in range(w.shape[0])], w, m, v)
        out[n] = [tsb(a) for a in res] if n == "w_in_sb" else res
    one = lambda a: a.reshape(1, -1, a.shape[-1])
    for n, r in small.items():
        res = update(n, [r.reshape((N_DEV,) + one(p[n]).shape[1:])], one(p[n]), one(opt_m[n]), one(opt_v[n]))
        out[n] = [a.reshape(p[n].shape) for a in res]
    loss = jnp.sum(landed[-1][:, 0, 0])
    return loss, grad_x, out


STACKED = ["ffn1_gate", "ffn1_up", "ffn1_down", "ffn2_gate", "ffn2_up", "ffn2_down", "w_in_pool", "w_in_sb", "w_mem_kv", "w_out"]
WEIGHTS = ["g_pre", "g_post", "g_mem", "ffn1_gate", "ffn1_up", "ffn1_down", "ffn2_gate", "ffn2_up", "ffn2_down",
           "w_in_pool", "pool_w", "pool_scale", "w_in_sb", "w_mem_kv", "w_out"]


def kernel(x, mem, g_pre, g_post, g_mem, ffn1_gate, ffn1_up, ffn1_down, ffn2_gate, ffn2_up, ffn2_down, w_in_pool, pool_w, pool_scale, w_in_sb, w_mem_kv, w_out, loss_target, m_g_pre, m_g_post, m_g_mem, m_ffn1_gate, m_ffn1_up, m_ffn1_down, m_ffn2_gate, m_ffn2_up, m_ffn2_down, m_w_in_pool, m_pool_w, m_pool_scale, m_w_in_sb, m_w_mem_kv, m_w_out, v_g_pre, v_g_post, v_g_mem, v_ffn1_gate, v_ffn1_up, v_ffn1_down, v_ffn2_gate, v_ffn2_up, v_ffn2_down, v_w_in_pool, v_pool_w, v_pool_scale, v_w_in_sb, v_w_mem_kv, v_w_out):
    given = dict(locals())
    p = {n: given[n] for n in WEIGHTS}
    opt_m = {n: given["m_" + n] for n in WEIGHTS}
    opt_v = {n: given["v_" + n] for n in WEIGHTS}
    loss, grad_x, out = _step(p, opt_m, opt_v, x[0], mem[0], loss_target[0])
    res = [loss, grad_x[None]]
    for q in range(4):
        res += [out[n][q] for n in WEIGHTS]
    return tuple(res)
```

```python
import functools
from typing import NamedTuple, Optional

import jax
import jax.numpy as jnp
from jax import lax
from jax.experimental import pallas as pl
from jax.experimental.pallas import tpu as pltpu

F32 = jnp.float32
BF16 = jnp.bfloat16
GRAD_WIRE = jnp.bfloat16

N_DEV = 8
DEPTH = 4
D_MODEL = 1024
D_FF = 2048
D_TOK = 512
D_MEMH = 256
D_MIX = D_TOK + D_MEMH
D_SB = 3 * D_TOK + D_MEMH
HEAD_DIM = 64
Q_BLOCK = 128
POOL_WINDOWS = (2, 4, 8, 16)
POOL_GROUP = 128
POOL_HALO = 16
EPS = 1e-6
ATT_SCALE = HEAD_DIM ** -0.5
SB_DEAD_LOG_WEIGHT = -110.0
SB_FWD_LANES = 512
SB_BWD_LANES = 256

ADAM_LR = 0.001
ADAM_B1 = 0.9
ADAM_B2 = 0.999
ADAM_EPS = 1e-08
ADAM_WD = 0.01
ADAM_STEP = 10

VMEM_LIMIT_BYTES = 56 * 1024 * 1024
ROW_TILE = 256
FFN_FWD_TILE = 512
FFN_BWD_TILE = 512
FFN_CHUNK = 256
WGRAD_TILE = 512
ADAM_TILE_ELEMS = 128 * 1024

MESH = pl.DeviceIdType.MESH
ANY = pl.BlockSpec(memory_space=pl.ANY)


def _tile(n, pref):
    t = 1 << (pref.bit_length() - 1)
    while n % t:
        t //= 2
    return t


def _dot(a, b):
    return jnp.dot(a, b, preferred_element_type=F32)


def _dot_nt(a, b):
    return lax.dot_general(a, b, (((1,), (1,)), ((), ())), preferred_element_type=F32)


def _dot_tn(a, b):
    return lax.dot_general(a, b, (((0,), (0,)), ((), ())), preferred_element_type=F32)


def _split_dot(x, m, terms):
    out = None
    rest = x
    for _ in range(terms):
        part = rest.astype(BF16)
        rest = rest - part.astype(F32)
        d = _dot(part, m)
        out = d if out is None else out + d
    return out


def _rms(x, g):
    r = lax.rsqrt(jnp.mean(x * x, axis=-1, keepdims=True) + EPS)
    return x * r * g


def _rms_bwd(x, g, dy):
    r = lax.rsqrt(jnp.mean(x * x, axis=-1, keepdims=True) + EPS)
    xh = x * r
    gdy = g * dy
    dx = r * (gdy - xh * jnp.mean(gdy * xh, axis=-1, keepdims=True))
    return dx, jnp.sum(dy * xh, axis=0, keepdims=True)


def _acc_rows(ref, val, first):
    @pl.when(first)
    def _():
        ref[...] = val

    @pl.when(jnp.logical_not(first))
    def _():
        ref[...] += val


def _params(sem=None):
    return pltpu.CompilerParams(dimension_semantics=sem, vmem_limit_bytes=VMEM_LIMIT_BYTES)


def _sds(shape, dtype):
    return jax.ShapeDtypeStruct(shape, dtype)


def _rows(tm, width, col=0):
    return pl.BlockSpec((tm, width), lambda i: (i, col))


def _full(shape):
    nd = len(shape)
    return pl.BlockSpec(shape, lambda *_: (0,) * nd)


def _resident(shape):
    nd = len(shape)
    return pl.BlockSpec(shape, lambda *_: (0,) * nd, pipeline_mode=pl.Buffered(1))


def _peers():
    x, y, c = lax.axis_index("x"), lax.axis_index("y"), lax.axis_index("c")
    peers = []
    for k in range(1, N_DEV):
        px = 1 - x if k & 4 else x
        py = 1 - y if k & 2 else y
        pc = 1 - c if k & 1 else c
        peers.append(((px, py, pc), 4 * px + 2 * py + pc))
    return 4 * x + 2 * y + c, peers


class _Xfer(NamedTuple):
    src: jax.Array
    layer: Optional[int] = None
    scatter: bool = False

    @property
    def landing(self):
        block = self.src.shape if self.layer is None and not self.scatter else self.src.shape[1:]
        return _sds((N_DEV,) + block, self.src.dtype)


def _comm_copies(xfers, src_refs, dst_refs, send_sems, recv_sems, local_sems):
    me, peers = _peers()

    def src(t, to):
        ref = src_refs[t] if xfers[t].layer is None else src_refs[t].at[xfers[t].layer]
        return ref.at[to] if xfers[t].scatter else ref

    copies = [pltpu.make_async_copy(src(t, me), dst_refs[t].at[me], local_sems.at[t]) for t in range(len(xfers))]
    for k, (dev, idx) in enumerate(peers):
        for t in range(len(xfers)):
            copies.append(pltpu.make_async_remote_copy(
                src_ref=src(t, idx), dst_ref=dst_refs[t].at[me], send_sem=send_sems.at[t, k], recv_sem=recv_sems.at[t, k],
                device_id=dev, device_id_type=MESH))
    return copies


def _pcall(body, *, name, grid, in_specs, out_specs, out_shape, args, scratch=(), ride=()):
    n_in, n_out, n_scr, nx = len(in_specs), len(out_specs), len(scratch), len(ride)
    params = _params(("arbitrary",) * len(grid))
    if not ride:
        res = pl.pallas_call(body, name=name, grid=grid, in_specs=list(in_specs), out_specs=list(out_specs),
                             out_shape=list(out_shape), scratch_shapes=list(scratch), compiler_params=params,
                             interpret=False)(*args)
        return list(res), []
    def riding(*refs):
        o0 = n_in + nx
        s0 = o0 + n_out + nx
        comm = (ride, refs[n_in:o0], refs[o0 + n_out:s0], *refs[s0 + n_scr:])
        ids = [pl.program_id(a) for a in range(len(grid))]
        first = functools.reduce(jnp.logical_and, [i == 0 for i in ids])
        last = functools.reduce(jnp.logical_and, [i == n - 1 for i, n in zip(ids, grid)])

        @pl.when(first)
        def _():
            for cp in _comm_copies(*comm):
                cp.start()

        body(*refs[:n_in], *refs[o0:o0 + n_out], *refs[s0:s0 + n_scr])

        @pl.when(last)
        def _():
            for cp in _comm_copies(*comm):
                cp.wait()

    sems = [pltpu.SemaphoreType.DMA((nx, N_DEV - 1)), pltpu.SemaphoreType.DMA((nx, N_DEV - 1)), pltpu.SemaphoreType.DMA((nx,))]
    res = pl.pallas_call(riding, name=name, grid=grid, in_specs=list(in_specs) + [ANY] * nx,
                         out_specs=list(out_specs) + [ANY] * nx, out_shape=list(out_shape) + [t.landing for t in ride],
                         scratch_shapes=list(scratch) + sems, compiler_params=params,
                         interpret=False)(*args, *[t.src for t in ride])
    return list(res[:n_out]), list(res[n_out:])


def _comm_call(xfers, name):
    return _pcall(lambda: None, name=name, grid=(1,), in_specs=[], out_specs=[], out_shape=[], args=[], ride=xfers)[1]


def _gather_two_level(xfers, name):
    n = len(xfers)
    assert not any(t.scatter for t in xfers)

    def body(*refs):
        ins, outs = refs[:n], refs[n:2 * n]
        send_sems, recv_sems, local_sems = refs[2 * n:]
        x, y, c = lax.axis_index("x"), lax.axis_index("y"), lax.axis_index("c")
        sibling = (x, y, 1 - c)
        chips = [(1 - x, y), (x, 1 - y), (1 - x, 1 - y)]
        slot = lambda dev: 4 * dev[0] + 2 * dev[1] + dev[2]

        def copy(t, k, block, to, own=False):
            src = ins[t] if xfers[t].layer is None else ins[t].at[xfers[t].layer]
            return pltpu.make_async_remote_copy(
                src_ref=src if own else outs[t].at[slot(block)], dst_ref=outs[t].at[slot(block)],
                send_sem=send_sems.at[t, k], recv_sem=recv_sems.at[t, k], device_id=to, device_id_type=MESH)

        me = (x, y, c)
        local, sent = [], []
        for t in range(n):
            src = ins[t] if xfers[t].layer is None else ins[t].at[xfers[t].layer]
            mine = pltpu.make_async_copy(src, outs[t].at[slot(me)], local_sems.at[t])
            mine.start()
            local.append(mine)
            first = [copy(t, 0, me, sibling, own=True)] + [copy(t, 1 + j, me, (*chip, c), own=True) for j, chip in enumerate(chips)]
            for cp in first:
                cp.start()
            sent += first
        for t in range(n):
            for j, chip in enumerate(chips):
                copy(t, 1 + j, (*chip, c), me).wait_recv()
                passed = copy(t, 4 + j, (*chip, c), sibling)
                passed.start()
                sent.append(passed)
        for t in range(n):
            copy(t, 0, sibling, me).wait_recv()
            for j, chip in enumerate(chips):
                copy(t, 4 + j, (*chip, 1 - c), me).wait_recv()
        for cp in sent:
            cp.wait_send()
        for cp in local:
            cp.wait()

    return pl.pallas_call(
        body, name=name, in_specs=[ANY] * n, out_specs=[ANY] * n, out_shape=[t.landing for t in xfers],
        scratch_shapes=[pltpu.SemaphoreType.DMA((n, N_DEV - 1)), pltpu.SemaphoreType.DMA((n, N_DEV - 1)),
                        pltpu.SemaphoreType.DMA((n,))],
        interpret=False,
    )(*[t.src for t in xfers])


def _ffn_fwd(h, gpre, gpost, wg, wu, wd, name, ride=()):
    S = h.shape[0]
    tm = _tile(S, FFN_FWD_TILE)
    nb = FFN_CHUNK

    def body(h_ref, gpre_ref, gpost_ref, wgt_ref, wut_ref, wd_ref, hn_ref, n_ref, gate_ref, up_ref, act_ref, f_ref):
        hv = h_ref[...]
        n = _rms(hv, gpre_ref[...]).astype(BF16)
        n_ref[...] = n
        for c in range(D_FF // nb):
            cols = slice(c * nb, (c + 1) * nb)
            g = _dot_nt(n, wgt_ref[cols, :])
            u = _dot_nt(n, wut_ref[cols, :])
            gate_ref[:, cols] = g.astype(BF16)
            up_ref[:, cols] = u.astype(BF16)
            act_ref[:, cols] = (g * jax.nn.sigmoid(g) * u).astype(BF16)
        f = _dot(act_ref[...], wd_ref[...])
        f_ref[...] = f
        hn_ref[...] = hv + 0.5 * _rms(f, gpost_ref[...])

    return _pcall(
        body, name=name, grid=(S // tm,),
        in_specs=[_rows(tm, D_MODEL), _full((1, D_MODEL)), _full((1, D_MODEL)),
                  _resident((D_FF, D_MODEL)), _resident((D_FF, D_MODEL)), _resident((D_FF, D_MODEL))],
        out_specs=[_rows(tm, D_MODEL), _rows(tm, D_MODEL), _rows(tm, D_FF), _rows(tm, D_FF), _rows(tm, D_FF),
                   _rows(tm, D_MODEL)],
        out_shape=[_sds((S, D_MODEL), F32), _sds((S, D_MODEL), BF16), _sds((S, D_FF), BF16), _sds((S, D_FF), BF16),
                   _sds((S, D_FF), BF16), _sds((S, D_MODEL), F32)],
        args=(h, gpre, gpost, wg, wu, wd), ride=ride)


def _ffn_bwd(dho, h, f, gate, up, gpre, gpost, wg, wu, wd, name, ride=()):
    S = h.shape[0]
    tm = _tile(S, FFN_BWD_TILE)
    nb = FFN_CHUNK

    def body(dho_ref, h_ref, f_ref, gate_ref, up_ref, gpre_ref, gpost_ref, wgt_ref, wut_ref, wd_ref,
             dh_ref, df_ref, dgate_ref, dup_ref, dgpre_ref, dgpost_ref):
        first = pl.program_id(0) == 0
        dho_v = dho_ref[...]
        dfx, dgpost = _rms_bwd(f_ref[...], gpost_ref[...], 0.5 * dho_v)
        dfb = dfx.astype(BF16)
        df_ref[...] = dfb
        for c in range(D_FF // nb):
            cols = slice(c * nb, (c + 1) * nb)
            dact = _dot_nt(dfb, wd_ref[cols, :])
            g = gate_ref[:, cols].astype(F32)
            u = up_ref[:, cols].astype(F32)
            s = jax.nn.sigmoid(g)
            dgate_ref[:, cols] = (dact * u * (s * (1.0 + g * (1.0 - s)))).astype(BF16)
            dup_ref[:, cols] = (dact * (g * s)).astype(BF16)
        dn = _dot(dgate_ref[...], wgt_ref[...]) + _dot(dup_ref[...], wut_ref[...])
        dhx, dgpre = _rms_bwd(h_ref[...], gpre_ref[...], dn)
        dh_ref[...] = dho_v + dhx
        _acc_rows(dgpre_ref, dgpre, first)
        _acc_rows(dgpost_ref, dgpost, first)

    return _pcall(
        body, name=name, grid=(S // tm,),
        in_specs=[_rows(tm, D_MODEL), _rows(tm, D_MODEL), _rows(tm, D_MODEL), _rows(tm, D_FF), _rows(tm, D_FF),
                  _full((1, D_MODEL)), _full((1, D_MODEL)),
                  _resident((D_FF, D_MODEL)), _resident((D_FF, D_MODEL)), _resident((D_FF, D_MODEL))],
        out_specs=[_rows(tm, D_MODEL), _rows(tm, D_MODEL), _rows(tm, D_FF), _rows(tm, D_FF),
                   _full((1, D_MODEL)), _full((1, D_MODEL))],
        out_shape=[_sds((S, D_MODEL), F32), _sds((S, D_MODEL), BF16), _sds((S, D_FF), BF16), _sds((S, D_FF), BF16),
                   _sds((1, D_MODEL), F32), _sds((1, D_MODEL), F32)],
        args=(dho, h, f, gate, up, gpre, gpost, wg, wu, wd), ride=ride)


def _wgrad(a_parts, b_parts, split, name, ride=()):
    S = a_parts[0].shape[0]
    bk = _tile(S, WGRAD_TILE)
    ms = [a.shape[1] for a in a_parts]
    ns = [b.shape[1] for b in b_parts]
    M, N = sum(ms), sum(ns)
    na, nbp = len(a_parts), len(b_parts)
    blk = (M // N_DEV, N) if split == "rows" else (M, N // N_DEV)
    steps = S // bk

    def body(*refs):
        a_refs, b_refs = refs[:na], refs[na:na + nbp]
        out_ref, acc_ref = refs[-2], refs[-1]
        k = pl.program_id(0)

        @pl.when(k == 0)
        def _():
            acc_ref[...] = jnp.zeros_like(acc_ref)

        r0 = 0
        for ai in range(na):
            av = a_refs[ai][...]
            c0 = 0
            for bi in range(nbp):
                acc_ref[r0:r0 + ms[ai], c0:c0 + ns[bi]] += _dot_tn(av, b_refs[bi][...])
                c0 += ns[bi]
            r0 += ms[ai]

        @pl.when(k == steps - 1)
        def _():
            for d in range(N_DEV):
                if split == "rows":
                    out_ref[d] = acc_ref[d * blk[0]:(d + 1) * blk[0], :].astype(GRAD_WIRE)
                else:
                    out_ref[d] = acc_ref[:, d * blk[1]:(d + 1) * blk[1]].astype(GRAD_WIRE)

    in_specs = [pl.BlockSpec((bk, m), lambda k: (k, 0)) for m in ms] + [pl.BlockSpec((bk, n), lambda k: (k, 0)) for n in ns]
    (out,), landed = _pcall(
        body, name=name, grid=(steps,), in_specs=in_specs, out_specs=[_full((N_DEV,) + blk)],
        out_shape=[_sds((N_DEV,) + blk, GRAD_WIRE)], scratch=[pltpu.VMEM((M, N), F32)],
        args=list(a_parts) + list(b_parts), ride=ride)
    return out, landed


def _mix_in_pool(h, g1, w_in, pool_w, pool_scale, name, ride=()):
    S = h.shape[0]
    tm = _tile(S, ROW_TILE)
    kb = D_MODEL // N_DEV

    def body(h_ref, g_ref, w_ref, pw_ref, ps_ref, u_ref, dpre_ref, tok_ref, qm_ref, ext_ref):
        i = pl.program_id(0)
        u = _rms(h_ref[...], g_ref[...]).astype(BF16)
        u_ref[...] = u
        proj = _dot(u, w_ref[...])
        qm_ref[...] = proj[:, D_TOK:].astype(BF16)
        x = proj[:, :D_TOK]

        @pl.when(i == 0)
        def _():
            ext_ref[0:POOL_HALO, :] = jnp.zeros((POOL_HALO, D_TOK), F32)

        ext_ref[POOL_HALO:, :] = x
        pos = i * tm + lax.broadcasted_iota(jnp.int32, (tm, 1), 0)
        for gi, w in enumerate(POOL_WINDOWS):
            cols = slice(gi * POOL_GROUP, (gi + 1) * POOL_GROUP)
            xs = x[:, cols]
            wsum = xs
            for k in range(1, w):
                wsum = wsum + ext_ref[POOL_HALO - k:POOL_HALO - k + tm, cols]
            cnt = jnp.minimum(pos + 1, w).astype(F32)
            dg = (wsum / cnt - xs).astype(BF16)
            dpre_ref[:, cols] = dg
            yv = _dot(dg, pw_ref[gi].astype(BF16))
            tok_ref[:, cols] = (yv * ps_ref[:, cols]).astype(BF16)
        ext_ref[0:POOL_HALO, :] = x[tm - POOL_HALO:, :]

    return _pcall(
        body, name=name, grid=(S // tm,),
        in_specs=[_rows(tm, D_MODEL), _full((1, D_MODEL)), _resident((D_MODEL, D_MIX)),
                  _full((len(POOL_WINDOWS), POOL_GROUP, POOL_GROUP)), _full((1, D_TOK))],
        out_specs=[_rows(tm, D_MODEL), _rows(tm, D_TOK), _rows(tm, D_TOK), _rows(tm, D_MEMH)],
        out_shape=[_sds((S, D_MODEL), BF16), _sds((S, D_TOK), BF16), _sds((S, D_TOK), BF16), _sds((S, D_MEMH), BF16)],
        scratch=[pltpu.VMEM((POOL_HALO + tm, D_TOK), F32)],
        args=(h, g1, w_in, pool_w, pool_scale), ride=ride)


def _pool_bwd(dtok, dpre, pool_w, pool_scale, name):
    S = dtok.shape[0]
    tm = _tile(S, ROW_TILE)
    nt = S // tm
    ng = len(POOL_WINDOWS)

    def body(dtok_ref, dpre_ref, pw_ref, ps_ref, dx_ref, dpw_ref, dps_ref, ext_ref):
        i = pl.program_id(0)
        first = i == 0
        t0 = (nt - 1 - i) * tm
        pos = t0 + lax.broadcasted_iota(jnp.int32, (tm, 1), 0)

        @pl.when(first)
        def _():
            ext_ref[tm:, :] = jnp.zeros((POOL_HALO, D_TOK), F32)

        dps = []
        for gi, w in enumerate(POOL_WINDOWS):
            cols = slice(gi * POOL_GROUP, (gi + 1) * POOL_GROUP)
            dg = dpre_ref[:, cols]
            pw = pw_ref[gi].astype(BF16)
            dt = dtok_ref[:, cols].astype(F32)
            yv = _dot(dg, pw)
            dps.append(jnp.sum(dt * yv, axis=0, keepdims=True))
            dy = (dt * ps_ref[:, cols]).astype(BF16)
            _acc_rows(dpw_ref.at[gi], _dot_tn(dg, dy), first)
            dd = _dot_nt(dy, pw)
            cnt = jnp.minimum(pos + 1, w).astype(F32)
            ext_ref[0:tm, cols] = dd / cnt
            wsum = ext_ref[0:tm, cols]
            for k in range(1, w):
                wsum = wsum + ext_ref[k:k + tm, cols]
            dx_ref[:, cols] = (wsum - dd).astype(BF16)
        _acc_rows(dps_ref, jnp.concatenate(dps, axis=1), first)
        ext_ref[tm:, :] = ext_ref[0:POOL_HALO, :]

    rev = lambda i: (nt - 1 - i, 0)
    return pl.pallas_call(
        body, name=name, grid=(nt,),
        in_specs=[pl.BlockSpec((tm, D_TOK), rev), pl.BlockSpec((tm, D_TOK), rev),
                  _full((ng, POOL_GROUP, POOL_GROUP)), _full((1, D_TOK))],
        out_specs=[pl.BlockSpec((tm, D_TOK), rev), _full((ng, POOL_GROUP, POOL_GROUP)), _full((1, D_TOK))],
        out_shape=[_sds((S, D_TOK), BF16), _sds((ng, POOL_GROUP, POOL_GROUP), F32), _sds((1, D_TOK), F32)],
        scratch_shapes=[pltpu.VMEM((tm + POOL_HALO, D_TOK), F32)],
        compiler_params=_params(("arbitrary",)),
        interpret=False,
    )(dtok, dpre, pool_w, pool_scale)


def _mix_in_sb(h, g1, wt, name):
    S = h.shape[0]
    tm = _tile(S, ROW_TILE)
    cb = 256

    def body(h_ref, g_ref, wt_ref, u_ref, proj_ref):
        u = _rms(h_ref[...], g_ref[...]).astype(BF16)
        u_ref[...] = u
        for c in range(D_SB // cb):
            proj_ref[:, c * cb:(c + 1) * cb] = _dot_nt(u, wt_ref[c * cb:(c + 1) * cb, :]).astype(BF16)

    return pl.pallas_call(
        body, name=name, grid=(S // tm,),
        in_specs=[_rows(tm, D_MODEL), _full((1, D_MODEL)), _resident((D_SB, D_MODEL))],
        out_specs=[_rows(tm, D_MODEL), _rows(tm, D_SB)],
        out_shape=[_sds((S, D_MODEL), BF16), _sds((S, D_SB), BF16)],
        compiler_params=_params(("arbitrary",)),
        interpret=False,
    )(h, g1, wt)


def _mix_in_bwd(dho, h, g1, parts, w, mode, name):
    S = h.shape[0]
    tm = _tile(S, ROW_TILE)
    widths = [p.shape[1] for p in parts]
    npart = len(parts)
    kb = D_MODEL // N_DEV

    def body(*refs):
        dho_ref, h_ref, g_ref = refs[:3]
        p_refs = refs[3:3 + npart]
        w_ref, dh_ref, dg_ref = refs[3 + npart:]
        first = pl.program_id(0) == 0
        dproj = jnp.concatenate([p[...] for p in p_refs], axis=1)
        du = _dot_nt(dproj, w_ref[...]) if mode == "pool" else _dot(dproj, w_ref[...])
        dhx, dg = _rms_bwd(h_ref[...], g_ref[...], du)
        dh_ref[...] = dho_ref[...] + dhx
        _acc_rows(dg_ref, dg, first)

    w_spec = _resident((D_MODEL, D_MIX)) if mode == "pool" else _resident((D_SB, D_MODEL))
    return pl.pallas_call(
        body, name=name, grid=(S // tm,),
        in_specs=[_rows(tm, D_MODEL), _rows(tm, D_MODEL), _full((1, D_MODEL))] + [_rows(tm, wd_) for wd_ in widths] + [w_spec],
        out_specs=[_rows(tm, D_MODEL), _full((1, D_MODEL))],
        out_shape=[_sds((S, D_MODEL), F32), _sds((1, D_MODEL), F32)],
        compiler_params=_params(("arbitrary",)),
        interpret=False,
    )(dho, h, g1, *parts, w)


def _sb_block(qcats, kb, mask, later, tri_later):
    z = jnp.concatenate([_dot_nt(qc, _group(kb, g)) for g, qc in enumerate(qcats)], axis=0) * ATT_SCALE
    en = jnp.exp(-jnp.abs(z))
    ls = jnp.minimum(z, 0.0) - jnp.log(1.0 + en)
    lf = ls - z if mask is None else jnp.where(mask, ls - z, 0.0)
    within = _split_dot(lf, tri_later, 2)
    a = jnp.exp(ls + within + later)
    if mask is not None:
        a = jnp.where(mask, a, 0.0)
    return z, en, a, jnp.sum(lf, axis=1, keepdims=True)


def _sb_causal(rows):
    row = jnp.bitwise_and(lax.broadcasted_iota(jnp.int32, (rows, Q_BLOCK), 0), Q_BLOCK - 1)
    return lax.broadcasted_iota(jnp.int32, (rows, Q_BLOCK), 1) < row


def _sb_walk(qi, block, state, later_of, unrolled):
    if unrolled:
        state = block(qi, _sb_causal, state)
        state = block(jnp.maximum(qi - 1, 0), qi >= 1, state)

    def step(carry):
        j, _, state = carry
        mask = None if unrolled else lambda rows: jnp.logical_or(_sb_causal(rows), j > 0)
        state = block(qi - j, mask, state)
        return j + 1, _sb_alive(later_of(state)), state

    first = jnp.int32(2 if unrolled else 0)
    return lax.while_loop(functools.partial(_sb_more, qi), step, (first, _sb_alive(later_of(state)), state))[2]


def _sb_alive(later):
    return jnp.max(later) > SB_DEAD_LOG_WEIGHT


def _sb_more(qi, carry):
    return jnp.logical_and(carry[0] <= qi, carry[1])


def _tri(strict):
    row = lax.broadcasted_iota(jnp.int32, (Q_BLOCK, Q_BLOCK), 0)
    col = lax.broadcasted_iota(jnp.int32, (Q_BLOCK, Q_BLOCK), 1)
    return (row > col if strict else row >= col).astype(BF16)


HEADS_PER_GROUP = 128 // HEAD_DIM
GROUP_ROWS = HEADS_PER_GROUP * Q_BLOCK


def _sb_head_masks():
    lane = lax.broadcasted_iota(jnp.int32, (Q_BLOCK, 128), 1)
    return [(lane >= e * HEAD_DIM) & (lane < (e + 1) * HEAD_DIM) for e in range(HEADS_PER_GROUP)]


def _group(x, g):
    return x[:, g * 128:(g + 1) * 128]


def _masked(hm, x):
    return jnp.where(hm, x, jnp.zeros_like(x))


def _stack_heads(hms, x):
    return jnp.concatenate([_masked(hm, x) for hm in hms], axis=0)


def _own_lanes(hms, r, rows=Q_BLOCK):
    return sum(_masked(hm, r[e * rows:(e + 1) * rows]) for e, hm in enumerate(hms))


def _sb_fwd(proj, name, ride=()):
    S = proj.shape[0]
    nq = S // Q_BLOCK
    W = SB_FWD_LANES
    nrow = D_TOK // W
    groups = W // 128

    def body(q_ref, k_ref, v_ref, o_ref, tok_ref):
        qi = pl.program_id(1)
        hms = _sb_head_masks()
        tri_later = _tri(True)
        q = q_ref[...]
        qcats = [_stack_heads(hms, _group(q, g)) for g in range(groups)]

        def block(kblock, mask, state):
            accs, later = state
            if callable(mask):
                mask = mask(later.shape[0])
            off = pl.multiple_of(kblock * Q_BLOCK, Q_BLOCK)
            kb = k_ref[pl.ds(off, Q_BLOCK), :]
            vb = v_ref[pl.ds(off, Q_BLOCK), :]
            _, _, a, bsum = _sb_block(qcats, kb, mask, later, tri_later)
            hi = a.astype(BF16)
            lo = (a - hi.astype(F32)).astype(BF16)
            accs = list(accs)
            for g in range(groups):
                rows = slice(g * GROUP_ROWS, (g + 1) * GROUP_ROWS)
                r = _dot(jnp.concatenate([hi[rows], lo[rows]], axis=0), _group(vb, g))
                accs[g] = accs[g] + _own_lanes(hms, r[:GROUP_ROWS] + r[GROUP_ROWS:])
            return tuple(accs), later + bsum

        init = ((jnp.zeros((Q_BLOCK, 128), F32),) * groups, jnp.zeros((groups * GROUP_ROWS, 1), F32))
        accs = _sb_walk(qi, block, init, lambda state: state[1], unrolled=True)[0]
        for g, acc in enumerate(accs):
            o_ref[:, g * 128:(g + 1) * 128] = acc
            tok_ref[:, g * 128:(g + 1) * 128] = acc.astype(BF16)

    blk = pl.BlockSpec((Q_BLOCK, W), lambda p, i: (i, p))
    return _pcall(
        body, name=name, grid=(nrow, nq),
        in_specs=[blk, pl.BlockSpec((S, W), lambda p, i: (0, nrow + p), pipeline_mode=pl.Buffered(1)),
                  pl.BlockSpec((S, W), lambda p, i: (0, 2 * nrow + p), pipeline_mode=pl.Buffered(1))],
        out_specs=[blk, blk],
        out_shape=[_sds((S, D_TOK), F32), _sds((S, D_TOK), BF16)],
        args=(proj, proj, proj), ride=ride)


def _sb_bwd(proj, dtok, o32, name, ride=()):
    S = proj.shape[0]
    nq = S // Q_BLOCK
    W = SB_BWD_LANES
    nrow = D_TOK // W
    groups = W // 128

    def body(q_ref, k_ref, v_ref, do_ref, o_ref, dq_ref, dk_ref, dv_ref, dk_acc, dv_acc):
        qi = pl.program_id(1)

        @pl.when(qi == 0)
        def _():
            dk_acc[...] = jnp.zeros_like(dk_acc)
            dv_acc[...] = jnp.zeros_like(dv_acc)

        hms = _sb_head_masks()
        tri_later = _tri(True)
        tri_from = _tri(False)
        q = q_ref[...]
        do = do_ref[...]
        dov = do.astype(F32) * o_ref[...]
        qcats = [_stack_heads(hms, _group(q, g)) for g in range(groups)]
        docats = [_stack_heads(hms, _group(do, g)) for g in range(groups)]
        rowtot = jnp.concatenate([jnp.sum(jnp.where(hm, _group(dov, g), 0.0), axis=1, keepdims=True)
                                  for g in range(groups) for hm in hms], axis=0)

        def block(kblock, mask, state):
            dqs, later, seen = state
            if callable(mask):
                mask = mask(later.shape[0])
            off = pl.multiple_of(kblock * Q_BLOCK, Q_BLOCK)
            kb = k_ref[pl.ds(off, Q_BLOCK), :]
            vb = v_ref[pl.ds(off, Q_BLOCK), :]
            z, en, a, bsum = _sb_block(qcats, kb, mask, later, tri_later)
            inv = 1.0 / (1.0 + en)
            beta = jnp.where(z >= 0, 1.0, en) * inv
            omb = jnp.where(z >= 0, en, 1.0) * inv
            dlogw = a * jnp.concatenate([_dot_nt(docats[g], _group(vb, g)) for g in range(groups)], axis=0)
            prefix = rowtot - seen - _split_dot(dlogw, tri_from, 2)
            dz = dlogw * omb - beta * prefix
            if mask is not None:
                dz = jnp.where(mask, dz, 0.0)
            dz = dz.astype(BF16)
            ab = a.astype(BF16)
            dqs = list(dqs)
            for g in range(groups):
                rows = slice(g * GROUP_ROWS, (g + 1) * GROUP_ROWS)
                lanes = slice(g * 128, (g + 1) * 128)
                dqs[g] = dqs[g] + _own_lanes(hms, _dot(dz[rows], _group(kb, g)))
                dk_acc[pl.ds(off, Q_BLOCK), lanes] += _dot_tn(dz[rows], qcats[g]) * ATT_SCALE
                dv_acc[pl.ds(off, Q_BLOCK), lanes] += _dot_tn(ab[rows], docats[g])
            return tuple(dqs), later + bsum, seen + jnp.sum(dlogw, axis=1, keepdims=True)

        zero = jnp.zeros((groups * GROUP_ROWS, 1), F32)
        init = ((jnp.zeros((Q_BLOCK, 128), F32),) * groups, zero, zero)
        dqs = _sb_walk(qi, block, init, lambda state: state[1], unrolled=False)[0]
        for g, dq in enumerate(dqs):
            dq_ref[:, g * 128:(g + 1) * 128] = (dq * ATT_SCALE).astype(BF16)

        @pl.when(qi == nq - 1)
        def _():
            dk_ref[...] = dk_acc[...].astype(BF16)
            dv_ref[...] = dv_acc[...].astype(BF16)

    blk = pl.BlockSpec((Q_BLOCK, W), lambda p, i: (i, p))
    col = pl.BlockSpec((S, W), lambda p, i: (0, p))
    return _pcall(
        body, name=name, grid=(nrow, nq),
        in_specs=[blk, pl.BlockSpec((S, W), lambda p, i: (0, nrow + p)), pl.BlockSpec((S, W), lambda p, i: (0, 2 * nrow + p)),
                  blk, blk],
        out_specs=[blk, col, col],
        out_shape=[_sds((S, D_TOK), BF16), _sds((S, D_TOK), BF16), _sds((S, D_TOK), BF16)],
        scratch=[pltpu.VMEM((S, W), F32), pltpu.VMEM((S, W), F32)],
        args=(proj, proj, proj, dtok, o32), ride=ride)


def _mem_kv_fwd(mem, g_mem, w_kv, name):
    lm = mem.shape[0]
    kb = D_MODEL // N_DEV

    def body(mem_ref, g_ref, w_ref, mn_ref, km_ref, vm_ref):
        mn = _rms(mem_ref[...], g_ref[...]).astype(BF16)
        mn_ref[...] = mn
        kv = jnp.zeros((lm, 2 * D_MEMH), F32)
        for d in range(N_DEV):
            kv = kv + _dot(mn[:, d * kb:(d + 1) * kb], w_ref[d])
        km_ref[...] = kv[:, :D_MEMH].astype(BF16)
        vm_ref[...] = kv[:, D_MEMH:].astype(BF16)

    return pl.pallas_call(
        body, name=name, grid=(1,),
        in_specs=[_full((lm, D_MODEL)), _full((1, D_MODEL)), _full((N_DEV, kb, 2 * D_MEMH))],
        out_specs=[_full((lm, D_MODEL)), _full((lm, D_MEMH)), _full((lm, D_MEMH))],
        out_shape=[_sds((lm, D_MODEL), BF16), _sds((lm, D_MEMH), BF16), _sds((lm, D_MEMH), BF16)],
        compiler_params=_params(("arbitrary",)),
        interpret=False,
    )(mem, g_mem, w_kv)


def _mem_kv_bwd(dkm, dvm, mem, g_mem, mem_n, w_kv, name):
    lm = mem.shape[0]
    kb = D_MODEL // N_DEV

    def body(dkm_ref, dvm_ref, mem_ref, g_ref, mn_ref, w_ref, dw_ref, dg_ref):
        dkv = jnp.concatenate([dkm_ref[...], dvm_ref[...]], axis=1).astype(BF16)
        dw = _dot_tn(mn_ref[...], dkv)
        for d in range(N_DEV):
            dw_ref[d] = dw[d * kb:(d + 1) * kb, :].astype(GRAD_WIRE)
        dmn = jnp.concatenate([_dot_nt(dkv, w_ref[d]) for d in range(N_DEV)], axis=1)
        _, dg = _rms_bwd(mem_ref[...], g_ref[...], dmn)
        dg_ref[...] = dg

    return pl.pallas_call(
        body, name=name, grid=(1,),
        in_specs=[_full((lm, D_MEMH)), _full((lm, D_MEMH)), _full((lm, D_MODEL)), _full((1, D_MODEL)),
                  _full((lm, D_MODEL)), _full((N_DEV, kb, 2 * D_MEMH))],
        out_specs=[_full((N_DEV, kb, 2 * D_MEMH)), _full((1, D_MODEL))],
        out_shape=[_sds((N_DEV, kb, 2 * D_MEMH), GRAD_WIRE), _sds((1, D_MODEL), F32)],
        compiler_params=_params(("arbitrary",)),
        interpret=False,
    )(dkm, dvm, mem, g_mem, mem_n, w_kv)


def _mem_heads(tm):
    lane = lax.broadcasted_iota(jnp.int32, (tm, D_MEMH), 1)
    return [(lane >= e * HEAD_DIM) & (lane < (e + 1) * HEAD_DIM) for e in range(D_MEMH // HEAD_DIM)]


def _softmax(s):
    m = jnp.max(s, axis=-1, keepdims=True)
    p = jnp.exp(s - m)
    return p / jnp.sum(p, axis=-1, keepdims=True)


def _mix_out_fwd(h, tok, qm, qm_col, km, vm, w_out, gpost, name, ride=()):
    S = h.shape[0]
    tm = _tile(S, ROW_TILE)
    lm = km.shape[0]
    nb = D_MODEL // N_DEV

    def body(h_ref, tok_ref, qm_ref, km_ref, vm_ref, w_ref, g_ref, hn_ref, mo_ref, mix_ref):
        qv = qm_ref[...]
        kv, vv = km_ref[...], vm_ref[...]
        hms = _mem_heads(tm)
        p = _softmax(_dot_nt(_stack_heads(hms, qv), kv) * ATT_SCALE)
        mob = _own_lanes(hms, _dot(p.astype(BF16), vv), tm).astype(BF16)
        mo_ref[...] = mob
        mix = _dot_nt(jnp.concatenate([tok_ref[...], mob], axis=1), w_ref[...])
        mix_ref[...] = mix
        hn_ref[...] = h_ref[...] + _rms(mix, g_ref[...])

    return _pcall(
        body, name=name, grid=(S // tm,),
        in_specs=[_rows(tm, D_MODEL), _rows(tm, D_TOK), _rows(tm, D_MEMH, qm_col), _full((lm, D_MEMH)), _full((lm, D_MEMH)),
                  _resident((D_MODEL, D_MIX)), _full((1, D_MODEL))],
        out_specs=[_rows(tm, D_MODEL), _rows(tm, D_MEMH), _rows(tm, D_MODEL)],
        out_shape=[_sds((S, D_MODEL), F32), _sds((S, D_MEMH), BF16), _sds((S, D_MODEL), F32)],
        args=(h, tok, qm, km, vm, w_out, gpost), ride=ride)


def _mix_out_bwd(dho, mix, qm, qm_col, km, vm, w_out, gpost, name, ride=()):
    S = dho.shape[0]
    tm = _tile(S, ROW_TILE)
    lm = km.shape[0]
    nb = D_MODEL // N_DEV

    def body(dho_ref, mix_ref, qm_ref, km_ref, vm_ref, w_ref, g_ref,
             dmix_ref, dtok_ref, dqm_ref, dkm_ref, dvm_ref, dg_ref):
        first = pl.program_id(0) == 0
        dmx, dg = _rms_bwd(mix_ref[...], g_ref[...], dho_ref[...])
        dmb = dmx.astype(BF16)
        dmix_ref[...] = dmb
        _acc_rows(dg_ref, dg, first)
        dcat = _dot(dmb, w_ref[...])
        dtok_ref[...] = dcat[:, :D_TOK].astype(BF16)
        dmo = dcat[:, D_TOK:].astype(BF16)
        qv = qm_ref[...]
        kv, vv = km_ref[...], vm_ref[...]
        hms = _mem_heads(tm)
        qcat = _stack_heads(hms, qv)
        dmcat = _stack_heads(hms, dmo)
        p = _softmax(_dot_nt(qcat, kv) * ATT_SCALE)
        dp = _dot_nt(dmcat, vv)
        ds = (p * (dp - jnp.sum(p * dp, axis=-1, keepdims=True))).astype(BF16)
        dq = _own_lanes(hms, _dot(ds, kv), tm)
        dk = _dot_tn(ds, qcat)
        dv = _dot_tn(p.astype(BF16), dmcat)
        dqm_ref[...] = (dq * ATT_SCALE).astype(BF16)
        _acc_rows(dkm_ref, dk * ATT_SCALE, first)
        _acc_rows(dvm_ref, dv, first)

    return _pcall(
        body, name=name, grid=(S // tm,),
        in_specs=[_rows(tm, D_MODEL), _rows(tm, D_MODEL), _rows(tm, D_MEMH, qm_col), _full((lm, D_MEMH)), _full((lm, D_MEMH)),
                  _resident((D_MODEL, D_MIX)), _full((1, D_MODEL))],
        out_specs=[_rows(tm, D_MODEL), _rows(tm, D_TOK), _rows(tm, D_MEMH), _full((lm, D_MEMH)), _full((lm, D_MEMH)),
                   _full((1, D_MODEL))],
        out_shape=[_sds((S, D_MODEL), BF16), _sds((S, D_TOK), BF16), _sds((S, D_MEMH), BF16),
                   _sds((lm, D_MEMH), F32), _sds((lm, D_MEMH), F32), _sds((1, D_MODEL), F32)],
        args=(dho, mix, qm, km, vm, w_out, gpost), ride=ride)


def _loss_head(y, target, name):
    S = y.shape[0]
    tm = _tile(S, ROW_TILE)
    nt = S // tm

    def body(y_ref, t_ref, dy_ref, loss_ref, acc_ref):
        i = pl.program_id(0)
        e = y_ref[...] - t_ref[...]
        dy_ref[...] = e * (1.0 / D_MODEL)
        _acc_rows(acc_ref, jnp.sum(e * e, axis=0, keepdims=True), i == 0)

        @pl.when(i == nt - 1)
        def _():
            tot = jnp.sum(acc_ref[...], axis=1, keepdims=True) * (0.5 / D_MODEL)
            loss_ref[...] = jnp.broadcast_to(tot, (1, 128))

    return pl.pallas_call(
        body, name=name, grid=(nt,),
        in_specs=[_rows(tm, D_MODEL), _rows(tm, D_MODEL)],
        out_specs=[_rows(tm, D_MODEL), _full((1, 128))],
        out_shape=[_sds((S, D_MODEL), F32), _sds((1, 128), F32)],
        scratch_shapes=[pltpu.VMEM((1, D_MODEL), F32)],
        compiler_params=_params(("arbitrary",)),
        interpret=False,
    )(y, target)


def _adamw(recv, w, m, v, l, into, name):
    L, R, C = w.shape
    tr = R if R * C <= ADAM_TILE_ELEMS else _tile(R, ADAM_TILE_ELEMS // C)
    c1 = 1.0 - ADAM_B1 ** ADAM_STEP
    c2 = 1.0 - ADAM_B2 ** ADAM_STEP

    def body(r_ref, w_ref, m_ref, v_ref, *rest):
        g_ref, d_ref, nm_ref, nv_ref = rest[-4:]
        g = r_ref[0].astype(F32)
        for s in range(1, N_DEV):
            g = g + r_ref[s].astype(F32)
        g_ref[...] = g
        nm = ADAM_B1 * m_ref[...] + (1.0 - ADAM_B1) * g
        nv = ADAM_B2 * v_ref[...] + (1.0 - ADAM_B2) * (g * g)
        nm_ref[...] = nm
        nv_ref[...] = nv
        d_ref[...] = -ADAM_LR * ((nm / c1) / (jnp.sqrt(nv / c2) + ADAM_EPS) + ADAM_WD * w_ref[...])

    t = pl.BlockSpec((None, tr, C), lambda i: (l, i, 0))
    kept = [] if into is None else list(into)
    return pl.pallas_call(
        body, name=name, grid=(R // tr,),
        in_specs=[pl.BlockSpec((N_DEV, tr, C), lambda i: (0, i, 0)), t, t, t] + [ANY] * len(kept),
        out_specs=[t, t, t, t],
        out_shape=[_sds((L, R, C), F32)] * 4,
        input_output_aliases={4 + q: q for q in range(len(kept))},
        compiler_params=_params(("arbitrary",)),
        interpret=False,
    )(recv, w, m, v, *kept)


def _step(p, opt_m, opt_v, x, mem, target):
    bf = lambda a: a.astype(BF16)
    row = lambda a: a.reshape(1, -1)
    tsb = lambda a: jnp.swapaxes(a, 1, 2)
    g_mem = p["g_mem"]

    wsb_t = tsb(p["w_in_sb"])
    travels_transposed = ("w_in_sb", "w_out", "ffn1_gate", "ffn1_up", "ffn2_gate", "ffn2_up")
    shard = {n: bf(tsb(p[n]) if n in travels_transposed else p[n]) for n in STACKED}
    ffn_weights = lambda which, i: [gw[k].reshape(D_FF, D_MODEL) for k in ffn(which, i)]
    w_in = lambda i: "w_in_pool" if i % 2 == 0 else "w_in_sb"
    ffn = lambda which, i: [(f"ffn{which}_{s}", i) for s in ("gate", "up", "down")]
    mixing = lambda i: [(w_in(i), i // 2), ("w_mem_kv", i), ("w_out", i)]

    gw = {}

    def gather(keys):
        return [_Xfer(shard[n], l) for n, l in keys]

    first = ffn(1, 0)
    landed = _gather_two_level(gather(first) + [_Xfer(p["g_pre"]), _Xfer(p["g_post"])], "gather_first")
    gw.update(zip(first, landed))
    unshard = lambda g: jnp.transpose(g, (1, 2, 0, 3)).reshape(DEPTH, 3, D_MODEL)
    g_pre, g_post = unshard(landed[-2]), unshard(landed[-1])

    def ahead(i):
        nxt = i + 1 < DEPTH
        if i % 2 == 0:
            start = i == 0
            return {"ffn1": ffn(2, i)[:2] + (mixing(0)[:1] if start else []), "mix_in": ffn(2, i)[2:] + (mixing(0)[1:] if start else []),
                    "sb": [], "mix_out": ffn(1, i + 1)[:1] if nxt else [], "ffn2": ffn(1, i + 1)[1:] if nxt else []}
        return {"ffn1": mixing(i), "mix_in": [], "sb": ffn(2, i) + (ffn(1, i + 1)[:2] if nxt else []),
                "mix_out": [], "ffn2": ffn(1, i + 1)[2:] + mixing(i + 1) if nxt else []}

    def gathering(keys, call):
        res, landed = call(ride=gather(keys))
        gw.update(zip(keys, landed))
        return res

    saved = []
    h = x
    for i in range(DEPTH):
        j = i // 2
        st = {"h0": h}
        carry = ahead(i)
        h, st["n1"], st["gate1"], st["up1"], st["act1"], st["f1"] = gathering(carry["ffn1"], functools.partial(
            _ffn_fwd, h, row(g_pre[i, 0]), row(g_post[i, 0]), *ffn_weights(1, i), f"ffn1_fwd_{i}"))
        st["h1"] = h
        if i % 2 == 0:
            st["u"], st["dpre"], st["tok"], st["qm"] = gathering(carry["mix_in"], functools.partial(
                _mix_in_pool, h, row(g_pre[i, 1]), gw[("w_in_pool", j)].reshape(D_MODEL, D_MIX),p["pool_w"][j], row(p["pool_scale"][j]),
                f"mix_in_pool_{i}"))
            qm, qm_col = st["qm"], 0
        else:
            st["u"], st["proj"] = _mix_in_sb(h, row(g_pre[i, 1]), gw[("w_in_sb", j)].reshape(D_SB, D_MODEL), f"mix_in_sb_{i}")
            st["o32"], st["tok"] = gathering(carry["sb"], functools.partial(_sb_fwd, st["proj"], f"sb_fwd_{i}"))
            qm, qm_col = st["proj"], 3 * D_TOK // D_MEMH
        st["mem_n"], st["km"], st["vm"] = _mem_kv_fwd(mem, row(g_mem[i]), gw[("w_mem_kv", i)], f"mem_kv_fwd_{i}")
        h, st["mo"], st["mix"] = gathering(carry["mix_out"], functools.partial(
            _mix_out_fwd, h, st["tok"], qm, qm_col, st["km"], st["vm"], gw[("w_out", i)].reshape(D_MODEL, D_MIX), row(g_post[i, 1]), f"mix_out_fwd_{i}"))
        st["h2"] = h
        h, st["n2"], st["gate2"], st["up2"], st["act2"], st["f2"] = gathering(carry["ffn2"], functools.partial(
            _ffn_fwd, h, row(g_pre[i, 2]), row(g_post[i, 2]), *ffn_weights(2, i), f"ffn2_fwd_{i}"))
        saved.append(st)

    dh, loss_part = _loss_head(h, target, "loss_head")

    grads = {}
    recv = {}
    dg_pre = [[None] * 3 for _ in range(DEPTH)]
    dg_post = [[None] * 3 for _ in range(DEPTH)]
    dg_mem = [None] * DEPTH
    dpool_w = [None, None]
    dpool_scale = [None, None]

    def scatter(keys):
        return [_Xfer(grads[k], scatter=True) for k in keys]

    def ffn_backward(dh, st, i, which, hkey, slot, riding, last):
        sfx = str(which)
        keys = ffn(which, i)
        (dh, df, dgate, dup, dg_pre[i][slot], dg_post[i][slot]), landed = _ffn_bwd(
            dh, st[hkey], st["f" + sfx], st["gate" + sfx], st["up" + sfx], row(g_pre[i, slot]), row(g_post[i, slot]),
            *ffn_weights(which, i), f"ffn{sfx}_bwd_{i}", ride=scatter(riding))
        recv.update(zip(riding, landed))
        riders = [mixing(i), keys[:1], keys[1:2]] if last else [[], [], []]
        operands = ((st["n" + sfx], dgate, "cols"), (st["n" + sfx], dup, "cols"), (st["act" + sfx], df, "rows"))
        for key, (a, b, split), riding in zip(keys, operands, riders):
            grads[key], landed = _wgrad([a], [b], split, f"wgrad_{key[0]}_{i}", ride=scatter(riding))
            recv.update(zip(riding, landed))
        return dh

    def behind(i):
        prev = ffn(1, i + 1) if i + 1 < DEPTH else []
        if i % 2 == 1:
            return {"ffn2": prev, "mix_out": [], "w_out": [], "sb": ffn(2, i), "ffn1": mixing(i)}
        if i > 0:
            return {"ffn2": prev, "mix_out": ffn(2, i)[:2], "w_out": [], "sb": [], "ffn1": ffn(2, i)[2:] + mixing(i)}
        return {"ffn2": prev, "mix_out": ffn(2, 0)[:2], "w_out": ffn(2, 0)[2:], "sb": [], "ffn1": []}

    for i in reversed(range(DEPTH)):
        j = i // 2
        st = saved[i]
        carry = behind(i)
        dh = ffn_backward(dh, st, i, 2, "h2", 2, carry["ffn2"], False)
        if i % 2 == 0:
            qm, qm_col = st["qm"], 0
        else:
            qm, qm_col = st["proj"], 3 * D_TOK // D_MEMH
        keys = carry["mix_out"]
        (dmix, dtok, dqm, dkm, dvm, dg_post[i][1]), landed = _mix_out_bwd(
            dh, st["mix"], qm, qm_col, st["km"], st["vm"], gw[("w_out", i)].reshape(D_MODEL, D_MIX), row(g_post[i, 1]), f"mix_out_bwd_{i}", ride=scatter(keys))
        recv.update(zip(keys, landed))
        grads[("w_out", i)], landed = _wgrad([st["tok"], st["mo"]], [dmix], "cols", f"wgrad_w_out_{i}", ride=scatter(carry["w_out"]))
        recv.update(zip(carry["w_out"], landed))
        grads[("w_mem_kv", i)], dg_mem[i] = _mem_kv_bwd(dkm, dvm, mem, row(g_mem[i]), st["mem_n"], gw[("w_mem_kv", i)],
                                                        f"mem_kv_bwd_{i}")
        if i % 2 == 0:
            dx, dpool_w[j], dpool_scale[j] = _pool_bwd(dtok, st["dpre"], p["pool_w"][j], row(p["pool_scale"][j]), f"pool_bwd_{i}")
            parts = [dx, dqm]
            dh, dg_pre[i][1] = _mix_in_bwd(dh, st["h1"], row(g_pre[i, 1]), parts, gw[("w_in_pool", j)].reshape(D_MODEL, D_MIX),"pool", f"mix_in_bwd_{i}")
            grads[("w_in_pool", j)], _ = _wgrad([st["u"]], parts, "rows", f"wgrad_w_in_pool_{i}")
        else:
            (dq, dk, dv), landed = _sb_bwd(st["proj"], dtok, st["o32"], f"sb_bwd_{i}", ride=scatter(carry["sb"]))
            recv.update(zip(carry["sb"], landed))
            parts = [dq, dk, dv, dqm]
            dh, dg_pre[i][1] = _mix_in_bwd(dh, st["h1"], row(g_pre[i, 1]), parts, gw[("w_in_sb", j)].reshape(D_SB, D_MODEL), "sb",
                                           f"mix_in_bwd_{i}")
            grads[("w_in_sb", j)], _ = _wgrad(parts, [st["u"]], "rows", f"wgrad_w_in_sb_{i}")
        dh = ffn_backward(dh, st, i, 1, "h0", 0, carry["ffn1"], i == 0)
    grad_x = dh

    shard8 = lambda rows_: jnp.transpose(jnp.stack([jnp.concatenate(r, axis=0) for r in rows_]).reshape(DEPTH, 3, N_DEV, -1),
                                         (2, 0, 1, 3))
    tail = ffn(1, 0)[2:]
    landed = _comm_call(
        scatter(tail) + [_Xfer(shard8(dg_pre), scatter=True), _Xfer(shard8(dg_post), scatter=True),
                         _Xfer(jnp.concatenate(dg_mem, axis=0)), _Xfer(jnp.stack(dpool_w)),
                         _Xfer(jnp.concatenate(dpool_scale, axis=0)), _Xfer(loss_part)], "exchange_last")
    recv.update(zip(tail, landed))
    small = dict(zip(["g_pre", "g_post", "g_mem", "pool_w", "pool_scale"], landed[len(tail):]))

    def update(name, slots, w, m, v):
        into = None
        for l, r in enumerate(slots):
            into = _adamw(r, w, m, v, l, into, f"adamw_{name}_{l}")
        return into

    out = {}
    for n in STACKED:
        w, m, v = (wsb_t, tsb(opt_m[n]), tsb(opt_v[n])) if n == "w_in_sb" else (p[n], opt_m[n], opt_v[n])
        res = update(n, [recv[(n, l)] for l in range(w.shape[0])], w, m, v)
        out[n] = [tsb(a) for a in res] if n == "w_in_sb" else res
    one = lambda a: a.reshape(1, -1, a.shape[-1])
    for n, r in small.items():
        res = update(n, [r.reshape((N_DEV,) + one(p[n]).shape[1:])], one(p[n]), one(opt_m[n]), one(opt_v[n]))
        out[n] = [a.reshape(p[n].shape) for a in res]
    loss = jnp.sum(landed[-1][:, 0, 0])
    return loss, grad_x, out


STACKED = ["ffn1_gate", "ffn1_up", "ffn1_down", "ffn2_gate", "ffn2_up", "ffn2_down", "w_in_pool", "w_in_sb", "w_mem_kv", "w_out"]
WEIGHTS = ["g_pre", "g_post", "g_mem", "ffn1_gate", "ffn1_up", "ffn1_down", "ffn2_gate", "ffn2_up", "ffn2_down",
           "w_in_pool", "pool_w", "pool_scale", "w_in_sb", "w_mem_kv", "w_out"]


def kernel(x, mem, g_pre, g_post, g_mem, ffn1_gate, ffn1_up, ffn1_down, ffn2_gate, ffn2_up, ffn2_down, w_in_pool, pool_w, pool_scale, w_in_sb, w_mem_kv, w_out, loss_target, m_g_pre, m_g_post, m_g_mem, m_ffn1_gate, m_ffn1_up, m_ffn1_down, m_ffn2_gate, m_ffn2_up, m_ffn2_down, m_w_in_pool, m_pool_w, m_pool_scale, m_w_in_sb, m_w_mem_kv, m_w_out, v_g_pre, v_g_post, v_g_mem, v_ffn1_gate, v_ffn1_up, v_ffn1_down, v_ffn2_gate, v_ffn2_up, v_ffn2_down, v_w_in_pool, v_pool_w, v_pool_scale, v_w_in_sb, v_w_mem_kv, v_w_out):
    given = dict(locals())
    p = {n: given[n] for n in WEIGHTS}
    opt_m = {n: given["m_" + n] for n in WEIGHTS}
    opt_v = {n: given["v_" + n] for n in WEIGHTS}
    loss, grad_x, out = _step(p, opt_m, opt_v, x[0], mem[0], loss_target[0])
    res = [loss, grad_x[None]]
    for q in range(4):
        res += [out[n][q] for n in WEIGHTS]
    return tuple(res)
```

```python
import functools
from typing import NamedTuple, Optional

import jax
import jax.numpy as jnp
from jax import lax
from jax.experimental import pallas as pl
from jax.experimental.pallas import tpu as pltpu

F32 = jnp.float32
BF16 = jnp.bfloat16
GRAD_WIRE = jnp.bfloat16

N_DEV = 8
DEPTH = 4
D_MODEL = 1024
D_FF = 2048
D_TOK = 512
D_MEMH = 256
D_MIX = D_TOK + D_MEMH
D_SB = 3 * D_TOK + D_MEMH
HEAD_DIM = 64
Q_BLOCK = 128
POOL_WINDOWS = (2, 4, 8, 16)
POOL_GROUP = 128
POOL_HALO = 16
EPS = 1e-6
ATT_SCALE = HEAD_DIM ** -0.5
SB_DEAD_LOG_WEIGHT = -110.0
SB_FWD_LANES = 512
SB_BWD_LANES = 256

ADAM_LR = 0.001
ADAM_B1 = 0.9
ADAM_B2 = 0.999
ADAM_EPS = 1e-08
ADAM_WD = 0.01
ADAM_STEP = 10

VMEM_LIMIT_BYTES = 56 * 1024 * 1024
ROW_TILE = 512
FFN_FWD_TILE = 512
FFN_BWD_TILE = 512
FFN_CHUNK = 256
WGRAD_TILE = 512
ADAM_TILE_ELEMS = 128 * 1024

MESH = pl.DeviceIdType.MESH
ANY = pl.BlockSpec(memory_space=pl.ANY)


def _tile(n, pref):
    t = 1 << (pref.bit_length() - 1)
    while n % t:
        t //= 2
    return t


def _dot(a, b):
    return jnp.dot(a, b, preferred_element_type=F32)


def _dot_nt(a, b):
    return lax.dot_general(a, b, (((1,), (1,)), ((), ())), preferred_element_type=F32)


def _dot_tn(a, b):
    return lax.dot_general(a, b, (((0,), (0,)), ((), ())), preferred_element_type=F32)


def _split_dot(x, m, terms):
    out = None
    rest = x
    for _ in range(terms):
        part = rest.astype(BF16)
        rest = rest - part.astype(F32)
        d = _dot(part, m)
        out = d if out is None else out + d
    return out


def _rms(x, g):
    r = lax.rsqrt(jnp.mean(x * x, axis=-1, keepdims=True) + EPS)
    return x * r * g


def _rms_bwd(x, g, dy):
    r = lax.rsqrt(jnp.mean(x * x, axis=-1, keepdims=True) + EPS)
    xh = x * r
    gdy = g * dy
    dx = r * (gdy - xh * jnp.mean(gdy * xh, axis=-1, keepdims=True))
    return dx, jnp.sum(dy * xh, axis=0, keepdims=True)


def _acc_rows(ref, val, first):
    @pl.when(first)
    def _():
        ref[...] = val

    @pl.when(jnp.logical_not(first))
    def _():
        ref[...] += val


def _params(sem=None):
    return pltpu.CompilerParams(dimension_semantics=sem, vmem_limit_bytes=VMEM_LIMIT_BYTES)


def _sds(shape, dtype):
    return jax.ShapeDtypeStruct(shape, dtype)


def _rows(tm, width, col=0):
    return pl.BlockSpec((tm, width), lambda i: (i, col))


def _full(shape):
    nd = len(shape)
    return pl.BlockSpec(shape, lambda *_: (0,) * nd)


def _resident(shape):
    nd = len(shape)
    return pl.BlockSpec(shape, lambda *_: (0,) * nd, pipeline_mode=pl.Buffered(1))


def _peers():
    x, y, c = lax.axis_index("x"), lax.axis_index("y"), lax.axis_index("c")
    peers = []
    for k in range(1, N_DEV):
        px = 1 - x if k & 4 else x
        py = 1 - y if k & 2 else y
        pc = 1 - c if k & 1 else c
        peers.append(((px, py, pc), 4 * px + 2 * py + pc))
    return 4 * x + 2 * y + c, peers


class _Xfer(NamedTuple):
    src: jax.Array
    layer: Optional[int] = None
    scatter: bool = False

    @property
    def landing(self):
        block = self.src.shape if self.layer is None and not self.scatter else self.src.shape[1:]
        return _sds((N_DEV,) + block, self.src.dtype)


def _comm_copies(xfers, src_refs, dst_refs, send_sems, recv_sems, local_sems):
    me, peers = _peers()

    def src(t, to):
        ref = src_refs[t] if xfers[t].layer is None else src_refs[t].at[xfers[t].layer]
        return ref.at[to] if xfers[t].scatter else ref

    copies = [pltpu.make_async_copy(src(t, me), dst_refs[t].at[me], local_sems.at[t]) for t in range(len(xfers))]
    for k, (dev, idx) in enumerate(peers):
        for t in range(len(xfers)):
            copies.append(pltpu.make_async_remote_copy(
                src_ref=src(t, idx), dst_ref=dst_refs[t].at[me], send_sem=send_sems.at[t, k], recv_sem=recv_sems.at[t, k],
                device_id=dev, device_id_type=MESH))
    return copies


def _pcall(body, *, name, grid, in_specs, out_specs, out_shape, args, scratch=(), ride=()):
    n_in, n_out, n_scr, nx = len(in_specs), len(out_specs), len(scratch), len(ride)
    params = _params(("arbitrary",) * len(grid))
    if not ride:
        res = pl.pallas_call(body, name=name, grid=grid, in_specs=list(in_specs), out_specs=list(out_specs),
                             out_shape=list(out_shape), scratch_shapes=list(scratch), compiler_params=params,
                             interpret=False)(*args)
        return list(res), []
    def riding(*refs):
        o0 = n_in + nx
        s0 = o0 + n_out + nx
        comm = (ride, refs[n_in:o0], refs[o0 + n_out:s0], *refs[s0 + n_scr:])
        ids = [pl.program_id(a) for a in range(len(grid))]
        first = functools.reduce(jnp.logical_and, [i == 0 for i in ids])
        last = functools.reduce(jnp.logical_and, [i == n - 1 for i, n in zip(ids, grid)])

        @pl.when(first)
        def _():
            for cp in _comm_copies(*comm):
                cp.start()

        body(*refs[:n_in], *refs[o0:o0 + n_out], *refs[s0:s0 + n_scr])

        @pl.when(last)
        def _():
            for cp in _comm_copies(*comm):
                cp.wait()

    sems = [pltpu.SemaphoreType.DMA((nx, N_DEV - 1)), pltpu.SemaphoreType.DMA((nx, N_DEV - 1)), pltpu.SemaphoreType.DMA((nx,))]
    res = pl.pallas_call(riding, name=name, grid=grid, in_specs=list(in_specs) + [ANY] * nx,
                         out_specs=list(out_specs) + [ANY] * nx, out_shape=list(out_shape) + [t.landing for t in ride],
                         scratch_shapes=list(scratch) + sems, compiler_params=params,
                         interpret=False)(*args, *[t.src for t in ride])
    return list(res[:n_out]), list(res[n_out:])


def _comm_call(xfers, name):
    return _pcall(lambda: None, name=name, grid=(1,), in_specs=[], out_specs=[], out_shape=[], args=[], ride=xfers)[1]


def _gather_two_level(xfers, name):
    n = len(xfers)
    assert not any(t.scatter for t in xfers)

    def body(*refs):
        ins, outs = refs[:n], refs[n:2 * n]
        send_sems, recv_sems, local_sems = refs[2 * n:]
        x, y, c = lax.axis_index("x"), lax.axis_index("y"), lax.axis_index("c")
        sibling = (x, y, 1 - c)
        chips = [(1 - x, y), (x, 1 - y), (1 - x, 1 - y)]
        slot = lambda dev: 4 * dev[0] + 2 * dev[1] + dev[2]

        def copy(t, k, block, to, own=False):
            src = ins[t] if xfers[t].layer is None else ins[t].at[xfers[t].layer]
            return pltpu.make_async_remote_copy(
                src_ref=src if own else outs[t].at[slot(block)], dst_ref=outs[t].at[slot(block)],
                send_sem=send_sems.at[t, k], recv_sem=recv_sems.at[t, k], device_id=to, device_id_type=MESH)

        me = (x, y, c)
        local, sent = [], []
        for t in range(n):
            src = ins[t] if xfers[t].layer is None else ins[t].at[xfers[t].layer]
            mine = pltpu.make_async_copy(src, outs[t].at[slot(me)], local_sems.at[t])
            mine.start()
            local.append(mine)
            first = [copy(t, 0, me, sibling, own=True)] + [copy(t, 1 + j, me, (*chip, c), own=True) for j, chip in enumerate(chips)]
            for cp in first:
                cp.start()
            sent += first
        for t in range(n):
            for j, chip in enumerate(chips):
                copy(t, 1 + j, (*chip, c), me).wait_recv()
                passed = copy(t, 4 + j, (*chip, c), sibling)
                passed.start()
                sent.append(passed)
        for t in range(n):
            copy(t, 0, sibling, me).wait_recv()
            for j, chip in enumerate(chips):
                copy(t, 4 + j, (*chip, 1 - c), me).wait_recv()
        for cp in sent:
            cp.wait_send()
        for cp in local:
            cp.wait()

    return pl.pallas_call(
        body, name=name, in_specs=[ANY] * n, out_specs=[ANY] * n, out_shape=[t.landing for t in xfers],
        scratch_shapes=[pltpu.SemaphoreType.DMA((n, N_DEV - 1)), pltpu.SemaphoreType.DMA((n, N_DEV - 1)),
                        pltpu.SemaphoreType.DMA((n,))],
        interpret=False,
    )(*[t.src for t in xfers])


def _ffn_fwd(h, gpre, gpost, wg, wu, wd, name, ride=()):
    S = h.shape[0]
    tm = _tile(S, FFN_FWD_TILE)
    nb = FFN_CHUNK

    def body(h_ref, gpre_ref, gpost_ref, wgt_ref, wut_ref, wd_ref, hn_ref, n_ref, gate_ref, up_ref, act_ref, f_ref):
        hv = h_ref[...]
        n = _rms(hv, gpre_ref[...]).astype(BF16)
        n_ref[...] = n
        for c in range(D_FF // nb):
            cols = slice(c * nb, (c + 1) * nb)
            g = _dot_nt(n, wgt_ref[cols, :])
            u = _dot_nt(n, wut_ref[cols, :])
            gate_ref[:, cols] = g.astype(BF16)
            up_ref[:, cols] = u.astype(BF16)
            act_ref[:, cols] = (g * jax.nn.sigmoid(g) * u).astype(BF16)
        f = _dot(act_ref[...], wd_ref[...])
        f_ref[...] = f
        hn_ref[...] = hv + 0.5 * _rms(f, gpost_ref[...])

    return _pcall(
        body, name=name, grid=(S // tm,),
        in_specs=[_rows(tm, D_MODEL), _full((1, D_MODEL)), _full((1, D_MODEL)),
                  _resident((D_FF, D_MODEL)), _resident((D_FF, D_MODEL)), _resident((D_FF, D_MODEL))],
        out_specs=[_rows(tm, D_MODEL), _rows(tm, D_MODEL), _rows(tm, D_FF), _rows(tm, D_FF), _rows(tm, D_FF),
                   _rows(tm, D_MODEL)],
        out_shape=[_sds((S, D_MODEL), F32), _sds((S, D_MODEL), BF16), _sds((S, D_FF), BF16), _sds((S, D_FF), BF16),
                   _sds((S, D_FF), BF16), _sds((S, D_MODEL), F32)],
        args=(h, gpre, gpost, wg, wu, wd), ride=ride)


def _ffn_bwd(dho, h, f, gate, up, gpre, gpost, wg, wu, wd, name, ride=()):
    S = h.shape[0]
    tm = _tile(S, FFN_BWD_TILE)
    nb = FFN_CHUNK

    def body(dho_ref, h_ref, f_ref, gate_ref, up_ref, gpre_ref, gpost_ref, wgt_ref, wut_ref, wd_ref,
             dh_ref, df_ref, dgate_ref, dup_ref, dgpre_ref, dgpost_ref):
        first = pl.program_id(0) == 0
        dho_v = dho_ref[...]
        dfx, dgpost = _rms_bwd(f_ref[...], gpost_ref[...], 0.5 * dho_v)
        dfb = dfx.astype(BF16)
        df_ref[...] = dfb
        for c in range(D_FF // nb):
            cols = slice(c * nb, (c + 1) * nb)
            dact = _dot_nt(dfb, wd_ref[cols, :])
            g = gate_ref[:, cols].astype(F32)
            u = up_ref[:, cols].astype(F32)
            s = jax.nn.sigmoid(g)
            dgate_ref[:, cols] = (dact * u * (s * (1.0 + g * (1.0 - s)))).astype(BF16)
            dup_ref[:, cols] = (dact * (g * s)).astype(BF16)
        dn = _dot(dgate_ref[...], wgt_ref[...]) + _dot(dup_ref[...], wut_ref[...])
        dhx, dgpre = _rms_bwd(h_ref[...], gpre_ref[...], dn)
        dh_ref[...] = dho_v + dhx
        _acc_rows(dgpre_ref, dgpre, first)
        _acc_rows(dgpost_ref, dgpost, first)

    return _pcall(
        body, name=name, grid=(S // tm,),
        in_specs=[_rows(tm, D_MODEL), _rows(tm, D_MODEL), _rows(tm, D_MODEL), _rows(tm, D_FF), _rows(tm, D_FF),
                  _full((1, D_MODEL)), _full((1, D_MODEL)),
                  _resident((D_FF, D_MODEL)), _resident((D_FF, D_MODEL)), _resident((D_FF, D_MODEL))],
        out_specs=[_rows(tm, D_MODEL), _rows(tm, D_MODEL), _rows(tm, D_FF), _rows(tm, D_FF),
                   _full((1, D_MODEL)), _full((1, D_MODEL))],
        out_shape=[_sds((S, D_MODEL), F32), _sds((S, D_MODEL), BF16), _sds((S, D_FF), BF16), _sds((S, D_FF), BF16),
                   _sds((1, D_MODEL), F32), _sds((1, D_MODEL), F32)],
        args=(dho, h, f, gate, up, gpre, gpost, wg, wu, wd), ride=ride)


def _wgrad(a_parts, b_parts, split, name, ride=()):
    S = a_parts[0].shape[0]
    bk = _tile(S, WGRAD_TILE)
    ms = [a.shape[1] for a in a_parts]
    ns = [b.shape[1] for b in b_parts]
    M, N = sum(ms), sum(ns)
    na, nbp = len(a_parts), len(b_parts)
    blk = (M // N_DEV, N) if split == "rows" else (M, N // N_DEV)
    steps = S // bk

    def body(*refs):
        a_refs, b_refs = refs[:na], refs[na:na + nbp]
        out_ref, acc_ref = refs[-2], refs[-1]
        k = pl.program_id(0)

        @pl.when(k == 0)
        def _():
            acc_ref[...] = jnp.zeros_like(acc_ref)

        r0 = 0
        for ai in range(na):
            av = a_refs[ai][...]
            c0 = 0
            for bi in range(nbp):
                acc_ref[r0:r0 + ms[ai], c0:c0 + ns[bi]] += _dot_tn(av, b_refs[bi][...])
                c0 += ns[bi]
            r0 += ms[ai]

        @pl.when(k == steps - 1)
        def _():
            for d in range(N_DEV):
                if split == "rows":
                    out_ref[d] = acc_ref[d * blk[0]:(d + 1) * blk[0], :].astype(GRAD_WIRE)
                else:
                    out_ref[d] = acc_ref[:, d * blk[1]:(d + 1) * blk[1]].astype(GRAD_WIRE)

    in_specs = [pl.BlockSpec((bk, m), lambda k: (k, 0)) for m in ms] + [pl.BlockSpec((bk, n), lambda k: (k, 0)) for n in ns]
    (out,), landed = _pcall(
        body, name=name, grid=(steps,), in_specs=in_specs, out_specs=[_full((N_DEV,) + blk)],
        out_shape=[_sds((N_DEV,) + blk, GRAD_WIRE)], scratch=[pltpu.VMEM((M, N), F32)],
        args=list(a_parts) + list(b_parts), ride=ride)
    return out, landed


def _mix_in_pool(h, g1, w_in, pool_w, pool_scale, name, ride=()):
    S = h.shape[0]
    tm = _tile(S, ROW_TILE)
    kb = D_MODEL // N_DEV

    def body(h_ref, g_ref, w_ref, pw_ref, ps_ref, u_ref, dpre_ref, tok_ref, qm_ref, ext_ref):
        i = pl.program_id(0)
        u = _rms(h_ref[...], g_ref[...]).astype(BF16)
        u_ref[...] = u
        proj = _dot(u, w_ref[...])
        qm_ref[...] = proj[:, D_TOK:].astype(BF16)
        x = proj[:, :D_TOK]

        @pl.when(i == 0)
        def _():
            ext_ref[0:POOL_HALO, :] = jnp.zeros((POOL_HALO, D_TOK), F32)

        ext_ref[POOL_HALO:, :] = x
        pos = i * tm + lax.broadcasted_iota(jnp.int32, (tm, 1), 0)
        for gi, w in enumerate(POOL_WINDOWS):
            cols = slice(gi * POOL_GROUP, (gi + 1) * POOL_GROUP)
            xs = x[:, cols]
            wsum = xs
            for k in range(1, w):
                wsum = wsum + ext_ref[POOL_HALO - k:POOL_HALO - k + tm, cols]
            cnt = jnp.minimum(pos + 1, w).astype(F32)
            dg = (wsum / cnt - xs).astype(BF16)
            dpre_ref[:, cols] = dg
            yv = _dot(dg, pw_ref[gi].astype(BF16))
            tok_ref[:, cols] = (yv * ps_ref[:, cols]).astype(BF16)
        ext_ref[0:POOL_HALO, :] = x[tm - POOL_HALO:, :]

    return _pcall(
        body, name=name, grid=(S // tm,),
        in_specs=[_rows(tm, D_MODEL), _full((1, D_MODEL)), _resident((D_MODEL, D_MIX)),
                  _full((len(POOL_WINDOWS), POOL_GROUP, POOL_GROUP)), _full((1, D_TOK))],
        out_specs=[_rows(tm, D_MODEL), _rows(tm, D_TOK), _rows(tm, D_TOK), _rows(tm, D_MEMH)],
        out_shape=[_sds((S, D_MODEL), BF16), _sds((S, D_TOK), BF16), _sds((S, D_TOK), BF16), _sds((S, D_MEMH), BF16)],
        scratch=[pltpu.VMEM((POOL_HALO + tm, D_TOK), F32)],
        args=(h, g1, w_in, pool_w, pool_scale), ride=ride)


def _pool_bwd(dtok, dpre, pool_w, pool_scale, name):
    S = dtok.shape[0]
    tm = _tile(S, ROW_TILE)
    nt = S // tm
    ng = len(POOL_WINDOWS)

    def body(dtok_ref, dpre_ref, pw_ref, ps_ref, dx_ref, dpw_ref, dps_ref, ext_ref):
        i = pl.program_id(0)
        first = i == 0
        t0 = (nt - 1 - i) * tm
        pos = t0 + lax.broadcasted_iota(jnp.int32, (tm, 1), 0)

        @pl.when(first)
        def _():
            ext_ref[tm:, :] = jnp.zeros((POOL_HALO, D_TOK), F32)

        dps = []
        for gi, w in enumerate(POOL_WINDOWS):
            cols = slice(gi * POOL_GROUP, (gi + 1) * POOL_GROUP)
            dg = dpre_ref[:, cols]
            pw = pw_ref[gi].astype(BF16)
            dt = dtok_ref[:, cols].astype(F32)
            yv = _dot(dg, pw)
            dps.append(jnp.sum(dt * yv, axis=0, keepdims=True))
            dy = (dt * ps_ref[:, cols]).astype(BF16)
            _acc_rows(dpw_ref.at[gi], _dot_tn(dg, dy), first)
            dd = _dot_nt(dy, pw)
            cnt = jnp.minimum(pos + 1, w).astype(F32)
            ext_ref[0:tm, cols] = dd / cnt
            wsum = ext_ref[0:tm, cols]
            for k in range(1, w):
                wsum = wsum + ext_ref[k:k + tm, cols]
            dx_ref[:, cols] = (wsum - dd).astype(BF16)
        _acc_rows(dps_ref, jnp.concatenate(dps, axis=1), first)
        ext_ref[tm:, :] = ext_ref[0:POOL_HALO, :]

    rev = lambda i: (nt - 1 - i, 0)
    return pl.pallas_call(
        body, name=name, grid=(nt,),
        in_specs=[pl.BlockSpec((tm, D_TOK), rev), pl.BlockSpec((tm, D_TOK), rev),
                  _full((ng, POOL_GROUP, POOL_GROUP)), _full((1, D_TOK))],
        out_specs=[pl.BlockSpec((tm, D_TOK), rev), _full((ng, POOL_GROUP, POOL_GROUP)), _full((1, D_TOK))],
        out_shape=[_sds((S, D_TOK), BF16), _sds((ng, POOL_GROUP, POOL_GROUP), F32), _sds((1, D_TOK), F32)],
        scratch_shapes=[pltpu.VMEM((tm + POOL_HALO, D_TOK), F32)],
        compiler_params=_params(("arbitrary",)),
        interpret=False,
    )(dtok, dpre, pool_w, pool_scale)


def _mix_in_sb(h, g1, wt, name):
    S = h.shape[0]
    tm = _tile(S, ROW_TILE)
    cb = 256

    def body(h_ref, g_ref, wt_ref, u_ref, proj_ref):
        u = _rms(h_ref[...], g_ref[...]).astype(BF16)
        u_ref[...] = u
        for c in range(D_SB // cb):
            proj_ref[:, c * cb:(c + 1) * cb] = _dot_nt(u, wt_ref[c * cb:(c + 1) * cb, :]).astype(BF16)

    return pl.pallas_call(
        body, name=name, grid=(S // tm,),
        in_specs=[_rows(tm, D_MODEL), _full((1, D_MODEL)), _resident((D_SB, D_MODEL))],
        out_specs=[_rows(tm, D_MODEL), _rows(tm, D_SB)],
        out_shape=[_sds((S, D_MODEL), BF16), _sds((S, D_SB), BF16)],
        compiler_params=_params(("arbitrary",)),
        interpret=False,
    )(h, g1, wt)


def _mix_in_bwd(dho, h, g1, parts, w, mode, name):
    S = h.shape[0]
    tm = _tile(S, ROW_TILE)
    widths = [p.shape[1] for p in parts]
    npart = len(parts)
    kb = D_MODEL // N_DEV

    def body(*refs):
        dho_ref, h_ref, g_ref = refs[:3]
        p_refs = refs[3:3 + npart]
        w_ref, dh_ref, dg_ref = refs[3 + npart:]
        first = pl.program_id(0) == 0
        dproj = jnp.concatenate([p[...] for p in p_refs], axis=1)
        du = _dot_nt(dproj, w_ref[...]) if mode == "pool" else _dot(dproj, w_ref[...])
        dhx, dg = _rms_bwd(h_ref[...], g_ref[...], du)
        dh_ref[...] = dho_ref[...] + dhx
        _acc_rows(dg_ref, dg, first)

    w_spec = _resident((D_MODEL, D_MIX)) if mode == "pool" else _resident((D_SB, D_MODEL))
    return pl.pallas_call(
        body, name=name, grid=(S // tm,),
        in_specs=[_rows(tm, D_MODEL), _rows(tm, D_MODEL), _full((1, D_MODEL))] + [_rows(tm, wd_) for wd_ in widths] + [w_spec],
        out_specs=[_rows(tm, D_MODEL), _full((1, D_MODEL))],
        out_shape=[_sds((S, D_MODEL), F32), _sds((1, D_MODEL), F32)],
        compiler_params=_params(("arbitrary",)),
        interpret=False,
    )(dho, h, g1, *parts, w)


def _sb_block(qcats, kb, mask, later, tri_later):
    z = jnp.concatenate([_dot_nt(qc, _group(kb, g)) for g, qc in enumerate(qcats)], axis=0) * ATT_SCALE
    en = jnp.exp(-jnp.abs(z))
    ls = jnp.minimum(z, 0.0) - jnp.log(1.0 + en)
    lf = ls - z if mask is None else jnp.where(mask, ls - z, 0.0)
    within = _split_dot(lf, tri_later, 2)
    a = jnp.exp(ls + within + later)
    if mask is not None:
        a = jnp.where(mask, a, 0.0)
    return z, en, a, jnp.sum(lf, axis=1, keepdims=True)


def _sb_causal(rows):
    row = jnp.bitwise_and(lax.broadcasted_iota(jnp.int32, (rows, Q_BLOCK), 0), Q_BLOCK - 1)
    return lax.broadcasted_iota(jnp.int32, (rows, Q_BLOCK), 1) < row


def _sb_walk(qi, block, state, later_of, unrolled):
    if unrolled:
        state = block(qi, _sb_causal, state)
        state = block(jnp.maximum(qi - 1, 0), qi >= 1, state)

    def step(carry):
        j, _, state = carry
        mask = None if unrolled else lambda rows: jnp.logical_or(_sb_causal(rows), j > 0)
        state = block(qi - j, mask, state)
        return j + 1, _sb_alive(later_of(state)), state

    first = jnp.int32(2 if unrolled else 0)
    return lax.while_loop(functools.partial(_sb_more, qi), step, (first, _sb_alive(later_of(state)), state))[2]


def _sb_alive(later):
    return jnp.max(later) > SB_DEAD_LOG_WEIGHT


def _sb_more(qi, carry):
    return jnp.logical_and(carry[0] <= qi, carry[1])


def _tri(strict):
    row = lax.broadcasted_iota(jnp.int32, (Q_BLOCK, Q_BLOCK), 0)
    col = lax.broadcasted_iota(jnp.int32, (Q_BLOCK, Q_BLOCK), 1)
    return (row > col if strict else row >= col).astype(BF16)


HEADS_PER_GROUP = 128 // HEAD_DIM
GROUP_ROWS = HEADS_PER_GROUP * Q_BLOCK


def _sb_head_masks():
    lane = lax.broadcasted_iota(jnp.int32, (Q_BLOCK, 128), 1)
    return [(lane >= e * HEAD_DIM) & (lane < (e + 1) * HEAD_DIM) for e in range(HEADS_PER_GROUP)]


def _group(x, g):
    return x[:, g * 128:(g + 1) * 128]


def _masked(hm, x):
    return jnp.where(hm, x, jnp.zeros_like(x))


def _stack_heads(hms, x):
    return jnp.concatenate([_masked(hm, x) for hm in hms], axis=0)


def _own_lanes(hms, r, rows=Q_BLOCK):
    return sum(_masked(hm, r[e * rows:(e + 1) * rows]) for e, hm in enumerate(hms))


def _sb_fwd(proj, name, ride=()):
    S = proj.shape[0]
    nq = S // Q_BLOCK
    W = SB_FWD_LANES
    nrow = D_TOK // W
    groups = W // 128

    def body(q_ref, k_ref, v_ref, o_ref, tok_ref):
        qi = pl.program_id(1)
        hms = _sb_head_masks()
        tri_later = _tri(True)
        q = q_ref[...]
        qcats = [_stack_heads(hms, _group(q, g)) for g in range(groups)]

        def block(kblock, mask, state):
            accs, later = state
            if callable(mask):
                mask = mask(later.shape[0])
            off = pl.multiple_of(kblock * Q_BLOCK, Q_BLOCK)
            kb = k_ref[pl.ds(off, Q_BLOCK), :]
            vb = v_ref[pl.ds(off, Q_BLOCK), :]
            _, _, a, bsum = _sb_block(qcats, kb, mask, later, tri_later)
            hi = a.astype(BF16)
            lo = (a - hi.astype(F32)).astype(BF16)
            accs = list(accs)
            for g in range(groups):
                rows = slice(g * GROUP_ROWS, (g + 1) * GROUP_ROWS)
                r = _dot(jnp.concatenate([hi[rows], lo[rows]], axis=0), _group(vb, g))
                accs[g] = accs[g] + _own_lanes(hms, r[:GROUP_ROWS] + r[GROUP_ROWS:])
            return tuple(accs), later + bsum

        init = ((jnp.zeros((Q_BLOCK, 128), F32),) * groups, jnp.zeros((groups * GROUP_ROWS, 1), F32))
        accs = _sb_walk(qi, block, init, lambda state: state[1], unrolled=True)[0]
        for g, acc in enumerate(accs):
            o_ref[:, g * 128:(g + 1) * 128] = acc
            tok_ref[:, g * 128:(g + 1) * 128] = acc.astype(BF16)

    blk = pl.BlockSpec((Q_BLOCK, W), lambda p, i: (i, p))
    return _pcall(
        body, name=name, grid=(nrow, nq),
        in_specs=[blk, pl.BlockSpec((S, W), lambda p, i: (0, nrow + p), pipeline_mode=pl.Buffered(1)),
                  pl.BlockSpec((S, W), lambda p, i: (0, 2 * nrow + p), pipeline_mode=pl.Buffered(1))],
        out_specs=[blk, blk],
        out_shape=[_sds((S, D_TOK), F32), _sds((S, D_TOK), BF16)],
        args=(proj, proj, proj), ride=ride)


def _sb_bwd(proj, dtok, o32, name, ride=()):
    S = proj.shape[0]
    nq = S // Q_BLOCK
    W = SB_BWD_LANES
    nrow = D_TOK // W
    groups = W // 128

    def body(q_ref, k_ref, v_ref, do_ref, o_ref, dq_ref, dk_ref, dv_ref, dk_acc, dv_acc):
        qi = pl.program_id(1)

        @pl.when(qi == 0)
        def _():
            dk_acc[...] = jnp.zeros_like(dk_acc)
            dv_acc[...] = jnp.zeros_like(dv_acc)

        hms = _sb_head_masks()
        tri_later = _tri(True)
        tri_from = _tri(False)
        q = q_ref[...]
        do = do_ref[...]
        dov = do.astype(F32) * o_ref[...]
        qcats = [_stack_heads(hms, _group(q, g)) for g in range(groups)]
        docats = [_stack_heads(hms, _group(do, g)) for g in range(groups)]
        rowtot = jnp.concatenate([jnp.sum(jnp.where(hm, _group(dov, g), 0.0), axis=1, keepdims=True)
                                  for g in range(groups) for hm in hms], axis=0)

        def block(kblock, mask, state):
            dqs, later, seen = state
            if callable(mask):
                mask = mask(later.shape[0])
            off = pl.multiple_of(kblock * Q_BLOCK, Q_BLOCK)
            kb = k_ref[pl.ds(off, Q_BLOCK), :]
            vb = v_ref[pl.ds(off, Q_BLOCK), :]
            z, en, a, bsum = _sb_block(qcats, kb, mask, later, tri_later)
            inv = 1.0 / (1.0 + en)
            beta = jnp.where(z >= 0, 1.0, en) * inv
            omb = jnp.where(z >= 0, en, 1.0) * inv
            dlogw = a * jnp.concatenate([_dot_nt(docats[g], _group(vb, g)) for g in range(groups)], axis=0)
            prefix = rowtot - seen - _split_dot(dlogw, tri_from, 2)
            dz = dlogw * omb - beta * prefix
            if mask is not None:
                dz = jnp.where(mask, dz, 0.0)
            dz = dz.astype(BF16)
            ab = a.astype(BF16)
            dqs = list(dqs)
            for g in range(groups):
                rows = slice(g * GROUP_ROWS, (g + 1) * GROUP_ROWS)
                lanes = slice(g * 128, (g + 1) * 128)
                dqs[g] = dqs[g] + _own_lanes(hms, _dot(dz[rows], _group(kb, g)))
                dk_acc[pl.ds(off, Q_BLOCK), lanes] += _dot_tn(dz[rows], qcats[g]) * ATT_SCALE
                dv_acc[pl.ds(off, Q_BLOCK), lanes] += _dot_tn(ab[rows], docats[g])
            return tuple(dqs), later + bsum, seen + jnp.sum(dlogw, axis=1, keepdims=True)

        zero = jnp.zeros((groups * GROUP_ROWS, 1), F32)
        init = ((jnp.zeros((Q_BLOCK, 128), F32),) * groups, zero, zero)
        dqs = _sb_walk(qi, block, init, lambda state: state[1], unrolled=False)[0]
        for g, dq in enumerate(dqs):
            dq_ref[:, g * 128:(g + 1) * 128] = (dq * ATT_SCALE).astype(BF16)

        @pl.when(qi == nq - 1)
        def _():
            dk_ref[...] = dk_acc[...].astype(BF16)
            dv_ref[...] = dv_acc[...].astype(BF16)

    blk = pl.BlockSpec((Q_BLOCK, W), lambda p, i: (i, p))
    col = pl.BlockSpec((S, W), lambda p, i: (0, p))
    return _pcall(
        body, name=name, grid=(nrow, nq),
        in_specs=[blk, pl.BlockSpec((S, W), lambda p, i: (0, nrow + p)), pl.BlockSpec((S, W), lambda p, i: (0, 2 * nrow + p)),
                  blk, blk],
        out_specs=[blk, col, col],
        out_shape=[_sds((S, D_TOK), BF16), _sds((S, D_TOK), BF16), _sds((S, D_TOK), BF16)],
        scratch=[pltpu.VMEM((S, W), F32), pltpu.VMEM((S, W), F32)],
        args=(proj, proj, proj, dtok, o32), ride=ride)


def _mem_kv_fwd(mem, g_mem, w_kv, name):
    lm = mem.shape[0]
    kb = D_MODEL // N_DEV

    def body(mem_ref, g_ref, w_ref, mn_ref, km_ref, vm_ref):
        mn = _rms(mem_ref[...], g_ref[...]).astype(BF16)
        mn_ref[...] = mn
        kv = jnp.zeros((lm, 2 * D_MEMH), F32)
        for d in range(N_DEV):
            kv = kv + _dot(mn[:, d * kb:(d + 1) * kb], w_ref[d])
        km_ref[...] = kv[:, :D_MEMH].astype(BF16)
        vm_ref[...] = kv[:, D_MEMH:].astype(BF16)

    return pl.pallas_call(
        body, name=name, grid=(1,),
        in_specs=[_full((lm, D_MODEL)), _full((1, D_MODEL)), _full((N_DEV, kb, 2 * D_MEMH))],
        out_specs=[_full((lm, D_MODEL)), _full((lm, D_MEMH)), _full((lm, D_MEMH))],
        out_shape=[_sds((lm, D_MODEL), BF16), _sds((lm, D_MEMH), BF16), _sds((lm, D_MEMH), BF16)],
        compiler_params=_params(("arbitrary",)),
        interpret=False,
    )(mem, g_mem, w_kv)


def _mem_kv_bwd(dkm, dvm, mem, g_mem, mem_n, w_kv, name):
    lm = mem.shape[0]
    kb = D_MODEL // N_DEV

    def body(dkm_ref, dvm_ref, mem_ref, g_ref, mn_ref, w_ref, dw_ref, dg_ref):
        dkv = jnp.concatenate([dkm_ref[...], dvm_ref[...]], axis=1).astype(BF16)
        dw = _dot_tn(mn_ref[...], dkv)
        for d in range(N_DEV):
            dw_ref[d] = dw[d * kb:(d + 1) * kb, :].astype(GRAD_WIRE)
        dmn = jnp.concatenate([_dot_nt(dkv, w_ref[d]) for d in range(N_DEV)], axis=1)
        _, dg = _rms_bwd(mem_ref[...], g_ref[...], dmn)
        dg_ref[...] = dg

    return pl.pallas_call(
        body, name=name, grid=(1,),
        in_specs=[_full((lm, D_MEMH)), _full((lm, D_MEMH)), _full((lm, D_MODEL)), _full((1, D_MODEL)),
                  _full((lm, D_MODEL)), _full((N_DEV, kb, 2 * D_MEMH))],
        out_specs=[_full((N_DEV, kb, 2 * D_MEMH)), _full((1, D_MODEL))],
        out_shape=[_sds((N_DEV, kb, 2 * D_MEMH), GRAD_WIRE), _sds((1, D_MODEL), F32)],
        compiler_params=_params(("arbitrary",)),
        interpret=False,
    )(dkm, dvm, mem, g_mem, mem_n, w_kv)


def _mem_heads(tm):
    lane = lax.broadcasted_iota(jnp.int32, (tm, D_MEMH), 1)
    return [(lane >= e * HEAD_DIM) & (lane < (e + 1) * HEAD_DIM) for e in range(D_MEMH // HEAD_DIM)]


def _softmax(s):
    m = jnp.max(s, axis=-1, keepdims=True)
    p = jnp.exp(s - m)
    return p / jnp.sum(p, axis=-1, keepdims=True)


def _mix_out_fwd(h, tok, qm, qm_col, km, vm, w_out, gpost, name, ride=()):
    S = h.shape[0]
    tm = _tile(S, ROW_TILE)
    lm = km.shape[0]
    nb = D_MODEL // N_DEV

    def body(h_ref, tok_ref, qm_ref, km_ref, vm_ref, w_ref, g_ref, hn_ref, mo_ref, mix_ref):
        qv = qm_ref[...]
        kv, vv = km_ref[...], vm_ref[...]
        hms = _mem_heads(tm)
        p = _softmax(_dot_nt(_stack_heads(hms, qv), kv) * ATT_SCALE)
        mob = _own_lanes(hms, _dot(p.astype(BF16), vv), tm).astype(BF16)
        mo_ref[...] = mob
        mix = _dot_nt(jnp.concatenate([tok_ref[...], mob], axis=1), w_ref[...])
        mix_ref[...] = mix
        hn_ref[...] = h_ref[...] + _rms(mix, g_ref[...])

    return _pcall(
        body, name=name, grid=(S // tm,),
        in_specs=[_rows(tm, D_MODEL), _rows(tm, D_TOK), _rows(tm, D_MEMH, qm_col), _full((lm, D_MEMH)), _full((lm, D_MEMH)),
                  _resident((D_MODEL, D_MIX)), _full((1, D_MODEL))],
        out_specs=[_rows(tm, D_MODEL), _rows(tm, D_MEMH), _rows(tm, D_MODEL)],
        out_shape=[_sds((S, D_MODEL), F32), _sds((S, D_MEMH), BF16), _sds((S, D_MODEL), F32)],
        args=(h, tok, qm, km, vm, w_out, gpost), ride=ride)


def _mix_out_bwd(dho, mix, qm, qm_col, km, vm, w_out, gpost, name, ride=()):
    S = dho.shape[0]
    tm = _tile(S, ROW_TILE)
    lm = km.shape[0]
    nb = D_MODEL // N_DEV

    def body(dho_ref, mix_ref, qm_ref, km_ref, vm_ref, w_ref, g_ref,
             dmix_ref, dtok_ref, dqm_ref, dkm_ref, dvm_ref, dg_ref):
        first = pl.program_id(0) == 0
        dmx, dg = _rms_bwd(mix_ref[...], g_ref[...], dho_ref[...])
        dmb = dmx.astype(BF16)
        dmix_ref[...] = dmb
        _acc_rows(dg_ref, dg, first)
        dcat = _dot(dmb, w_ref[...])
        dtok_ref[...] = dcat[:, :D_TOK].astype(BF16)
        dmo = dcat[:, D_TOK:].astype(BF16)
        qv = qm_ref[...]
        kv, vv = km_ref[...], vm_ref[...]
        hms = _mem_heads(tm)
        qcat = _stack_heads(hms, qv)
        dmcat = _stack_heads(hms, dmo)
        p = _softmax(_dot_nt(qcat, kv) * ATT_SCALE)
        dp = _dot_nt(dmcat, vv)
        ds = (p * (dp - jnp.sum(p * dp, axis=-1, keepdims=True))).astype(BF16)
        dq = _own_lanes(hms, _dot(ds, kv), tm)
        dk = _dot_tn(ds, qcat)
        dv = _dot_tn(p.astype(BF16), dmcat)
        dqm_ref[...] = (dq * ATT_SCALE).astype(BF16)
        _acc_rows(dkm_ref, dk * ATT_SCALE, first)
        _acc_rows(dvm_ref, dv, first)

    return _pcall(
        body, name=name, grid=(S // tm,),
        in_specs=[_rows(tm, D_MODEL), _rows(tm, D_MODEL), _rows(tm, D_MEMH, qm_col), _full((lm, D_MEMH)), _full((lm, D_MEMH)),
                  _resident((D_MODEL, D_MIX)), _full((1, D_MODEL))],
        out_specs=[_rows(tm, D_MODEL), _rows(tm, D_TOK), _rows(tm, D_MEMH), _full((lm, D_MEMH)), _full((lm, D_MEMH)),
                   _full((1, D_MODEL))],
        out_shape=[_sds((S, D_MODEL), BF16), _sds((S, D_TOK), BF16), _sds((S, D_MEMH), BF16),
                   _sds((lm, D_MEMH), F32), _sds((lm, D_MEMH), F32), _sds((1, D_MODEL), F32)],
        args=(dho, mix, qm, km, vm, w_out, gpost), ride=ride)


def _loss_head(y, target, name):
    S = y.shape[0]
    tm = _tile(S, ROW_TILE)
    nt = S // tm

    def body(y_ref, t_ref, dy_ref, loss_ref, acc_ref):
        i = pl.program_id(0)
        e = y_ref[...] - t_ref[...]
        dy_ref[...] = e * (1.0 / D_MODEL)
        _acc_rows(acc_ref, jnp.sum(e * e, axis=0, keepdims=True), i == 0)

        @pl.when(i == nt - 1)
        def _():
            tot = jnp.sum(acc_ref[...], axis=1, keepdims=True) * (0.5 / D_MODEL)
            loss_ref[...] = jnp.broadcast_to(tot, (1, 128))

    return pl.pallas_call(
        body, name=name, grid=(nt,),
        in_specs=[_rows(tm, D_MODEL), _rows(tm, D_MODEL)],
        out_specs=[_rows(tm, D_MODEL), _full((1, 128))],
        out_shape=[_sds((S, D_MODEL), F32), _sds((1, 128), F32)],
        scratch_shapes=[pltpu.VMEM((1, D_MODEL), F32)],
        compiler_params=_params(("arbitrary",)),
        interpret=False,
    )(y, target)


def _adamw(recv, w, m, v, l, into, name):
    L, R, C = w.shape
    tr = R if R * C <= ADAM_TILE_ELEMS else _tile(R, ADAM_TILE_ELEMS // C)
    c1 = 1.0 - ADAM_B1 ** ADAM_STEP
    c2 = 1.0 - ADAM_B2 ** ADAM_STEP

    def body(r_ref, w_ref, m_ref, v_ref, *rest):
        g_ref, d_ref, nm_ref, nv_ref = rest[-4:]
        g = r_ref[0].astype(F32)
        for s in range(1, N_DEV):
            g = g + r_ref[s].astype(F32)
        g_ref[...] = g
        nm = ADAM_B1 * m_ref[...] + (1.0 - ADAM_B1) * g
        nv = ADAM_B2 * v_ref[...] + (1.0 - ADAM_B2) * (g * g)
        nm_ref[...] = nm
        nv_ref[...] = nv
        d_ref[...] = -ADAM_LR * ((nm / c1) / (jnp.sqrt(nv / c2) + ADAM_EPS) + ADAM_WD * w_ref[...])

    t = pl.BlockSpec((None, tr, C), lambda i: (l, i, 0))
    kept = [] if into is None else list(into)
    return pl.pallas_call(
        body, name=name, grid=(R // tr,),
        in_specs=[pl.BlockSpec((N_DEV, tr, C), lambda i: (0, i, 0)), t, t, t] + [ANY] * len(kept),
        out_specs=[t, t, t, t],
        out_shape=[_sds((L, R, C), F32)] * 4,
        input_output_aliases={4 + q: q for q in range(len(kept))},
        compiler_params=_params(("arbitrary",)),
        interpret=False,
    )(recv, w, m, v, *kept)


def _step(p, opt_m, opt_v, x, mem, target):
    bf = lambda a: a.astype(BF16)
    row = lambda a: a.reshape(1, -1)
    tsb = lambda a: jnp.swapaxes(a, 1, 2)
    g_mem = p["g_mem"]

    wsb_t = tsb(p["w_in_sb"])
    travels_transposed = ("w_in_sb", "w_out", "ffn1_gate", "ffn1_up", "ffn2_gate", "ffn2_up")
    shard = {n: bf(tsb(p[n]) if n in travels_transposed else p[n]) for n in STACKED}
    ffn_weights = lambda which, i: [gw[k].reshape(D_FF, D_MODEL) for k in ffn(which, i)]
    w_in = lambda i: "w_in_pool" if i % 2 == 0 else "w_in_sb"
    ffn = lambda which, i: [(f"ffn{which}_{s}", i) for s in ("gate", "up", "down")]
    mixing = lambda i: [(w_in(i), i // 2), ("w_mem_kv", i), ("w_out", i)]

    gw = {}

    def gather(keys):
        return [_Xfer(shard[n], l) for n, l in keys]

    first = ffn(1, 0)
    landed = _gather_two_level(gather(first) + [_Xfer(p["g_pre"]), _Xfer(p["g_post"])], "gather_first")
    gw.update(zip(first, landed))
    unshard = lambda g: jnp.transpose(g, (1, 2, 0, 3)).reshape(DEPTH, 3, D_MODEL)
    g_pre, g_post = unshard(landed[-2]), unshard(landed[-1])

    def ahead(i):
        nxt = i + 1 < DEPTH
        if i % 2 == 0:
            start = i == 0
            return {"ffn1": ffn(2, i)[:2] + (mixing(0)[::2] if start else []), "mix_in": ffn(2, i)[2:] + (mixing(0)[1:2] if start else []),
                    "sb": [], "mix_out": ffn(1, i + 1)[:1] if nxt else [], "ffn2": ffn(1, i + 1)[1:] if nxt else []}
        return {"ffn1": mixing(i), "mix_in": [], "sb": ffn(2, i) + (ffn(1, i + 1)[:2] if nxt else []),
                "mix_out": [], "ffn2": ffn(1, i + 1)[2:] + mixing(i + 1) if nxt else []}

    def gathering(keys, call):
        res, landed = call(ride=gather(keys))
        gw.update(zip(keys, landed))
        return res

    saved = []
    h = x
    for i in range(DEPTH):
        j = i // 2
        st = {"h0": h}
        carry = ahead(i)
        h, st["n1"], st["gate1"], st["up1"], st["act1"], st["f1"] = gathering(carry["ffn1"], functools.partial(
            _ffn_fwd, h, row(g_pre[i, 0]), row(g_post[i, 0]), *ffn_weights(1, i), f"ffn1_fwd_{i}"))
        st["h1"] = h
        if i % 2 == 0:
            st["u"], st["dpre"], st["tok"], st["qm"] = gathering(carry["mix_in"], functools.partial(
                _mix_in_pool, h, row(g_pre[i, 1]), gw[("w_in_pool", j)].reshape(D_MODEL, D_MIX),p["pool_w"][j], row(p["pool_scale"][j]),
                f"mix_in_pool_{i}"))
            qm, qm_col = st["qm"], 0
        else:
            st["u"], st["proj"] = _mix_in_sb(h, row(g_pre[i, 1]), gw[("w_in_sb", j)].reshape(D_SB, D_MODEL), f"mix_in_sb_{i}")
            st["o32"], st["tok"] = gathering(carry["sb"], functools.partial(_sb_fwd, st["proj"], f"sb_fwd_{i}"))
            qm, qm_col = st["proj"], 3 * D_TOK // D_MEMH
        st["mem_n"], st["km"], st["vm"] = _mem_kv_fwd(mem, row(g_mem[i]), gw[("w_mem_kv", i)], f"mem_kv_fwd_{i}")
        h, st["mo"], st["mix"] = gathering(carry["mix_out"], functools.partial(
            _mix_out_fwd, h, st["tok"], qm, qm_col, st["km"], st["vm"], gw[("w_out", i)].reshape(D_MODEL, D_MIX), row(g_post[i, 1]), f"mix_out_fwd_{i}"))
        st["h2"] = h
        h, st["n2"], st["gate2"], st["up2"], st["act2"], st["f2"] = gathering(carry["ffn2"], functools.partial(
            _ffn_fwd, h, row(g_pre[i, 2]), row(g_post[i, 2]), *ffn_weights(2, i), f"ffn2_fwd_{i}"))
        saved.append(st)

    dh, loss_part = _loss_head(h, target, "loss_head")

    grads = {}
    recv = {}
    dg_pre = [[None] * 3 for _ in range(DEPTH)]
    dg_post = [[None] * 3 for _ in range(DEPTH)]
    dg_mem = [None] * DEPTH
    dpool_w = [None, None]
    dpool_scale = [None, None]

    def scatter(keys):
        return [_Xfer(grads[k], scatter=True) for k in keys]

    def ffn_backward(dh, st, i, which, hkey, slot, riding, last):
        sfx = str(which)
        keys = ffn(which, i)
        (dh, df, dgate, dup, dg_pre[i][slot], dg_post[i][slot]), landed = _ffn_bwd(
            dh, st[hkey], st["f" + sfx], st["gate" + sfx], st["up" + sfx], row(g_pre[i, slot]), row(g_post[i, slot]),
            *ffn_weights(which, i), f"ffn{sfx}_bwd_{i}", ride=scatter(riding))
        recv.update(zip(riding, landed))
        riders = [mixing(i), keys[:1], keys[1:2]] if last else [[], [], []]
        operands = ((st["n" + sfx], dgate, "cols"), (st["n" + sfx], dup, "cols"), (st["act" + sfx], df, "rows"))
        for key, (a, b, split), riding in zip(keys, operands, riders):
            grads[key], landed = _wgrad([a], [b], split, f"wgrad_{key[0]}_{i}", ride=scatter(riding))
            recv.update(zip(riding, landed))
        return dh

    def behind(i):
        prev = ffn(1, i + 1) if i + 1 < DEPTH else []
        if i % 2 == 1:
            return {"ffn2": prev, "mix_out": [], "w_out": [], "sb": ffn(2, i), "ffn1": mixing(i)}
        if i > 0:
            return {"ffn2": prev, "mix_out": ffn(2, i)[:2], "w_out": [], "sb": [], "ffn1": ffn(2, i)[2:] + mixing(i)}
        return {"ffn2": prev, "mix_out": ffn(2, 0)[:2], "w_out": ffn(2, 0)[2:], "sb": [], "ffn1": []}

    for i in reversed(range(DEPTH)):
        j = i // 2
        st = saved[i]
        carry = behind(i)
        dh = ffn_backward(dh, st, i, 2, "h2", 2, carry["ffn2"], False)
        if i % 2 == 0:
            qm, qm_col = st["qm"], 0
        else:
            qm, qm_col = st["proj"], 3 * D_TOK // D_MEMH
        keys = carry["mix_out"]
        (dmix, dtok, dqm, dkm, dvm, dg_post[i][1]), landed = _mix_out_bwd(
            dh, st["mix"], qm, qm_col, st["km"], st["vm"], gw[("w_out", i)].reshape(D_MODEL, D_MIX), row(g_post[i, 1]), f"mix_out_bwd_{i}", ride=scatter(keys))
        recv.update(zip(keys, landed))
        grads[("w_out", i)], landed = _wgrad([st["tok"], st["mo"]], [dmix], "cols", f"wgrad_w_out_{i}", ride=scatter(carry["w_out"]))
        recv.update(zip(carry["w_out"], landed))
        grads[("w_mem_kv", i)], dg_mem[i] = _mem_kv_bwd(dkm, dvm, mem, row(g_mem[i]), st["mem_n"], gw[("w_mem_kv", i)],
                                                        f"mem_kv_bwd_{i}")
        if i % 2 == 0:
            dx, dpool_w[j], dpool_scale[j] = _pool_bwd(dtok, st["dpre"], p["pool_w"][j], row(p["pool_scale"][j]), f"pool_bwd_{i}")
            parts = [dx, dqm]
            dh, dg_pre[i][1] = _mix_in_bwd(dh, st["h1"], row(g_pre[i, 1]), parts, gw[("w_in_pool", j)].reshape(D_MODEL, D_MIX),"pool", f"mix_in_bwd_{i}")
            grads[("w_in_pool", j)], _ = _wgrad([st["u"]], parts, "rows", f"wgrad_w_in_pool_{i}")
        else:
            (dq, dk, dv), landed = _sb_bwd(st["proj"], dtok, st["o32"], f"sb_bwd_{i}", ride=scatter(carry["sb"]))
            recv.update(zip(carry["sb"], landed))
            parts = [dq, dk, dv, dqm]
            dh, dg_pre[i][1] = _mix_in_bwd(dh, st["h1"], row(g_pre[i, 1]), parts, gw[("w_in_sb", j)].reshape(D_SB, D_MODEL), "sb",
                                           f"mix_in_bwd_{i}")
            grads[("w_in_sb", j)], _ = _wgrad(parts, [st["u"]], "rows", f"wgrad_w_in_sb_{i}")
        dh = ffn_backward(dh, st, i, 1, "h0", 0, carry["ffn1"], i == 0)
    grad_x = dh

    shard8 = lambda rows_: jnp.transpose(jnp.stack([jnp.concatenate(r, axis=0) for r in rows_]).reshape(DEPTH, 3, N_DEV, -1),
                                         (2, 0, 1, 3))
    tail = ffn(1, 0)[2:]
    landed = _comm_call(
        scatter(tail) + [_Xfer(shard8(dg_pre), scatter=True), _Xfer(shard8(dg_post), scatter=True),
                         _Xfer(jnp.concatenate(dg_mem, axis=0)), _Xfer(jnp.stack(dpool_w)),
                         _Xfer(jnp.concatenate(dpool_scale, axis=0)), _Xfer(loss_part)], "exchange_last")
    recv.update(zip(tail, landed))
    small = dict(zip(["g_pre", "g_post", "g_mem", "pool_w", "pool_scale"], landed[len(tail):]))

    def update(name, slots, w, m, v):
        into = None
        for l, r in enumerate(slots):
            into = _adamw(r, w, m, v, l, into, f"adamw_{name}_{l}")
        return into

    out = {}
    for n in STACKED:
        w, m, v = (wsb_t, tsb(opt_m[n]), tsb(opt_v[n])) if n == "w_in_sb" else (p[n], opt_m[n], opt_v[n])
        res = update(n, [recv[(n, l)] for l in range(w.shape[0])], w, m, v)
        out[n] = [tsb(a) for a in res] if n == "w_in_sb" else res
    one = lambda a: a.reshape(1, -1, a.shape[-1])
    for n, r in small.items():
        res = update(n, [r.reshape((N_DEV,) + one(p[n]).shape[1:])], one(p[n]), one(opt_m[n]), one(opt_v[n]))
        out[n] = [a.reshape(p[n].shape) for a in res]
    loss = jnp.sum(landed[-1][:, 0, 0])
    return loss, grad_x, out


STACKED = ["ffn1_gate", "ffn1_up", "ffn1_down", "ffn2_gate", "ffn2_up", "ffn2_down", "w_in_pool", "w_in_sb", "w_mem_kv", "w_out"]
WEIGHTS = ["g_pre", "g_post", "g_mem", "ffn1_gate", "ffn1_up", "ffn1_down", "ffn2_gate", "ffn2_up", "ffn2_down",
           "w_in_pool", "pool_w", "pool_scale", "w_in_sb", "w_mem_kv", "w_out"]


def kernel(x, mem, g_pre, g_post, g_mem, ffn1_gate, ffn1_up, ffn1_down, ffn2_gate, ffn2_up, ffn2_down, w_in_pool, pool_w, pool_scale, w_in_sb, w_mem_kv, w_out, loss_target, m_g_pre, m_g_post, m_g_mem, m_ffn1_gate, m_ffn1_up, m_ffn1_down, m_ffn2_gate, m_ffn2_up, m_ffn2_down, m_w_in_pool, m_pool_w, m_pool_scale, m_w_in_sb, m_w_mem_kv, m_w_out, v_g_pre, v_g_post, v_g_mem, v_ffn1_gate, v_ffn1_up, v_ffn1_down, v_ffn2_gate, v_ffn2_up, v_ffn2_down, v_w_in_pool, v_pool_w, v_pool_scale, v_w_in_sb, v_w_mem_kv, v_w_out):
    given = dict(locals())
    p = {n: given[n] for n in WEIGHTS}
    opt_m = {n: given["m_" + n] for n in WEIGHTS}
    opt_v = {n: given["v_" + n] for n in WEIGHTS}
    loss, grad_x, out = _step(p, opt_m, opt_v, x[0], mem[0], loss_target[0])
    res = [loss, grad_x[None]]
    for q in range(4):
        res += [out[n][q] for n in WEIGHTS]
    return tuple(res)
```

```python
import functools
from typing import NamedTuple, Optional

import jax
import jax.numpy as jnp
from jax import lax
from jax.experimental import pallas as pl
from jax.experimental.pallas import tpu as pltpu

F32 = jnp.float32
BF16 = jnp.bfloat16
GRAD_WIRE = jnp.bfloat16

N_DEV = 8
DEPTH = 4
D_MODEL = 1024
D_FF = 2048
D_TOK = 512
D_MEMH = 256
D_MIX = D_TOK + D_MEMH
D_SB = 3 * D_TOK + D_MEMH
HEAD_DIM = 64
Q_BLOCK = 128
POOL_WINDOWS = (2, 4, 8, 16)
POOL_GROUP = 128
POOL_HALO = 16
EPS = 1e-6
ATT_SCALE = HEAD_DIM ** -0.5
SB_DEAD_LOG_WEIGHT = -110.0
SB_FWD_LANES = 512
SB_BWD_LANES = 256

ADAM_LR = 0.001
ADAM_B1 = 0.9
ADAM_B2 = 0.999
ADAM_EPS = 1e-08
ADAM_WD = 0.01
ADAM_STEP = 10

VMEM_LIMIT_BYTES = 56 * 1024 * 1024
ROW_TILE = 512
FFN_FWD_TILE = 512
FFN_BWD_TILE = 512
FFN_CHUNK = 256
WGRAD_TILE = 512
ADAM_TILE_ELEMS = 128 * 1024

MESH = pl.DeviceIdType.MESH
ANY = pl.BlockSpec(memory_space=pl.ANY)


def _tile(n, pref):
    t = 1 << (pref.bit_length() - 1)
    while n % t:
        t //= 2
    return t


def _dot(a, b):
    return jnp.dot(a, b, preferred_element_type=F32)


def _dot_nt(a, b):
    return lax.dot_general(a, b, (((1,), (1,)), ((), ())), preferred_element_type=F32)


def _dot_tn(a, b):
    return lax.dot_general(a, b, (((0,), (0,)), ((), ())), preferred_element_type=F32)


def _split_dot(x, m, terms):
    out = None
    rest = x
    for _ in range(terms):
        part = rest.astype(BF16)
        rest = rest - part.astype(F32)
        d = _dot(part, m)
        out = d if out is None else out + d
    return out


def _rms(x, g):
    r = lax.rsqrt(jnp.mean(x * x, axis=-1, keepdims=True) + EPS)
    return x * r * g


def _rms_bwd(x, g, dy):
    r = lax.rsqrt(jnp.mean(x * x, axis=-1, keepdims=True) + EPS)
    xh = x * r
    gdy = g * dy
    dx = r * (gdy - xh * jnp.mean(gdy * xh, axis=-1, keepdims=True))
    return dx, jnp.sum(dy * xh, axis=0, keepdims=True)


def _acc_rows(ref, val, first):
    @pl.when(first)
    def _():
        ref[...] = val

    @pl.when(jnp.logical_not(first))
    def _():
        ref[...] += val


def _params(sem=None):
    return pltpu.CompilerParams(dimension_semantics=sem, vmem_limit_bytes=VMEM_LIMIT_BYTES)


def _sds(shape, dtype):
    return jax.ShapeDtypeStruct(shape, dtype)


def _rows(tm, width, col=0):
    return pl.BlockSpec((tm, width), lambda i: (i, col))


def _full(shape):
    nd = len(shape)
    return pl.BlockSpec(shape, lambda *_: (0,) * nd)


def _resident(shape):
    nd = len(shape)
    return pl.BlockSpec(shape, lambda *_: (0,) * nd, pipeline_mode=pl.Buffered(1))


def _peers():
    x, y, c = lax.axis_index("x"), lax.axis_index("y"), lax.axis_index("c")
    peers = []
    for k in range(1, N_DEV):
        px = 1 - x if k & 4 else x
        py = 1 - y if k & 2 else y
        pc = 1 - c if k & 1 else c
        peers.append(((px, py, pc), 4 * px + 2 * py + pc))
    return 4 * x + 2 * y + c, peers


class _Xfer(NamedTuple):
    src: jax.Array
    layer: Optional[int] = None
    scatter: bool = False

    @property
    def landing(self):
        block = self.src.shape if self.layer is None and not self.scatter else self.src.shape[1:]
        return _sds((N_DEV,) + block, self.src.dtype)


def _comm_copies(xfers, src_refs, dst_refs, send_sems, recv_sems, local_sems):
    me, peers = _peers()

    def src(t, to):
        ref = src_refs[t] if xfers[t].layer is None else src_refs[t].at[xfers[t].layer]
        return ref.at[to] if xfers[t].scatter else ref

    copies = [pltpu.make_async_copy(src(t, me), dst_refs[t].at[me], local_sems.at[t]) for t in range(len(xfers))]
    for k, (dev, idx) in enumerate(peers):
        for t in range(len(xfers)):
            copies.append(pltpu.make_async_remote_copy(
                src_ref=src(t, idx), dst_ref=dst_refs[t].at[me], send_sem=send_sems.at[t, k], recv_sem=recv_sems.at[t, k],
                device_id=dev, device_id_type=MESH))
    return copies


def _pcall(body, *, name, grid, in_specs, out_specs, out_shape, args, scratch=(), ride=()):
    n_in, n_out, n_scr, nx = len(in_specs), len(out_specs), len(scratch), len(ride)
    params = _params(("arbitrary",) * len(grid))
    if not ride:
        res = pl.pallas_call(body, name=name, grid=grid, in_specs=list(in_specs), out_specs=list(out_specs),
                             out_shape=list(out_shape), scratch_shapes=list(scratch), compiler_params=params,
                             interpret=False)(*args)
        return list(res), []
    def riding(*refs):
        o0 = n_in + nx
        s0 = o0 + n_out + nx
        comm = (ride, refs[n_in:o0], refs[o0 + n_out:s0], *refs[s0 + n_scr:])
        ids = [pl.program_id(a) for a in range(len(grid))]
        first = functools.reduce(jnp.logical_and, [i == 0 for i in ids])
        last = functools.reduce(jnp.logical_and, [i == n - 1 for i, n in zip(ids, grid)])

        @pl.when(first)
        def _():
            for cp in _comm_copies(*comm):
                cp.start()

        body(*refs[:n_in], *refs[o0:o0 + n_out], *refs[s0:s0 + n_scr])

        @pl.when(last)
        def _():
            for cp in _comm_copies(*comm):
                cp.wait()

    sems = [pltpu.SemaphoreType.DMA((nx, N_DEV - 1)), pltpu.SemaphoreType.DMA((nx, N_DEV - 1)), pltpu.SemaphoreType.DMA((nx,))]
    res = pl.pallas_call(riding, name=name, grid=grid, in_specs=list(in_specs) + [ANY] * nx,
                         out_specs=list(out_specs) + [ANY] * nx, out_shape=list(out_shape) + [t.landing for t in ride],
                         scratch_shapes=list(scratch) + sems, compiler_params=params,
                         interpret=False)(*args, *[t.src for t in ride])
    return list(res[:n_out]), list(res[n_out:])


def _comm_call(xfers, name):
    return _pcall(lambda: None, name=name, grid=(1,), in_specs=[], out_specs=[], out_shape=[], args=[], ride=xfers)[1]


def _gather_two_level(xfers, name):
    n = len(xfers)
    assert not any(t.scatter for t in xfers)

    def body(*refs):
        ins, outs = refs[:n], refs[n:2 * n]
        send_sems, recv_sems, local_sems = refs[2 * n:]
        x, y, c = lax.axis_index("x"), lax.axis_index("y"), lax.axis_index("c")
        sibling = (x, y, 1 - c)
        chips = [(1 - x, y), (x, 1 - y), (1 - x, 1 - y)]
        slot = lambda dev: 4 * dev[0] + 2 * dev[1] + dev[2]

        def copy(t, k, block, to, own=False):
            src = ins[t] if xfers[t].layer is None else ins[t].at[xfers[t].layer]
            return pltpu.make_async_remote_copy(
                src_ref=src if own else outs[t].at[slot(block)], dst_ref=outs[t].at[slot(block)],
                send_sem=send_sems.at[t, k], recv_sem=recv_sems.at[t, k], device_id=to, device_id_type=MESH)

        me = (x, y, c)
        local, sent = [], []
        for t in range(n):
            src = ins[t] if xfers[t].layer is None else ins[t].at[xfers[t].layer]
            mine = pltpu.make_async_copy(src, outs[t].at[slot(me)], local_sems.at[t])
            mine.start()
            local.append(mine)
            first = [copy(t, 0, me, sibling, own=True)] + [copy(t, 1 + j, me, (*chip, c), own=True) for j, chip in enumerate(chips)]
            for cp in first:
                cp.start()
            sent += first
        for t in range(n):
            for j, chip in enumerate(chips):
                copy(t, 1 + j, (*chip, c), me).wait_recv()
                passed = copy(t, 4 + j, (*chip, c), sibling)
                passed.start()
                sent.append(passed)
        for t in range(n):
            copy(t, 0, sibling, me).wait_recv()
            for j, chip in enumerate(chips):
                copy(t, 4 + j, (*chip, 1 - c), me).wait_recv()
        for cp in sent:
            cp.wait_send()
        for cp in local:
            cp.wait()

    return pl.pallas_call(
        body, name=name, in_specs=[ANY] * n, out_specs=[ANY] * n, out_shape=[t.landing for t in xfers],
        scratch_shapes=[pltpu.SemaphoreType.DMA((n, N_DEV - 1)), pltpu.SemaphoreType.DMA((n, N_DEV - 1)),
                        pltpu.SemaphoreType.DMA((n,))],
        interpret=False,
    )(*[t.src for t in xfers])


def _ffn_fwd(h, gpre, gpost, wg, wu, wd, name, ride=()):
    S = h.shape[0]
    tm = _tile(S, FFN_FWD_TILE)
    nb = FFN_CHUNK

    def body(h_ref, gpre_ref, gpost_ref, wgt_ref, wut_ref, wd_ref, hn_ref, n_ref, gate_ref, up_ref, act_ref, f_ref):
        hv = h_ref[...]
        n = _rms(hv, gpre_ref[...]).astype(BF16)
        n_ref[...] = n
        for c in range(D_FF // nb):
            cols = slice(c * nb, (c + 1) * nb)
            g = _dot_nt(n, wgt_ref[cols, :])
            u = _dot_nt(n, wut_ref[cols, :])
            gate_ref[:, cols] = g.astype(BF16)
            up_ref[:, cols] = u.astype(BF16)
            act_ref[:, cols] = (g * jax.nn.sigmoid(g) * u).astype(BF16)
        f = _dot(act_ref[...], wd_ref[...])
        f_ref[...] = f
        hn_ref[...] = hv + 0.5 * _rms(f, gpost_ref[...])

    return _pcall(
        body, name=name, grid=(S // tm,),
        in_specs=[_rows(tm, D_MODEL), _full((1, D_MODEL)), _full((1, D_MODEL)),
                  _resident((D_FF, D_MODEL)), _resident((D_FF, D_MODEL)), _resident((D_FF, D_MODEL))],
        out_specs=[_rows(tm, D_MODEL), _rows(tm, D_MODEL), _rows(tm, D_FF), _rows(tm, D_FF), _rows(tm, D_FF),
                   _rows(tm, D_MODEL)],
        out_shape=[_sds((S, D_MODEL), F32), _sds((S, D_MODEL), BF16), _sds((S, D_FF), BF16), _sds((S, D_FF), BF16),
                   _sds((S, D_FF), BF16), _sds((S, D_MODEL), F32)],
        args=(h, gpre, gpost, wg, wu, wd), ride=ride)


def _ffn_bwd(dho, h, f, gate, up, gpre, gpost, wg, wu, wd, name, ride=()):
    S = h.shape[0]
    tm = _tile(S, FFN_BWD_TILE)
    nb = FFN_CHUNK

    def body(dho_ref, h_ref, f_ref, gate_ref, up_ref, gpre_ref, gpost_ref, wgt_ref, wut_ref, wd_ref,
             dh_ref, df_ref, dgate_ref, dup_ref, dgpre_ref, dgpost_ref):
        first = pl.program_id(0) == 0
        dho_v = dho_ref[...]
        dfx, dgpost = _rms_bwd(f_ref[...], gpost_ref[...], 0.5 * dho_v)
        dfb = dfx.astype(BF16)
        df_ref[...] = dfb
        for c in range(D_FF // nb):
            cols = slice(c * nb, (c + 1) * nb)
            dact = _dot_nt(dfb, wd_ref[cols, :])
            g = gate_ref[:, cols].astype(F32)
            u = up_ref[:, cols].astype(F32)
            s = jax.nn.sigmoid(g)
            dgate_ref[:, cols] = (dact * u * (s * (1.0 + g * (1.0 - s)))).astype(BF16)
            dup_ref[:, cols] = (dact * (g * s)).astype(BF16)
        dn = _dot(dgate_ref[...], wgt_ref[...]) + _dot(dup_ref[...], wut_ref[...])
        dhx, dgpre = _rms_bwd(h_ref[...], gpre_ref[...], dn)
        dh_ref[...] = dho_v + dhx
        _acc_rows(dgpre_ref, dgpre, first)
        _acc_rows(dgpost_ref, dgpost, first)

    return _pcall(
        body, name=name, grid=(S // tm,),
        in_specs=[_rows(tm, D_MODEL), _rows(tm, D_MODEL), _rows(tm, D_MODEL), _rows(tm, D_FF), _rows(tm, D_FF),
                  _full((1, D_MODEL)), _full((1, D_MODEL)),
                  _resident((D_FF, D_MODEL)), _resident((D_FF, D_MODEL)), _resident((D_FF, D_MODEL))],
        out_specs=[_rows(tm, D_MODEL), _rows(tm, D_MODEL), _rows(tm, D_FF), _rows(tm, D_FF),
                   _full((1, D_MODEL)), _full((1, D_MODEL))],
        out_shape=[_sds((S, D_MODEL), F32), _sds((S, D_MODEL), BF16), _sds((S, D_FF), BF16), _sds((S, D_FF), BF16),
                   _sds((1, D_MODEL), F32), _sds((1, D_MODEL), F32)],
        args=(dho, h, f, gate, up, gpre, gpost, wg, wu, wd), ride=ride)


def _wgrad(a_parts, b_parts, split, name, ride=()):
    S = a_parts[0].shape[0]
    bk = _tile(S, WGRAD_TILE)
    ms = [a.shape[1] for a in a_parts]
    ns = [b.shape[1] for b in b_parts]
    M, N = sum(ms), sum(ns)
    na, nbp = len(a_parts), len(b_parts)
    blk = (M // N_DEV, N) if split == "rows" else (M, N // N_DEV)
    steps = S // bk

    def body(*refs):
        a_refs, b_refs = refs[:na], refs[na:na + nbp]
        out_ref, acc_ref = refs[-2], refs[-1]
        k = pl.program_id(0)

        @pl.when(k == 0)
        def _():
            acc_ref[...] = jnp.zeros_like(acc_ref)

        r0 = 0
        for ai in range(na):
            av = a_refs[ai][...]
            c0 = 0
            for bi in range(nbp):
                acc_ref[r0:r0 + ms[ai], c0:c0 + ns[bi]] += _dot_tn(av, b_refs[bi][...])
                c0 += ns[bi]
            r0 += ms[ai]

        @pl.when(k == steps - 1)
        def _():
            for d in range(N_DEV):
                if split == "rows":
                    out_ref[d] = acc_ref[d * blk[0]:(d + 1) * blk[0], :].astype(GRAD_WIRE)
                else:
                    out_ref[d] = acc_ref[:, d * blk[1]:(d + 1) * blk[1]].astype(GRAD_WIRE)

    in_specs = [pl.BlockSpec((bk, m), lambda k: (k, 0)) for m in ms] + [pl.BlockSpec((bk, n), lambda k: (k, 0)) for n in ns]
    (out,), landed = _pcall(
        body, name=name, grid=(steps,), in_specs=in_specs, out_specs=[_full((N_DEV,) + blk)],
        out_shape=[_sds((N_DEV,) + blk, GRAD_WIRE)], scratch=[pltpu.VMEM((M, N), F32)],
        args=list(a_parts) + list(b_parts), ride=ride)
    return out, landed


def _mix_in_pool(h, g1, w_in, pool_w, pool_scale, name, ride=()):
    S = h.shape[0]
    tm = _tile(S, ROW_TILE)
    kb = D_MODEL // N_DEV

    def body(h_ref, g_ref, w_ref, pw_ref, ps_ref, u_ref, dpre_ref, tok_ref, qm_ref, ext_ref):
        i = pl.program_id(0)
        u = _rms(h_ref[...], g_ref[...]).astype(BF16)
        u_ref[...] = u
        proj = _dot(u, w_ref[...])
        qm_ref[...] = proj[:, D_TOK:].astype(BF16)
        x = proj[:, :D_TOK]

        @pl.when(i == 0)
        def _():
            ext_ref[0:POOL_HALO, :] = jnp.zeros((POOL_HALO, D_TOK), F32)

        ext_ref[POOL_HALO:, :] = x
        pos = i * tm + lax.broadcasted_iota(jnp.int32, (tm, 1), 0)
        for gi, w in enumerate(POOL_WINDOWS):
            cols = slice(gi * POOL_GROUP, (gi + 1) * POOL_GROUP)
            xs = x[:, cols]
            wsum = xs
            for k in range(1, w):
                wsum = wsum + ext_ref[POOL_HALO - k:POOL_HALO - k + tm, cols]
            cnt = jnp.minimum(pos + 1, w).astype(F32)
            dg = (wsum / cnt - xs).astype(BF16)
            dpre_ref[:, cols] = dg
            yv = _dot(dg, pw_ref[gi].astype(BF16))
            tok_ref[:, cols] = (yv * ps_ref[:, cols]).astype(BF16)
        ext_ref[0:POOL_HALO, :] = x[tm - POOL_HALO:, :]

    return _pcall(
        body, name=name, grid=(S // tm,),
        in_specs=[_rows(tm, D_MODEL), _full((1, D_MODEL)), _resident((D_MODEL, D_MIX)),
                  _full((len(POOL_WINDOWS), POOL_GROUP, POOL_GROUP)), _full((1, D_TOK))],
        out_specs=[_rows(tm, D_MODEL), _rows(tm, D_TOK), _rows(tm, D_TOK), _rows(tm, D_MEMH)],
        out_shape=[_sds((S, D_MODEL), BF16), _sds((S, D_TOK), BF16), _sds((S, D_TOK), BF16), _sds((S, D_MEMH), BF16)],
        scratch=[pltpu.VMEM((POOL_HALO + tm, D_TOK), F32)],
        args=(h, g1, w_in, pool_w, pool_scale), ride=ride)


def _pool_bwd(dtok, dpre, pool_w, pool_scale, name, ride=()):
    S = dtok.shape[0]
    tm = _tile(S, ROW_TILE)
    nt = S // tm
    ng = len(POOL_WINDOWS)

    def body(dtok_ref, dpre_ref, pw_ref, ps_ref, dx_ref, dpw_ref, dps_ref, ext_ref):
        i = pl.program_id(0)
        first = i == 0
        t0 = (nt - 1 - i) * tm
        pos = t0 + lax.broadcasted_iota(jnp.int32, (tm, 1), 0)

        @pl.when(first)
        def _():
            ext_ref[tm:, :] = jnp.zeros((POOL_HALO, D_TOK), F32)

        dps = []
        for gi, w in enumerate(POOL_WINDOWS):
            cols = slice(gi * POOL_GROUP, (gi + 1) * POOL_GROUP)
            dg = dpre_ref[:, cols]
            pw = pw_ref[gi].astype(BF16)
            dt = dtok_ref[:, cols].astype(F32)
            yv = _dot(dg, pw)
            dps.append(jnp.sum(dt * yv, axis=0, keepdims=True))
            dy = (dt * ps_ref[:, cols]).astype(BF16)
            _acc_rows(dpw_ref.at[gi], _dot_tn(dg, dy), first)
            dd = _dot_nt(dy, pw)
            cnt = jnp.minimum(pos + 1, w).astype(F32)
            ext_ref[0:tm, cols] = dd / cnt
            wsum = ext_ref[0:tm, cols]
            for k in range(1, w):
                wsum = wsum + ext_ref[k:k + tm, cols]
            dx_ref[:, cols] = (wsum - dd).astype(BF16)
        _acc_rows(dps_ref, jnp.concatenate(dps, axis=1), first)
        ext_ref[tm:, :] = ext_ref[0:POOL_HALO, :]

    rev = lambda i: (nt - 1 - i, 0)
    return _pcall(
        body, name=name, grid=(nt,),
        in_specs=[pl.BlockSpec((tm, D_TOK), rev), pl.BlockSpec((tm, D_TOK), rev),
                  _full((ng, POOL_GROUP, POOL_GROUP)), _full((1, D_TOK))],
        out_specs=[pl.BlockSpec((tm, D_TOK), rev), _full((ng, POOL_GROUP, POOL_GROUP)), _full((1, D_TOK))],
        out_shape=[_sds((S, D_TOK), BF16), _sds((ng, POOL_GROUP, POOL_GROUP), F32), _sds((1, D_TOK), F32)],
        scratch=[pltpu.VMEM((tm + POOL_HALO, D_TOK), F32)],
        args=(dtok, dpre, pool_w, pool_scale), ride=ride)


def _mix_in_sb(h, g1, wt, name):
    S = h.shape[0]
    tm = _tile(S, ROW_TILE)
    cb = 256

    def body(h_ref, g_ref, wt_ref, u_ref, proj_ref):
        u = _rms(h_ref[...], g_ref[...]).astype(BF16)
        u_ref[...] = u
        for c in range(D_SB // cb):
            proj_ref[:, c * cb:(c + 1) * cb] = _dot_nt(u, wt_ref[c * cb:(c + 1) * cb, :]).astype(BF16)

    return pl.pallas_call(
        body, name=name, grid=(S // tm,),
        in_specs=[_rows(tm, D_MODEL), _full((1, D_MODEL)), _resident((D_SB, D_MODEL))],
        out_specs=[_rows(tm, D_MODEL), _rows(tm, D_SB)],
        out_shape=[_sds((S, D_MODEL), BF16), _sds((S, D_SB), BF16)],
        compiler_params=_params(("arbitrary",)),
        interpret=False,
    )(h, g1, wt)


def _mix_in_bwd(dho, h, g1, parts, w, mode, name, ride=()):
    S = h.shape[0]
    tm = _tile(S, ROW_TILE)
    widths = [p.shape[1] for p in parts]
    npart = len(parts)
    kb = D_MODEL // N_DEV

    def body(*refs):
        dho_ref, h_ref, g_ref = refs[:3]
        p_refs = refs[3:3 + npart]
        w_ref, dh_ref, dg_ref = refs[3 + npart:]
        first = pl.program_id(0) == 0
        dproj = jnp.concatenate([p[...] for p in p_refs], axis=1)
        du = _dot_nt(dproj, w_ref[...]) if mode == "pool" else _dot(dproj, w_ref[...])
        dhx, dg = _rms_bwd(h_ref[...], g_ref[...], du)
        dh_ref[...] = dho_ref[...] + dhx
        _acc_rows(dg_ref, dg, first)

    w_spec = _resident((D_MODEL, D_MIX)) if mode == "pool" else _resident((D_SB, D_MODEL))
    return _pcall(
        body, name=name, grid=(S // tm,),
        in_specs=[_rows(tm, D_MODEL), _rows(tm, D_MODEL), _full((1, D_MODEL))] + [_rows(tm, wd_) for wd_ in widths] + [w_spec],
        out_specs=[_rows(tm, D_MODEL), _full((1, D_MODEL))],
        out_shape=[_sds((S, D_MODEL), F32), _sds((1, D_MODEL), F32)],
        args=(dho, h, g1, *parts, w), ride=ride)


def _sb_block(qcats, kb, mask, later, tri_later):
    z = jnp.concatenate([_dot_nt(qc, _group(kb, g)) for g, qc in enumerate(qcats)], axis=0) * ATT_SCALE
    en = jnp.exp(-jnp.abs(z))
    ls = jnp.minimum(z, 0.0) - jnp.log(1.0 + en)
    lf = ls - z if mask is None else jnp.where(mask, ls - z, 0.0)
    within = _split_dot(lf, tri_later, 2)
    a = jnp.exp(ls + within + later)
    if mask is not None:
        a = jnp.where(mask, a, 0.0)
    return z, en, a, jnp.sum(lf, axis=1, keepdims=True)


def _sb_causal(rows):
    row = jnp.bitwise_and(lax.broadcasted_iota(jnp.int32, (rows, Q_BLOCK), 0), Q_BLOCK - 1)
    return lax.broadcasted_iota(jnp.int32, (rows, Q_BLOCK), 1) < row


def _sb_walk(qi, block, state, later_of, unrolled):
    if unrolled:
        state = block(qi, _sb_causal, state)
        state = block(jnp.maximum(qi - 1, 0), qi >= 1, state)

    def step(carry):
        j, _, state = carry
        mask = None if unrolled else lambda rows: jnp.logical_or(_sb_causal(rows), j > 0)
        state = block(qi - j, mask, state)
        return j + 1, _sb_alive(later_of(state)), state

    first = jnp.int32(2 if unrolled else 0)
    return lax.while_loop(functools.partial(_sb_more, qi), step, (first, _sb_alive(later_of(state)), state))[2]


def _sb_alive(later):
    return jnp.max(later) > SB_DEAD_LOG_WEIGHT


def _sb_more(qi, carry):
    return jnp.logical_and(carry[0] <= qi, carry[1])


def _tri(strict):
    row = lax.broadcasted_iota(jnp.int32, (Q_BLOCK, Q_BLOCK), 0)
    col = lax.broadcasted_iota(jnp.int32, (Q_BLOCK, Q_BLOCK), 1)
    return (row > col if strict else row >= col).astype(BF16)


HEADS_PER_GROUP = 128 // HEAD_DIM
GROUP_ROWS = HEADS_PER_GROUP * Q_BLOCK


def _sb_head_masks():
    lane = lax.broadcasted_iota(jnp.int32, (Q_BLOCK, 128), 1)
    return [(lane >= e * HEAD_DIM) & (lane < (e + 1) * HEAD_DIM) for e in range(HEADS_PER_GROUP)]


def _group(x, g):
    return x[:, g * 128:(g + 1) * 128]


def _masked(hm, x):
    return jnp.where(hm, x, jnp.zeros_like(x))


def _stack_heads(hms, x):
    return jnp.concatenate([_masked(hm, x) for hm in hms], axis=0)


def _own_lanes(hms, r, rows=Q_BLOCK):
    return sum(_masked(hm, r[e * rows:(e + 1) * rows]) for e, hm in enumerate(hms))


def _sb_fwd(proj, name, ride=()):
    S = proj.shape[0]
    nq = S // Q_BLOCK
    W = SB_FWD_LANES
    nrow = D_TOK // W
    groups = W // 128

    def body(q_ref, k_ref, v_ref, o_ref, tok_ref):
        qi = pl.program_id(1)
        hms = _sb_head_masks()
        tri_later = _tri(True)
        q = q_ref[...]
        qcats = [_stack_heads(hms, _group(q, g)) for g in range(groups)]

        def block(kblock, mask, state):
            accs, later = state
            if callable(mask):
                mask = mask(later.shape[0])
            off = pl.multiple_of(kblock * Q_BLOCK, Q_BLOCK)
            kb = k_ref[pl.ds(off, Q_BLOCK), :]
            vb = v_ref[pl.ds(off, Q_BLOCK), :]
            _, _, a, bsum = _sb_block(qcats, kb, mask, later, tri_later)
            hi = a.astype(BF16)
            lo = (a - hi.astype(F32)).astype(BF16)
            accs = list(accs)
            for g in range(groups):
                rows = slice(g * GROUP_ROWS, (g + 1) * GROUP_ROWS)
                r = _dot(jnp.concatenate([hi[rows], lo[rows]], axis=0), _group(vb, g))
                accs[g] = accs[g] + _own_lanes(hms, r[:GROUP_ROWS] + r[GROUP_ROWS:])
            return tuple(accs), later + bsum

        init = ((jnp.zeros((Q_BLOCK, 128), F32),) * groups, jnp.zeros((groups * GROUP_ROWS, 1), F32))
        accs = _sb_walk(qi, block, init, lambda state: state[1], unrolled=True)[0]
        for g, acc in enumerate(accs):
            o_ref[:, g * 128:(g + 1) * 128] = acc
            tok_ref[:, g * 128:(g + 1) * 128] = acc.astype(BF16)

    blk = pl.BlockSpec((Q_BLOCK, W), lambda p, i: (i, p))
    return _pcall(
        body, name=name, grid=(nrow, nq),
        in_specs=[blk, pl.BlockSpec((S, W), lambda p, i: (0, nrow + p), pipeline_mode=pl.Buffered(1)),
                  pl.BlockSpec((S, W), lambda p, i: (0, 2 * nrow + p), pipeline_mode=pl.Buffered(1))],
        out_specs=[blk, blk],
        out_shape=[_sds((S, D_TOK), F32), _sds((S, D_TOK), BF16)],
        args=(proj, proj, proj), ride=ride)


def _sb_bwd(proj, dtok, o32, name, ride=()):
    S = proj.shape[0]
    nq = S // Q_BLOCK
    W = SB_BWD_LANES
    nrow = D_TOK // W
    groups = W // 128

    def body(q_ref, k_ref, v_ref, do_ref, o_ref, dq_ref, dk_ref, dv_ref, dk_acc, dv_acc):
        qi = pl.program_id(1)

        @pl.when(qi == 0)
        def _():
            dk_acc[...] = jnp.zeros_like(dk_acc)
            dv_acc[...] = jnp.zeros_like(dv_acc)

        hms = _sb_head_masks()
        tri_later = _tri(True)
        tri_from = _tri(False)
        q = q_ref[...]
        do = do_ref[...]
        dov = do.astype(F32) * o_ref[...]
        qcats = [_stack_heads(hms, _group(q, g)) for g in range(groups)]
        docats = [_stack_heads(hms, _group(do, g)) for g in range(groups)]
        rowtot = jnp.concatenate([jnp.sum(jnp.where(hm, _group(dov, g), 0.0), axis=1, keepdims=True)
                                  for g in range(groups) for hm in hms], axis=0)

        def block(kblock, mask, state):
            dqs, later, seen = state
            if callable(mask):
                mask = mask(later.shape[0])
            off = pl.multiple_of(kblock * Q_BLOCK, Q_BLOCK)
            kb = k_ref[pl.ds(off, Q_BLOCK), :]
            vb = v_ref[pl.ds(off, Q_BLOCK), :]
            z, en, a, bsum = _sb_block(qcats, kb, mask, later, tri_later)
            inv = 1.0 / (1.0 + en)
            beta = jnp.where(z >= 0, 1.0, en) * inv
            omb = jnp.where(z >= 0, en, 1.0) * inv
            dlogw = a * jnp.concatenate([_dot_nt(docats[g], _group(vb, g)) for g in range(groups)], axis=0)
            prefix = rowtot - seen - _split_dot(dlogw, tri_from, 2)
            dz = dlogw * omb - beta * prefix
            if mask is not None:
                dz = jnp.where(mask, dz, 0.0)
            dz = dz.astype(BF16)
            ab = a.astype(BF16)
            dqs = list(dqs)
            for g in range(groups):
                rows = slice(g * GROUP_ROWS, (g + 1) * GROUP_ROWS)
                lanes = slice(g * 128, (g + 1) * 128)
                dqs[g] = dqs[g] + _own_lanes(hms, _dot(dz[rows], _group(kb, g)))
                dk_acc[pl.ds(off, Q_BLOCK), lanes] += _dot_tn(dz[rows], qcats[g]) * ATT_SCALE
                dv_acc[pl.ds(off, Q_BLOCK), lanes] += _dot_tn(ab[rows], docats[g])
            return tuple(dqs), later + bsum, seen + jnp.sum(dlogw, axis=1, keepdims=True)

        zero = jnp.zeros((groups * GROUP_ROWS, 1), F32)
        init = ((jnp.zeros((Q_BLOCK, 128), F32),) * groups, zero, zero)
        dqs = _sb_walk(qi, block, init, lambda state: state[1], unrolled=False)[0]
        for g, dq in enumerate(dqs):
            dq_ref[:, g * 128:(g + 1) * 128] = (dq * ATT_SCALE).astype(BF16)

        @pl.when(qi == nq - 1)
        def _():
            dk_ref[...] = dk_acc[...].astype(BF16)
            dv_ref[...] = dv_acc[...].astype(BF16)

    blk = pl.BlockSpec((Q_BLOCK, W), lambda p, i: (i, p))
    col = pl.BlockSpec((S, W), lambda p, i: (0, p))
    return _pcall(
        body, name=name, grid=(nrow, nq),
        in_specs=[blk, pl.BlockSpec((S, W), lambda p, i: (0, nrow + p)), pl.BlockSpec((S, W), lambda p, i: (0, 2 * nrow + p)),
                  blk, blk],
        out_specs=[blk, col, col],
        out_shape=[_sds((S, D_TOK), BF16), _sds((S, D_TOK), BF16), _sds((S, D_TOK), BF16)],
        scratch=[pltpu.VMEM((S, W), F32), pltpu.VMEM((S, W), F32)],
        args=(proj, proj, proj, dtok, o32), ride=ride)


def _mem_kv_fwd(mem, g_mem, w_kv, name):
    lm = mem.shape[0]
    kb = D_MODEL // N_DEV

    def body(mem_ref, g_ref, w_ref, mn_ref, km_ref, vm_ref):
        mn = _rms(mem_ref[...], g_ref[...]).astype(BF16)
        mn_ref[...] = mn
        kv = jnp.zeros((lm, 2 * D_MEMH), F32)
        for d in range(N_DEV):
            kv = kv + _dot(mn[:, d * kb:(d + 1) * kb], w_ref[d])
        km_ref[...] = kv[:, :D_MEMH].astype(BF16)
        vm_ref[...] = kv[:, D_MEMH:].astype(BF16)

    return pl.pallas_call(
        body, name=name, grid=(1,),
        in_specs=[_full((lm, D_MODEL)), _full((1, D_MODEL)), _full((N_DEV, kb, 2 * D_MEMH))],
        out_specs=[_full((lm, D_MODEL)), _full((lm, D_MEMH)), _full((lm, D_MEMH))],
        out_shape=[_sds((lm, D_MODEL), BF16), _sds((lm, D_MEMH), BF16), _sds((lm, D_MEMH), BF16)],
        compiler_params=_params(("arbitrary",)),
        interpret=False,
    )(mem, g_mem, w_kv)


def _mem_kv_bwd(dkm, dvm, mem, g_mem, mem_n, w_kv, name):
    lm = mem.shape[0]
    kb = D_MODEL // N_DEV

    def body(dkm_ref, dvm_ref, mem_ref, g_ref, mn_ref, w_ref, dw_ref, dg_ref):
        dkv = jnp.concatenate([dkm_ref[...], dvm_ref[...]], axis=1).astype(BF16)
        dw = _dot_tn(mn_ref[...], dkv)
        for d in range(N_DEV):
            dw_ref[d] = dw[d * kb:(d + 1) * kb, :].astype(GRAD_WIRE)
        dmn = jnp.concatenate([_dot_nt(dkv, w_ref[d]) for d in range(N_DEV)], axis=1)
        _, dg = _rms_bwd(mem_ref[...], g_ref[...], dmn)
        dg_ref[...] = dg

    return pl.pallas_call(
        body, name=name, grid=(1,),
        in_specs=[_full((lm, D_MEMH)), _full((lm, D_MEMH)), _full((lm, D_MODEL)), _full((1, D_MODEL)),
                  _full((lm, D_MODEL)), _full((N_DEV, kb, 2 * D_MEMH))],
        out_specs=[_full((N_DEV, kb, 2 * D_MEMH)), _full((1, D_MODEL))],
        out_shape=[_sds((N_DEV, kb, 2 * D_MEMH), GRAD_WIRE), _sds((1, D_MODEL), F32)],
        compiler_params=_params(("arbitrary",)),
        interpret=False,
    )(dkm, dvm, mem, g_mem, mem_n, w_kv)


def _mem_heads(tm):
    lane = lax.broadcasted_iota(jnp.int32, (tm, D_MEMH), 1)
    return [(lane >= e * HEAD_DIM) & (lane < (e + 1) * HEAD_DIM) for e in range(D_MEMH // HEAD_DIM)]


def _softmax(s):
    m = jnp.max(s, axis=-1, keepdims=True)
    p = jnp.exp(s - m)
    return p / jnp.sum(p, axis=-1, keepdims=True)


def _mix_out_fwd(h, tok, qm, qm_col, km, vm, w_out, gpost, name, ride=()):
    S = h.shape[0]
    tm = _tile(S, ROW_TILE)
    lm = km.shape[0]
    nb = D_MODEL // N_DEV

    def body(h_ref, tok_ref, qm_ref, km_ref, vm_ref, w_ref, g_ref, hn_ref, mo_ref, mix_ref):
        qv = qm_ref[...]
        kv, vv = km_ref[...], vm_ref[...]
        hms = _mem_heads(tm)
        p = _softmax(_dot_nt(_stack_heads(hms, qv), kv) * ATT_SCALE)
        mob = _own_lanes(hms, _dot(p.astype(BF16), vv), tm).astype(BF16)
        mo_ref[...] = mob
        mix = _dot_nt(jnp.concatenate([tok_ref[...], mob], axis=1), w_ref[...])
        mix_ref[...] = mix
        hn_ref[...] = h_ref[...] + _rms(mix, g_ref[...])

    return _pcall(
        body, name=name, grid=(S // tm,),
        in_specs=[_rows(tm, D_MODEL), _rows(tm, D_TOK), _rows(tm, D_MEMH, qm_col), _full((lm, D_MEMH)), _full((lm, D_MEMH)),
                  _resident((D_MODEL, D_MIX)), _full((1, D_MODEL))],
        out_specs=[_rows(tm, D_MODEL), _rows(tm, D_MEMH), _rows(tm, D_MODEL)],
        out_shape=[_sds((S, D_MODEL), F32), _sds((S, D_MEMH), BF16), _sds((S, D_MODEL), F32)],
        args=(h, tok, qm, km, vm, w_out, gpost), ride=ride)


def _mix_out_bwd(dho, mix, qm, qm_col, km, vm, w_out, gpost, name, ride=()):
    S = dho.shape[0]
    tm = _tile(S, ROW_TILE)
    lm = km.shape[0]
    nb = D_MODEL // N_DEV

    def body(dho_ref, mix_ref, qm_ref, km_ref, vm_ref, w_ref, g_ref,
             dmix_ref, dtok_ref, dqm_ref, dkm_ref, dvm_ref, dg_ref):
        first = pl.program_id(0) == 0
        dmx, dg = _rms_bwd(mix_ref[...], g_ref[...], dho_ref[...])
        dmb = dmx.astype(BF16)
        dmix_ref[...] = dmb
        _acc_rows(dg_ref, dg, first)
        dcat = _dot(dmb, w_ref[...])
        dtok_ref[...] = dcat[:, :D_TOK].astype(BF16)
        dmo = dcat[:, D_TOK:].astype(BF16)
        qv = qm_ref[...]
        kv, vv = km_ref[...], vm_ref[...]
        hms = _mem_heads(tm)
        qcat = _stack_heads(hms, qv)
        dmcat = _stack_heads(hms, dmo)
        p = _softmax(_dot_nt(qcat, kv) * ATT_SCALE)
        dp = _dot_nt(dmcat, vv)
        ds = (p * (dp - jnp.sum(p * dp, axis=-1, keepdims=True))).astype(BF16)
        dq = _own_lanes(hms, _dot(ds, kv), tm)
        dk = _dot_tn(ds, qcat)
        dv = _dot_tn(p.astype(BF16), dmcat)
        dqm_ref[...] = (dq * ATT_SCALE).astype(BF16)
        _acc_rows(dkm_ref, dk * ATT_SCALE, first)
        _acc_rows(dvm_ref, dv, first)

    return _pcall(
        body, name=name, grid=(S // tm,),
        in_specs=[_rows(tm, D_MODEL), _rows(tm, D_MODEL), _rows(tm, D_MEMH, qm_col), _full((lm, D_MEMH)), _full((lm, D_MEMH)),
                  _resident((D_MODEL, D_MIX)), _full((1, D_MODEL))],
        out_specs=[_rows(tm, D_MODEL), _rows(tm, D_TOK), _rows(tm, D_MEMH), _full((lm, D_MEMH)), _full((lm, D_MEMH)),
                   _full((1, D_MODEL))],
        out_shape=[_sds((S, D_MODEL), BF16), _sds((S, D_TOK), BF16), _sds((S, D_MEMH), BF16),
                   _sds((lm, D_MEMH), F32), _sds((lm, D_MEMH), F32), _sds((1, D_MODEL), F32)],
        args=(dho, mix, qm, km, vm, w_out, gpost), ride=ride)


def _loss_head(y, target, name):
    S = y.shape[0]
    tm = _tile(S, ROW_TILE)
    nt = S // tm

    def body(y_ref, t_ref, dy_ref, loss_ref, acc_ref):
        i = pl.program_id(0)
        e = y_ref[...] - t_ref[...]
        dy_ref[...] = e * (1.0 / D_MODEL)
        _acc_rows(acc_ref, jnp.sum(e * e, axis=0, keepdims=True), i == 0)

        @pl.when(i == nt - 1)
        def _():
            tot = jnp.sum(acc_ref[...], axis=1, keepdims=True) * (0.5 / D_MODEL)
            loss_ref[...] = jnp.broadcast_to(tot, (1, 128))

    return pl.pallas_call(
        body, name=name, grid=(nt,),
        in_specs=[_rows(tm, D_MODEL), _rows(tm, D_MODEL)],
        out_specs=[_rows(tm, D_MODEL), _full((1, 128))],
        out_shape=[_sds((S, D_MODEL), F32), _sds((1, 128), F32)],
        scratch_shapes=[pltpu.VMEM((1, D_MODEL), F32)],
        compiler_params=_params(("arbitrary",)),
        interpret=False,
    )(y, target)


def _adamw(recv, w, m, v, l, into, name):
    L, R, C = w.shape
    tr = R if R * C <= ADAM_TILE_ELEMS else _tile(R, ADAM_TILE_ELEMS // C)
    c1 = 1.0 - ADAM_B1 ** ADAM_STEP
    c2 = 1.0 - ADAM_B2 ** ADAM_STEP

    def body(r_ref, w_ref, m_ref, v_ref, *rest):
        g_ref, d_ref, nm_ref, nv_ref = rest[-4:]
        g = r_ref[0].astype(F32)
        for s in range(1, N_DEV):
            g = g + r_ref[s].astype(F32)
        g_ref[...] = g
        nm = ADAM_B1 * m_ref[...] + (1.0 - ADAM_B1) * g
        nv = ADAM_B2 * v_ref[...] + (1.0 - ADAM_B2) * (g * g)
        nm_ref[...] = nm
        nv_ref[...] = nv
        d_ref[...] = -ADAM_LR * ((nm / c1) / (jnp.sqrt(nv / c2) + ADAM_EPS) + ADAM_WD * w_ref[...])

    t = pl.BlockSpec((None, tr, C), lambda i: (l, i, 0))
    kept = [] if into is None else list(into)
    return pl.pallas_call(
        body, name=name, grid=(R // tr,),
        in_specs=[pl.BlockSpec((N_DEV, tr, C), lambda i: (0, i, 0)), t, t, t] + [ANY] * len(kept),
        out_specs=[t, t, t, t],
        out_shape=[_sds((L, R, C), F32)] * 4,
        input_output_aliases={4 + q: q for q in range(len(kept))},
        compiler_params=_params(("arbitrary",)),
        interpret=False,
    )(recv, w, m, v, *kept)


def _step(p, opt_m, opt_v, x, mem, target):
    bf = lambda a: a.astype(BF16)
    row = lambda a: a.reshape(1, -1)
    tsb = lambda a: jnp.swapaxes(a, 1, 2)
    g_mem = p["g_mem"]

    wsb_t = tsb(p["w_in_sb"])
    travels_transposed = ("w_in_sb", "w_out", "ffn1_gate", "ffn1_up", "ffn2_gate", "ffn2_up")
    shard = {n: bf(tsb(p[n]) if n in travels_transposed else p[n]) for n in STACKED}
    ffn_weights = lambda which, i: [gw[k].reshape(D_FF, D_MODEL) for k in ffn(which, i)]
    w_in = lambda i: "w_in_pool" if i % 2 == 0 else "w_in_sb"
    ffn = lambda which, i: [(f"ffn{which}_{s}", i) for s in ("gate", "up", "down")]
    mixing = lambda i: [(w_in(i), i // 2), ("w_mem_kv", i), ("w_out", i)]

    gw = {}

    def gather(keys):
        return [_Xfer(shard[n], l) for n, l in keys]

    first = ffn(1, 0)
    landed = _gather_two_level(gather(first) + [_Xfer(p["g_pre"]), _Xfer(p["g_post"])], "gather_first")
    gw.update(zip(first, landed))
    unshard = lambda g: jnp.transpose(g, (1, 2, 0, 3)).reshape(DEPTH, 3, D_MODEL)
    g_pre, g_post = unshard(landed[-2]), unshard(landed[-1])

    def ahead(i):
        nxt = i + 1 < DEPTH
        if i % 2 == 0:
            start = i == 0
            return {"ffn1": ffn(2, i)[:2] + (mixing(0)[::2] if start else []), "mix_in": ffn(2, i)[2:] + (mixing(0)[1:2] if start else []),
                    "sb": [], "mix_out": ffn(1, i + 1)[:1] if nxt else [], "ffn2": ffn(1, i + 1)[1:] if nxt else []}
        return {"ffn1": mixing(i), "mix_in": [], "sb": ffn(2, i) + (ffn(1, i + 1)[:2] if nxt else []),
                "mix_out": [], "ffn2": ffn(1, i + 1)[2:] + mixing(i + 1) if nxt else []}

    def gathering(keys, call):
        res, landed = call(ride=gather(keys))
        gw.update(zip(keys, landed))
        return res

    saved = []
    h = x
    for i in range(DEPTH):
        j = i // 2
        st = {"h0": h}
        carry = ahead(i)
        h, st["n1"], st["gate1"], st["up1"], st["act1"], st["f1"] = gathering(carry["ffn1"], functools.partial(
            _ffn_fwd, h, row(g_pre[i, 0]), row(g_post[i, 0]), *ffn_weights(1, i), f"ffn1_fwd_{i}"))
        st["h1"] = h
        if i % 2 == 0:
            st["u"], st["dpre"], st["tok"], st["qm"] = gathering(carry["mix_in"], functools.partial(
                _mix_in_pool, h, row(g_pre[i, 1]), gw[("w_in_pool", j)].reshape(D_MODEL, D_MIX),p["pool_w"][j], row(p["pool_scale"][j]),
                f"mix_in_pool_{i}"))
            qm, qm_col = st["qm"], 0
        else:
            st["u"], st["proj"] = _mix_in_sb(h, row(g_pre[i, 1]), gw[("w_in_sb", j)].reshape(D_SB, D_MODEL), f"mix_in_sb_{i}")
            st["o32"], st["tok"] = gathering(carry["sb"], functools.partial(_sb_fwd, st["proj"], f"sb_fwd_{i}"))
            qm, qm_col = st["proj"], 3 * D_TOK // D_MEMH
        st["mem_n"], st["km"], st["vm"] = _mem_kv_fwd(mem, row(g_mem[i]), gw[("w_mem_kv", i)], f"mem_kv_fwd_{i}")
        h, st["mo"], st["mix"] = gathering(carry["mix_out"], functools.partial(
            _mix_out_fwd, h, st["tok"], qm, qm_col, st["km"], st["vm"], gw[("w_out", i)].reshape(D_MODEL, D_MIX), row(g_post[i, 1]), f"mix_out_fwd_{i}"))
        st["h2"] = h
        h, st["n2"], st["gate2"], st["up2"], st["act2"], st["f2"] = gathering(carry["ffn2"], functools.partial(
            _ffn_fwd, h, row(g_pre[i, 2]), row(g_post[i, 2]), *ffn_weights(2, i), f"ffn2_fwd_{i}"))
        saved.append(st)

    dh, loss_part = _loss_head(h, target, "loss_head")

    grads = {}
    recv = {}
    dg_pre = [[None] * 3 for _ in range(DEPTH)]
    dg_post = [[None] * 3 for _ in range(DEPTH)]
    dg_mem = [None] * DEPTH
    dpool_w = [None, None]
    dpool_scale = [None, None]

    def scatter(keys):
        return [_Xfer(grads[k], scatter=True) for k in keys]

    def ffn_backward(dh, st, i, which, hkey, slot, riding, last):
        sfx = str(which)
        keys = ffn(which, i)
        (dh, df, dgate, dup, dg_pre[i][slot], dg_post[i][slot]), landed = _ffn_bwd(
            dh, st[hkey], st["f" + sfx], st["gate" + sfx], st["up" + sfx], row(g_pre[i, slot]), row(g_post[i, slot]),
            *ffn_weights(which, i), f"ffn{sfx}_bwd_{i}", ride=scatter(riding))
        recv.update(zip(riding, landed))
        riders = [mixing(i), keys[:1], keys[1:2]] if last else [[], [], []]
        operands = ((st["n" + sfx], dgate, "cols"), (st["n" + sfx], dup, "cols"), (st["act" + sfx], df, "rows"))
        for key, (a, b, split), riding in zip(keys, operands, riders):
            grads[key], landed = _wgrad([a], [b], split, f"wgrad_{key[0]}_{i}", ride=scatter(riding))
            recv.update(zip(riding, landed))
        return dh

    def behind(i):
        prev = ffn(1, i + 1) if i + 1 < DEPTH else []
        none = {"mix_out": [], "pool": [], "mix_in": [], "sb": []}
        if i % 2 == 1:
            return {**none, "ffn2": prev, "sb": ffn(2, i), "ffn1": mixing(i)}
        if i > 0:
            return {**none, "ffn2": prev, "mix_out": ffn(2, i)[:1], "ffn1": ffn(2, i)[1:] + mixing(i)}
        return {**none, "ffn2": prev, "mix_out": ffn(2, 0)[:1], "pool": ffn(2, 0)[1:2], "mix_in": ffn(2, 0)[2:], "ffn1": []}

    for i in reversed(range(DEPTH)):
        j = i // 2
        st = saved[i]
        carry = behind(i)
        dh = ffn_backward(dh, st, i, 2, "h2", 2, carry["ffn2"], False)
        if i % 2 == 0:
            qm, qm_col = st["qm"], 0
        else:
            qm, qm_col = st["proj"], 3 * D_TOK // D_MEMH
        keys = carry["mix_out"]
        (dmix, dtok, dqm, dkm, dvm, dg_post[i][1]), landed = _mix_out_bwd(
            dh, st["mix"], qm, qm_col, st["km"], st["vm"], gw[("w_out", i)].reshape(D_MODEL, D_MIX), row(g_post[i, 1]), f"mix_out_bwd_{i}", ride=scatter(keys))
        recv.update(zip(keys, landed))
        grads[("w_out", i)], _ = _wgrad([st["tok"], st["mo"]], [dmix], "cols", f"wgrad_w_out_{i}")
        grads[("w_mem_kv", i)], dg_mem[i] = _mem_kv_bwd(dkm, dvm, mem, row(g_mem[i]), st["mem_n"], gw[("w_mem_kv", i)],
                                                        f"mem_kv_bwd_{i}")
        if i % 2 == 0:
            (dx, dpool_w[j], dpool_scale[j]), landed = _pool_bwd(dtok, st["dpre"], p["pool_w"][j], row(p["pool_scale"][j]),
                                                                 f"pool_bwd_{i}", ride=scatter(carry["pool"]))
            recv.update(zip(carry["pool"], landed))
            parts = [dx, dqm]
            (dh, dg_pre[i][1]), landed = _mix_in_bwd(dh, st["h1"], row(g_pre[i, 1]), parts, gw[("w_in_pool", j)].reshape(D_MODEL, D_MIX),
                                                     "pool", f"mix_in_bwd_{i}", ride=scatter(carry["mix_in"]))
            recv.update(zip(carry["mix_in"], landed))
            grads[("w_in_pool", j)], _ = _wgrad([st["u"]], parts, "rows", f"wgrad_w_in_pool_{i}")
        else:
            (dq, dk, dv), landed = _sb_bwd(st["proj"], dtok, st["o32"], f"sb_bwd_{i}", ride=scatter(carry["sb"]))
            recv.update(zip(carry["sb"], landed))
            parts = [dq, dk, dv, dqm]
            (dh, dg_pre[i][1]), _ = _mix_in_bwd(dh, st["h1"], row(g_pre[i, 1]), parts, gw[("w_in_sb", j)].reshape(D_SB, D_MODEL), "sb",
                                                f"mix_in_bwd_{i}")
            grads[("w_in_sb", j)], _ = _wgrad(parts, [st["u"]], "rows", f"wgrad_w_in_sb_{i}")
        dh = ffn_backward(dh, st, i, 1, "h0", 0, carry["ffn1"], i == 0)
    grad_x = dh

    shard8 = lambda rows_: jnp.transpose(jnp.stack([jnp.concatenate(r, axis=0) for r in rows_]).reshape(DEPTH, 3, N_DEV, -1),
                                         (2, 0, 1, 3))
    tail = ffn(1, 0)[2:]
    landed = _comm_call(
        scatter(tail) + [_Xfer(shard8(dg_pre), scatter=True), _Xfer(shard8(dg_post), scatter=True),
                         _Xfer(jnp.concatenate(dg_mem, axis=0)), _Xfer(jnp.stack(dpool_w)),
                         _Xfer(jnp.concatenate(dpool_scale, axis=0)), _Xfer(loss_part)], "exchange_last")
    recv.update(zip(tail, landed))
    small = dict(zip(["g_pre", "g_post", "g_mem", "pool_w", "pool_scale"], landed[len(tail):]))

    def update(name, slots, w, m, v):
        into = None
        for l, r in enumerate(slots):
            into = _adamw(r, w, m, v, l, into, f"adamw_{name}_{l}")
        return into

    out = {}
    for n in STACKED:
        w, m, v = (wsb_t, tsb(opt_m[n]), tsb(opt_v[n])) if n == "w_in_sb" else (p[n], opt_m[n], opt_v[n])
        res = update(n, [recv[(n, l)] for l in range(w.shape[0])], w, m, v)
        out[n] = [tsb(a) for a in res] if n == "w_in_sb" else res
    one = lambda a: a.reshape(1, -1, a.shape[-1])
    for n, r in small.items():
        res = update(n, [r.reshape((N_DEV,) + one(p[n]).shape[1:])], one(p[n]), one(opt_m[n]), one(opt_v[n]))
        out[n] = [a.reshape(p[n].shape) for a in res]
    loss = jnp.sum(landed[-1][:, 0, 0])
    return loss, grad_x, out


STACKED = ["ffn1_gate", "ffn1_up", "ffn1_down", "ffn2_gate", "ffn2_up", "ffn2_down", "w_in_pool", "w_in_sb", "w_mem_kv", "w_out"]
WEIGHTS = ["g_pre", "g_post", "g_mem", "ffn1_gate", "ffn1_up", "ffn1_down", "ffn2_gate", "ffn2_up", "ffn2_down",
           "w_in_pool", "pool_w", "pool_scale", "w_in_sb", "w_mem_kv", "w_out"]


def kernel(x, mem, g_pre, g_post, g_mem, ffn1_gate, ffn1_up, ffn1_down, ffn2_gate, ffn2_up, ffn2_down, w_in_pool, pool_w, pool_scale, w_in_sb, w_mem_kv, w_out, loss_target, m_g_pre, m_g_post, m_g_mem, m_ffn1_gate, m_ffn1_up, m_ffn1_down, m_ffn2_gate, m_ffn2_up, m_ffn2_down, m_w_in_pool, m_pool_w, m_pool_scale, m_w_in_sb, m_w_mem_kv, m_w_out, v_g_pre, v_g_post, v_g_mem, v_ffn1_gate, v_ffn1_up, v_ffn1_down, v_ffn2_gate, v_ffn2_up, v_ffn2_down, v_w_in_pool, v_pool_w, v_pool_scale, v_w_in_sb, v_w_mem_kv, v_w_out):
    given = dict(locals())
    p = {n: given[n] for n in WEIGHTS}
    opt_m = {n: given["m_" + n] for n in WEIGHTS}
    opt_v = {n: given["v_" + n] for n in WEIGHTS}
    loss, grad_x, out = _step(p, opt_m, opt_v, x[0], mem[0], loss_target[0])
    res = [loss, grad_x[None]]
    for q in range(4):
        res += [out[n][q] for n in WEIGHTS]
    return tuple(res)
```

```python
import functools
from typing import NamedTuple, Optional

import jax
import jax.numpy as jnp
from jax import lax
from jax.experimental import pallas as pl
from jax.experimental.pallas import tpu as pltpu

F32 = jnp.float32
BF16 = jnp.bfloat16
GRAD_WIRE = jnp.bfloat16

N_DEV = 8
DEPTH = 4
D_MODEL = 1024
D_FF = 2048
D_TOK = 512
D_MEMH = 256
D_MIX = D_TOK + D_MEMH
D_SB = 3 * D_TOK + D_MEMH
HEAD_DIM = 64
Q_BLOCK = 128
POOL_WINDOWS = (2, 4, 8, 16)
POOL_GROUP = 128
POOL_HALO = 16
EPS = 1e-6
ATT_SCALE = HEAD_DIM ** -0.5
SB_DEAD_LOG_WEIGHT = -110.0
SB_FWD_LANES = 512
SB_BWD_LANES = 256

ADAM_LR = 0.001
ADAM_B1 = 0.9
ADAM_B2 = 0.999
ADAM_EPS = 1e-08
ADAM_WD = 0.01
ADAM_STEP = 10

VMEM_LIMIT_BYTES = 56 * 1024 * 1024
ROW_TILE = 512
FFN_FWD_TILE = 512
FFN_BWD_TILE = 512
FFN_CHUNK = 256
WGRAD_TILE = 512
ADAM_TILE_ELEMS = 128 * 1024

MESH = pl.DeviceIdType.MESH
ANY = pl.BlockSpec(memory_space=pl.ANY)


def _tile(n, pref):
    t = 1 << (pref.bit_length() - 1)
    while n % t:
        t //= 2
    return t


def _dot(a, b):
    return jnp.dot(a, b, preferred_element_type=F32)


def _dot_nt(a, b):
    return lax.dot_general(a, b, (((1,), (1,)), ((), ())), preferred_element_type=F32)


def _dot_tn(a, b):
    return lax.dot_general(a, b, (((0,), (0,)), ((), ())), preferred_element_type=F32)


def _split_dot(x, m, terms):
    out = None
    rest = x
    for _ in range(terms):
        part = rest.astype(BF16)
        rest = rest - part.astype(F32)
        d = _dot(part, m)
        out = d if out is None else out + d
    return out


def _rms(x, g):
    r = lax.rsqrt(jnp.mean(x * x, axis=-1, keepdims=True) + EPS)
    return x * r * g


def _rms_bwd(x, g, dy):
    r = lax.rsqrt(jnp.mean(x * x, axis=-1, keepdims=True) + EPS)
    xh = x * r
    gdy = g * dy
    dx = r * (gdy - xh * jnp.mean(gdy * xh, axis=-1, keepdims=True))
    return dx, jnp.sum(dy * xh, axis=0, keepdims=True)


def _acc_rows(ref, val, first):
    @pl.when(first)
    def _():
        ref[...] = val

    @pl.when(jnp.logical_not(first))
    def _():
        ref[...] += val


def _params(sem=None):
    return pltpu.CompilerParams(dimension_semantics=sem, vmem_limit_bytes=VMEM_LIMIT_BYTES)


def _sds(shape, dtype):
    return jax.ShapeDtypeStruct(shape, dtype)


def _rows(tm, width, col=0):
    return pl.BlockSpec((tm, width), lambda i: (i, col))


def _full(shape):
    nd = len(shape)
    return pl.BlockSpec(shape, lambda *_: (0,) * nd)


def _resident(shape):
    nd = len(shape)
    return pl.BlockSpec(shape, lambda *_: (0,) * nd, pipeline_mode=pl.Buffered(1))


def _peers():
    x, y, c = lax.axis_index("x"), lax.axis_index("y"), lax.axis_index("c")
    peers = []
    for k in range(1, N_DEV):
        px = 1 - x if k & 4 else x
        py = 1 - y if k & 2 else y
        pc = 1 - c if k & 1 else c
        peers.append(((px, py, pc), 4 * px + 2 * py + pc))
    return 4 * x + 2 * y + c, peers


class _Xfer(NamedTuple):
    src: jax.Array
    layer: Optional[int] = None
    scatter: bool = False

    @property
    def landing(self):
        block = self.src.shape if self.layer is None and not self.scatter else self.src.shape[1:]
        return _sds((N_DEV,) + block, self.src.dtype)


def _comm_copies(xfers, src_refs, dst_refs, send_sems, recv_sems, local_sems):
    me, peers = _peers()

    def src(t, to):
        ref = src_refs[t] if xfers[t].layer is None else src_refs[t].at[xfers[t].layer]
        return ref.at[to] if xfers[t].scatter else ref

    copies = [pltpu.make_async_copy(src(t, me), dst_refs[t].at[me], local_sems.at[t]) for t in range(len(xfers))]
    for k, (dev, idx) in enumerate(peers):
        for t in range(len(xfers)):
            copies.append(pltpu.make_async_remote_copy(
                src_ref=src(t, idx), dst_ref=dst_refs[t].at[me], send_sem=send_sems.at[t, k], recv_sem=recv_sems.at[t, k],
                device_id=dev, device_id_type=MESH))
    return copies


def _pcall(body, *, name, grid, in_specs, out_specs, out_shape, args, scratch=(), ride=()):
    n_in, n_out, n_scr, nx = len(in_specs), len(out_specs), len(scratch), len(ride)
    params = _params(("arbitrary",) * len(grid))
    if not ride:
        res = pl.pallas_call(body, name=name, grid=grid, in_specs=list(in_specs), out_specs=list(out_specs),
                             out_shape=list(out_shape), scratch_shapes=list(scratch), compiler_params=params,
                             interpret=False)(*args)
        return list(res), []
    def riding(*refs):
        o0 = n_in + nx
        s0 = o0 + n_out + nx
        comm = (ride, refs[n_in:o0], refs[o0 + n_out:s0], *refs[s0 + n_scr:])
        ids = [pl.program_id(a) for a in range(len(grid))]
        first = functools.reduce(jnp.logical_and, [i == 0 for i in ids])
        last = functools.reduce(jnp.logical_and, [i == n - 1 for i, n in zip(ids, grid)])

        @pl.when(first)
        def _():
            for cp in _comm_copies(*comm):
                cp.start()

        body(*refs[:n_in], *refs[o0:o0 + n_out], *refs[s0:s0 + n_scr])

        @pl.when(last)
        def _():
            for cp in _comm_copies(*comm):
                cp.wait()

    sems = [pltpu.SemaphoreType.DMA((nx, N_DEV - 1)), pltpu.SemaphoreType.DMA((nx, N_DEV - 1)), pltpu.SemaphoreType.DMA((nx,))]
    res = pl.pallas_call(riding, name=name, grid=grid, in_specs=list(in_specs) + [ANY] * nx,
                         out_specs=list(out_specs) + [ANY] * nx, out_shape=list(out_shape) + [t.landing for t in ride],
                         scratch_shapes=list(scratch) + sems, compiler_params=params,
                         interpret=False)(*args, *[t.src for t in ride])
    return list(res[:n_out]), list(res[n_out:])


def _comm_call(xfers, name):
    return _pcall(lambda: None, name=name, grid=(1,), in_specs=[], out_specs=[], out_shape=[], args=[], ride=xfers)[1]


def _gather_two_level(xfers, name):
    n = len(xfers)
    assert not any(t.scatter for t in xfers)

    def body(*refs):
        ins, outs = refs[:n], refs[n:2 * n]
        send_sems, recv_sems, local_sems = refs[2 * n:]
        x, y, c = lax.axis_index("x"), lax.axis_index("y"), lax.axis_index("c")
        sibling = (x, y, 1 - c)
        chips = [(1 - x, y), (x, 1 - y), (1 - x, 1 - y)]
        slot = lambda dev: 4 * dev[0] + 2 * dev[1] + dev[2]

        def copy(t, k, block, to, own=False):
            src = ins[t] if xfers[t].layer is None else ins[t].at[xfers[t].layer]
            return pltpu.make_async_remote_copy(
                src_ref=src if own else outs[t].at[slot(block)], dst_ref=outs[t].at[slot(block)],
                send_sem=send_sems.at[t, k], recv_sem=recv_sems.at[t, k], device_id=to, device_id_type=MESH)

        me = (x, y, c)
        local, sent = [], []
        for t in range(n):
            src = ins[t] if xfers[t].layer is None else ins[t].at[xfers[t].layer]
            mine = pltpu.make_async_copy(src, outs[t].at[slot(me)], local_sems.at[t])
            mine.start()
            local.append(mine)
            first = [copy(t, 0, me, sibling, own=True)] + [copy(t, 1 + j, me, (*chip, c), own=True) for j, chip in enumerate(chips)]
            for cp in first:
                cp.start()
            sent += first
        for t in range(n):
            for j, chip in enumerate(chips):
                copy(t, 1 + j, (*chip, c), me).wait_recv()
                passed = copy(t, 4 + j, (*chip, c), sibling)
                passed.start()
                sent.append(passed)
        for t in range(n):
            copy(t, 0, sibling, me).wait_recv()
            for j, chip in enumerate(chips):
                copy(t, 4 + j, (*chip, 1 - c), me).wait_recv()
        for cp in sent:
            cp.wait_send()
        for cp in local:
            cp.wait()

    return pl.pallas_call(
        body, name=name, in_specs=[ANY] * n, out_specs=[ANY] * n, out_shape=[t.landing for t in xfers],
        scratch_shapes=[pltpu.SemaphoreType.DMA((n, N_DEV - 1)), pltpu.SemaphoreType.DMA((n, N_DEV - 1)),
                        pltpu.SemaphoreType.DMA((n,))],
        interpret=False,
    )(*[t.src for t in xfers])


def _ffn_fwd(h, gpre, gpost, wg, wu, wd, name, ride=()):
    S = h.shape[0]
    tm = _tile(S, FFN_FWD_TILE)
    nb = FFN_CHUNK

    def body(h_ref, gpre_ref, gpost_ref, wgt_ref, wut_ref, wd_ref, hn_ref, n_ref, gate_ref, up_ref, act_ref, f_ref):
        hv = h_ref[...]
        n = _rms(hv, gpre_ref[...]).astype(BF16)
        n_ref[...] = n
        for c in range(D_FF // nb):
            cols = slice(c * nb, (c + 1) * nb)
            g = _dot_nt(n, wgt_ref[cols, :])
            u = _dot_nt(n, wut_ref[cols, :])
            gate_ref[:, cols] = g.astype(BF16)
            up_ref[:, cols] = u.astype(BF16)
            act_ref[:, cols] = (g * jax.nn.sigmoid(g) * u).astype(BF16)
        f = _dot(act_ref[...], wd_ref[...])
        f_ref[...] = f
        hn_ref[...] = hv + 0.5 * _rms(f, gpost_ref[...])

    return _pcall(
        body, name=name, grid=(S // tm,),
        in_specs=[_rows(tm, D_MODEL), _full((1, D_MODEL)), _full((1, D_MODEL)),
                  _resident((D_FF, D_MODEL)), _resident((D_FF, D_MODEL)), _resident((D_FF, D_MODEL))],
        out_specs=[_rows(tm, D_MODEL), _rows(tm, D_MODEL), _rows(tm, D_FF), _rows(tm, D_FF), _rows(tm, D_FF),
                   _rows(tm, D_MODEL)],
        out_shape=[_sds((S, D_MODEL), F32), _sds((S, D_MODEL), BF16), _sds((S, D_FF), BF16), _sds((S, D_FF), BF16),
                   _sds((S, D_FF), BF16), _sds((S, D_MODEL), F32)],
        args=(h, gpre, gpost, wg, wu, wd), ride=ride)


def _ffn_bwd(dho, h, f, gate, up, gpre, gpost, wg, wu, wd, name, ride=()):
    S = h.shape[0]
    tm = _tile(S, FFN_BWD_TILE)
    nb = FFN_CHUNK

    def body(dho_ref, h_ref, f_ref, gate_ref, up_ref, gpre_ref, gpost_ref, wgt_ref, wut_ref, wd_ref,
             dh_ref, df_ref, dgate_ref, dup_ref, dgpre_ref, dgpost_ref):
        first = pl.program_id(0) == 0
        dho_v = dho_ref[...]
        dfx, dgpost = _rms_bwd(f_ref[...], gpost_ref[...], 0.5 * dho_v)
        dfb = dfx.astype(BF16)
        df_ref[...] = dfb
        for c in range(D_FF // nb):
            cols = slice(c * nb, (c + 1) * nb)
            dact = _dot_nt(dfb, wd_ref[cols, :])
            g = gate_ref[:, cols].astype(F32)
            u = up_ref[:, cols].astype(F32)
            s = jax.nn.sigmoid(g)
            dgate_ref[:, cols] = (dact * u * (s * (1.0 + g * (1.0 - s)))).astype(BF16)
            dup_ref[:, cols] = (dact * (g * s)).astype(BF16)
        dn = _dot(dgate_ref[...], wgt_ref[...]) + _dot(dup_ref[...], wut_ref[...])
        dhx, dgpre = _rms_bwd(h_ref[...], gpre_ref[...], dn)
        dh_ref[...] = dho_v + dhx
        _acc_rows(dgpre_ref, dgpre, first)
        _acc_rows(dgpost_ref, dgpost, first)

    return _pcall(
        body, name=name, grid=(S // tm,),
        in_specs=[_rows(tm, D_MODEL), _rows(tm, D_MODEL), _rows(tm, D_MODEL), _rows(tm, D_FF), _rows(tm, D_FF),
                  _full((1, D_MODEL)), _full((1, D_MODEL)),
                  _resident((D_FF, D_MODEL)), _resident((D_FF, D_MODEL)), _resident((D_FF, D_MODEL))],
        out_specs=[_rows(tm, D_MODEL), _rows(tm, D_MODEL), _rows(tm, D_FF), _rows(tm, D_FF),
                   _full((1, D_MODEL)), _full((1, D_MODEL))],
        out_shape=[_sds((S, D_MODEL), F32), _sds((S, D_MODEL), BF16), _sds((S, D_FF), BF16), _sds((S, D_FF), BF16),
                   _sds((1, D_MODEL), F32), _sds((1, D_MODEL), F32)],
        args=(dho, h, f, gate, up, gpre, gpost, wg, wu, wd), ride=ride)


def _wgrad(a_parts, b_parts, split, name, ride=()):
    S = a_parts[0].shape[0]
    bk = _tile(S, WGRAD_TILE)
    ms = [a.shape[1] for a in a_parts]
    ns = [b.shape[1] for b in b_parts]
    M, N = sum(ms), sum(ns)
    na, nbp = len(a_parts), len(b_parts)
    blk = (M // N_DEV, N) if split == "rows" else (M, N // N_DEV)
    steps = S // bk

    def body(*refs):
        a_refs, b_refs = refs[:na], refs[na:na + nbp]
        out_ref, acc_ref = refs[-2], refs[-1]
        k = pl.program_id(0)

        @pl.when(k == 0)
        def _():
            acc_ref[...] = jnp.zeros_like(acc_ref)

        r0 = 0
        for ai in range(na):
            av = a_refs[ai][...]
            c0 = 0
            for bi in range(nbp):
                acc_ref[r0:r0 + ms[ai], c0:c0 + ns[bi]] += _dot_tn(av, b_refs[bi][...])
                c0 += ns[bi]
            r0 += ms[ai]

        @pl.when(k == steps - 1)
        def _():
            for d in range(N_DEV):
                if split == "rows":
                    out_ref[d] = acc_ref[d * blk[0]:(d + 1) * blk[0], :].astype(GRAD_WIRE)
                else:
                    out_ref[d] = acc_ref[:, d * blk[1]:(d + 1) * blk[1]].astype(GRAD_WIRE)

    in_specs = [pl.BlockSpec((bk, m), lambda k: (k, 0)) for m in ms] + [pl.BlockSpec((bk, n), lambda k: (k, 0)) for n in ns]
    (out,), landed = _pcall(
        body, name=name, grid=(steps,), in_specs=in_specs, out_specs=[_full((N_DEV,) + blk)],
        out_shape=[_sds((N_DEV,) + blk, GRAD_WIRE)], scratch=[pltpu.VMEM((M, N), F32)],
        args=list(a_parts) + list(b_parts), ride=ride)
    return out, landed


def _mix_in_pool(h, g1, w_in, pool_w, pool_scale, name, ride=()):
    S = h.shape[0]
    tm = _tile(S, ROW_TILE)
    kb = D_MODEL // N_DEV

    def body(h_ref, g_ref, w_ref, pw_ref, ps_ref, u_ref, dpre_ref, tok_ref, qm_ref, ext_ref):
        i = pl.program_id(0)
        u = _rms(h_ref[...], g_ref[...]).astype(BF16)
        u_ref[...] = u
        proj = _dot(u, w_ref[...])
        qm_ref[...] = proj[:, D_TOK:].astype(BF16)
        x = proj[:, :D_TOK]

        @pl.when(i == 0)
        def _():
            ext_ref[0:POOL_HALO, :] = jnp.zeros((POOL_HALO, D_TOK), F32)

        ext_ref[POOL_HALO:, :] = x
        pos = i * tm + lax.broadcasted_iota(jnp.int32, (tm, 1), 0)
        for gi, w in enumerate(POOL_WINDOWS):
            cols = slice(gi * POOL_GROUP, (gi + 1) * POOL_GROUP)
            xs = x[:, cols]
            wsum = xs
            for k in range(1, w):
                wsum = wsum + ext_ref[POOL_HALO - k:POOL_HALO - k + tm, cols]
            cnt = jnp.minimum(pos + 1, w).astype(F32)
            dg = (wsum / cnt - xs).astype(BF16)
            dpre_ref[:, cols] = dg
            yv = _dot(dg, pw_ref[gi].astype(BF16))
            tok_ref[:, cols] = (yv * ps_ref[:, cols]).astype(BF16)
        ext_ref[0:POOL_HALO, :] = x[tm - POOL_HALO:, :]

    return _pcall(
        body, name=name, grid=(S // tm,),
        in_specs=[_rows(tm, D_MODEL), _full((1, D_MODEL)), _resident((D_MODEL, D_MIX)),
                  _full((len(POOL_WINDOWS), POOL_GROUP, POOL_GROUP)), _full((1, D_TOK))],
        out_specs=[_rows(tm, D_MODEL), _rows(tm, D_TOK), _rows(tm, D_TOK), _rows(tm, D_MEMH)],
        out_shape=[_sds((S, D_MODEL), BF16), _sds((S, D_TOK), BF16), _sds((S, D_TOK), BF16), _sds((S, D_MEMH), BF16)],
        scratch=[pltpu.VMEM((POOL_HALO + tm, D_TOK), F32)],
        args=(h, g1, w_in, pool_w, pool_scale), ride=ride)


def _pool_bwd(dtok, dpre, pool_w, pool_scale, name, ride=()):
    S = dtok.shape[0]
    tm = _tile(S, ROW_TILE)
    nt = S // tm
    ng = len(POOL_WINDOWS)

    def body(dtok_ref, dpre_ref, pw_ref, ps_ref, dx_ref, dpw_ref, dps_ref, ext_ref):
        i = pl.program_id(0)
        first = i == 0
        t0 = (nt - 1 - i) * tm
        pos = t0 + lax.broadcasted_iota(jnp.int32, (tm, 1), 0)

        @pl.when(first)
        def _():
            ext_ref[tm:, :] = jnp.zeros((POOL_HALO, D_TOK), F32)

        dps = []
        for gi, w in enumerate(POOL_WINDOWS):
            cols = slice(gi * POOL_GROUP, (gi + 1) * POOL_GROUP)
            dg = dpre_ref[:, cols]
            pw = pw_ref[gi].astype(BF16)
            dt = dtok_ref[:, cols].astype(F32)
            yv = _dot(dg, pw)
            dps.append(jnp.sum(dt * yv, axis=0, keepdims=True))
            dy = (dt * ps_ref[:, cols]).astype(BF16)
            _acc_rows(dpw_ref.at[gi], _dot_tn(dg, dy), first)
            dd = _dot_nt(dy, pw)
            cnt = jnp.minimum(pos + 1, w).astype(F32)
            ext_ref[0:tm, cols] = dd / cnt
            wsum = ext_ref[0:tm, cols]
            for k in range(1, w):
                wsum = wsum + ext_ref[k:k + tm, cols]
            dx_ref[:, cols] = (wsum - dd).astype(BF16)
        _acc_rows(dps_ref, jnp.concatenate(dps, axis=1), first)
        ext_ref[tm:, :] = ext_ref[0:POOL_HALO, :]

    rev = lambda i: (nt - 1 - i, 0)
    return _pcall(
        body, name=name, grid=(nt,),
        in_specs=[pl.BlockSpec((tm, D_TOK), rev), pl.BlockSpec((tm, D_TOK), rev),
                  _full((ng, POOL_GROUP, POOL_GROUP)), _full((1, D_TOK))],
        out_specs=[pl.BlockSpec((tm, D_TOK), rev), _full((ng, POOL_GROUP, POOL_GROUP)), _full((1, D_TOK))],
        out_shape=[_sds((S, D_TOK), BF16), _sds((ng, POOL_GROUP, POOL_GROUP), F32), _sds((1, D_TOK), F32)],
        scratch=[pltpu.VMEM((tm + POOL_HALO, D_TOK), F32)],
        args=(dtok, dpre, pool_w, pool_scale), ride=ride)


def _mix_in_sb(h, g1, wt, name):
    S = h.shape[0]
    tm = _tile(S, ROW_TILE)
    cb = 256

    def body(h_ref, g_ref, wt_ref, u_ref, proj_ref):
        u = _rms(h_ref[...], g_ref[...]).astype(BF16)
        u_ref[...] = u
        for c in range(D_SB // cb):
            proj_ref[:, c * cb:(c + 1) * cb] = _dot_nt(u, wt_ref[c * cb:(c + 1) * cb, :]).astype(BF16)

    return pl.pallas_call(
        body, name=name, grid=(S // tm,),
        in_specs=[_rows(tm, D_MODEL), _full((1, D_MODEL)), _resident((D_SB, D_MODEL))],
        out_specs=[_rows(tm, D_MODEL), _rows(tm, D_SB)],
        out_shape=[_sds((S, D_MODEL), BF16), _sds((S, D_SB), BF16)],
        compiler_params=_params(("arbitrary",)),
        interpret=False,
    )(h, g1, wt)


def _mix_in_bwd(dho, h, g1, parts, w, mode, name, ride=()):
    S = h.shape[0]
    tm = _tile(S, ROW_TILE)
    widths = [p.shape[1] for p in parts]
    npart = len(parts)
    kb = D_MODEL // N_DEV

    def body(*refs):
        dho_ref, h_ref, g_ref = refs[:3]
        p_refs = refs[3:3 + npart]
        w_ref, dh_ref, dg_ref = refs[3 + npart:]
        first = pl.program_id(0) == 0
        dproj = jnp.concatenate([p[...] for p in p_refs], axis=1)
        du = _dot_nt(dproj, w_ref[...]) if mode == "pool" else _dot(dproj, w_ref[...])
        dhx, dg = _rms_bwd(h_ref[...], g_ref[...], du)
        dh_ref[...] = dho_ref[...] + dhx
        _acc_rows(dg_ref, dg, first)

    w_spec = _resident((D_MODEL, D_MIX)) if mode == "pool" else _resident((D_SB, D_MODEL))
    return _pcall(
        body, name=name, grid=(S // tm,),
        in_specs=[_rows(tm, D_MODEL), _rows(tm, D_MODEL), _full((1, D_MODEL))] + [_rows(tm, wd_) for wd_ in widths] + [w_spec],
        out_specs=[_rows(tm, D_MODEL), _full((1, D_MODEL))],
        out_shape=[_sds((S, D_MODEL), F32), _sds((1, D_MODEL), F32)],
        args=(dho, h, g1, *parts, w), ride=ride)


def _sb_block(qcats, kb, mask, later, tri_later):
    z = jnp.concatenate([_dot_nt(qc, _group(kb, g)) for g, qc in enumerate(qcats)], axis=0) * ATT_SCALE
    en = jnp.exp(-jnp.abs(z))
    ls = jnp.minimum(z, 0.0) - jnp.log(1.0 + en)
    lf = ls - z if mask is None else jnp.where(mask, ls - z, 0.0)
    within = _split_dot(lf, tri_later, 2)
    a = jnp.exp(ls + within + later)
    if mask is not None:
        a = jnp.where(mask, a, 0.0)
    return z, en, a, jnp.sum(lf, axis=1, keepdims=True)


def _sb_causal(rows):
    row = jnp.bitwise_and(lax.broadcasted_iota(jnp.int32, (rows, Q_BLOCK), 0), Q_BLOCK - 1)
    return lax.broadcasted_iota(jnp.int32, (rows, Q_BLOCK), 1) < row


def _sb_walk(qi, block, state, later_of, unrolled):
    if unrolled:
        state = block(qi, _sb_causal, state)
        state = block(jnp.maximum(qi - 1, 0), qi >= 1, state)

    def step(carry):
        j, _, state = carry
        mask = None if unrolled else lambda rows: jnp.logical_or(_sb_causal(rows), j > 0)
        state = block(qi - j, mask, state)
        return j + 1, _sb_alive(later_of(state)), state

    first = jnp.int32(2 if unrolled else 0)
    return lax.while_loop(functools.partial(_sb_more, qi), step, (first, _sb_alive(later_of(state)), state))[2]


def _sb_alive(later):
    return jnp.max(later) > SB_DEAD_LOG_WEIGHT


def _sb_more(qi, carry):
    return jnp.logical_and(carry[0] <= qi, carry[1])


def _tri(strict):
    row = lax.broadcasted_iota(jnp.int32, (Q_BLOCK, Q_BLOCK), 0)
    col = lax.broadcasted_iota(jnp.int32, (Q_BLOCK, Q_BLOCK), 1)
    return (row > col if strict else row >= col).astype(BF16)


HEADS_PER_GROUP = 128 // HEAD_DIM
GROUP_ROWS = HEADS_PER_GROUP * Q_BLOCK


def _sb_head_masks():
    lane = lax.broadcasted_iota(jnp.int32, (Q_BLOCK, 128), 1)
    return [(lane >= e * HEAD_DIM) & (lane < (e + 1) * HEAD_DIM) for e in range(HEADS_PER_GROUP)]


def _group(x, g):
    return x[:, g * 128:(g + 1) * 128]


def _masked(hm, x):
    return jnp.where(hm, x, jnp.zeros_like(x))


def _stack_heads(hms, x):
    return jnp.concatenate([_masked(hm, x) for hm in hms], axis=0)


def _own_lanes(hms, r, rows=Q_BLOCK):
    return sum(_masked(hm, r[e * rows:(e + 1) * rows]) for e, hm in enumerate(hms))


def _sb_fwd(proj, name, ride=()):
    S = proj.shape[0]
    nq = S // Q_BLOCK
    W = SB_FWD_LANES
    nrow = D_TOK // W
    groups = W // 128

    def body(q_ref, k_ref, v_ref, o_ref, tok_ref):
        qi = pl.program_id(1)
        hms = _sb_head_masks()
        tri_later = _tri(True)
        q = q_ref[...]
        qcats = [_stack_heads(hms, _group(q, g)) for g in range(groups)]

        def block(kblock, mask, state):
            accs, later = state
            if callable(mask):
                mask = mask(later.shape[0])
            off = pl.multiple_of(kblock * Q_BLOCK, Q_BLOCK)
            kb = k_ref[pl.ds(off, Q_BLOCK), :]
            vb = v_ref[pl.ds(off, Q_BLOCK), :]
            _, _, a, bsum = _sb_block(qcats, kb, mask, later, tri_later)
            hi = a.astype(BF16)
            lo = (a - hi.astype(F32)).astype(BF16)
            accs = list(accs)
            for g in range(groups):
                rows = slice(g * GROUP_ROWS, (g + 1) * GROUP_ROWS)
                r = _dot(jnp.concatenate([hi[rows], lo[rows]], axis=0), _group(vb, g))
                accs[g] = accs[g] + _own_lanes(hms, r[:GROUP_ROWS] + r[GROUP_ROWS:])
            return tuple(accs), later + bsum

        init = ((jnp.zeros((Q_BLOCK, 128), F32),) * groups, jnp.zeros((groups * GROUP_ROWS, 1), F32))
        accs = _sb_walk(qi, block, init, lambda state: state[1], unrolled=True)[0]
        for g, acc in enumerate(accs):
            o_ref[:, g * 128:(g + 1) * 128] = acc
            tok_ref[:, g * 128:(g + 1) * 128] = acc.astype(BF16)

    blk = pl.BlockSpec((Q_BLOCK, W), lambda p, i: (i, p))
    return _pcall(
        body, name=name, grid=(nrow, nq),
        in_specs=[blk, pl.BlockSpec((S, W), lambda p, i: (0, nrow + p), pipeline_mode=pl.Buffered(1)),
                  pl.BlockSpec((S, W), lambda p, i: (0, 2 * nrow + p), pipeline_mode=pl.Buffered(1))],
        out_specs=[blk, blk],
        out_shape=[_sds((S, D_TOK), F32), _sds((S, D_TOK), BF16)],
        args=(proj, proj, proj), ride=ride)


def _sb_bwd(proj, dtok, o32, name, ride=()):
    S = proj.shape[0]
    nq = S // Q_BLOCK
    W = SB_BWD_LANES
    nrow = D_TOK // W
    groups = W // 128

    def body(q_ref, k_ref, v_ref, do_ref, o_ref, dq_ref, dk_ref, dv_ref, dk_acc, dv_acc):
        qi = pl.program_id(1)

        @pl.when(qi == 0)
        def _():
            dk_acc[...] = jnp.zeros_like(dk_acc)
            dv_acc[...] = jnp.zeros_like(dv_acc)

        hms = _sb_head_masks()
        tri_later = _tri(True)
        tri_from = _tri(False)
        q = q_ref[...]
        do = do_ref[...]
        dov = do.astype(F32) * o_ref[...]
        qcats = [_stack_heads(hms, _group(q, g)) for g in range(groups)]
        docats = [_stack_heads(hms, _group(do, g)) for g in range(groups)]
        rowtot = jnp.concatenate([jnp.sum(jnp.where(hm, _group(dov, g), 0.0), axis=1, keepdims=True)
                                  for g in range(groups) for hm in hms], axis=0)

        def block(kblock, mask, state):
            dqs, later, seen = state
            if callable(mask):
                mask = mask(later.shape[0])
            off = pl.multiple_of(kblock * Q_BLOCK, Q_BLOCK)
            kb = k_ref[pl.ds(off, Q_BLOCK), :]
            vb = v_ref[pl.ds(off, Q_BLOCK), :]
            z, en, a, bsum = _sb_block(qcats, kb, mask, later, tri_later)
            inv = 1.0 / (1.0 + en)
            beta = jnp.where(z >= 0, 1.0, en) * inv
            omb = jnp.where(z >= 0, en, 1.0) * inv
            dlogw = a * jnp.concatenate([_dot_nt(docats[g], _group(vb, g)) for g in range(groups)], axis=0)
            prefix = rowtot - seen - _split_dot(dlogw, tri_from, 2)
            dz = dlogw * omb - beta * prefix
            if mask is not None:
                dz = jnp.where(mask, dz, 0.0)
            dz = dz.astype(BF16)
            ab = a.astype(BF16)
            dqs = list(dqs)
            for g in range(groups):
                rows = slice(g * GROUP_ROWS, (g + 1) * GROUP_ROWS)
                lanes = slice(g * 128, (g + 1) * 128)
                dqs[g] = dqs[g] + _own_lanes(hms, _dot(dz[rows], _group(kb, g)))
                dk_acc[pl.ds(off, Q_BLOCK), lanes] += _dot_tn(dz[rows], qcats[g]) * ATT_SCALE
                dv_acc[pl.ds(off, Q_BLOCK), lanes] += _dot_tn(ab[rows], docats[g])
            return tuple(dqs), later + bsum, seen + jnp.sum(dlogw, axis=1, keepdims=True)

        zero = jnp.zeros((groups * GROUP_ROWS, 1), F32)
        init = ((jnp.zeros((Q_BLOCK, 128), F32),) * groups, zero, zero)
        dqs = _sb_walk(qi, block, init, lambda state: state[1], unrolled=False)[0]
        for g, dq in enumerate(dqs):
            dq_ref[:, g * 128:(g + 1) * 128] = (dq * ATT_SCALE).astype(BF16)

        @pl.when(qi == nq - 1)
        def _():
            dk_ref[...] = dk_acc[...].astype(BF16)
            dv_ref[...] = dv_acc[...].astype(BF16)

    blk = pl.BlockSpec((Q_BLOCK, W), lambda p, i: (i, p))
    col = pl.BlockSpec((S, W), lambda p, i: (0, p))
    return _pcall(
        body, name=name, grid=(nrow, nq),
        in_specs=[blk, pl.BlockSpec((S, W), lambda p, i: (0, nrow + p)), pl.BlockSpec((S, W), lambda p, i: (0, 2 * nrow + p)),
                  blk, blk],
        out_specs=[blk, col, col],
        out_shape=[_sds((S, D_TOK), BF16), _sds((S, D_TOK), BF16), _sds((S, D_TOK), BF16)],
        scratch=[pltpu.VMEM((S, W), F32), pltpu.VMEM((S, W), F32)],
        args=(proj, proj, proj, dtok, o32), ride=ride)


def _mem_kv_fwd(mem, g_mem, w_kv, name):
    lm = mem.shape[0]
    kb = D_MODEL // N_DEV

    def body(mem_ref, g_ref, w_ref, mn_ref, km_ref, vm_ref):
        mn = _rms(mem_ref[...], g_ref[...]).astype(BF16)
        mn_ref[...] = mn
        kv = jnp.zeros((lm, 2 * D_MEMH), F32)
        for d in range(N_DEV):
            kv = kv + _dot(mn[:, d * kb:(d + 1) * kb], w_ref[d])
        km_ref[...] = kv[:, :D_MEMH].astype(BF16)
        vm_ref[...] = kv[:, D_MEMH:].astype(BF16)

    return pl.pallas_call(
        body, name=name, grid=(1,),
        in_specs=[_full((lm, D_MODEL)), _full((1, D_MODEL)), _full((N_DEV, kb, 2 * D_MEMH))],
        out_specs=[_full((lm, D_MODEL)), _full((lm, D_MEMH)), _full((lm, D_MEMH))],
        out_shape=[_sds((lm, D_MODEL), BF16), _sds((lm, D_MEMH), BF16), _sds((lm, D_MEMH), BF16)],
        compiler_params=_params(("arbitrary",)),
        interpret=False,
    )(mem, g_mem, w_kv)


def _mem_kv_bwd(dkm, dvm, mem, g_mem, mem_n, w_kv, name):
    lm = mem.shape[0]
    kb = D_MODEL // N_DEV

    def body(dkm_ref, dvm_ref, mem_ref, g_ref, mn_ref, w_ref, dw_ref, dg_ref):
        dkv = jnp.concatenate([dkm_ref[...], dvm_ref[...]], axis=1).astype(BF16)
        dw = _dot_tn(mn_ref[...], dkv)
        for d in range(N_DEV):
            dw_ref[d] = dw[d * kb:(d + 1) * kb, :].astype(GRAD_WIRE)
        dmn = jnp.concatenate([_dot_nt(dkv, w_ref[d]) for d in range(N_DEV)], axis=1)
        _, dg = _rms_bwd(mem_ref[...], g_ref[...], dmn)
        dg_ref[...] = dg

    return pl.pallas_call(
        body, name=name, grid=(1,),
        in_specs=[_full((lm, D_MEMH)), _full((lm, D_MEMH)), _full((lm, D_MODEL)), _full((1, D_MODEL)),
                  _full((lm, D_MODEL)), _full((N_DEV, kb, 2 * D_MEMH))],
        out_specs=[_full((N_DEV, kb, 2 * D_MEMH)), _full((1, D_MODEL))],
        out_shape=[_sds((N_DEV, kb, 2 * D_MEMH), GRAD_WIRE), _sds((1, D_MODEL), F32)],
        compiler_params=_params(("arbitrary",)),
        interpret=False,
    )(dkm, dvm, mem, g_mem, mem_n, w_kv)


def _mem_heads(tm):
    lane = lax.broadcasted_iota(jnp.int32, (tm, D_MEMH), 1)
    return [(lane >= e * HEAD_DIM) & (lane < (e + 1) * HEAD_DIM) for e in range(D_MEMH // HEAD_DIM)]


def _softmax(s):
    m = jnp.max(s, axis=-1, keepdims=True)
    p = jnp.exp(s - m)
    return p / jnp.sum(p, axis=-1, keepdims=True)


def _mix_out_fwd(h, tok, qm, qm_col, km, vm, w_out, gpost, name, ride=()):
    S = h.shape[0]
    tm = _tile(S, ROW_TILE)
    lm = km.shape[0]
    nb = D_MODEL // N_DEV

    def body(h_ref, tok_ref, qm_ref, km_ref, vm_ref, w_ref, g_ref, hn_ref, mo_ref, mix_ref):
        qv = qm_ref[...]
        kv, vv = km_ref[...], vm_ref[...]
        hms = _mem_heads(tm)
        p = _softmax(_dot_nt(_stack_heads(hms, qv), kv) * ATT_SCALE)
        mob = _own_lanes(hms, _dot(p.astype(BF16), vv), tm).astype(BF16)
        mo_ref[...] = mob
        mix = _dot_nt(jnp.concatenate([tok_ref[...], mob], axis=1), w_ref[...])
        mix_ref[...] = mix
        hn_ref[...] = h_ref[...] + _rms(mix, g_ref[...])

    return _pcall(
        body, name=name, grid=(S // tm,),
        in_specs=[_rows(tm, D_MODEL), _rows(tm, D_TOK), _rows(tm, D_MEMH, qm_col), _full((lm, D_MEMH)), _full((lm, D_MEMH)),
                  _resident((D_MODEL, D_MIX)), _full((1, D_MODEL))],
        out_specs=[_rows(tm, D_MODEL), _rows(tm, D_MEMH), _rows(tm, D_MODEL)],
        out_shape=[_sds((S, D_MODEL), F32), _sds((S, D_MEMH), BF16), _sds((S, D_MODEL), F32)],
        args=(h, tok, qm, km, vm, w_out, gpost), ride=ride)


def _mix_out_bwd(dho, mix, qm, qm_col, km, vm, w_out, gpost, name, ride=()):
    S = dho.shape[0]
    tm = _tile(S, ROW_TILE)
    lm = km.shape[0]
    nb = D_MODEL // N_DEV

    def body(dho_ref, mix_ref, qm_ref, km_ref, vm_ref, w_ref, g_ref,
             dmix_ref, dtok_ref, dqm_ref, dkm_ref, dvm_ref, dg_ref):
        first = pl.program_id(0) == 0
        dmx, dg = _rms_bwd(mix_ref[...], g_ref[...], dho_ref[...])
        dmb = dmx.astype(BF16)
        dmix_ref[...] = dmb
        _acc_rows(dg_ref, dg, first)
        dcat = _dot(dmb, w_ref[...])
        dtok_ref[...] = dcat[:, :D_TOK].astype(BF16)
        dmo = dcat[:, D_TOK:].astype(BF16)
        qv = qm_ref[...]
        kv, vv = km_ref[...], vm_ref[...]
        hms = _mem_heads(tm)
        qcat = _stack_heads(hms, qv)
        dmcat = _stack_heads(hms, dmo)
        p = _softmax(_dot_nt(qcat, kv) * ATT_SCALE)
        dp = _dot_nt(dmcat, vv)
        ds = (p * (dp - jnp.sum(p * dp, axis=-1, keepdims=True))).astype(BF16)
        dq = _own_lanes(hms, _dot(ds, kv), tm)
        dk = _dot_tn(ds, qcat)
        dv = _dot_tn(p.astype(BF16), dmcat)
        dqm_ref[...] = (dq * ATT_SCALE).astype(BF16)
        _acc_rows(dkm_ref, dk * ATT_SCALE, first)
        _acc_rows(dvm_ref, dv, first)

    return _pcall(
        body, name=name, grid=(S // tm,),
        in_specs=[_rows(tm, D_MODEL), _rows(tm, D_MODEL), _rows(tm, D_MEMH, qm_col), _full((lm, D_MEMH)), _full((lm, D_MEMH)),
                  _resident((D_MODEL, D_MIX)), _full((1, D_MODEL))],
        out_specs=[_rows(tm, D_MODEL), _rows(tm, D_TOK), _rows(tm, D_MEMH), _full((lm, D_MEMH)), _full((lm, D_MEMH)),
                   _full((1, D_MODEL))],
        out_shape=[_sds((S, D_MODEL), BF16), _sds((S, D_TOK), BF16), _sds((S, D_MEMH), BF16),
                   _sds((lm, D_MEMH), F32), _sds((lm, D_MEMH), F32), _sds((1, D_MODEL), F32)],
        args=(dho, mix, qm, km, vm, w_out, gpost), ride=ride)


def _loss_head(y, target, name):
    S = y.shape[0]
    tm = _tile(S, ROW_TILE)
    nt = S // tm

    def body(y_ref, t_ref, dy_ref, loss_ref, acc_ref):
        i = pl.program_id(0)
        e = y_ref[...] - t_ref[...]
        dy_ref[...] = e * (1.0 / D_MODEL)
        _acc_rows(acc_ref, jnp.sum(e * e, axis=0, keepdims=True), i == 0)

        @pl.when(i == nt - 1)
        def _():
            tot = jnp.sum(acc_ref[...], axis=1, keepdims=True) * (0.5 / D_MODEL)
            loss_ref[...] = jnp.broadcast_to(tot, (1, 128))

    return pl.pallas_call(
        body, name=name, grid=(nt,),
        in_specs=[_rows(tm, D_MODEL), _rows(tm, D_MODEL)],
        out_specs=[_rows(tm, D_MODEL), _full((1, 128))],
        out_shape=[_sds((S, D_MODEL), F32), _sds((1, 128), F32)],
        scratch_shapes=[pltpu.VMEM((1, D_MODEL), F32)],
        compiler_params=_params(("arbitrary",)),
        interpret=False,
    )(y, target)


def _adamw(recv, w, m, v, l, into, name):
    L, R, C = w.shape
    tr = R if R * C <= ADAM_TILE_ELEMS else _tile(R, ADAM_TILE_ELEMS // C)
    c1 = 1.0 - ADAM_B1 ** ADAM_STEP
    c2 = 1.0 - ADAM_B2 ** ADAM_STEP

    def body(r_ref, w_ref, m_ref, v_ref, *rest):
        g_ref, d_ref, nm_ref, nv_ref = rest[-4:]
        g = r_ref[0].astype(F32)
        for s in range(1, N_DEV):
            g = g + r_ref[s].astype(F32)
        g_ref[...] = g
        nm = ADAM_B1 * m_ref[...] + (1.0 - ADAM_B1) * g
        nv = ADAM_B2 * v_ref[...] + (1.0 - ADAM_B2) * (g * g)
        nm_ref[...] = nm
        nv_ref[...] = nv
        d_ref[...] = -ADAM_LR * ((nm / c1) / (jnp.sqrt(nv / c2) + ADAM_EPS) + ADAM_WD * w_ref[...])

    t = pl.BlockSpec((None, tr, C), lambda i: (l, i, 0))
    kept = [] if into is None else list(into)
    return pl.pallas_call(
        body, name=name, grid=(R // tr,),
        in_specs=[pl.BlockSpec((N_DEV, tr, C), lambda i: (0, i, 0)), t, t, t] + [ANY] * len(kept),
        out_specs=[t, t, t, t],
        out_shape=[_sds((L, R, C), F32)] * 4,
        input_output_aliases={4 + q: q for q in range(len(kept))},
        compiler_params=_params(("arbitrary",)),
        interpret=False,
    )(recv, w, m, v, *kept)


def _step(p, opt_m, opt_v, x, mem, target):
    bf = lambda a: a.astype(BF16)
    row = lambda a: a.reshape(1, -1)
    tsb = lambda a: jnp.swapaxes(a, 1, 2)
    g_mem = p["g_mem"]

    wsb_t = tsb(p["w_in_sb"])
    travels_transposed = ("w_in_sb", "w_out", "ffn1_gate", "ffn1_up", "ffn2_gate", "ffn2_up")
    shard = {n: bf(tsb(p[n]) if n in travels_transposed else p[n]) for n in STACKED}
    ffn_weights = lambda which, i: [gw[k].reshape(D_FF, D_MODEL) for k in ffn(which, i)]
    w_in = lambda i: "w_in_pool" if i % 2 == 0 else "w_in_sb"
    ffn = lambda which, i: [(f"ffn{which}_{s}", i) for s in ("gate", "up", "down")]
    mixing = lambda i: [(w_in(i), i // 2), ("w_mem_kv", i), ("w_out", i)]

    gw = {}

    def gather(keys):
        return [_Xfer(shard[n], l) for n, l in keys]

    first = ffn(1, 0)
    landed = _gather_two_level(gather(first) + [_Xfer(p["g_pre"]), _Xfer(p["g_post"])], "gather_first")
    gw.update(zip(first, landed))
    unshard = lambda g: jnp.transpose(g, (1, 2, 0, 3)).reshape(DEPTH, 3, D_MODEL)
    g_pre, g_post = unshard(landed[-2]), unshard(landed[-1])

    def ahead(i):
        nxt = i + 1 < DEPTH
        if i % 2 == 0:
            start = i == 0
            return {"ffn1": ffn(2, i)[:2] + (mixing(0)[::2] if start else []), "mix_in": ffn(2, i)[2:] + (mixing(0)[1:2] if start else []),
                    "sb": [], "mix_out": ffn(1, i + 1)[:1] if nxt else [], "ffn2": ffn(1, i + 1)[1:] if nxt else []}
        return {"ffn1": mixing(i), "mix_in": [], "sb": ffn(2, i), "mix_out": ffn(1, i + 1)[:1] if nxt else [],
                "ffn2": ffn(1, i + 1)[1:] + mixing(i + 1) if nxt else []}

    def gathering(keys, call):
        res, landed = call(ride=gather(keys))
        gw.update(zip(keys, landed))
        return res

    saved = []
    h = x
    for i in range(DEPTH):
        j = i // 2
        st = {"h0": h}
        carry = ahead(i)
        h, st["n1"], st["gate1"], st["up1"], st["act1"], st["f1"] = gathering(carry["ffn1"], functools.partial(
            _ffn_fwd, h, row(g_pre[i, 0]), row(g_post[i, 0]), *ffn_weights(1, i), f"ffn1_fwd_{i}"))
        st["h1"] = h
        if i % 2 == 0:
            st["u"], st["dpre"], st["tok"], st["qm"] = gathering(carry["mix_in"], functools.partial(
                _mix_in_pool, h, row(g_pre[i, 1]), gw[("w_in_pool", j)].reshape(D_MODEL, D_MIX),p["pool_w"][j], row(p["pool_scale"][j]),
                f"mix_in_pool_{i}"))
            qm, qm_col = st["qm"], 0
        else:
            st["u"], st["proj"] = _mix_in_sb(h, row(g_pre[i, 1]), gw[("w_in_sb", j)].reshape(D_SB, D_MODEL), f"mix_in_sb_{i}")
            st["o32"], st["tok"] = gathering(carry["sb"], functools.partial(_sb_fwd, st["proj"], f"sb_fwd_{i}"))
            qm, qm_col = st["proj"], 3 * D_TOK // D_MEMH
        st["mem_n"], st["km"], st["vm"] = _mem_kv_fwd(mem, row(g_mem[i]), gw[("w_mem_kv", i)], f"mem_kv_fwd_{i}")
        h, st["mo"], st["mix"] = gathering(carry["mix_out"], functools.partial(
            _mix_out_fwd, h, st["tok"], qm, qm_col, st["km"], st["vm"], gw[("w_out", i)].reshape(D_MODEL, D_MIX), row(g_post[i, 1]), f"mix_out_fwd_{i}"))
        st["h2"] = h
        h, st["n2"], st["gate2"], st["up2"], st["act2"], st["f2"] = gathering(carry["ffn2"], functools.partial(
            _ffn_fwd, h, row(g_pre[i, 2]), row(g_post[i, 2]), *ffn_weights(2, i), f"ffn2_fwd_{i}"))
        saved.append(st)

    dh, loss_part = _loss_head(h, target, "loss_head")

    grads = {}
    recv = {}
    dg_pre = [[None] * 3 for _ in range(DEPTH)]
    dg_post = [[None] * 3 for _ in range(DEPTH)]
    dg_mem = [None] * DEPTH
    dpool_w = [None, None]
    dpool_scale = [None, None]

    def scatter(keys):
        return [_Xfer(grads[k], scatter=True) for k in keys]

    def ffn_backward(dh, st, i, which, hkey, slot, riding, last):
        sfx = str(which)
        keys = ffn(which, i)
        (dh, df, dgate, dup, dg_pre[i][slot], dg_post[i][slot]), landed = _ffn_bwd(
            dh, st[hkey], st["f" + sfx], st["gate" + sfx], st["up" + sfx], row(g_pre[i, slot]), row(g_post[i, slot]),
            *ffn_weights(which, i), f"ffn{sfx}_bwd_{i}", ride=scatter(riding))
        recv.update(zip(riding, landed))
        riders = [mixing(i), keys[:1], keys[1:2]] if last else [[], [], []]
        operands = ((st["n" + sfx], dgate, "cols"), (st["n" + sfx], dup, "cols"), (st["act" + sfx], df, "rows"))
        for key, (a, b, split), riding in zip(keys, operands, riders):
            grads[key], landed = _wgrad([a], [b], split, f"wgrad_{key[0]}_{i}", ride=scatter(riding))
            recv.update(zip(riding, landed))
        return dh

    def behind(i):
        prev = ffn(1, i + 1) if i + 1 < DEPTH else []
        none = {"mix_out": [], "pool": [], "mix_in": [], "sb": []}
        if i % 2 == 1:
            return {**none, "ffn2": prev, "sb": ffn(2, i), "ffn1": mixing(i)}
        if i > 0:
            return {**none, "ffn2": prev, "mix_out": ffn(2, i)[:1], "ffn1": ffn(2, i)[1:] + mixing(i)}
        return {**none, "ffn2": prev, "mix_out": ffn(2, 0)[:1], "pool": ffn(2, 0)[1:2], "mix_in": ffn(2, 0)[2:], "ffn1": []}

    for i in reversed(range(DEPTH)):
        j = i // 2
        st = saved[i]
        carry = behind(i)
        dh = ffn_backward(dh, st, i, 2, "h2", 2, carry["ffn2"], False)
        if i % 2 == 0:
            qm, qm_col = st["qm"], 0
        else:
            qm, qm_col = st["proj"], 3 * D_TOK // D_MEMH
        keys = carry["mix_out"]
        (dmix, dtok, dqm, dkm, dvm, dg_post[i][1]), landed = _mix_out_bwd(
            dh, st["mix"], qm, qm_col, st["km"], st["vm"], gw[("w_out", i)].reshape(D_MODEL, D_MIX), row(g_post[i, 1]), f"mix_out_bwd_{i}", ride=scatter(keys))
        recv.update(zip(keys, landed))
        grads[("w_out", i)], _ = _wgrad([st["tok"], st["mo"]], [dmix], "cols", f"wgrad_w_out_{i}")
        grads[("w_mem_kv", i)], dg_mem[i] = _mem_kv_bwd(dkm, dvm, mem, row(g_mem[i]), st["mem_n"], gw[("w_mem_kv", i)],
                                                        f"mem_kv_bwd_{i}")
        if i % 2 == 0:
            (dx, dpool_w[j], dpool_scale[j]), landed = _pool_bwd(dtok, st["dpre"], p["pool_w"][j], row(p["pool_scale"][j]),
                                                                 f"pool_bwd_{i}", ride=scatter(carry["pool"]))
            recv.update(zip(carry["pool"], landed))
            parts = [dx, dqm]
            (dh, dg_pre[i][1]), landed = _mix_in_bwd(dh, st["h1"], row(g_pre[i, 1]), parts, gw[("w_in_pool", j)].reshape(D_MODEL, D_MIX),
                                                     "pool", f"mix_in_bwd_{i}", ride=scatter(carry["mix_in"]))
            recv.update(zip(carry["mix_in"], landed))
            grads[("w_in_pool", j)], _ = _wgrad([st["u"]], parts, "rows", f"wgrad_w_in_pool_{i}")
        else:
            (dq, dk, dv), landed = _sb_bwd(st["proj"], dtok, st["o32"], f"sb_bwd_{i}", ride=scatter(carry["sb"]))
            recv.update(zip(carry["sb"], landed))
            parts = [dq, dk, dv, dqm]
            (dh, dg_pre[i][1]), _ = _mix_in_bwd(dh, st["h1"], row(g_pre[i, 1]), parts, gw[("w_in_sb", j)].reshape(D_SB, D_MODEL), "sb",
                                                f"mix_in_bwd_{i}")
            grads[("w_in_sb", j)], _ = _wgrad(parts, [st["u"]], "rows", f"wgrad_w_in_sb_{i}")
        dh = ffn_backward(dh, st, i, 1, "h0", 0, carry["ffn1"], i == 0)
    grad_x = dh

    shard8 = lambda rows_: jnp.transpose(jnp.stack([jnp.concatenate(r, axis=0) for r in rows_]).reshape(DEPTH, 3, N_DEV, -1),
                                         (2, 0, 1, 3))
    tail = ffn(1, 0)[2:]
    landed = _comm_call(
        scatter(tail) + [_Xfer(shard8(dg_pre), scatter=True), _Xfer(shard8(dg_post), scatter=True),
                         _Xfer(jnp.concatenate(dg_mem, axis=0)), _Xfer(jnp.stack(dpool_w)),
                         _Xfer(jnp.concatenate(dpool_scale, axis=0)), _Xfer(loss_part)], "exchange_last")
    recv.update(zip(tail, landed))
    small = dict(zip(["g_pre", "g_post", "g_mem", "pool_w", "pool_scale"], landed[len(tail):]))

    def update(name, slots, w, m, v):
        into = None
        for l, r in enumerate(slots):
            into = _adamw(r, w, m, v, l, into, f"adamw_{name}_{l}")
        return into

    out = {}
    for n in STACKED:
        w, m, v = (wsb_t, tsb(opt_m[n]), tsb(opt_v[n])) if n == "w_in_sb" else (p[n], opt_m[n], opt_v[n])
        res = update(n, [recv[(n, l)] for l in range(w.shape[0])], w, m, v)
        out[n] = [tsb(a) for a in res] if n == "w_in_sb" else res
    one = lambda a: a.reshape(1, -1, a.shape[-1])
    for n, r in small.items():
        res = update(n, [r.reshape((N_DEV,) + one(p[n]).shape[1:])], one(p[n]), one(opt_m[n]), one(opt_v[n]))
        out[n] = [a.reshape(p[n].shape) for a in res]
    loss = jnp.sum(landed[-1][:, 0, 0])
    return loss, grad_x, out


STACKED = ["ffn1_gate", "ffn1_up", "ffn1_down", "ffn2_gate", "ffn2_up", "ffn2_down", "w_in_pool", "w_in_sb", "w_mem_kv", "w_out"]
WEIGHTS = ["g_pre", "g_post", "g_mem", "ffn1_gate", "ffn1_up", "ffn1_down", "ffn2_gate", "ffn2_up", "ffn2_down",
           "w_in_pool", "pool_w", "pool_scale", "w_in_sb", "w_mem_kv", "w_out"]


def kernel(x, mem, g_pre, g_post, g_mem, ffn1_gate, ffn1_up, ffn1_down, ffn2_gate, ffn2_up, ffn2_down, w_in_pool, pool_w, pool_scale, w_in_sb, w_mem_kv, w_out, loss_target, m_g_pre, m_g_post, m_g_mem, m_ffn1_gate, m_ffn1_up, m_ffn1_down, m_ffn2_gate, m_ffn2_up, m_ffn2_down, m_w_in_pool, m_pool_w, m_pool_scale, m_w_in_sb, m_w_mem_kv, m_w_out, v_g_pre, v_g_post, v_g_mem, v_ffn1_gate, v_ffn1_up, v_ffn1_down, v_ffn2_gate, v_ffn2_up, v_ffn2_down, v_w_in_pool, v_pool_w, v_pool_scale, v_w_in_sb, v_w_mem_kv, v_w_out):
    given = dict(locals())
    p = {n: given[n] for n in WEIGHTS}
    opt_m = {n: given["m_" + n] for n in WEIGHTS}
    opt_v = {n: given["v_" + n] for n in WEIGHTS}
    loss, grad_x, out = _step(p, opt_m, opt_v, x[0], mem[0], loss_target[0])
    res = [loss, grad_x[None]]
    for q in range(4):
        res += [out[n][q] for n in WEIGHTS]
    return tuple(res)
```
